```python
import math
import jax, jax.numpy as jnp
from jax import lax
import numpy as np

D_MODEL = 4096
BATCH = 1
SEQ = 8192
DEPTH = 1

ATTN_HEADS = 32
ATTN_HEAD_DIM = 128
ATTN_WIDTH = ATTN_HEADS * ATTN_HEAD_DIM
Q_LORA_RANK = 1024
KV_LORA_RANK = 512
IDX_HEADS = 32
IDX_HEAD_DIM = 64
TOPK_MAX = 256
Q_BLOCK = 128
MLSTM_HEADS = 8
MLSTM_QK_DIM = (D_MODEL // 2) // MLSTM_HEADS
MLSTM_V_DIM = D_MODEL // MLSTM_HEADS
MLSTM_QK_WIDTH = MLSTM_HEADS * MLSTM_QK_DIM
MLSTM_WIDTH = MLSTM_HEADS * MLSTM_V_DIM
MLSTM_CHUNK = 64
CONV_WIDTH = 4
GATE_SOFTCAP = 15.0
DEEPNORM_ALPHA = (2.0 * DEPTH) ** 0.25
DEEPNORM_BETA = (8.0 * DEPTH) ** -0.25
NORM_EPS = 1e-6
IN_WIDTHS = (Q_LORA_RANK, KV_LORA_RANK, IDX_HEAD_DIM, IDX_HEADS, ATTN_WIDTH, 2 * MLSTM_QK_WIDTH, MLSTM_WIDTH, MLSTM_WIDTH, MLSTM_HEADS, MLSTM_HEADS, MLSTM_WIDTH, D_MODEL, D_MODEL)
IN_TOTAL = sum(IN_WIDTHS)
FGATE_OFFSET = sum(IN_WIDTHS[:9])

kernel_name = 'dsa_mlstm_gated_hybrid_deepnorm_adaln'


def rms_norm(x, g):
    xf = x.astype(jnp.float32)
    y = xf * lax.rsqrt(jnp.mean(xf * xf, axis=-1, keepdims=True) + NORM_EPS)
    return (y * g).astype(x.dtype)


def layer_norm(x, g, b):
    xf = x.astype(jnp.float32)
    mu = jnp.mean(xf, axis=-1, keepdims=True)
    var = jnp.mean(jnp.square(xf - mu), axis=-1, keepdims=True)
    return ((xf - mu) * lax.rsqrt(var + NORM_EPS) * g + b).astype(x.dtype)


def softcap(x):
    return GATE_SOFTCAP * jnp.tanh(x / GATE_SOFTCAP)


def alibi_slopes(n_heads):
    return jnp.exp2(-8.0 * jnp.arange(1, n_heads + 1, dtype=jnp.float32) / n_heads)


def causal_depthwise_conv(x, w, b):
    width = w.shape[0]
    seq = x.shape[1]
    xp = jnp.pad(x, ((0, 0), (width - 1, 0), (0, 0)))
    y = b
    for j in range(width):
        y = y + xp[:, j:j + seq] * w[j]
    return y


def dsa_attention(q, q_idx, w_idx, k_idx, c_kv, w_uk, w_uv):
    bsz, seq = q.shape[0], q.shape[1]
    n_sel = min(TOPK_MAX, seq // 4)
    n_blocks = seq // Q_BLOCK
    slopes = alibi_slopes(ATTN_HEADS)
    key_pos = jnp.arange(seq, dtype=jnp.int32)
    scale = ATTN_HEAD_DIM ** -0.5

    def to_blocks(a):
        return a.reshape(bsz, n_blocks, Q_BLOCK, *a.shape[2:]).swapaxes(0, 1)

    def block_fn(args):
        blk, q_b, qi_b, wi_b = args
        q_pos = blk * Q_BLOCK + jnp.arange(Q_BLOCK, dtype=jnp.int32)
        rel = jnp.einsum('bqhd,bsd->bqhs', qi_b, k_idx)
        score = jnp.einsum('bqhs,bqh->bqs', jax.nn.relu(rel), wi_b).astype(jnp.float32)
        score = jnp.where(key_pos[None, :] <= q_pos[:, None], score, -jnp.inf)
        _, idx = lax.top_k(score, n_sel)
        kv_sel = jax.vmap(lambda kv, i: kv[i])(c_kv, idx)
        q_abs = jnp.einsum('bqhd,hdr->bqhr', q_b, w_uk)
        logits = jnp.einsum('bqhr,bqkr->bhqk', q_abs, kv_sel).astype(jnp.float32) * scale
        dist = (q_pos[None, :, None] - idx).astype(jnp.float32)
        logits = logits - slopes[:, None, None] * dist[:, None]
        valid = (idx <= q_pos[None, :, None])[:, None]
        logits = jnp.where(valid, logits, -jnp.inf)
        p = jax.nn.softmax(logits, axis=-1).astype(c_kv.dtype)
        o_lat = jnp.einsum('bhqk,bqkr->bqhr', p, kv_sel)
        o = jnp.einsum('bqhr,hrd->bqhd', o_lat, w_uv)
        return o.reshape(bsz, Q_BLOCK, ATTN_WIDTH)

    out = lax.map(block_fn, (jnp.arange(n_blocks, dtype=jnp.int32), to_blocks(q), to_blocks(q_idx), to_blocks(w_idx)))
    return out.swapaxes(0, 1).reshape(bsz, seq, ATTN_WIDTH)


def mlstm_chunkwise(q, k, v, ig, fg):
    bsz, seq, nh, dk = q.shape
    dv = v.shape[-1]
    L = MLSTM_CHUNK
    nc = seq // L
    logf = jax.nn.log_sigmoid(fg)

    def chunk4(a):
        return a.reshape(bsz, nc, L, nh, a.shape[-1]).transpose(1, 0, 3, 2, 4)

    def chunk3(a):
        return a.reshape(bsz, nc, L, nh).transpose(1, 0, 3, 2)

    causal = jnp.tril(jnp.ones((L, L), dtype=bool))

    def step(carry, xs):
        C, n, m = carry
        qc, kc, vc, ic, fc = xs
        b = jnp.cumsum(fc, axis=-1)
        dmat = jnp.where(causal, b[..., :, None] - b[..., None, :] + ic[..., None, :], -jnp.inf)
        m_inter = b + m[..., None]
        m_t = jnp.maximum(m_inter, jnp.max(dmat, axis=-1))
        s = jnp.einsum('bhjd,bhsd->bhjs', qc, kc) * jnp.exp(dmat - m_t[..., None])
        inter = jnp.exp(m_inter - m_t)
        num = jnp.einsum('bhjs,bhsv->bhjv', s, vc) + inter[..., None] * jnp.einsum('bhjd,bhdv->bhjv', qc, C)
        den = jnp.sum(s, axis=-1) + inter * jnp.einsum('bhjd,bhd->bhj', qc, n)
        h = num / jnp.maximum(jnp.abs(den), jnp.exp(-m_t))[..., None]
        g = b[..., -1]
        m_new = m_t[..., -1]
        wgt = jnp.exp(g[..., None] - b + ic - m_new[..., None])
        decay = jnp.exp(g + m - m_new)
        wk = wgt[..., None] * kc
        C_new = decay[..., None, None] * C + jnp.einsum('bhsd,bhsv->bhdv', wk, vc)
        n_new = decay[..., None] * n + jnp.sum(wk, axis=2)
        return (C_new, n_new, m_new), h

    init = (jnp.zeros((bsz, nh, dk, dv), jnp.float32), jnp.zeros((bsz, nh, dk), jnp.float32), jnp.zeros((bsz, nh), jnp.float32))
    _, h = lax.scan(step, init, (chunk4(q), chunk4(k), chunk4(v), chunk3(ig), chunk3(logf)))
    return h.transpose(1, 0, 3, 2, 4).reshape(bsz, seq, nh, dv)


def hybrid_layer(x, c, w_ada, b_ada, w_in, b_in, g_q, g_kv, w_uq, w_iq, w_uk, w_uv, g_kidx, b_kidx, conv_w, conv_b, g_mh, w_attn_out, w_mlstm_out, w_out, ln_g, ln_b):
    bsz, seq, _ = x.shape
    mod = jax.nn.silu(c) @ w_ada + b_ada
    shift, scale, gate = jnp.split(mod, 3, axis=-1)
    u = x * (1.0 + scale[:, None]) + shift[:, None]
    proj = u @ w_in + b_in
    split_points = np.cumsum(IN_WIDTHS)[:-1].tolist()
    (q_lat, kv_lat, k_idx, w_idx, z_attn, qk_m, v_m, o_m, i_m, f_m, z_m, gate_attn, gate_mlstm) = jnp.split(proj, split_points, axis=-1)

    c_q = rms_norm(q_lat, g_q)
    q = (c_q @ w_uq).reshape(bsz, seq, ATTN_HEADS, ATTN_HEAD_DIM)
    q_idx = (c_q @ w_iq).reshape(bsz, seq, IDX_HEADS, IDX_HEAD_DIM)
    c_kv = rms_norm(kv_lat, g_kv)
    k_idx = layer_norm(k_idx, g_kidx, b_kidx)
    w_idx = w_idx * (IDX_HEADS ** -0.5 * IDX_HEAD_DIM ** -0.5)
    o_attn = dsa_attention(q, q_idx, w_idx, k_idx, c_kv, w_uk, w_uv)
    y_attn = (o_attn * jax.nn.silu(z_attn)) @ w_attn_out

    qk_m = jax.nn.silu(causal_depthwise_conv(qk_m, conv_w, conv_b))
    q_m, k_m = jnp.split(qk_m, 2, axis=-1)
    q_m = q_m.reshape(bsz, seq, MLSTM_HEADS, MLSTM_QK_DIM).astype(jnp.float32)
    k_m = k_m.reshape(bsz, seq, MLSTM_HEADS, MLSTM_QK_DIM).astype(jnp.float32) * (MLSTM_QK_DIM ** -0.5)
    v_m = v_m.reshape(bsz, seq, MLSTM_HEADS, MLSTM_V_DIM).astype(jnp.float32)
    ig = softcap(i_m.astype(jnp.float32))
    fg = softcap(f_m.astype(jnp.float32))
    h = mlstm_chunkwise(q_m, k_m, v_m, ig, fg)
    h = rms_norm(h, g_mh).reshape(bsz, seq, MLSTM_WIDTH).astype(x.dtype)
    h = h * jax.nn.sigmoid(o_m)
    y_mlstm = (h * jax.nn.silu(z_m)) @ w_mlstm_out

    merged = jax.nn.sigmoid(gate_attn) * y_attn + jax.nn.sigmoid(gate_mlstm) * y_mlstm
    out = merged @ w_out
    return layer_norm(DEEPNORM_ALPHA * x + gate[:, None] * out, ln_g, ln_b)


def setup_inputs(seed: int = 0) -> dict:
    key = jax.random.key(seed)
    ks = jax.random.split(key, 24)

    def nrm(k, shape, std):
        return jax.random.normal(k, shape, jnp.float32) * std

    beta = DEEPNORM_BETA
    x = nrm(ks[0], (BATCH, SEQ, D_MODEL), 1.0)
    c = nrm(ks[1], (BATCH, D_MODEL), 1.0)
    w_ada = nrm(ks[2], (DEPTH, D_MODEL, 3 * D_MODEL), 0.5 * D_MODEL ** -0.5)
    b_ada = nrm(ks[3], (DEPTH, 3 * D_MODEL), 0.02).at[:, 2 * D_MODEL:].add(1.0)
    w_in = nrm(ks[4], (DEPTH, D_MODEL, IN_TOTAL), D_MODEL ** -0.5)
    b_in = nrm(ks[5], (DEPTH, IN_TOTAL), 0.01).at[:, FGATE_OFFSET:FGATE_OFFSET + MLSTM_HEADS].add(jnp.linspace(3.0, 6.0, MLSTM_HEADS))
    g_q = 1.0 + nrm(ks[6], (DEPTH, Q_LORA_RANK), 0.01)
    g_kv = 1.0 + nrm(ks[7], (DEPTH, KV_LORA_RANK), 0.01)
    w_uq = nrm(ks[8], (DEPTH, Q_LORA_RANK, ATTN_WIDTH), Q_LORA_RANK ** -0.5)
    w_iq = nrm(ks[9], (DEPTH, Q_LORA_RANK, IDX_HEADS * IDX_HEAD_DIM), Q_LORA_RANK ** -0.5)
    w_uk = nrm(ks[10], (DEPTH, ATTN_HEADS, ATTN_HEAD_DIM, KV_LORA_RANK), KV_LORA_RANK ** -0.5)
    w_uv = nrm(ks[11], (DEPTH, ATTN_HEADS, KV_LORA_RANK, ATTN_HEAD_DIM), beta * KV_LORA_RANK ** -0.5)
    g_kidx = 1.0 + nrm(ks[12], (DEPTH, IDX_HEAD_DIM), 0.01)
    b_kidx = nrm(ks[13], (DEPTH, IDX_HEAD_DIM), 0.01)
    conv_w = nrm(ks[14], (DEPTH, CONV_WIDTH, 2 * MLSTM_QK_WIDTH), CONV_WIDTH ** -0.5)
    conv_b = nrm(ks[15], (DEPTH, 2 * MLSTM_QK_WIDTH), 0.01)
    g_mh = 1.0 + nrm(ks[16], (DEPTH, MLSTM_HEADS, MLSTM_V_DIM), 0.01)
    w_attn_out = nrm(ks[17], (DEPTH, ATTN_WIDTH, D_MODEL), beta * ATTN_WIDTH ** -0.5)
    w_mlstm_out = nrm(ks[18], (DEPTH, MLSTM_WIDTH, D_MODEL), beta * MLSTM_WIDTH ** -0.5)
    w_out = nrm(ks[19], (DEPTH, D_MODEL, D_MODEL), beta * D_MODEL ** -0.5)
    ln_g = 1.0 + nrm(ks[20], (DEPTH, D_MODEL), 0.01)
    ln_b = nrm(ks[21], (DEPTH, D_MODEL), 0.01)
    return {'x': x, 'c': c, 'w_ada': w_ada, 'b_ada': b_ada, 'w_in': w_in, 'b_in': b_in, 'g_q': g_q, 'g_kv': g_kv, 'w_uq': w_uq, 'w_iq': w_iq, 'w_uk': w_uk, 'w_uv': w_uv, 'g_kidx': g_kidx, 'b_kidx': b_kidx, 'conv_w': conv_w, 'conv_b': conv_b, 'g_mh': g_mh, 'w_attn_out': w_attn_out, 'w_mlstm_out': w_mlstm_out, 'w_out': w_out, 'ln_g': ln_g, 'ln_b': ln_b}


def reference(x, c, w_ada, b_ada, w_in, b_in, g_q, g_kv, w_uq, w_iq, w_uk, w_uv, g_kidx, b_kidx, conv_w, conv_b, g_mh, w_attn_out, w_mlstm_out, w_out, ln_g, ln_b):
    for layer in range(DEPTH):
        x = hybrid_layer(x, c, w_ada[layer], b_ada[layer], w_in[layer], b_in[layer], g_q[layer], g_kv[layer], w_uq[layer], w_iq[layer], w_uk[layer], w_uv[layer], g_kidx[layer], b_kidx[layer], conv_w[layer], conv_b[layer], g_mh[layer], w_attn_out[layer], w_mlstm_out[layer], w_out[layer], ln_g[layer], ln_b[layer])
    return x
```

```python
import functools

import jax
import jax.numpy as jnp
from jax import lax
from jax.experimental import pallas as pl
from jax.experimental.pallas import tpu as pltpu

F32 = jnp.float32
BF16 = jnp.bfloat16
I32 = jnp.int32

D_MODEL = 4096
ATTN_HEADS = 32
ATTN_HEAD_DIM = 128
ATTN_WIDTH = ATTN_HEADS * ATTN_HEAD_DIM
Q_LORA_RANK = 1024
KV_LORA_RANK = 512
IDX_HEADS = 32
IDX_HEAD_DIM = 64
TOPK_MAX = 256
MLSTM_HEADS = 8
MLSTM_QK_DIM = (D_MODEL // 2) // MLSTM_HEADS
MLSTM_V_DIM = D_MODEL // MLSTM_HEADS
MLSTM_QK_WIDTH = MLSTM_HEADS * MLSTM_QK_DIM
MLSTM_WIDTH = MLSTM_HEADS * MLSTM_V_DIM
MLSTM_CHUNK = 64
CONV_WIDTH = 4
GATE_SOFTCAP = 15.0
DEEPNORM_ALPHA = 2.0 ** 0.25
NORM_EPS = 1e-6

IN_WIDTHS = (Q_LORA_RANK, KV_LORA_RANK, IDX_HEAD_DIM, IDX_HEADS, ATTN_WIDTH, 2 * MLSTM_QK_WIDTH, MLSTM_WIDTH,
             MLSTM_WIDTH, MLSTM_HEADS, MLSTM_HEADS, MLSTM_WIDTH, D_MODEL, D_MODEL)
IN_NAMES = ("q_lat", "kv_lat", "k_idx", "w_idx", "z_attn", "qk_m", "v_m", "o_m", "i_m", "f_m", "z_m", "g_attn", "g_mlstm")
IN_OFFSETS = {n: sum(IN_WIDTHS[:i]) for i, n in enumerate(IN_NAMES)}
IN_WIDTH_OF = dict(zip(IN_NAMES, IN_WIDTHS))

P_ORDER = ("z_attn", "qk_m", "v_m", "o_m", "z_m", "g_attn", "g_mlstm", "q_lat", "kv_lat", "k_idx", "w_idx", "i_m", "f_m")
P_OFFSETS = {}
_off = 0
for _n in P_ORDER:
    P_OFFSETS[_n] = _off
    _off += IN_WIDTH_OF[_n]
P_USED = _off
PROJ_TN = 1024
P_TOTAL = -(-P_USED // PROJ_TN) * PROJ_TN
SMALL_W = 128
SMALL_OFF = P_OFFSETS["k_idx"]
SM_WIDX = IDX_HEAD_DIM
SM_I = SM_WIDX + IDX_HEADS
SM_F = SM_I + MLSTM_HEADS

VMEM_CAP_BYTES = 60 * 1024 * 1024

NEG_MASK = -1e30
M_INIT = -1e20
INT_MIN = -2 ** 31


def _cparams(sem, vmem_mb):
    return pltpu.CompilerParams(dimension_semantics=sem, vmem_limit_bytes=min(vmem_mb * 1024 * 1024, VMEM_CAP_BYTES))


def _sigmoid(x):
    return jax.nn.sigmoid(x)


def _silu(x):
    return x * jax.nn.sigmoid(x)


def _const_spec(shape):
    nd = len(shape)
    return pl.BlockSpec(shape, lambda *_: (0,) * nd, pipeline_mode=pl.Buffered(1))


def _ada_kernel(c_ref, w_ref, b_ref, o_ref):
    c = c_ref[...]
    o_ref[...] = jnp.sum(w_ref[...] * _silu(c), axis=0, keepdims=True) + b_ref[...]


def _ada(c_col, w_ada, b_ada):
    d, n = w_ada.shape
    tn = 512
    return pl.pallas_call(
        _ada_kernel,
        out_shape=jax.ShapeDtypeStruct((1, n), F32),
        grid=(n // tn,),
        in_specs=[pl.BlockSpec((d, 1), lambda j: (0, 0)),
                  pl.BlockSpec((d, tn), lambda j: (0, j)),
                  pl.BlockSpec((1, tn), lambda j: (0, j))],
        out_specs=pl.BlockSpec((1, tn), lambda j: (0, j)),
        compiler_params=_cparams(("arbitrary",), 32),
        name="ada",
    )(c_col, w_ada, b_ada)


def _modulate_kernel(x_ref, shift_ref, scale_ref, u_ref):
    u_ref[...] = (x_ref[...] * (1.0 + scale_ref[...]) + shift_ref[...]).astype(BF16)


def _modulate(x2, mod):
    s, d = x2.shape
    tm = min(512, s)
    return pl.pallas_call(
        _modulate_kernel,
        out_shape=jax.ShapeDtypeStruct((s, d), BF16),
        grid=(s // tm,),
        in_specs=[pl.BlockSpec((tm, d), lambda i: (i, 0)),
                  pl.BlockSpec((1, d), lambda i: (0, 0)),
                  pl.BlockSpec((1, d), lambda i: (0, 1))],
        out_specs=pl.BlockSpec((tm, d), lambda i: (i, 0)),
        compiler_params=_cparams(("arbitrary",), 40),
        name="modulate",
    )(x2, mod, mod)


def _proj_kernel(u_ref, w_ref, b_ref, o_ref):
    o_ref[...] = jnp.dot(u_ref[...], w_ref[...], preferred_element_type=F32) + b_ref[...]


def _proj(u, w_cat, b_cat):
    s, d = u.shape
    n = w_cat.shape[1]
    tm = min(1024, s)
    tn = PROJ_TN
    return pl.pallas_call(
        _proj_kernel,
        out_shape=jax.ShapeDtypeStruct((s, n), F32),
        grid=(n // tn, s // tm),
        in_specs=[pl.BlockSpec((tm, d), lambda j, i: (i, 0)),
                  pl.BlockSpec((d, tn), lambda j, i: (0, j)),
                  pl.BlockSpec((1, tn), lambda j, i: (0, j))],
        out_specs=pl.BlockSpec((tm, tn), lambda j, i: (i, j)),
        compiler_params=_cparams(("arbitrary", "arbitrary"), 56),
        name="proj",
    )(u, w_cat, b_cat)


def _qpath_kernel(ql_ref, g_ref, wuq_ref, wiq_ref, qT_ref, qiT_ref, *, scale):
    x = ql_ref[...]
    cq = (x * lax.rsqrt(jnp.mean(x * x, axis=-1, keepdims=True) + NORM_EPS) * g_ref[...]).astype(BF16)
    nt = (((1,), (1,)), ((), ()))
    qT = lax.dot_general(wuq_ref[...], cq, nt, preferred_element_type=F32)
    qT_ref[...] = (qT * scale).reshape(qT_ref.shape).astype(BF16)
    qiT = lax.dot_general(wiq_ref[...], cq, nt, preferred_element_type=F32)
    qiT_ref[...] = qiT.reshape(qiT_ref.shape).astype(BF16)


def _qpath(proj, g_q, w_uqT, w_iqT, tq):
    s = proj.shape[0]
    r = Q_LORA_RANK
    return pl.pallas_call(
        functools.partial(_qpath_kernel, scale=ATTN_HEAD_DIM ** -0.5),
        out_shape=(jax.ShapeDtypeStruct((ATTN_HEADS, ATTN_HEAD_DIM, s), BF16),
                   jax.ShapeDtypeStruct((IDX_HEADS, IDX_HEAD_DIM, s), BF16)),
        grid=(s // tq,),
        in_specs=[pl.BlockSpec((tq, r), lambda i: (i, P_OFFSETS["q_lat"] // r)),
                  _const_spec((1, r)),
                  _const_spec(w_uqT.shape),
                  _const_spec(w_iqT.shape)],
        out_specs=(pl.BlockSpec((ATTN_HEADS, ATTN_HEAD_DIM, tq), lambda i: (0, 0, i)),
                   pl.BlockSpec((IDX_HEADS, IDX_HEAD_DIM, tq), lambda i: (0, 0, i))),
        compiler_params=_cparams(("arbitrary",), 48),
        name="qpath",
    )(proj, g_q, w_uqT, w_iqT)


def _kvpath_kernel(kvl_ref, sm_ref, gkv_ref, gk_ref, bk_ref, wuk_ref, wuv_ref, k_ref, vT_ref, kidx_ref, widx_ref, *, wscale):
    x = kvl_ref[...]
    ckv = (x * lax.rsqrt(jnp.mean(x * x, axis=-1, keepdims=True) + NORM_EPS) * gkv_ref[...]).astype(BF16)
    kfull = jnp.dot(ckv, wuk_ref[...], preferred_element_type=F32)
    for h in range(ATTN_HEADS):
        k_ref[h] = kfull[:, h * ATTN_HEAD_DIM:(h + 1) * ATTN_HEAD_DIM].astype(BF16)
    nt = (((1,), (1,)), ((), ()))
    vT = lax.dot_general(wuv_ref[...], ckv, nt, preferred_element_type=F32)
    vT_ref[...] = vT.reshape(vT_ref.shape).astype(BF16)
    sm = sm_ref[...]
    ki = sm[:, :IDX_HEAD_DIM]
    mu = jnp.mean(ki, axis=-1, keepdims=True)
    var = jnp.mean(jnp.square(ki - mu), axis=-1, keepdims=True)
    kidx_ref[...] = ((ki - mu) * lax.rsqrt(var + NORM_EPS) * gk_ref[...] + bk_ref[...]).astype(BF16)
    widx_ref[...] = sm[:, SM_WIDX:SM_WIDX + IDX_HEADS] * wscale


def _kvpath(proj, g_kv, g_kidx, b_kidx, w_ukT, w_uvT, tm):
    s = proj.shape[0]
    r = KV_LORA_RANK
    return pl.pallas_call(
        functools.partial(_kvpath_kernel, wscale=IDX_HEADS ** -0.5 * IDX_HEAD_DIM ** -0.5),
        out_shape=(jax.ShapeDtypeStruct((ATTN_HEADS, s, ATTN_HEAD_DIM), BF16),
                   jax.ShapeDtypeStruct((ATTN_HEADS, ATTN_HEAD_DIM, s), BF16),
                   jax.ShapeDtypeStruct((s, IDX_HEAD_DIM), BF16),
                   jax.ShapeDtypeStruct((s, IDX_HEADS), F32)),
        grid=(s // tm,),
        in_specs=[pl.BlockSpec((tm, r), lambda i: (i, P_OFFSETS["kv_lat"] // r)),
                  pl.BlockSpec((tm, SMALL_W), lambda i: (i, SMALL_OFF // SMALL_W)),
                  _const_spec((1, r)),
                  _const_spec((1, IDX_HEAD_DIM)),
                  _const_spec((1, IDX_HEAD_DIM)),
                  _const_spec(w_ukT.shape),
                  _const_spec(w_uvT.shape)],
        out_specs=(pl.BlockSpec((ATTN_HEADS, tm, ATTN_HEAD_DIM), lambda i: (0, i, 0)),
                   pl.BlockSpec((ATTN_HEADS, ATTN_HEAD_DIM, tm), lambda i: (0, 0, i)),
                   pl.BlockSpec((tm, IDX_HEAD_DIM), lambda i: (i, 0)),
                   pl.BlockSpec((tm, IDX_HEADS), lambda i: (i, 0))),
        compiler_params=_cparams(("arbitrary",), 48),
        name="kvpath",
    )(proj, proj, g_kv, g_kidx, b_kidx, w_ukT, w_uvT)


def _indexer_kernel(kidx_ref, qiT_ref, wT_ref, keys_ref, thr_ref, *, seq, tq, nsel):
    i = pl.program_id(0)
    ch = 128
    cb = tq
    n_score = (i + 1) * (tq // ch)
    n_count = i + 1
    tpos = i * tq + lax.broadcasted_iota(I32, (ch, tq), 1)

    def score_chunk(c, carry):
        r0 = pl.multiple_of(c * ch, ch)
        kc = kidx_ref[pl.ds(r0, ch), :]

        def head(h, acc):
            r = jnp.dot(kc, qiT_ref[h], preferred_element_type=F32)
            return acc + jnp.maximum(r, 0.0) * wT_ref[h]
        acc = lax.fori_loop(0, IDX_HEADS, head, jnp.zeros((ch, tq), F32))
        bits = pltpu.bitcast(acc, I32)
        key = jnp.where(bits < 0, bits ^ 0x7FFFFFFF, bits)
        key = jnp.where(acc == 0.0, 0, key)
        spos = r0 + lax.broadcasted_iota(I32, (ch, tq), 0)
        keys_ref[pl.ds(r0, ch), :] = jnp.where(spos <= tpos, key, INT_MIN)
        return carry
    lax.fori_loop(0, n_score, score_chunk, 0)

    def fill_chunk(c, carry):
        keys_ref[pl.ds(pl.multiple_of(c * cb, cb), cb), :] = jnp.full((cb, tq), INT_MIN, I32)
        return carry
    lax.fori_loop(n_count, seq // cb, fill_chunk, 0)

    def count(pred):
        def body(c, part):
            r0 = pl.multiple_of(c * cb, cb)
            m = jnp.where(pred(keys_ref[pl.ds(r0, cb), :], r0), 1, 0)
            return part + jnp.sum(m.reshape(cb // 8, 8, tq), axis=0)
        part = lax.fori_loop(0, n_count, body, jnp.zeros((8, tq), I32))
        return jnp.sum(part, axis=0, keepdims=True)

    def count_ge(cand):
        return count(lambda blk, r0: blk >= cand)

    t0 = jnp.where(count_ge(jnp.zeros((1, tq), I32)) >= nsel, 0, INT_MIN).astype(I32)

    def bit_step(b, t):
        cand = t + jnp.left_shift(jnp.int32(1), 30 - b)
        return jnp.where(count_ge(cand) >= nsel, cand, t)
    thr = lax.fori_loop(0, 31, bit_step, t0)
    thr_ref[...] = thr

    tie = (count_ge(thr) > nsel) & (thr > INT_MIN)

    @pl.when(jnp.max(tie.astype(I32)) > 0)
    def _break_ties():
        need = nsel - count(lambda blk, r0: blk > thr)

        def eq_below(j):
            return count(lambda blk, r0: (blk == thr) & (r0 + lax.broadcasted_iota(I32, (cb, tq), 0) < j))

        def jbit(b, j):
            test = j + jnp.left_shift(jnp.int32(1), (seq.bit_length() - 2) - b)
            return jnp.where(eq_below(test) < need, test, j)
        jlast = lax.fori_loop(0, seq.bit_length() - 1, jbit, jnp.zeros((1, tq), I32))

        def demote(c, carry):
            r0 = pl.multiple_of(c * cb, cb)
            blk = keys_ref[pl.ds(r0, cb), :]
            row = r0 + lax.broadcasted_iota(I32, (cb, tq), 0)
            keys_ref[pl.ds(r0, cb), :] = jnp.where(tie & (blk == thr) & (row > jlast), thr - 1, blk)
            return carry
        lax.fori_loop(0, n_count, demote, 0)


def _indexer(kidx, qiT, wT, tq, nsel):
    s = kidx.shape[0]
    return pl.pallas_call(
        functools.partial(_indexer_kernel, seq=s, tq=tq, nsel=nsel),
        out_shape=(jax.ShapeDtypeStruct((s, s), I32), jax.ShapeDtypeStruct((1, s), I32)),
        grid=(s // tq,),
        in_specs=[_const_spec((s, IDX_HEAD_DIM)),
                  pl.BlockSpec((IDX_HEADS, IDX_HEAD_DIM, tq), lambda i: (0, 0, i)),
                  pl.BlockSpec((IDX_HEADS, 1, tq), lambda i: (0, 0, i))],
        out_specs=(pl.BlockSpec((s, tq), lambda i: (0, i)),
                   pl.BlockSpec((1, tq), lambda i: (0, i))),
        compiler_params=_cparams(("arbitrary",), 40),
        name="indexer",
    )(kidx, qiT, wT)


def _attn_kernel(qT_ref, k_ref, vT_ref, keys_ref, thr_ref, z_ref, sl_ref, o_ref,
                 acc_ref, m_ref, l_ref, bias_ref, dist_ref, *, tq, tk, nk):
    qi = pl.program_id(0)
    kj = pl.program_id(1)
    last = (qi * tq + tq - 1) // tk

    @pl.when(kj == 0)
    def _init():
        acc_ref[...] = jnp.zeros(acc_ref.shape, F32)
        m_ref[...] = jnp.full(m_ref.shape, M_INIT, F32)
        l_ref[...] = jnp.zeros(l_ref.shape, F32)

    @pl.when(kj <= last)
    def _compute():
        spos = kj * tk + lax.broadcasted_iota(I32, (tk, tq), 0)
        tpos = qi * tq + lax.broadcasted_iota(I32, (tk, tq), 1)
        sel = (keys_ref[...] >= thr_ref[...]) & (spos <= tpos)
        bias_ref[...] = jnp.where(sel, 0.0, NEG_MASK)
        dist_ref[...] = (tpos - spos).astype(F32)

        def head(h, carry):
            s = jnp.dot(k_ref[h], qT_ref[h], preferred_element_type=F32)
            lg = s + bias_ref[...] - sl_ref[h] * dist_ref[...]
            m_old = m_ref[h]
            m_new = jnp.maximum(m_old, jnp.max(lg, axis=0, keepdims=True))
            p = jnp.exp(lg - m_new)
            alpha = jnp.exp(m_old - m_new)
            l_ref[h] = alpha * l_ref[h] + jnp.sum(p, axis=0, keepdims=True)
            acc_ref[h] = alpha * acc_ref[h] + jnp.dot(vT_ref[h], p.astype(BF16), preferred_element_type=F32)
            m_ref[h] = m_new
            return carry
        lax.fori_loop(0, ATTN_HEADS, head, 0)

    @pl.when(kj == nk - 1)
    def _finish():
        for h in range(ATTN_HEADS):
            cols = slice(h * ATTN_HEAD_DIM, (h + 1) * ATTN_HEAD_DIM)
            o = (acc_ref[h] * (1.0 / l_ref[h])).T
            o_ref[:, cols] = (o * _silu(z_ref[:, cols])).astype(BF16)


def _attention(qT, k, vT, keys, thr, proj, slopes, tq, tk):
    s = k.shape[1]
    nk = s // tk

    def kv_blk(qi, kj):
        return jnp.minimum(kj, (qi * tq + tq - 1) // tk)
    return pl.pallas_call(
        functools.partial(_attn_kernel, tq=tq, tk=tk, nk=nk),
        out_shape=jax.ShapeDtypeStruct((s, ATTN_WIDTH), BF16),
        grid=(s // tq, nk),
        in_specs=[pl.BlockSpec((ATTN_HEADS, ATTN_HEAD_DIM, tq), lambda qi, kj: (0, 0, qi)),
                  pl.BlockSpec((ATTN_HEADS, tk, ATTN_HEAD_DIM), lambda qi, kj: (0, kv_blk(qi, kj), 0)),
                  pl.BlockSpec((ATTN_HEADS, ATTN_HEAD_DIM, tk), lambda qi, kj: (0, 0, kv_blk(qi, kj))),
                  pl.BlockSpec((tk, tq), lambda qi, kj: (kv_blk(qi, kj), qi)),
                  pl.BlockSpec((1, tq), lambda qi, kj: (0, qi)),
                  pl.BlockSpec((tq, ATTN_WIDTH), lambda qi, kj: (qi, P_OFFSETS["z_attn"] // ATTN_WIDTH)),
                  _const_spec((ATTN_HEADS, 1, tq))],
        out_specs=pl.BlockSpec((tq, ATTN_WIDTH), lambda qi, kj: (qi, 0)),
        scratch_shapes=[pltpu.VMEM((ATTN_HEADS, ATTN_HEAD_DIM, tq), F32),
                        pltpu.VMEM((ATTN_HEADS, 1, tq), F32),
                        pltpu.VMEM((ATTN_HEADS, 1, tq), F32),
                        pltpu.VMEM((tk, tq), F32),
                        pltpu.VMEM((tk, tq), F32)],
        compiler_params=_cparams(("arbitrary", "arbitrary"), 56),
        name="attn",
    )(qT, k, vT, keys, thr, proj, slopes)


def _conv_kernel(x_ref, prev_ref, w_ref, b_ref, q_ref, k_ref, *, kscale):
    i = pl.program_id(0)
    x = x_ref[...]
    prev = jnp.where(i > 0, prev_ref[...], 0.0)
    head = jnp.concatenate([prev, x[:8]], axis=0)
    y = b_ref[...]
    yh = b_ref[...]
    for j in range(CONV_WIDTH):
        d = CONV_WIDTH - 1 - j
        xs = x if d == 0 else pltpu.roll(x, d, 0)
        hs = head if d == 0 else pltpu.roll(head, d, 0)
        y = y + xs * w_ref[j:j + 1, :]
        yh = yh + hs[8:] * w_ref[j:j + 1, :]
    y = jnp.concatenate([yh, y[8:]], axis=0)
    y = _silu(y)
    half = y.shape[1] // 2
    q_ref[...] = y[:, :half].astype(BF16)
    k_ref[...] = (y[:, half:] * kscale).astype(BF16)


def _conv(proj, conv_w, conv_b, tm):
    s = proj.shape[0]
    c = 2 * MLSTM_QK_WIDTH
    cb = P_OFFSETS["qk_m"] // c
    return pl.pallas_call(
        functools.partial(_conv_kernel, kscale=MLSTM_QK_DIM ** -0.5),
        out_shape=(jax.ShapeDtypeStruct((s, MLSTM_QK_WIDTH), BF16), jax.ShapeDtypeStruct((s, MLSTM_QK_WIDTH), BF16)),
        grid=(s // tm,),
        in_specs=[pl.BlockSpec((tm, c), lambda i: (i, cb)),
                  pl.BlockSpec((8, c), lambda i: (jnp.maximum(i * (tm // 8) - 1, 0), cb)),
                  _const_spec((CONV_WIDTH, c)),
                  _const_spec((1, c))],
        out_specs=(pl.BlockSpec((tm, MLSTM_QK_WIDTH), lambda i: (i, 0)),
                   pl.BlockSpec((tm, MLSTM_QK_WIDTH), lambda i: (i, 0))),
        compiler_params=_cparams(("arbitrary",), 40),
        name="conv",
    )(proj, proj, conv_w, conv_b)


def _softcap(x):
    return GATE_SOFTCAP * jnp.tanh(x / GATE_SOFTCAP)


def _mlstm_kernel(q_ref, k_ref, v_ref, og_ref, z_ref, sm_ref, g_ref, out_ref, c_ref, n_ref, m_ref, *, chunk):
    hd = pl.program_id(0)
    ci = pl.program_id(1)
    L = chunk

    @pl.when(ci == 0)
    def _init():
        c_ref[...] = jnp.zeros(c_ref.shape, F32)
        n_ref[...] = jnp.zeros(n_ref.shape, F32)
        m_ref[...] = jnp.zeros(m_ref.shape, F32)

    sm = sm_ref[...]
    lane = lax.broadcasted_iota(I32, sm.shape, 1)
    ig_col = _softcap(jnp.sum(jnp.where(lane == SM_I + hd, sm, 0.0), axis=1, keepdims=True))
    fg_col = _softcap(jnp.sum(jnp.where(lane == SM_F + hd, sm, 0.0), axis=1, keepdims=True))
    logf_col = jnp.minimum(fg_col, 0.0) - jnp.log1p(jnp.exp(-jnp.abs(fg_col)))

    r_i = lax.broadcasted_iota(I32, (L, L), 0)
    c_i = lax.broadcasted_iota(I32, (L, L), 1)
    eye = r_i == c_i
    tril = r_i >= c_i
    logf_row = jnp.sum(jnp.where(eye, logf_col, 0.0), axis=0, keepdims=True)
    ig_row = jnp.sum(jnp.where(eye, ig_col, 0.0), axis=0, keepdims=True)
    b_col = jnp.sum(jnp.where(tril, logf_row, 0.0), axis=1, keepdims=True)
    b_row = jnp.sum(jnp.where(r_i <= c_i, logf_col, 0.0), axis=0, keepdims=True)
    dmat = jnp.where(tril, b_col - b_row + ig_row, -jnp.inf)
    m_prev = m_ref[...]
    m_inter = b_col + m_prev
    m_t = jnp.maximum(m_inter, jnp.max(dmat, axis=1, keepdims=True))

    qc = q_ref[...]
    kc = k_ref[...]
    vc = v_ref[...].astype(BF16)
    nt = (((1,), (1,)), ((), ()))
    s = lax.dot_general(qc, kc, nt, preferred_element_type=F32) * jnp.exp(dmat - m_t)
    inter = jnp.exp(m_inter - m_t)
    num = (jnp.dot(s.astype(BF16), vc, preferred_element_type=F32)
           + inter * jnp.dot(qc, c_ref[...].astype(BF16), preferred_element_type=F32))
    qn = jnp.sum(qc.astype(F32) * n_ref[...], axis=1, keepdims=True)
    den = jnp.sum(s, axis=1, keepdims=True) + inter * qn
    hh = num / jnp.maximum(jnp.abs(den), jnp.exp(-m_t))

    g_last = b_col[L - 1:L, :]
    m_new = m_t[L - 1:L, :]
    wgt = jnp.exp(g_last - b_col + ig_col - m_new)
    decay = jnp.exp(g_last + m_prev - m_new)
    wk = wgt * kc.astype(F32)
    tn = (((0,), (0,)), ((), ()))
    c_ref[...] = decay * c_ref[...] + lax.dot_general(wk.astype(BF16), vc, tn, preferred_element_type=F32)
    n_ref[...] = decay * n_ref[...] + jnp.sum(wk, axis=0, keepdims=True)
    m_ref[...] = m_new

    hn = hh * lax.rsqrt(jnp.mean(hh * hh, axis=-1, keepdims=True) + NORM_EPS) * g_ref[0]
    out_ref[...] = (hn * _sigmoid(og_ref[...]) * _silu(z_ref[...])).astype(BF16)


def _mlstm(qm, km, proj, g_mh3, chunk):
    s = qm.shape[0]
    dk, dv = MLSTM_QK_DIM, MLSTM_V_DIM
    vb, ob, zb = (P_OFFSETS[n] // dv for n in ("v_m", "o_m", "z_m"))
    return pl.pallas_call(
        functools.partial(_mlstm_kernel, chunk=chunk),
        out_shape=jax.ShapeDtypeStruct((s, MLSTM_WIDTH), BF16),
        grid=(MLSTM_HEADS, s // chunk),
        in_specs=[pl.BlockSpec((chunk, dk), lambda h, c: (c, h)),
                  pl.BlockSpec((chunk, dk), lambda h, c: (c, h)),
                  pl.BlockSpec((chunk, dv), lambda h, c: (c, vb + h)),
                  pl.BlockSpec((chunk, dv), lambda h, c: (c, ob + h)),
                  pl.BlockSpec((chunk, dv), lambda h, c: (c, zb + h)),
                  pl.BlockSpec((chunk, SMALL_W), lambda h, c: (c, SMALL_OFF // SMALL_W)),
                  pl.BlockSpec((1, 1, dv), lambda h, c: (h, 0, 0))],
        out_specs=pl.BlockSpec((chunk, dv), lambda h, c: (c, h)),
        scratch_shapes=[pltpu.VMEM((dk, dv), F32), pltpu.VMEM((1, dk), F32), pltpu.VMEM((1, 1), F32)],
        compiler_params=_cparams(("arbitrary", "arbitrary"), 32),
        name="mlstm",
    )(qm, km, proj, proj, proj, proj, g_mh3)


def _merge_kernel(a1_ref, a2_ref, w1_ref, w2_ref, ga_ref, gm_ref, o_ref):
    y1 = jnp.dot(a1_ref[...], w1_ref[...], preferred_element_type=F32)
    y2 = jnp.dot(a2_ref[...], w2_ref[...], preferred_element_type=F32)
    o_ref[...] = (_sigmoid(ga_ref[...]) * y1 + _sigmoid(gm_ref[...]) * y2).astype(BF16)


def _merge(a1, a2, w1, w2, proj):
    s, d = a1.shape
    tm = min(512, s)
    tn = 512
    gab, gmb = P_OFFSETS["g_attn"] // tn, P_OFFSETS["g_mlstm"] // tn
    return pl.pallas_call(
        _merge_kernel,
        out_shape=jax.ShapeDtypeStruct((s, D_MODEL), BF16),
        grid=(s // tm, D_MODEL // tn),
        in_specs=[pl.BlockSpec((tm, d), lambda i, j: (i, 0)),
                  pl.BlockSpec((tm, d), lambda i, j: (i, 0)),
                  pl.BlockSpec((d, tn), lambda i, j: (0, j)),
                  pl.BlockSpec((d, tn), lambda i, j: (0, j)),
                  pl.BlockSpec((tm, tn), lambda i, j: (i, gab + j)),
                  pl.BlockSpec((tm, tn), lambda i, j: (i, gmb + j))],
        out_specs=pl.BlockSpec((tm, tn), lambda i, j: (i, j)),
        compiler_params=_cparams(("arbitrary", "arbitrary"), 48),
        name="merge",
    )(a1, a2, w1, w2, proj, proj)


def _final_kernel(mg_ref, w_ref, x_ref, gate_ref, lg_ref, lb_ref, o_ref, buf_ref, *, tn, nn):
    j = pl.program_id(1)
    buf_ref[j] = jnp.dot(mg_ref[...], w_ref[...], preferred_element_type=F32)

    @pl.when(j == nn - 1)
    def _norm():
        d = nn * tn
        ssum = 0.0
        for jj in range(nn):
            cols = slice(jj * tn, (jj + 1) * tn)
            r = DEEPNORM_ALPHA * x_ref[:, cols] + gate_ref[:, cols] * buf_ref[jj]
            buf_ref[jj] = r
            ssum = ssum + jnp.sum(r, axis=-1, keepdims=True)
        mu = ssum / d
        vsum = 0.0
        for jj in range(nn):
            vsum = vsum + jnp.sum(jnp.square(buf_ref[jj] - mu), axis=-1, keepdims=True)
        inv = lax.rsqrt(vsum / d + NORM_EPS)
        for jj in range(nn):
            cols = slice(jj * tn, (jj + 1) * tn)
            o_ref[:, cols] = (buf_ref[jj] - mu) * inv * lg_ref[:, cols] + lb_ref[:, cols]


def _final(merged, w_out, x2, mod, ln_g, ln_b):
    s, d = x2.shape
    tm = min(256, s)
    tn = 512
    nn = d // tn
    return pl.pallas_call(
        functools.partial(_final_kernel, tn=tn, nn=nn),
        out_shape=jax.ShapeDtypeStruct((s, d), F32),
        grid=(s // tm, nn),
        in_specs=[pl.BlockSpec((tm, d), lambda i, j: (i, 0)),
                  pl.BlockSpec((d, tn), lambda i, j: (0, j)),
                  pl.BlockSpec((tm, d), lambda i, j: (i, 0)),
                  pl.BlockSpec((1, d), lambda i, j: (0, 2)),
                  pl.BlockSpec((1, d), lambda i, j: (0, 0)),
                  pl.BlockSpec((1, d), lambda i, j: (0, 0))],
        out_specs=pl.BlockSpec((tm, d), lambda i, j: (i, 0)),
        scratch_shapes=[pltpu.VMEM((nn, tm, tn), F32)],
        compiler_params=_cparams(("arbitrary", "arbitrary"), 48),
        name="final",
    )(merged, w_out, x2, mod, ln_g, ln_b)


def _regroup_cols(a, pad_to):
    parts = [a[..., IN_OFFSETS[n]:IN_OFFSETS[n] + IN_WIDTH_OF[n]] for n in P_ORDER]
    parts.append(jnp.zeros(a.shape[:-1] + (pad_to - P_USED,), a.dtype))
    return jnp.concatenate(parts, axis=-1)


def _layer(x2, c, w_ada, b_ada, w_in, b_in, g_q, g_kv, w_uq, w_iq, w_uk, w_uv, g_kidx, b_kidx, conv_w, conv_b, g_mh,
           w_attn_out, w_mlstm_out, w_out, ln_g, ln_b):
    s, d = x2.shape
    assert d == D_MODEL and s % 1024 == 0, (s, d)
    tq, tk = 256, 512
    nsel = min(TOPK_MAX, s // 4)

    w_cat = _regroup_cols(w_in, P_TOTAL).astype(BF16)
    b_cat = _regroup_cols(b_in, P_TOTAL).reshape(1, P_TOTAL)
    w_uqT = w_uq.T.astype(BF16)
    w_iqT = w_iq.T.astype(BF16)
    w_ukT = w_uk.reshape(ATTN_WIDTH, KV_LORA_RANK).T.astype(BF16)
    w_uvT = w_uv.transpose(0, 2, 1).reshape(ATTN_WIDTH, KV_LORA_RANK).astype(BF16)
    slopes = jnp.exp2(-8.0 * jnp.arange(1, ATTN_HEADS + 1, dtype=F32) / ATTN_HEADS)
    slopes = jnp.broadcast_to(slopes[:, None, None], (ATTN_HEADS, 1, tq))

    mod = _ada(c.reshape(d, 1), w_ada, b_ada.reshape(1, -1))
    u = _modulate(x2, mod)
    proj = _proj(u, w_cat, b_cat)

    qT, qiT = _qpath(proj, g_q.reshape(1, -1), w_uqT, w_iqT, tq)
    k, vT, kidx, widx = _kvpath(proj, g_kv.reshape(1, -1), g_kidx.reshape(1, -1), b_kidx.reshape(1, -1), w_ukT, w_uvT, tq)
    wT = widx.T.reshape(IDX_HEADS, 1, s)
    keys, thr = _indexer(kidx, qiT, wT, tq, nsel)
    a_attn = _attention(qT, k, vT, keys, thr, proj, slopes, tq, tk)

    qm, km = _conv(proj, conv_w, conv_b.reshape(1, -1), tq)
    a_mlstm = _mlstm(qm, km, proj, g_mh.reshape(MLSTM_HEADS, 1, MLSTM_V_DIM), MLSTM_CHUNK)

    merged = _merge(a_attn, a_mlstm, w_attn_out.astype(BF16), w_mlstm_out.astype(BF16), proj)
    return _final(merged, w_out.astype(BF16), x2, mod, ln_g.reshape(1, -1), ln_b.reshape(1, -1))


def kernel(x, c, w_ada, b_ada, w_in, b_in, g_q, g_kv, w_uq, w_iq, w_uk, w_uv, g_kidx, b_kidx, conv_w, conv_b, g_mh,
           w_attn_out, w_mlstm_out, w_out, ln_g, ln_b):
    bsz, seq, d = x.shape
    assert bsz == 1 and w_ada.shape[0] == 1, "single batch, single layer"
    out = _layer(x.reshape(seq, d), c, w_ada[0], b_ada[0], w_in[0], b_in[0], g_q[0], g_kv[0], w_uq[0], w_iq[0],
                 w_uk[0], w_uv[0], g_kidx[0], b_kidx[0], conv_w[0], conv_b[0], g_mh[0], w_attn_out[0],
                 w_mlstm_out[0], w_out[0], ln_g[0], ln_b[0])
    return out.reshape(bsz, seq, d)
```

```python
import functools

import jax
import jax.numpy as jnp
from jax import lax
from jax.experimental import pallas as pl
from jax.experimental.pallas import tpu as pltpu

F32 = jnp.float32
BF16 = jnp.bfloat16
I32 = jnp.int32

D_MODEL = 4096
ATTN_HEADS = 32
ATTN_HEAD_DIM = 128
ATTN_WIDTH = ATTN_HEADS * ATTN_HEAD_DIM
Q_LORA_RANK = 1024
KV_LORA_RANK = 512
IDX_HEADS = 32
IDX_HEAD_DIM = 64
TOPK_MAX = 256
MLSTM_HEADS = 8
MLSTM_QK_DIM = (D_MODEL // 2) // MLSTM_HEADS
MLSTM_V_DIM = D_MODEL // MLSTM_HEADS
MLSTM_QK_WIDTH = MLSTM_HEADS * MLSTM_QK_DIM
MLSTM_WIDTH = MLSTM_HEADS * MLSTM_V_DIM
MLSTM_CHUNK = 64
CONV_WIDTH = 4
GATE_SOFTCAP = 15.0
DEEPNORM_ALPHA = 2.0 ** 0.25
NORM_EPS = 1e-6

IN_WIDTHS = (Q_LORA_RANK, KV_LORA_RANK, IDX_HEAD_DIM, IDX_HEADS, ATTN_WIDTH, 2 * MLSTM_QK_WIDTH, MLSTM_WIDTH,
             MLSTM_WIDTH, MLSTM_HEADS, MLSTM_HEADS, MLSTM_WIDTH, D_MODEL, D_MODEL)
IN_NAMES = ("q_lat", "kv_lat", "k_idx", "w_idx", "z_attn", "qk_m", "v_m", "o_m", "i_m", "f_m", "z_m", "g_attn", "g_mlstm")
IN_OFFSETS = {n: sum(IN_WIDTHS[:i]) for i, n in enumerate(IN_NAMES)}
IN_WIDTH_OF = dict(zip(IN_NAMES, IN_WIDTHS))

P_ORDER = ("z_attn", "qk_m", "v_m", "o_m", "z_m", "g_attn", "g_mlstm", "q_lat", "kv_lat", "k_idx", "w_idx", "i_m", "f_m")
P_OFFSETS = {}
_off = 0
for _n in P_ORDER:
    P_OFFSETS[_n] = _off
    _off += IN_WIDTH_OF[_n]
P_USED = _off
PROJ_TN = 1024
P_TOTAL = -(-P_USED // PROJ_TN) * PROJ_TN
SMALL_W = 128
SMALL_OFF = P_OFFSETS["k_idx"]
SM_WIDX = IDX_HEAD_DIM
SM_I = SM_WIDX + IDX_HEADS
SM_F = SM_I + MLSTM_HEADS

VMEM_CAP_BYTES = 60 * 1024 * 1024

MASK_DIST = 1e30
M_INIT = -1e20
LOG2E = 1.4426950408889634
ATTN_ROWS = 128
HEAD_GROUP = 8
INT_MIN = -2 ** 31
KEY_NEG_INF = INT_MIN + 0x7FFFFF


def _cparams(sem, vmem_mb):
    return pltpu.CompilerParams(dimension_semantics=sem, vmem_limit_bytes=min(vmem_mb * 1024 * 1024, VMEM_CAP_BYTES))


def _sigmoid(x):
    return jax.nn.sigmoid(x)


def _silu(x):
    return x * jax.nn.sigmoid(x)


def _const_spec(shape):
    nd = len(shape)
    return pl.BlockSpec(shape, lambda *_: (0,) * nd, pipeline_mode=pl.Buffered(1))


def _ada_kernel(c_ref, w_ref, b_ref, o_ref):
    c = c_ref[...]
    o_ref[...] = jnp.sum(w_ref[...] * _silu(c), axis=0, keepdims=True) + b_ref[...]


def _ada(c_col, w_ada, b_ada):
    d, n = w_ada.shape
    tn = 512
    return pl.pallas_call(
        _ada_kernel,
        out_shape=jax.ShapeDtypeStruct((1, n), F32),
        grid=(n // tn,),
        in_specs=[pl.BlockSpec((d, 1), lambda j: (0, 0)),
                  pl.BlockSpec((d, tn), lambda j: (0, j)),
                  pl.BlockSpec((1, tn), lambda j: (0, j))],
        out_specs=pl.BlockSpec((1, tn), lambda j: (0, j)),
        compiler_params=_cparams(("arbitrary",), 32),
        name="ada",
    )(c_col, w_ada, b_ada)


def _modulate_kernel(x_ref, shift_ref, scale_ref, u_ref):
    u_ref[...] = (x_ref[...] * (1.0 + scale_ref[...]) + shift_ref[...]).astype(BF16)


def _modulate(x2, mod):
    s, d = x2.shape
    tm = min(512, s)
    return pl.pallas_call(
        _modulate_kernel,
        out_shape=jax.ShapeDtypeStruct((s, d), BF16),
        grid=(s // tm,),
        in_specs=[pl.BlockSpec((tm, d), lambda i: (i, 0)),
                  pl.BlockSpec((1, d), lambda i: (0, 0)),
                  pl.BlockSpec((1, d), lambda i: (0, 1))],
        out_specs=pl.BlockSpec((tm, d), lambda i: (i, 0)),
        compiler_params=_cparams(("arbitrary",), 40),
        name="modulate",
    )(x2, mod, mod)


def _proj_kernel(u_ref, w_ref, b_ref, o_ref):
    o_ref[...] = jnp.dot(u_ref[...], w_ref[...], preferred_element_type=F32) + b_ref[...]


def _proj(u, w_cat, b_cat):
    s, d = u.shape
    n = w_cat.shape[1]
    tm = min(1024, s)
    tn = PROJ_TN
    return pl.pallas_call(
        _proj_kernel,
        out_shape=jax.ShapeDtypeStruct((s, n), F32),
        grid=(n // tn, s // tm),
        in_specs=[pl.BlockSpec((tm, d), lambda j, i: (i, 0)),
                  pl.BlockSpec((d, tn), lambda j, i: (0, j)),
                  pl.BlockSpec((1, tn), lambda j, i: (0, j))],
        out_specs=pl.BlockSpec((tm, tn), lambda j, i: (i, j)),
        compiler_params=_cparams(("arbitrary", "arbitrary"), 56),
        name="proj",
    )(u, w_cat, b_cat)


def _qpath_kernel(ql_ref, g_ref, wuq_ref, wiq_ref, qT_ref, qiT_ref, *, scale):
    x = ql_ref[...]
    cq = (x * lax.rsqrt(jnp.mean(x * x, axis=-1, keepdims=True) + NORM_EPS) * g_ref[...]).astype(BF16)
    nt = (((1,), (1,)), ((), ()))
    qT = lax.dot_general(wuq_ref[...], cq, nt, preferred_element_type=F32)
    qT_ref[...] = (qT * scale).reshape(qT_ref.shape).astype(BF16)
    qiT = lax.dot_general(wiq_ref[...], cq, nt, preferred_element_type=F32)
    qiT_ref[...] = qiT.reshape(qiT_ref.shape).astype(BF16)


def _qpath(proj, g_q, w_uqT, w_iqT, tq):
    s = proj.shape[0]
    r = Q_LORA_RANK
    return pl.pallas_call(
        functools.partial(_qpath_kernel, scale=ATTN_HEAD_DIM ** -0.5 * LOG2E),
        out_shape=(jax.ShapeDtypeStruct((ATTN_HEADS, ATTN_HEAD_DIM, s), BF16),
                   jax.ShapeDtypeStruct((IDX_HEADS, IDX_HEAD_DIM, s), BF16)),
        grid=(s // tq,),
        in_specs=[pl.BlockSpec((tq, r), lambda i: (i, P_OFFSETS["q_lat"] // r)),
                  _const_spec((1, r)),
                  _const_spec(w_uqT.shape),
                  _const_spec(w_iqT.shape)],
        out_specs=(pl.BlockSpec((ATTN_HEADS, ATTN_HEAD_DIM, tq), lambda i: (0, 0, i)),
                   pl.BlockSpec((IDX_HEADS, IDX_HEAD_DIM, tq), lambda i: (0, 0, i))),
        compiler_params=_cparams(("arbitrary",), 48),
        name="qpath",
    )(proj, g_q, w_uqT, w_iqT)


def _kvpath_kernel(kvl_ref, sm_ref, gkv_ref, gk_ref, bk_ref, wuk_ref, wuv_ref, k_ref, vT_ref, kidx_ref, widx_ref, *, wscale):
    x = kvl_ref[...]
    ckv = (x * lax.rsqrt(jnp.mean(x * x, axis=-1, keepdims=True) + NORM_EPS) * gkv_ref[...]).astype(BF16)
    kfull = jnp.dot(ckv, wuk_ref[...], preferred_element_type=F32)
    for h in range(ATTN_HEADS):
        k_ref[h] = kfull[:, h * ATTN_HEAD_DIM:(h + 1) * ATTN_HEAD_DIM].astype(BF16)
    nt = (((1,), (1,)), ((), ()))
    vT = lax.dot_general(wuv_ref[...], ckv, nt, preferred_element_type=F32)
    vT_ref[...] = vT.reshape(vT_ref.shape).astype(BF16)
    sm = sm_ref[...]
    ki = sm[:, :IDX_HEAD_DIM]
    mu = jnp.mean(ki, axis=-1, keepdims=True)
    var = jnp.mean(jnp.square(ki - mu), axis=-1, keepdims=True)
    kidx_ref[...] = ((ki - mu) * lax.rsqrt(var + NORM_EPS) * gk_ref[...] + bk_ref[...]).astype(BF16)
    widx_ref[...] = sm[:, SM_WIDX:SM_WIDX + IDX_HEADS] * wscale


def _kvpath(proj, g_kv, g_kidx, b_kidx, w_ukT, w_uvT, tm):
    s = proj.shape[0]
    r = KV_LORA_RANK
    return pl.pallas_call(
        functools.partial(_kvpath_kernel, wscale=IDX_HEADS ** -0.5 * IDX_HEAD_DIM ** -0.5),
        out_shape=(jax.ShapeDtypeStruct((ATTN_HEADS, s, ATTN_HEAD_DIM), BF16),
                   jax.ShapeDtypeStruct((ATTN_HEADS, ATTN_HEAD_DIM, s), BF16),
                   jax.ShapeDtypeStruct((s, IDX_HEAD_DIM), BF16),
                   jax.ShapeDtypeStruct((s, IDX_HEADS), F32)),
        grid=(s // tm,),
        in_specs=[pl.BlockSpec((tm, r), lambda i: (i, P_OFFSETS["kv_lat"] // r)),
                  pl.BlockSpec((tm, SMALL_W), lambda i: (i, SMALL_OFF // SMALL_W)),
                  _const_spec((1, r)),
                  _const_spec((1, IDX_HEAD_DIM)),
                  _const_spec((1, IDX_HEAD_DIM)),
                  _const_spec(w_ukT.shape),
                  _const_spec(w_uvT.shape)],
        out_specs=(pl.BlockSpec((ATTN_HEADS, tm, ATTN_HEAD_DIM), lambda i: (0, i, 0)),
                   pl.BlockSpec((ATTN_HEADS, ATTN_HEAD_DIM, tm), lambda i: (0, 0, i)),
                   pl.BlockSpec((tm, IDX_HEAD_DIM), lambda i: (i, 0)),
                   pl.BlockSpec((tm, IDX_HEADS), lambda i: (i, 0))),
        compiler_params=_cparams(("arbitrary",), 48),
        name="kvpath",
    )(proj, proj, g_kv, g_kidx, b_kidx, w_ukT, w_uvT)


def _key_to_float(key):
    bits = jnp.where(key >= 0, key, key ^ 0x7FFFFFFF)
    return jnp.where(key < KEY_NEG_INF, -jnp.inf, pltpu.bitcast(bits, F32))


def _indexer_kernel(kidx_ref, qiT_ref, wT_ref, sc_ref, thr_ref, *, seq, tq, nsel):
    i = pl.program_id(0)
    ch = 128
    cb = tq
    n_score = (i + 1) * (tq // ch)
    n_count = i + 1
    tpos = i * tq + lax.broadcasted_iota(I32, (ch, tq), 1)

    def score_chunk(c, carry):
        r0 = pl.multiple_of(c * ch, ch)
        kc = kidx_ref[pl.ds(r0, ch), :]

        acc = jnp.zeros((ch, tq), F32)
        for h in range(IDX_HEADS):
            r = jnp.dot(kc, qiT_ref[h], preferred_element_type=F32)
            acc = acc + jnp.maximum(r, 0.0) * wT_ref[h]
        spos = r0 + lax.broadcasted_iota(I32, (ch, tq), 0)
        sc_ref[pl.ds(r0, ch), :] = jnp.where(spos <= tpos, acc, -jnp.inf)
        return carry
    lax.fori_loop(0, n_score, score_chunk, 0)

    def fill_chunk(c, carry):
        sc_ref[pl.ds(pl.multiple_of(c * cb, cb), cb), :] = jnp.full((cb, tq), -jnp.inf, F32)
        return carry
    lax.fori_loop(n_count, seq // cb, fill_chunk, 0)

    def count(pred):
        def body(c, part):
            r0 = pl.multiple_of(c * cb, cb)
            m = jnp.where(pred(sc_ref[pl.ds(r0, cb), :], r0), 1, 0)
            return part + jnp.sum(m.reshape(cb // 8, 8, tq), axis=0)
        part = lax.fori_loop(0, n_count, body, jnp.zeros((8, tq), I32))
        return jnp.sum(part, axis=0, keepdims=True)

    def count_ge(cand_key):
        cand = _key_to_float(cand_key)
        return count(lambda blk, r0: blk >= cand)

    t0 = jnp.where(count_ge(jnp.zeros((1, tq), I32)) >= nsel, 0, INT_MIN).astype(I32)

    def bit_step(b, t):
        cand = t + jnp.left_shift(jnp.int32(1), 30 - b)
        return jnp.where(count_ge(cand) >= nsel, cand, t)
    thr = _key_to_float(lax.fori_loop(0, 31, bit_step, t0))
    thr_ref[...] = thr

    tie = (count(lambda blk, r0: blk >= thr) > nsel) & (thr > -jnp.inf)

    @pl.when(jnp.max(tie.astype(I32)) > 0)
    def _break_ties():
        need = nsel - count(lambda blk, r0: blk > thr)

        def eq_below(j):
            return count(lambda blk, r0: (blk == thr) & (r0 + lax.broadcasted_iota(I32, (cb, tq), 0) < j))

        def jbit(b, j):
            test = j + jnp.left_shift(jnp.int32(1), (seq.bit_length() - 2) - b)
            return jnp.where(eq_below(test) < need, test, j)
        jlast = lax.fori_loop(0, seq.bit_length() - 1, jbit, jnp.zeros((1, tq), I32))

        def demote(c, carry):
            r0 = pl.multiple_of(c * cb, cb)
            blk = sc_ref[pl.ds(r0, cb), :]
            row = r0 + lax.broadcasted_iota(I32, (cb, tq), 0)
            sc_ref[pl.ds(r0, cb), :] = jnp.where(tie & (blk == thr) & (row > jlast), -jnp.inf, blk)
            return carry
        lax.fori_loop(0, n_count, demote, 0)


def _indexer(kidx, qiT, wT, tq, nsel):
    s = kidx.shape[0]
    return pl.pallas_call(
        functools.partial(_indexer_kernel, seq=s, tq=tq, nsel=nsel),
        out_shape=(jax.ShapeDtypeStruct((s, s), F32), jax.ShapeDtypeStruct((1, s), F32)),
        grid=(s // tq,),
        in_specs=[_const_spec((s, IDX_HEAD_DIM)),
                  pl.BlockSpec((IDX_HEADS, IDX_HEAD_DIM, tq), lambda i: (0, 0, i)),
                  pl.BlockSpec((IDX_HEADS, 1, tq), lambda i: (0, 0, i))],
        out_specs=(pl.BlockSpec((s, tq), lambda i: (0, i)),
                   pl.BlockSpec((1, tq), lambda i: (0, i))),
        compiler_params=_cparams(("arbitrary",), 40),
        name="indexer",
    )(kidx, qiT, wT)


def _attn_kernel(qT_ref, k_ref, vT_ref, keys_ref, thr_ref, z_ref, sl_ref, o_ref,
                 acc_ref, m_ref, l_ref, dm_ref, lg_ref, p_ref, *, tq, tk, nk):
    qi = pl.program_id(0)
    kj = pl.program_id(1)
    last = (qi * tq + tq - 1) // tk

    @pl.when(kj == 0)
    def _init():
        acc_ref[...] = jnp.zeros(acc_ref.shape, F32)
        m_ref[...] = jnp.full(m_ref.shape, M_INIT, F32)
        l_ref[...] = jnp.zeros(l_ref.shape, F32)

    @pl.when(kj <= last)
    def _compute():
        spos = kj * tk + lax.broadcasted_iota(I32, (tk, tq), 0)
        tpos = qi * tq + lax.broadcasted_iota(I32, (tk, tq), 1)
        sel = (keys_ref[...] >= thr_ref[...]) & (spos <= tpos)
        dm_ref[...] = jnp.where(sel, (tpos - spos).astype(F32), MASK_DIST)

        def group(g, carry):
            def logits(u):
                h = g * HEAD_GROUP + u
                qh = qT_ref[h]
                slope = sl_ref[h]
                part = jnp.full((8, tq), M_INIT, F32)
                for c in range(tk // ATTN_ROWS):
                    rows = pl.ds(c * ATTN_ROWS, ATTN_ROWS)
                    s = jnp.dot(k_ref[h, rows, :], qh, preferred_element_type=F32)
                    lg = s - slope * dm_ref[rows, :]
                    lg_ref[u % 2, rows, :] = lg
                    part = jnp.maximum(part, jnp.max(lg.reshape(ATTN_ROWS // 8, 8, tq), axis=0))
                m_old = m_ref[g, u]
                return m_old, jnp.maximum(m_old, jnp.max(part, axis=0, keepdims=True))

            def probs(u, m_old, m_new):
                part = jnp.zeros((8, tq), F32)
                for c in range(tk // ATTN_ROWS):
                    rows = pl.ds(c * ATTN_ROWS, ATTN_ROWS)
                    p = jnp.exp2(lg_ref[u % 2, rows, :] - m_new)
                    part = part + jnp.sum(p.reshape(ATTN_ROWS // 8, 8, tq), axis=0)
                    p_ref[u % 2, rows, :] = p.astype(BF16)
                alpha = jnp.exp2(m_old - m_new)
                l_ref[g, u] = alpha * l_ref[g, u] + jnp.sum(part, axis=0, keepdims=True)
                m_ref[g, u] = m_new
                return alpha

            def values(u, alpha):
                h = g * HEAD_GROUP + u
                acc_ref[g, u] = alpha * acc_ref[g, u] + jnp.dot(vT_ref[h], p_ref[u % 2], preferred_element_type=F32)

            stats = logits(0)
            alpha_prev = None
            for u in range(HEAD_GROUP):
                stats_next = logits(u + 1) if u + 1 < HEAD_GROUP else None
                alpha = probs(u, *stats)
                if u >= 1:
                    values(u - 1, alpha_prev)
                stats, alpha_prev = stats_next, alpha
            values(HEAD_GROUP - 1, alpha_prev)
            return carry
        lax.fori_loop(0, ATTN_HEADS // HEAD_GROUP, group, 0)

    @pl.when(kj == nk - 1)
    def _finish():
        for h in range(ATTN_HEADS):
            g, u = divmod(h, HEAD_GROUP)
            cols = slice(h * ATTN_HEAD_DIM, (h + 1) * ATTN_HEAD_DIM)
            o = (acc_ref[g, u] * (1.0 / l_ref[g, u])).T
            o_ref[:, cols] = (o * _silu(z_ref[:, cols])).astype(BF16)


def _attention(qT, k, vT, keys, thr, proj, slopes, tq, tk):
    s = k.shape[1]
    nk = s // tk
    ng = ATTN_HEADS // HEAD_GROUP

    def kv_blk(qi, kj):
        return jnp.minimum(kj, (qi * tq + tq - 1) // tk)
    return pl.pallas_call(
        functools.partial(_attn_kernel, tq=tq, tk=tk, nk=nk),
        out_shape=jax.ShapeDtypeStruct((s, ATTN_WIDTH), BF16),
        grid=(s // tq, nk),
        in_specs=[pl.BlockSpec((ATTN_HEADS, ATTN_HEAD_DIM, tq), lambda qi, kj: (0, 0, qi)),
                  pl.BlockSpec((ATTN_HEADS, tk, ATTN_HEAD_DIM), lambda qi, kj: (0, kv_blk(qi, kj), 0)),
                  pl.BlockSpec((ATTN_HEADS, ATTN_HEAD_DIM, tk), lambda qi, kj: (0, 0, kv_blk(qi, kj))),
                  pl.BlockSpec((tk, tq), lambda qi, kj: (kv_blk(qi, kj), qi)),
                  pl.BlockSpec((1, tq), lambda qi, kj: (0, qi)),
                  pl.BlockSpec((tq, ATTN_WIDTH), lambda qi, kj: (qi, P_OFFSETS["z_attn"] // ATTN_WIDTH)),
                  _const_spec((ATTN_HEADS, 1, tq))],
        out_specs=pl.BlockSpec((tq, ATTN_WIDTH), lambda qi, kj: (qi, 0)),
        scratch_shapes=[pltpu.VMEM((ng, HEAD_GROUP, ATTN_HEAD_DIM, tq), F32),
                        pltpu.VMEM((ng, HEAD_GROUP, 1, tq), F32),
                        pltpu.VMEM((ng, HEAD_GROUP, 1, tq), F32),
                        pltpu.VMEM((tk, tq), F32),
                        pltpu.VMEM((2, tk, tq), F32),
                        pltpu.VMEM((2, tk, tq), BF16)],
        compiler_params=_cparams(("arbitrary", "arbitrary"), 56),
        name="attn",
    )(qT, k, vT, keys, thr, proj, slopes)


def _conv_kernel(x_ref, prev_ref, w_ref, b_ref, q_ref, k_ref, *, kscale):
    i = pl.program_id(0)
    x = x_ref[...]
    prev = jnp.where(i > 0, prev_ref[...], 0.0)
    head = jnp.concatenate([prev, x[:8]], axis=0)
    y = b_ref[...]
    yh = b_ref[...]
    for j in range(CONV_WIDTH):
        d = CONV_WIDTH - 1 - j
        xs = x if d == 0 else pltpu.roll(x, d, 0)
        hs = head if d == 0 else pltpu.roll(head, d, 0)
        y = y + xs * w_ref[j:j + 1, :]
        yh = yh + hs[8:] * w_ref[j:j + 1, :]
    y = jnp.concatenate([yh, y[8:]], axis=0)
    y = _silu(y)
    half = y.shape[1] // 2
    q_ref[...] = y[:, :half].astype(BF16)
    k_ref[...] = (y[:, half:] * kscale).astype(BF16)


def _conv(proj, conv_w, conv_b, tm):
    s = proj.shape[0]
    c = 2 * MLSTM_QK_WIDTH
    cb = P_OFFSETS["qk_m"] // c
    return pl.pallas_call(
        functools.partial(_conv_kernel, kscale=MLSTM_QK_DIM ** -0.5),
        out_shape=(jax.ShapeDtypeStruct((s, MLSTM_QK_WIDTH), BF16), jax.ShapeDtypeStruct((s, MLSTM_QK_WIDTH), BF16)),
        grid=(s // tm,),
        in_specs=[pl.BlockSpec((tm, c), lambda i: (i, cb)),
                  pl.BlockSpec((8, c), lambda i: (jnp.maximum(i * (tm // 8) - 1, 0), cb)),
                  _const_spec((CONV_WIDTH, c)),
                  _const_spec((1, c))],
        out_specs=(pl.BlockSpec((tm, MLSTM_QK_WIDTH), lambda i: (i, 0)),
                   pl.BlockSpec((tm, MLSTM_QK_WIDTH), lambda i: (i, 0))),
        compiler_params=_cparams(("arbitrary",), 40),
        name="conv",
    )(proj, proj, conv_w, conv_b)


def _softcap(x):
    return GATE_SOFTCAP * jnp.tanh(x / GATE_SOFTCAP)


def _mlstm_kernel(q_ref, k_ref, v_ref, og_ref, z_ref, sm_ref, g_ref, out_ref, c_ref, n_ref, m_ref, *, chunk):
    hd = pl.program_id(0)
    ci = pl.program_id(1)
    L = chunk

    @pl.when(ci == 0)
    def _init():
        c_ref[...] = jnp.zeros(c_ref.shape, F32)
        n_ref[...] = jnp.zeros(n_ref.shape, F32)
        m_ref[...] = jnp.zeros(m_ref.shape, F32)

    sm = sm_ref[...]
    lane = lax.broadcasted_iota(I32, sm.shape, 1)
    ig_col = _softcap(jnp.sum(jnp.where(lane == SM_I + hd, sm, 0.0), axis=1, keepdims=True))
    fg_col = _softcap(jnp.sum(jnp.where(lane == SM_F + hd, sm, 0.0), axis=1, keepdims=True))
    logf_col = jnp.minimum(fg_col, 0.0) - jnp.log1p(jnp.exp(-jnp.abs(fg_col)))

    r_i = lax.broadcasted_iota(I32, (L, L), 0)
    c_i = lax.broadcasted_iota(I32, (L, L), 1)
    eye = r_i == c_i
    tril = r_i >= c_i
    logf_row = jnp.sum(jnp.where(eye, logf_col, 0.0), axis=0, keepdims=True)
    ig_row = jnp.sum(jnp.where(eye, ig_col, 0.0), axis=0, keepdims=True)
    b_col = jnp.sum(jnp.where(tril, logf_row, 0.0), axis=1, keepdims=True)
    b_row = jnp.sum(jnp.where(r_i <= c_i, logf_col, 0.0), axis=0, keepdims=True)
    dmat = jnp.where(tril, b_col - b_row + ig_row, -jnp.inf)
    m_prev = m_ref[...]
    m_inter = b_col + m_prev
    m_t = jnp.maximum(m_inter, jnp.max(dmat, axis=1, keepdims=True))

    qc = q_ref[...]
    kc = k_ref[...]
    vc = v_ref[...].astype(BF16)
    nt = (((1,), (1,)), ((), ()))
    s = lax.dot_general(qc, kc, nt, preferred_element_type=F32) * jnp.exp(dmat - m_t)
    inter = jnp.exp(m_inter - m_t)
    num = (jnp.dot(s.astype(BF16), vc, preferred_element_type=F32)
           + inter * jnp.dot(qc, c_ref[...].astype(BF16), preferred_element_type=F32))
    qn = jnp.sum(qc.astype(F32) * n_ref[...], axis=1, keepdims=True)
    den = jnp.sum(s, axis=1, keepdims=True) + inter * qn
    hh = num / jnp.maximum(jnp.abs(den), jnp.exp(-m_t))

    g_last = b_col[L - 1:L, :]
    m_new = m_t[L - 1:L, :]
    wgt = jnp.exp(g_last - b_col + ig_col - m_new)
    decay = jnp.exp(g_last + m_prev - m_new)
    wk = wgt * kc.astype(F32)
    tn = (((0,), (0,)), ((), ()))
    c_ref[...] = decay * c_ref[...] + lax.dot_general(wk.astype(BF16), vc, tn, preferred_element_type=F32)
    n_ref[...] = decay * n_ref[...] + jnp.sum(wk, axis=0, keepdims=True)
    m_ref[...] = m_new

    hn = hh * lax.rsqrt(jnp.mean(hh * hh, axis=-1, keepdims=True) + NORM_EPS) * g_ref[0]
    out_ref[...] = (hn * _sigmoid(og_ref[...]) * _silu(z_ref[...])).astype(BF16)


def _mlstm(qm, km, proj, g_mh3, chunk):
    s = qm.shape[0]
    dk, dv = MLSTM_QK_DIM, MLSTM_V_DIM
    vb, ob, zb = (P_OFFSETS[n] // dv for n in ("v_m", "o_m", "z_m"))
    return pl.pallas_call(
        functools.partial(_mlstm_kernel, chunk=chunk),
        out_shape=jax.ShapeDtypeStruct((s, MLSTM_WIDTH), BF16),
        grid=(MLSTM_HEADS, s // chunk),
        in_specs=[pl.BlockSpec((chunk, dk), lambda h, c: (c, h)),
                  pl.BlockSpec((chunk, dk), lambda h, c: (c, h)),
                  pl.BlockSpec((chunk, dv), lambda h, c: (c, vb + h)),
                  pl.BlockSpec((chunk, dv), lambda h, c: (c, ob + h)),
                  pl.BlockSpec((chunk, dv), lambda h, c: (c, zb + h)),
                  pl.BlockSpec((chunk, SMALL_W), lambda h, c: (c, SMALL_OFF // SMALL_W)),
                  pl.BlockSpec((1, 1, dv), lambda h, c: (h, 0, 0))],
        out_specs=pl.BlockSpec((chunk, dv), lambda h, c: (c, h)),
        scratch_shapes=[pltpu.VMEM((dk, dv), F32), pltpu.VMEM((1, dk), F32), pltpu.VMEM((1, 1), F32)],
        compiler_params=_cparams(("arbitrary", "arbitrary"), 32),
        name="mlstm",
    )(qm, km, proj, proj, proj, proj, g_mh3)


def _merge_kernel(a1_ref, a2_ref, w1_ref, w2_ref, ga_ref, gm_ref, o_ref):
    y1 = jnp.dot(a1_ref[...], w1_ref[...], preferred_element_type=F32)
    y2 = jnp.dot(a2_ref[...], w2_ref[...], preferred_element_type=F32)
    o_ref[...] = (_sigmoid(ga_ref[...]) * y1 + _sigmoid(gm_ref[...]) * y2).astype(BF16)


def _merge(a1, a2, w1, w2, proj):
    s, d = a1.shape
    tm = min(512, s)
    tn = 512
    gab, gmb = P_OFFSETS["g_attn"] // tn, P_OFFSETS["g_mlstm"] // tn
    return pl.pallas_call(
        _merge_kernel,
        out_shape=jax.ShapeDtypeStruct((s, D_MODEL), BF16),
        grid=(s // tm, D_MODEL // tn),
        in_specs=[pl.BlockSpec((tm, d), lambda i, j: (i, 0)),
                  pl.BlockSpec((tm, d), lambda i, j: (i, 0)),
                  pl.BlockSpec((d, tn), lambda i, j: (0, j)),
                  pl.BlockSpec((d, tn), lambda i, j: (0, j)),
                  pl.BlockSpec((tm, tn), lambda i, j: (i, gab + j)),
                  pl.BlockSpec((tm, tn), lambda i, j: (i, gmb + j))],
        out_specs=pl.BlockSpec((tm, tn), lambda i, j: (i, j)),
        compiler_params=_cparams(("arbitrary", "arbitrary"), 48),
        name="merge",
    )(a1, a2, w1, w2, proj, proj)


def _final_kernel(mg_ref, w_ref, x_ref, gate_ref, lg_ref, lb_ref, o_ref, buf_ref, *, tn, nn):
    j = pl.program_id(1)
    buf_ref[j] = jnp.dot(mg_ref[...], w_ref[...], preferred_element_type=F32)

    @pl.when(j == nn - 1)
    def _norm():
        d = nn * tn
        ssum = 0.0
        for jj in range(nn):
            cols = slice(jj * tn, (jj + 1) * tn)
            r = DEEPNORM_ALPHA * x_ref[:, cols] + gate_ref[:, cols] * buf_ref[jj]
            buf_ref[jj] = r
            ssum = ssum + jnp.sum(r, axis=-1, keepdims=True)
        mu = ssum / d
        vsum = 0.0
        for jj in range(nn):
            vsum = vsum + jnp.sum(jnp.square(buf_ref[jj] - mu), axis=-1, keepdims=True)
        inv = lax.rsqrt(vsum / d + NORM_EPS)
        for jj in range(nn):
            cols = slice(jj * tn, (jj + 1) * tn)
            o_ref[:, cols] = (buf_ref[jj] - mu) * inv * lg_ref[:, cols] + lb_ref[:, cols]


def _final(merged, w_out, x2, mod, ln_g, ln_b):
    s, d = x2.shape
    tm = min(256, s)
    tn = 512
    nn = d // tn
    return pl.pallas_call(
        functools.partial(_final_kernel, tn=tn, nn=nn),
        out_shape=jax.ShapeDtypeStruct((s, d), F32),
        grid=(s // tm, nn),
        in_specs=[pl.BlockSpec((tm, d), lambda i, j: (i, 0)),
                  pl.BlockSpec((d, tn), lambda i, j: (0, j)),
                  pl.BlockSpec((tm, d), lambda i, j: (i, 0)),
                  pl.BlockSpec((1, d), lambda i, j: (0, 2)),
                  pl.BlockSpec((1, d), lambda i, j: (0, 0)),
                  pl.BlockSpec((1, d), lambda i, j: (0, 0))],
        out_specs=pl.BlockSpec((tm, d), lambda i, j: (i, 0)),
        scratch_shapes=[pltpu.VMEM((nn, tm, tn), F32)],
        compiler_params=_cparams(("arbitrary", "arbitrary"), 48),
        name="final",
    )(merged, w_out, x2, mod, ln_g, ln_b)


def _regroup_cols(a, pad_to):
    parts = [a[..., IN_OFFSETS[n]:IN_OFFSETS[n] + IN_WIDTH_OF[n]] for n in P_ORDER]
    parts.append(jnp.zeros(a.shape[:-1] + (pad_to - P_USED,), a.dtype))
    return jnp.concatenate(parts, axis=-1)


def _layer(x2, c, w_ada, b_ada, w_in, b_in, g_q, g_kv, w_uq, w_iq, w_uk, w_uv, g_kidx, b_kidx, conv_w, conv_b, g_mh,
           w_attn_out, w_mlstm_out, w_out, ln_g, ln_b):
    s, d = x2.shape
    assert d == D_MODEL and s % 1024 == 0, (s, d)
    tq, tk = 256, 512
    nsel = min(TOPK_MAX, s // 4)

    w_cat = _regroup_cols(w_in, P_TOTAL).astype(BF16)
    b_cat = _regroup_cols(b_in, P_TOTAL).reshape(1, P_TOTAL)
    w_uqT = w_uq.T.astype(BF16)
    w_iqT = w_iq.T.astype(BF16)
    w_ukT = w_uk.reshape(ATTN_WIDTH, KV_LORA_RANK).T.astype(BF16)
    w_uvT = w_uv.transpose(0, 2, 1).reshape(ATTN_WIDTH, KV_LORA_RANK).astype(BF16)
    slopes = jnp.exp2(-8.0 * jnp.arange(1, ATTN_HEADS + 1, dtype=F32) / ATTN_HEADS)
    slopes = jnp.broadcast_to((slopes * LOG2E)[:, None, None], (ATTN_HEADS, 1, tq))

    mod = _ada(c.reshape(d, 1), w_ada, b_ada.reshape(1, -1))
    u = _modulate(x2, mod)
    proj = _proj(u, w_cat, b_cat)

    qT, qiT = _qpath(proj, g_q.reshape(1, -1), w_uqT, w_iqT, tq)
    k, vT, kidx, widx = _kvpath(proj, g_kv.reshape(1, -1), g_kidx.reshape(1, -1), b_kidx.reshape(1, -1), w_ukT, w_uvT, tq)
    wT = widx.T.reshape(IDX_HEADS, 1, s)
    keys, thr = _indexer(kidx, qiT, wT, tq, nsel)
    a_attn = _attention(qT, k, vT, keys, thr, proj, slopes, tq, tk)

    qm, km = _conv(proj, conv_w, conv_b.reshape(1, -1), tq)
    a_mlstm = _mlstm(qm, km, proj, g_mh.reshape(MLSTM_HEADS, 1, MLSTM_V_DIM), MLSTM_CHUNK)

    merged = _merge(a_attn, a_mlstm, w_attn_out.astype(BF16), w_mlstm_out.astype(BF16), proj)
    return _final(merged, w_out.astype(BF16), x2, mod, ln_g.reshape(1, -1), ln_b.reshape(1, -1))


def kernel(x, c, w_ada, b_ada, w_in, b_in, g_q, g_kv, w_uq, w_iq, w_uk, w_uv, g_kidx, b_kidx, conv_w, conv_b, g_mh,
           w_attn_out, w_mlstm_out, w_out, ln_g, ln_b):
    bsz, seq, d = x.shape
    assert bsz == 1 and w_ada.shape[0] == 1, "single batch, single layer"
    out = _layer(x.reshape(seq, d), c, w_ada[0], b_ada[0], w_in[0], b_in[0], g_q[0], g_kv[0], w_uq[0], w_iq[0],
                 w_uk[0], w_uv[0], g_kidx[0], b_kidx[0], conv_w[0], conv_b[0], g_mh[0], w_attn_out[0],
                 w_mlstm_out[0], w_out[0], ln_g[0], ln_b[0])
    return out.reshape(bsz, seq, d)
```

```python
import functools

import jax
import jax.numpy as jnp
from jax import lax
from jax.experimental import pallas as pl
from jax.experimental.pallas import tpu as pltpu

F32 = jnp.float32
BF16 = jnp.bfloat16
I32 = jnp.int32

D_MODEL = 4096
ATTN_HEADS = 32
ATTN_HEAD_DIM = 128
ATTN_WIDTH = ATTN_HEADS * ATTN_HEAD_DIM
Q_LORA_RANK = 1024
KV_LORA_RANK = 512
IDX_HEADS = 32
IDX_HEAD_DIM = 64
TOPK_MAX = 256
MLSTM_HEADS = 8
MLSTM_QK_DIM = (D_MODEL // 2) // MLSTM_HEADS
MLSTM_V_DIM = D_MODEL // MLSTM_HEADS
MLSTM_QK_WIDTH = MLSTM_HEADS * MLSTM_QK_DIM
MLSTM_WIDTH = MLSTM_HEADS * MLSTM_V_DIM
MLSTM_CHUNK = 256
CONV_WIDTH = 4
GATE_SOFTCAP = 15.0
DEEPNORM_ALPHA = 2.0 ** 0.25
NORM_EPS = 1e-6

IN_WIDTHS = (Q_LORA_RANK, KV_LORA_RANK, IDX_HEAD_DIM, IDX_HEADS, ATTN_WIDTH, 2 * MLSTM_QK_WIDTH, MLSTM_WIDTH,
             MLSTM_WIDTH, MLSTM_HEADS, MLSTM_HEADS, MLSTM_WIDTH, D_MODEL, D_MODEL)
IN_NAMES = ("q_lat", "kv_lat", "k_idx", "w_idx", "z_attn", "qk_m", "v_m", "o_m", "i_m", "f_m", "z_m", "g_attn", "g_mlstm")
IN_OFFSETS = {n: sum(IN_WIDTHS[:i]) for i, n in enumerate(IN_NAMES)}
IN_WIDTH_OF = dict(zip(IN_NAMES, IN_WIDTHS))

P_ORDER = ("z_attn", "qk_m", "v_m", "o_m", "z_m", "g_attn", "g_mlstm", "q_lat", "kv_lat", "k_idx", "w_idx", "i_m", "f_m")
P_OFFSETS = {}
_off = 0
for _n in P_ORDER:
    P_OFFSETS[_n] = _off
    _off += IN_WIDTH_OF[_n]
P_USED = _off
PROJ_TN = 1024
P_TOTAL = -(-P_USED // PROJ_TN) * PROJ_TN
SMALL_W = 128
SMALL_OFF = P_OFFSETS["k_idx"]
SM_WIDX = IDX_HEAD_DIM
SM_I = SM_WIDX + IDX_HEADS
SM_F = SM_I + MLSTM_HEADS

VMEM_CAP_BYTES = 60 * 1024 * 1024

MASK_DIST = 1e30
M_INIT = -1e20
LOG2E = 1.4426950408889634
V_ONES = 16
V_ROWS = ATTN_HEAD_DIM + V_ONES
ATTN_ROWS = 128
HEAD_GROUP = 8
INT_MIN = -2 ** 31
KEY_NEG_INF = INT_MIN + 0x7FFFFF


def _cparams(sem, vmem_mb):
    return pltpu.CompilerParams(dimension_semantics=sem, vmem_limit_bytes=min(vmem_mb * 1024 * 1024, VMEM_CAP_BYTES))


def _sigmoid(x):
    return jax.nn.sigmoid(x)


def _silu(x):
    return x * jax.nn.sigmoid(x)


def _const_spec(shape):
    nd = len(shape)
    return pl.BlockSpec(shape, lambda *_: (0,) * nd, pipeline_mode=pl.Buffered(1))


def _ada_kernel(c_ref, w_ref, b_ref, o_ref):
    c = c_ref[...]
    o_ref[...] = jnp.sum(w_ref[...] * _silu(c), axis=0, keepdims=True) + b_ref[...]


def _ada(c_col, w_ada, b_ada):
    d, n = w_ada.shape
    tn = 512
    return pl.pallas_call(
        _ada_kernel,
        out_shape=jax.ShapeDtypeStruct((1, n), F32),
        grid=(n // tn,),
        in_specs=[pl.BlockSpec((d, 1), lambda j: (0, 0)),
                  pl.BlockSpec((d, tn), lambda j: (0, j)),
                  pl.BlockSpec((1, tn), lambda j: (0, j))],
        out_specs=pl.BlockSpec((1, tn), lambda j: (0, j)),
        compiler_params=_cparams(("arbitrary",), 32),
        name="ada",
    )(c_col, w_ada, b_ada)


def _modulate_kernel(x_ref, shift_ref, scale_ref, u_ref):
    u_ref[...] = (x_ref[...] * (1.0 + scale_ref[...]) + shift_ref[...]).astype(BF16)


def _modulate(x2, mod):
    s, d = x2.shape
    tm = min(512, s)
    return pl.pallas_call(
        _modulate_kernel,
        out_shape=jax.ShapeDtypeStruct((s, d), BF16),
        grid=(s // tm,),
        in_specs=[pl.BlockSpec((tm, d), lambda i: (i, 0)),
                  pl.BlockSpec((1, d), lambda i: (0, 0)),
                  pl.BlockSpec((1, d), lambda i: (0, 1))],
        out_specs=pl.BlockSpec((tm, d), lambda i: (i, 0)),
        compiler_params=_cparams(("arbitrary",), 40),
        name="modulate",
    )(x2, mod, mod)


def _proj_kernel(u_ref, w_ref, b_ref, o_ref):
    o_ref[...] = jnp.dot(u_ref[...], w_ref[...], preferred_element_type=F32) + b_ref[...]


def _proj(u, w_cat, b_cat):
    s, d = u.shape
    n = w_cat.shape[1]
    tm = min(1024, s)
    tn = PROJ_TN
    return pl.pallas_call(
        _proj_kernel,
        out_shape=jax.ShapeDtypeStruct((s, n), F32),
        grid=(n // tn, s // tm),
        in_specs=[pl.BlockSpec((tm, d), lambda j, i: (i, 0)),
                  pl.BlockSpec((d, tn), lambda j, i: (0, j)),
                  pl.BlockSpec((1, tn), lambda j, i: (0, j))],
        out_specs=pl.BlockSpec((tm, tn), lambda j, i: (i, j)),
        compiler_params=_cparams(("arbitrary", "arbitrary"), 56),
        name="proj",
    )(u, w_cat, b_cat)


def _qpath_kernel(ql_ref, g_ref, wuq_ref, wiq_ref, qT_ref, qiT_ref, *, scale):
    x = ql_ref[...]
    cq = (x * lax.rsqrt(jnp.mean(x * x, axis=-1, keepdims=True) + NORM_EPS) * g_ref[...]).astype(BF16)
    nt = (((1,), (1,)), ((), ()))
    qT = lax.dot_general(wuq_ref[...], cq, nt, preferred_element_type=F32)
    qT_ref[...] = (qT * scale).reshape(qT_ref.shape).astype(BF16)
    qiT = lax.dot_general(wiq_ref[...], cq, nt, preferred_element_type=F32)
    qiT_ref[...] = qiT.reshape(qiT_ref.shape).astype(BF16)


def _qpath(proj, g_q, w_uqT, w_iqT, tq):
    s = proj.shape[0]
    r = Q_LORA_RANK
    return pl.pallas_call(
        functools.partial(_qpath_kernel, scale=ATTN_HEAD_DIM ** -0.5 * LOG2E),
        out_shape=(jax.ShapeDtypeStruct((ATTN_HEADS, ATTN_HEAD_DIM, s), BF16),
                   jax.ShapeDtypeStruct((IDX_HEADS, IDX_HEAD_DIM, s), BF16)),
        grid=(s // tq,),
        in_specs=[pl.BlockSpec((tq, r), lambda i: (i, P_OFFSETS["q_lat"] // r)),
                  _const_spec((1, r)),
                  _const_spec(w_uqT.shape),
                  _const_spec(w_iqT.shape)],
        out_specs=(pl.BlockSpec((ATTN_HEADS, ATTN_HEAD_DIM, tq), lambda i: (0, 0, i)),
                   pl.BlockSpec((IDX_HEADS, IDX_HEAD_DIM, tq), lambda i: (0, 0, i))),
        compiler_params=_cparams(("arbitrary",), 48),
        name="qpath",
    )(proj, g_q, w_uqT, w_iqT)


def _kvpath_kernel(kvl_ref, sm_ref, gkv_ref, gk_ref, bk_ref, wuk_ref, wuv_ref, k_ref, vT_ref, kidx_ref, widx_ref, *, wscale):
    x = kvl_ref[...]
    ckv = (x * lax.rsqrt(jnp.mean(x * x, axis=-1, keepdims=True) + NORM_EPS) * gkv_ref[...]).astype(BF16)
    kfull = jnp.dot(ckv, wuk_ref[...], preferred_element_type=F32)
    for h in range(ATTN_HEADS):
        k_ref[h] = kfull[:, h * ATTN_HEAD_DIM:(h + 1) * ATTN_HEAD_DIM].astype(BF16)
    nt = (((1,), (1,)), ((), ()))
    vT = lax.dot_general(wuv_ref[...], ckv, nt, preferred_element_type=F32)
    vT_ref[:, :ATTN_HEAD_DIM, :] = vT.reshape(ATTN_HEADS, ATTN_HEAD_DIM, -1).astype(BF16)
    vT_ref[:, ATTN_HEAD_DIM:, :] = jnp.ones((ATTN_HEADS, V_ONES, vT_ref.shape[2]), BF16)
    sm = sm_ref[...]
    ki = sm[:, :IDX_HEAD_DIM]
    mu = jnp.mean(ki, axis=-1, keepdims=True)
    var = jnp.mean(jnp.square(ki - mu), axis=-1, keepdims=True)
    kidx_ref[...] = ((ki - mu) * lax.rsqrt(var + NORM_EPS) * gk_ref[...] + bk_ref[...]).astype(BF16)
    widx_ref[...] = sm[:, SM_WIDX:SM_WIDX + IDX_HEADS] * wscale


def _kvpath(proj, g_kv, g_kidx, b_kidx, w_ukT, w_uvT, tm):
    s = proj.shape[0]
    r = KV_LORA_RANK
    return pl.pallas_call(
        functools.partial(_kvpath_kernel, wscale=IDX_HEADS ** -0.5 * IDX_HEAD_DIM ** -0.5),
        out_shape=(jax.ShapeDtypeStruct((ATTN_HEADS, s, ATTN_HEAD_DIM), BF16),
                   jax.ShapeDtypeStruct((ATTN_HEADS, V_ROWS, s), BF16),
                   jax.ShapeDtypeStruct((s, IDX_HEAD_DIM), BF16),
                   jax.ShapeDtypeStruct((s, IDX_HEADS), F32)),
        grid=(s // tm,),
        in_specs=[pl.BlockSpec((tm, r), lambda i: (i, P_OFFSETS["kv_lat"] // r)),
                  pl.BlockSpec((tm, SMALL_W), lambda i: (i, SMALL_OFF // SMALL_W)),
                  _const_spec((1, r)),
                  _const_spec((1, IDX_HEAD_DIM)),
                  _const_spec((1, IDX_HEAD_DIM)),
                  _const_spec(w_ukT.shape),
                  _const_spec(w_uvT.shape)],
        out_specs=(pl.BlockSpec((ATTN_HEADS, tm, ATTN_HEAD_DIM), lambda i: (0, i, 0)),
                   pl.BlockSpec((ATTN_HEADS, V_ROWS, tm), lambda i: (0, 0, i)),
                   pl.BlockSpec((tm, IDX_HEAD_DIM), lambda i: (i, 0)),
                   pl.BlockSpec((tm, IDX_HEADS), lambda i: (i, 0))),
        compiler_params=_cparams(("arbitrary",), 48),
        name="kvpath",
    )(proj, proj, g_kv, g_kidx, b_kidx, w_ukT, w_uvT)


def _key_to_float(key):
    bits = jnp.where(key >= 0, key, key ^ 0x7FFFFFFF)
    return jnp.where(key < KEY_NEG_INF, -jnp.inf, pltpu.bitcast(bits, F32))


def _indexer_kernel(kidx_ref, qiT_ref, wT_ref, sc_ref, thr_ref, *, seq, tq, nsel):
    i = pl.program_id(0)
    ch = 128
    cb = tq
    n_score = (i + 1) * (tq // ch)
    n_count = i + 1
    tpos = i * tq + lax.broadcasted_iota(I32, (ch, tq), 1)

    def score_chunk(c, carry):
        r0 = pl.multiple_of(c * ch, ch)
        kc = kidx_ref[pl.ds(r0, ch), :]

        acc = jnp.zeros((ch, tq), F32)
        for h in range(IDX_HEADS):
            r = jnp.dot(kc, qiT_ref[h], preferred_element_type=F32)
            acc = acc + jnp.maximum(r, 0.0) * wT_ref[h]
        spos = r0 + lax.broadcasted_iota(I32, (ch, tq), 0)
        sc_ref[pl.ds(r0, ch), :] = jnp.where(spos <= tpos, acc, -jnp.inf)
        return carry
    lax.fori_loop(0, n_score, score_chunk, 0)

    def fill_chunk(c, carry):
        sc_ref[pl.ds(pl.multiple_of(c * cb, cb), cb), :] = jnp.full((cb, tq), -jnp.inf, F32)
        return carry
    lax.fori_loop(n_count, seq // cb, fill_chunk, 0)

    def count(pred):
        def body(c, part):
            r0 = pl.multiple_of(c * cb, cb)
            m = jnp.where(pred(sc_ref[pl.ds(r0, cb), :], r0), 1, 0)
            return part + jnp.sum(m.reshape(cb // 8, 8, tq), axis=0)
        part = lax.fori_loop(0, n_count, body, jnp.zeros((8, tq), I32))
        return jnp.sum(part, axis=0, keepdims=True)

    def count_ge(cand_key):
        cand = _key_to_float(cand_key)
        return count(lambda blk, r0: blk >= cand)

    t0 = jnp.where(count_ge(jnp.zeros((1, tq), I32)) >= nsel, 0, INT_MIN).astype(I32)

    def bit_step(b, t):
        cand = t + jnp.left_shift(jnp.int32(1), 30 - b)
        return jnp.where(count_ge(cand) >= nsel, cand, t)
    thr = _key_to_float(lax.fori_loop(0, 31, bit_step, t0))
    thr_ref[...] = thr

    tie = (count(lambda blk, r0: blk >= thr) > nsel) & (thr > -jnp.inf)

    @pl.when(jnp.max(tie.astype(I32)) > 0)
    def _break_ties():
        need = nsel - count(lambda blk, r0: blk > thr)

        def eq_below(j):
            return count(lambda blk, r0: (blk == thr) & (r0 + lax.broadcasted_iota(I32, (cb, tq), 0) < j))

        def jbit(b, j):
            test = j + jnp.left_shift(jnp.int32(1), (seq.bit_length() - 2) - b)
            return jnp.where(eq_below(test) < need, test, j)
        jlast = lax.fori_loop(0, seq.bit_length() - 1, jbit, jnp.zeros((1, tq), I32))

        def demote(c, carry):
            r0 = pl.multiple_of(c * cb, cb)
            blk = sc_ref[pl.ds(r0, cb), :]
            row = r0 + lax.broadcasted_iota(I32, (cb, tq), 0)
            sc_ref[pl.ds(r0, cb), :] = jnp.where(tie & (blk == thr) & (row > jlast), -jnp.inf, blk)
            return carry
        lax.fori_loop(0, n_count, demote, 0)


def _indexer(kidx, qiT, wT, tq, nsel):
    s = kidx.shape[0]
    return pl.pallas_call(
        functools.partial(_indexer_kernel, seq=s, tq=tq, nsel=nsel),
        out_shape=(jax.ShapeDtypeStruct((s, s), F32), jax.ShapeDtypeStruct((1, s), F32)),
        grid=(s // tq,),
        in_specs=[_const_spec((s, IDX_HEAD_DIM)),
                  pl.BlockSpec((IDX_HEADS, IDX_HEAD_DIM, tq), lambda i: (0, 0, i)),
                  pl.BlockSpec((IDX_HEADS, 1, tq), lambda i: (0, 0, i))],
        out_specs=(pl.BlockSpec((s, tq), lambda i: (0, i)),
                   pl.BlockSpec((1, tq), lambda i: (0, i))),
        compiler_params=_cparams(("arbitrary",), 40),
        name="indexer",
    )(kidx, qiT, wT)


def _attn_kernel(qT_ref, k_ref, vT_ref, keys_ref, thr_ref, z_ref, sl_ref, o_ref,
                 acc_ref, m_ref, dm_ref, lg_ref, p_ref, *, tq, tk, nk):
    qi = pl.program_id(0)
    kj = pl.program_id(1)
    last = (qi * tq + tq - 1) // tk

    @pl.when(kj == 0)
    def _init():
        acc_ref[...] = jnp.zeros(acc_ref.shape, F32)
        m_ref[...] = jnp.full(m_ref.shape, M_INIT, F32)

    @pl.when(kj <= last)
    def _compute():
        spos = kj * tk + lax.broadcasted_iota(I32, (tk, tq), 0)
        tpos = qi * tq + lax.broadcasted_iota(I32, (tk, tq), 1)
        sel = (keys_ref[...] >= thr_ref[...]) & (spos <= tpos)
        dm_ref[...] = jnp.where(sel, (tpos - spos).astype(F32), MASK_DIST)

        def group(g, carry):
            def logits(u):
                h = g * HEAD_GROUP + u
                qh = qT_ref[h]
                slope = sl_ref[h]
                part = jnp.full((8, tq), M_INIT, F32)
                for c in range(tk // ATTN_ROWS):
                    rows = pl.ds(c * ATTN_ROWS, ATTN_ROWS)
                    s = jnp.dot(k_ref[h, rows, :], qh, preferred_element_type=F32)
                    lg = s - slope * dm_ref[rows, :]
                    lg_ref[u % 2, rows, :] = lg
                    part = jnp.maximum(part, jnp.max(lg.reshape(ATTN_ROWS // 8, 8, tq), axis=0))
                m_old = m_ref[g, u]
                return m_old, jnp.maximum(m_old, jnp.max(part, axis=0, keepdims=True))

            def probs(u, m_old, m_new):
                for c in range(tk // ATTN_ROWS):
                    rows = pl.ds(c * ATTN_ROWS, ATTN_ROWS)
                    p_ref[u % 2, rows, :] = jnp.exp2(lg_ref[u % 2, rows, :] - m_new).astype(BF16)
                m_ref[g, u] = m_new
                return jnp.exp2(m_old - m_new)

            def values(u, alpha):
                h = g * HEAD_GROUP + u
                acc_ref[g, u] = alpha * acc_ref[g, u] + jnp.dot(vT_ref[h], p_ref[u % 2], preferred_element_type=F32)

            stats = logits(0)
            alpha_prev = None
            for u in range(HEAD_GROUP):
                stats_next = logits(u + 1) if u + 1 < HEAD_GROUP else None
                alpha = probs(u, *stats)
                if u >= 1:
                    values(u - 1, alpha_prev)
                stats, alpha_prev = stats_next, alpha
            values(HEAD_GROUP - 1, alpha_prev)
            return carry
        lax.fori_loop(0, ATTN_HEADS // HEAD_GROUP, group, 0)

    @pl.when(kj == nk - 1)
    def _finish():
        for h in range(ATTN_HEADS):
            g, u = divmod(h, HEAD_GROUP)
            cols = slice(h * ATTN_HEAD_DIM, (h + 1) * ATTN_HEAD_DIM)
            acc = acc_ref[g, u]
            o = (acc[:ATTN_HEAD_DIM] * (1.0 / acc[ATTN_HEAD_DIM:ATTN_HEAD_DIM + 1])).T
            o_ref[:, cols] = (o * _silu(z_ref[:, cols])).astype(BF16)


def _attention(qT, k, vT, keys, thr, proj, slopes, tq, tk):
    s = k.shape[1]
    nk = s // tk
    ng = ATTN_HEADS // HEAD_GROUP

    def kv_blk(qi, kj):
        return jnp.minimum(kj, (qi * tq + tq - 1) // tk)
    return pl.pallas_call(
        functools.partial(_attn_kernel, tq=tq, tk=tk, nk=nk),
        out_shape=jax.ShapeDtypeStruct((s, ATTN_WIDTH), BF16),
        grid=(s // tq, nk),
        in_specs=[pl.BlockSpec((ATTN_HEADS, ATTN_HEAD_DIM, tq), lambda qi, kj: (0, 0, qi)),
                  pl.BlockSpec((ATTN_HEADS, tk, ATTN_HEAD_DIM), lambda qi, kj: (0, kv_blk(qi, kj), 0)),
                  pl.BlockSpec((ATTN_HEADS, V_ROWS, tk), lambda qi, kj: (0, 0, kv_blk(qi, kj))),
                  pl.BlockSpec((tk, tq), lambda qi, kj: (kv_blk(qi, kj), qi)),
                  pl.BlockSpec((1, tq), lambda qi, kj: (0, qi)),
                  pl.BlockSpec((tq, ATTN_WIDTH), lambda qi, kj: (qi, P_OFFSETS["z_attn"] // ATTN_WIDTH)),
                  _const_spec((ATTN_HEADS, 1, tq))],
        out_specs=pl.BlockSpec((tq, ATTN_WIDTH), lambda qi, kj: (qi, 0)),
        scratch_shapes=[pltpu.VMEM((ng, HEAD_GROUP, V_ROWS, tq), F32),
                        pltpu.VMEM((ng, HEAD_GROUP, 1, tq), F32),
                        pltpu.VMEM((tk, tq), F32),
                        pltpu.VMEM((2, tk, tq), F32),
                        pltpu.VMEM((2, tk, tq), BF16)],
        compiler_params=_cparams(("arbitrary", "arbitrary"), 56),
        name="attn",
    )(qT, k, vT, keys, thr, proj, slopes)


def _conv_kernel(x_ref, prev_ref, w_ref, b_ref, q_ref, k_ref, *, kscale):
    i = pl.program_id(0)
    x = x_ref[...]
    prev = jnp.where(i > 0, prev_ref[...], 0.0)
    head = jnp.concatenate([prev, x[:8]], axis=0)
    y = b_ref[...]
    yh = b_ref[...]
    for j in range(CONV_WIDTH):
        d = CONV_WIDTH - 1 - j
        xs = x if d == 0 else pltpu.roll(x, d, 0)
        hs = head if d == 0 else pltpu.roll(head, d, 0)
        y = y + xs * w_ref[j:j + 1, :]
        yh = yh + hs[8:] * w_ref[j:j + 1, :]
    y = jnp.concatenate([yh, y[8:]], axis=0)
    y = _silu(y)
    half = y.shape[1] // 2
    q_ref[...] = y[:, :half].astype(BF16)
    k_ref[...] = (y[:, half:] * kscale).astype(BF16)


def _conv(proj, conv_w, conv_b, tm):
    s = proj.shape[0]
    c = 2 * MLSTM_QK_WIDTH
    cb = P_OFFSETS["qk_m"] // c
    return pl.pallas_call(
        functools.partial(_conv_kernel, kscale=MLSTM_QK_DIM ** -0.5),
        out_shape=(jax.ShapeDtypeStruct((s, MLSTM_QK_WIDTH), BF16), jax.ShapeDtypeStruct((s, MLSTM_QK_WIDTH), BF16)),
        grid=(s // tm,),
        in_specs=[pl.BlockSpec((tm, c), lambda i: (i, cb)),
                  pl.BlockSpec((8, c), lambda i: (jnp.maximum(i * (tm // 8) - 1, 0), cb)),
                  _const_spec((CONV_WIDTH, c)),
                  _const_spec((1, c))],
        out_specs=(pl.BlockSpec((tm, MLSTM_QK_WIDTH), lambda i: (i, 0)),
                   pl.BlockSpec((tm, MLSTM_QK_WIDTH), lambda i: (i, 0))),
        compiler_params=_cparams(("arbitrary",), 40),
        name="conv",
    )(proj, proj, conv_w, conv_b)


def _softcap(x):
    return GATE_SOFTCAP * jnp.tanh(x / GATE_SOFTCAP)


def _mlstm_kernel(q_ref, k_ref, v_ref, og_ref, z_ref, sm_ref, g_ref, out_ref, c_ref, n_ref, m_ref, *, chunk):
    hd = pl.program_id(0)
    ci = pl.program_id(1)
    L = chunk

    @pl.when(ci == 0)
    def _init():
        c_ref[...] = jnp.zeros(c_ref.shape, F32)
        n_ref[...] = jnp.zeros(n_ref.shape, F32)
        m_ref[...] = jnp.zeros(m_ref.shape, F32)

    sm = sm_ref[...]
    lane = lax.broadcasted_iota(I32, sm.shape, 1)
    ig_col = _softcap(jnp.sum(jnp.where(lane == SM_I + hd, sm, 0.0), axis=1, keepdims=True))
    fg_col = _softcap(jnp.sum(jnp.where(lane == SM_F + hd, sm, 0.0), axis=1, keepdims=True))
    logf_col = jnp.minimum(fg_col, 0.0) - jnp.log1p(jnp.exp(-jnp.abs(fg_col)))

    r_i = lax.broadcasted_iota(I32, (L, L), 0)
    c_i = lax.broadcasted_iota(I32, (L, L), 1)
    eye = r_i == c_i
    tril = r_i >= c_i
    logf_row = jnp.sum(jnp.where(eye, logf_col, 0.0), axis=0, keepdims=True)
    ig_row = jnp.sum(jnp.where(eye, ig_col, 0.0), axis=0, keepdims=True)
    b_col = jnp.sum(jnp.where(tril, logf_row, 0.0), axis=1, keepdims=True)
    b_row = jnp.sum(jnp.where(r_i <= c_i, logf_col, 0.0), axis=0, keepdims=True)
    dmat = jnp.where(tril, b_col - b_row + ig_row, -jnp.inf)
    m_prev = m_ref[...]
    m_inter = b_col + m_prev
    m_t = jnp.maximum(m_inter, jnp.max(dmat, axis=1, keepdims=True))

    qc = q_ref[...]
    kc = k_ref[...]
    vc = v_ref[...].astype(BF16)
    nt = (((1,), (1,)), ((), ()))
    s = lax.dot_general(qc, kc, nt, preferred_element_type=F32) * jnp.exp(dmat - m_t)
    inter = jnp.exp(m_inter - m_t)
    num = (jnp.dot(s.astype(BF16), vc, preferred_element_type=F32)
           + inter * jnp.dot(qc, c_ref[...].astype(BF16), preferred_element_type=F32))
    qn = jnp.sum(qc.astype(F32) * n_ref[...], axis=1, keepdims=True)
    den = jnp.sum(s, axis=1, keepdims=True) + inter * qn
    hh = num / jnp.maximum(jnp.abs(den), jnp.exp(-m_t))

    g_last = b_col[L - 1:L, :]
    m_new = m_t[L - 1:L, :]
    wgt = jnp.exp(g_last - b_col + ig_col - m_new)
    decay = jnp.exp(g_last + m_prev - m_new)
    wk = wgt * kc.astype(F32)
    tn = (((0,), (0,)), ((), ()))
    c_ref[...] = decay * c_ref[...] + lax.dot_general(wk.astype(BF16), vc, tn, preferred_element_type=F32)
    n_ref[...] = decay * n_ref[...] + jnp.sum(wk, axis=0, keepdims=True)
    m_ref[...] = m_new

    hn = hh * lax.rsqrt(jnp.mean(hh * hh, axis=-1, keepdims=True) + NORM_EPS) * g_ref[0]
    out_ref[...] = (hn * _sigmoid(og_ref[...]) * _silu(z_ref[...])).astype(BF16)


def _mlstm(qm, km, proj, g_mh3, chunk):
    s = qm.shape[0]
    dk, dv = MLSTM_QK_DIM, MLSTM_V_DIM
    vb, ob, zb = (P_OFFSETS[n] // dv for n in ("v_m", "o_m", "z_m"))
    return pl.pallas_call(
        functools.partial(_mlstm_kernel, chunk=chunk),
        out_shape=jax.ShapeDtypeStruct((s, MLSTM_WIDTH), BF16),
        grid=(MLSTM_HEADS, s // chunk),
        in_specs=[pl.BlockSpec((chunk, dk), lambda h, c: (c, h)),
                  pl.BlockSpec((chunk, dk), lambda h, c: (c, h)),
                  pl.BlockSpec((chunk, dv), lambda h, c: (c, vb + h)),
                  pl.BlockSpec((chunk, dv), lambda h, c: (c, ob + h)),
                  pl.BlockSpec((chunk, dv), lambda h, c: (c, zb + h)),
                  pl.BlockSpec((chunk, SMALL_W), lambda h, c: (c, SMALL_OFF // SMALL_W)),
                  pl.BlockSpec((1, 1, dv), lambda h, c: (h, 0, 0))],
        out_specs=pl.BlockSpec((chunk, dv), lambda h, c: (c, h)),
        scratch_shapes=[pltpu.VMEM((dk, dv), F32), pltpu.VMEM((1, dk), F32), pltpu.VMEM((1, 1), F32)],
        compiler_params=_cparams(("arbitrary", "arbitrary"), 32),
        name="mlstm",
    )(qm, km, proj, proj, proj, proj, g_mh3)


def _merge_kernel(a1_ref, a2_ref, w1_ref, w2_ref, ga_ref, gm_ref, o_ref):
    y1 = jnp.dot(a1_ref[...], w1_ref[...], preferred_element_type=F32)
    y2 = jnp.dot(a2_ref[...], w2_ref[...], preferred_element_type=F32)
    o_ref[...] = (_sigmoid(ga_ref[...]) * y1 + _sigmoid(gm_ref[...]) * y2).astype(BF16)


def _merge(a1, a2, w1, w2, proj):
    s, d = a1.shape
    tm = min(512, s)
    tn = 512
    gab, gmb = P_OFFSETS["g_attn"] // tn, P_OFFSETS["g_mlstm"] // tn
    return pl.pallas_call(
        _merge_kernel,
        out_shape=jax.ShapeDtypeStruct((s, D_MODEL), BF16),
        grid=(s // tm, D_MODEL // tn),
        in_specs=[pl.BlockSpec((tm, d), lambda i, j: (i, 0)),
                  pl.BlockSpec((tm, d), lambda i, j: (i, 0)),
                  pl.BlockSpec((d, tn), lambda i, j: (0, j)),
                  pl.BlockSpec((d, tn), lambda i, j: (0, j)),
                  pl.BlockSpec((tm, tn), lambda i, j: (i, gab + j)),
                  pl.BlockSpec((tm, tn), lambda i, j: (i, gmb + j))],
        out_specs=pl.BlockSpec((tm, tn), lambda i, j: (i, j)),
        compiler_params=_cparams(("arbitrary", "arbitrary"), 48),
        name="merge",
    )(a1, a2, w1, w2, proj, proj)


def _final_kernel(mg_ref, w_ref, x_ref, gate_ref, lg_ref, lb_ref, o_ref, buf_ref, *, tn, nn):
    j = pl.program_id(1)
    buf_ref[j] = jnp.dot(mg_ref[...], w_ref[...], preferred_element_type=F32)

    @pl.when(j == nn - 1)
    def _norm():
        d = nn * tn
        ssum = 0.0
        for jj in range(nn):
            cols = slice(jj * tn, (jj + 1) * tn)
            r = DEEPNORM_ALPHA * x_ref[:, cols] + gate_ref[:, cols] * buf_ref[jj]
            buf_ref[jj] = r
            ssum = ssum + jnp.sum(r, axis=-1, keepdims=True)
        mu = ssum / d
        vsum = 0.0
        for jj in range(nn):
            vsum = vsum + jnp.sum(jnp.square(buf_ref[jj] - mu), axis=-1, keepdims=True)
        inv = lax.rsqrt(vsum / d + NORM_EPS)
        for jj in range(nn):
            cols = slice(jj * tn, (jj + 1) * tn)
            o_ref[:, cols] = (buf_ref[jj] - mu) * inv * lg_ref[:, cols] + lb_ref[:, cols]


def _final(merged, w_out, x2, mod, ln_g, ln_b):
    s, d = x2.shape
    tm = min(256, s)
    tn = 512
    nn = d // tn
    return pl.pallas_call(
        functools.partial(_final_kernel, tn=tn, nn=nn),
        out_shape=jax.ShapeDtypeStruct((s, d), F32),
        grid=(s // tm, nn),
        in_specs=[pl.BlockSpec((tm, d), lambda i, j: (i, 0)),
                  pl.BlockSpec((d, tn), lambda i, j: (0, j)),
                  pl.BlockSpec((tm, d), lambda i, j: (i, 0)),
                  pl.BlockSpec((1, d), lambda i, j: (0, 2)),
                  pl.BlockSpec((1, d), lambda i, j: (0, 0)),
                  pl.BlockSpec((1, d), lambda i, j: (0, 0))],
        out_specs=pl.BlockSpec((tm, d), lambda i, j: (i, 0)),
        scratch_shapes=[pltpu.VMEM((nn, tm, tn), F32)],
        compiler_params=_cparams(("arbitrary", "arbitrary"), 48),
        name="final",
    )(merged, w_out, x2, mod, ln_g, ln_b)


def _regroup_cols(a, pad_to):
    parts = [a[..., IN_OFFSETS[n]:IN_OFFSETS[n] + IN_WIDTH_OF[n]] for n in P_ORDER]
    parts.append(jnp.zeros(a.shape[:-1] + (pad_to - P_USED,), a.dtype))
    return jnp.concatenate(parts, axis=-1)


def _layer(x2, c, w_ada, b_ada, w_in, b_in, g_q, g_kv, w_uq, w_iq, w_uk, w_uv, g_kidx, b_kidx, conv_w, conv_b, g_mh,
           w_attn_out, w_mlstm_out, w_out, ln_g, ln_b):
    s, d = x2.shape
    assert d == D_MODEL and s % 1024 == 0, (s, d)
    tq, tk = 256, 512
    nsel = min(TOPK_MAX, s // 4)

    w_cat = _regroup_cols(w_in, P_TOTAL).astype(BF16)
    b_cat = _regroup_cols(b_in, P_TOTAL).reshape(1, P_TOTAL)
    w_uqT = w_uq.T.astype(BF16)
    w_iqT = w_iq.T.astype(BF16)
    w_ukT = w_uk.reshape(ATTN_WIDTH, KV_LORA_RANK).T.astype(BF16)
    w_uvT = w_uv.transpose(0, 2, 1).reshape(ATTN_WIDTH, KV_LORA_RANK).astype(BF16)
    slopes = jnp.exp2(-8.0 * jnp.arange(1, ATTN_HEADS + 1, dtype=F32) / ATTN_HEADS)
    slopes = jnp.broadcast_to((slopes * LOG2E)[:, None, None], (ATTN_HEADS, 1, tq))

    mod = _ada(c.reshape(d, 1), w_ada, b_ada.reshape(1, -1))
    u = _modulate(x2, mod)
    proj = _proj(u, w_cat, b_cat)

    qT, qiT = _qpath(proj, g_q.reshape(1, -1), w_uqT, w_iqT, tq)
    k, vT, kidx, widx = _kvpath(proj, g_kv.reshape(1, -1), g_kidx.reshape(1, -1), b_kidx.reshape(1, -1), w_ukT, w_uvT, tq)
    wT = widx.T.reshape(IDX_HEADS, 1, s)
    keys, thr = _indexer(kidx, qiT, wT, tq, nsel)
    a_attn = _attention(qT, k, vT, keys, thr, proj, slopes, tq, tk)

    qm, km = _conv(proj, conv_w, conv_b.reshape(1, -1), tq)
    a_mlstm = _mlstm(qm, km, proj, g_mh.reshape(MLSTM_HEADS, 1, MLSTM_V_DIM), MLSTM_CHUNK)

    merged = _merge(a_attn, a_mlstm, w_attn_out.astype(BF16), w_mlstm_out.astype(BF16), proj)
    return _final(merged, w_out.astype(BF16), x2, mod, ln_g.reshape(1, -1), ln_b.reshape(1, -1))


def kernel(x, c, w_ada, b_ada, w_in, b_in, g_q, g_kv, w_uq, w_iq, w_uk, w_uv, g_kidx, b_kidx, conv_w, conv_b, g_mh,
           w_attn_out, w_mlstm_out, w_out, ln_g, ln_b):
    bsz, seq, d = x.shape
    assert bsz == 1 and w_ada.shape[0] == 1, "single batch, single layer"
    out = _layer(x.reshape(seq, d), c, w_ada[0], b_ada[0], w_in[0], b_in[0], g_q[0], g_kv[0], w_uq[0], w_iq[0],
                 w_uk[0], w_uv[0], g_kidx[0], b_kidx[0], conv_w[0], conv_b[0], g_mh[0], w_attn_out[0],
                 w_mlstm_out[0], w_out[0], ln_g[0], ln_b[0])
    return out.reshape(bsz, seq, d)
```

```python
import functools

import jax
import jax.numpy as jnp
from jax import lax
from jax.experimental import pallas as pl
from jax.experimental.pallas import tpu as pltpu

F32 = jnp.float32
BF16 = jnp.bfloat16
I32 = jnp.int32

D_MODEL = 4096
ATTN_HEADS = 32
ATTN_HEAD_DIM = 128
ATTN_WIDTH = ATTN_HEADS * ATTN_HEAD_DIM
Q_LORA_RANK = 1024
KV_LORA_RANK = 512
IDX_HEADS = 32
IDX_HEAD_DIM = 64
TOPK_MAX = 256
MLSTM_HEADS = 8
MLSTM_QK_DIM = (D_MODEL // 2) // MLSTM_HEADS
MLSTM_V_DIM = D_MODEL // MLSTM_HEADS
MLSTM_QK_WIDTH = MLSTM_HEADS * MLSTM_QK_DIM
MLSTM_WIDTH = MLSTM_HEADS * MLSTM_V_DIM
MLSTM_CHUNK = 256
CONV_WIDTH = 4
GATE_SOFTCAP = 15.0
DEEPNORM_ALPHA = 2.0 ** 0.25
NORM_EPS = 1e-6

IN_WIDTHS = (Q_LORA_RANK, KV_LORA_RANK, IDX_HEAD_DIM, IDX_HEADS, ATTN_WIDTH, 2 * MLSTM_QK_WIDTH, MLSTM_WIDTH,
             MLSTM_WIDTH, MLSTM_HEADS, MLSTM_HEADS, MLSTM_WIDTH, D_MODEL, D_MODEL)
IN_NAMES = ("q_lat", "kv_lat", "k_idx", "w_idx", "z_attn", "qk_m", "v_m", "o_m", "i_m", "f_m", "z_m", "g_attn", "g_mlstm")
IN_OFFSETS = {n: sum(IN_WIDTHS[:i]) for i, n in enumerate(IN_NAMES)}
IN_WIDTH_OF = dict(zip(IN_NAMES, IN_WIDTHS))

P_ORDER = ("z_attn", "qk_m", "v_m", "o_m", "z_m", "g_attn", "g_mlstm", "q_lat", "kv_lat", "k_idx", "w_idx", "i_m", "f_m")
P_OFFSETS = {}
_off = 0
for _n in P_ORDER:
    P_OFFSETS[_n] = _off
    _off += IN_WIDTH_OF[_n]
P_USED = _off
PROJ_TN = 1024
P_TOTAL = -(-P_USED // PROJ_TN) * PROJ_TN
SMALL_W = 128
SMALL_OFF = P_OFFSETS["k_idx"]
SM_WIDX = IDX_HEAD_DIM
SM_I = SM_WIDX + IDX_HEADS
SM_F = SM_I + MLSTM_HEADS

VMEM_CAP_BYTES = 60 * 1024 * 1024

MASK_DIST = 1e30
M_INIT = -1e20
LOG2E = 1.4426950408889634
V_ONES = 16
V_ROWS = ATTN_HEAD_DIM + V_ONES
ATTN_ROWS = 128
HEAD_GROUP = 8
INT_MIN = -2 ** 31
KEY_NEG_INF = INT_MIN + 0x7FFFFF


def _cparams(sem, vmem_mb):
    return pltpu.CompilerParams(dimension_semantics=sem, vmem_limit_bytes=min(vmem_mb * 1024 * 1024, VMEM_CAP_BYTES))


def _sigmoid(x):
    return jax.nn.sigmoid(x)


def _silu(x):
    return x * jax.nn.sigmoid(x)


def _const_spec(shape):
    nd = len(shape)
    return pl.BlockSpec(shape, lambda *_: (0,) * nd, pipeline_mode=pl.Buffered(1))


def _ada_kernel(c_ref, w_ref, b_ref, o_ref):
    c = c_ref[...]
    o_ref[...] = jnp.sum(w_ref[...] * _silu(c), axis=0, keepdims=True) + b_ref[...]


def _ada(c_col, w_ada, b_ada):
    d, n = w_ada.shape
    tn = 512
    return pl.pallas_call(
        _ada_kernel,
        out_shape=jax.ShapeDtypeStruct((1, n), F32),
        grid=(n // tn,),
        in_specs=[pl.BlockSpec((d, 1), lambda j: (0, 0)),
                  pl.BlockSpec((d, tn), lambda j: (0, j)),
                  pl.BlockSpec((1, tn), lambda j: (0, j))],
        out_specs=pl.BlockSpec((1, tn), lambda j: (0, j)),
        compiler_params=_cparams(("arbitrary",), 32),
        name="ada",
    )(c_col, w_ada, b_ada)


def _modulate_kernel(x_ref, shift_ref, scale_ref, u_ref):
    u_ref[...] = (x_ref[...] * (1.0 + scale_ref[...]) + shift_ref[...]).astype(BF16)


def _modulate(x2, mod):
    s, d = x2.shape
    tm = min(512, s)
    return pl.pallas_call(
        _modulate_kernel,
        out_shape=jax.ShapeDtypeStruct((s, d), BF16),
        grid=(s // tm,),
        in_specs=[pl.BlockSpec((tm, d), lambda i: (i, 0)),
                  pl.BlockSpec((1, d), lambda i: (0, 0)),
                  pl.BlockSpec((1, d), lambda i: (0, 1))],
        out_specs=pl.BlockSpec((tm, d), lambda i: (i, 0)),
        compiler_params=_cparams(("arbitrary",), 40),
        name="modulate",
    )(x2, mod, mod)


def _proj_kernel(u_ref, w_ref, b_ref, o_ref):
    o_ref[...] = jnp.dot(u_ref[...], w_ref[...], preferred_element_type=F32) + b_ref[...]


def _proj(u, w_cat, b_cat):
    s, d = u.shape
    n = w_cat.shape[1]
    tm = min(1024, s)
    tn = PROJ_TN
    return pl.pallas_call(
        _proj_kernel,
        out_shape=jax.ShapeDtypeStruct((s, n), F32),
        grid=(n // tn, s // tm),
        in_specs=[pl.BlockSpec((tm, d), lambda j, i: (i, 0)),
                  pl.BlockSpec((d, tn), lambda j, i: (0, j)),
                  pl.BlockSpec((1, tn), lambda j, i: (0, j))],
        out_specs=pl.BlockSpec((tm, tn), lambda j, i: (i, j)),
        compiler_params=_cparams(("arbitrary", "arbitrary"), 56),
        name="proj",
    )(u, w_cat, b_cat)


def _qpath_kernel(ql_ref, g_ref, wuq_ref, wiq_ref, qT_ref, qiT_ref, *, scale):
    x = ql_ref[...]
    cq = (x * lax.rsqrt(jnp.mean(x * x, axis=-1, keepdims=True) + NORM_EPS) * g_ref[...]).astype(BF16)
    nt = (((1,), (1,)), ((), ()))
    qT = lax.dot_general(wuq_ref[...], cq, nt, preferred_element_type=F32)
    qT_ref[...] = (qT * scale).reshape(qT_ref.shape).astype(BF16)
    qiT = lax.dot_general(wiq_ref[...], cq, nt, preferred_element_type=F32)
    qiT_ref[...] = qiT.reshape(qiT_ref.shape).astype(BF16)


def _qpath(proj, g_q, w_uqT, w_iqT, tq):
    s = proj.shape[0]
    r = Q_LORA_RANK
    return pl.pallas_call(
        functools.partial(_qpath_kernel, scale=ATTN_HEAD_DIM ** -0.5 * LOG2E),
        out_shape=(jax.ShapeDtypeStruct((ATTN_HEADS, ATTN_HEAD_DIM, s), BF16),
                   jax.ShapeDtypeStruct((IDX_HEADS, IDX_HEAD_DIM, s), BF16)),
        grid=(s // tq,),
        in_specs=[pl.BlockSpec((tq, r), lambda i: (i, P_OFFSETS["q_lat"] // r)),
                  _const_spec((1, r)),
                  _const_spec(w_uqT.shape),
                  _const_spec(w_iqT.shape)],
        out_specs=(pl.BlockSpec((ATTN_HEADS, ATTN_HEAD_DIM, tq), lambda i: (0, 0, i)),
                   pl.BlockSpec((IDX_HEADS, IDX_HEAD_DIM, tq), lambda i: (0, 0, i))),
        compiler_params=_cparams(("arbitrary",), 48),
        name="qpath",
    )(proj, g_q, w_uqT, w_iqT)


def _kvpath_kernel(kvl_ref, sm_ref, gkv_ref, gk_ref, bk_ref, wuk_ref, wuv_ref, k_ref, vT_ref, kidx_ref, widx_ref, *, wscale):
    x = kvl_ref[...]
    ckv = (x * lax.rsqrt(jnp.mean(x * x, axis=-1, keepdims=True) + NORM_EPS) * gkv_ref[...]).astype(BF16)
    kfull = jnp.dot(ckv, wuk_ref[...], preferred_element_type=F32)
    for h in range(ATTN_HEADS):
        k_ref[h] = kfull[:, h * ATTN_HEAD_DIM:(h + 1) * ATTN_HEAD_DIM].astype(BF16)
    nt = (((1,), (1,)), ((), ()))
    vT = lax.dot_general(wuv_ref[...], ckv, nt, preferred_element_type=F32)
    vT_ref[:, :ATTN_HEAD_DIM, :] = vT.reshape(ATTN_HEADS, ATTN_HEAD_DIM, -1).astype(BF16)
    vT_ref[:, ATTN_HEAD_DIM:, :] = jnp.ones((ATTN_HEADS, V_ONES, vT_ref.shape[2]), BF16)
    sm = sm_ref[...]
    ki = sm[:, :IDX_HEAD_DIM]
    mu = jnp.mean(ki, axis=-1, keepdims=True)
    var = jnp.mean(jnp.square(ki - mu), axis=-1, keepdims=True)
    kidx_ref[...] = ((ki - mu) * lax.rsqrt(var + NORM_EPS) * gk_ref[...] + bk_ref[...]).astype(BF16)
    widx_ref[...] = sm[:, SM_WIDX:SM_WIDX + IDX_HEADS] * wscale


def _kvpath(proj, g_kv, g_kidx, b_kidx, w_ukT, w_uvT, tm):
    s = proj.shape[0]
    r = KV_LORA_RANK
    return pl.pallas_call(
        functools.partial(_kvpath_kernel, wscale=IDX_HEADS ** -0.5 * IDX_HEAD_DIM ** -0.5),
        out_shape=(jax.ShapeDtypeStruct((ATTN_HEADS, s, ATTN_HEAD_DIM), BF16),
                   jax.ShapeDtypeStruct((ATTN_HEADS, V_ROWS, s), BF16),
                   jax.ShapeDtypeStruct((s, IDX_HEAD_DIM), BF16),
                   jax.ShapeDtypeStruct((s, IDX_HEADS), F32)),
        grid=(s // tm,),
        in_specs=[pl.BlockSpec((tm, r), lambda i: (i, P_OFFSETS["kv_lat"] // r)),
                  pl.BlockSpec((tm, SMALL_W), lambda i: (i, SMALL_OFF // SMALL_W)),
                  _const_spec((1, r)),
                  _const_spec((1, IDX_HEAD_DIM)),
                  _const_spec((1, IDX_HEAD_DIM)),
                  _const_spec(w_ukT.shape),
                  _const_spec(w_uvT.shape)],
        out_specs=(pl.BlockSpec((ATTN_HEADS, tm, ATTN_HEAD_DIM), lambda i: (0, i, 0)),
                   pl.BlockSpec((ATTN_HEADS, V_ROWS, tm), lambda i: (0, 0, i)),
                   pl.BlockSpec((tm, IDX_HEAD_DIM), lambda i: (i, 0)),
                   pl.BlockSpec((tm, IDX_HEADS), lambda i: (i, 0))),
        compiler_params=_cparams(("arbitrary",), 48),
        name="kvpath",
    )(proj, proj, g_kv, g_kidx, b_kidx, w_ukT, w_uvT)


def _key_to_float(key):
    bits = jnp.where(key >= 0, key, key ^ 0x7FFFFFFF)
    return jnp.where(key < KEY_NEG_INF, -jnp.inf, pltpu.bitcast(bits, F32))


def _indexer_kernel(kidx_ref, qiT_ref, wT_ref, sc_ref, thr_ref, *, seq, tq, nsel):
    i = pl.program_id(0)
    ch = 128
    cb = tq
    n_score = (i + 1) * (tq // ch)
    n_count = i + 1
    tpos = i * tq + lax.broadcasted_iota(I32, (ch, tq), 1)

    def score_chunk(c, carry):
        r0 = pl.multiple_of(c * ch, ch)
        kc = kidx_ref[pl.ds(r0, ch), :]

        acc = jnp.zeros((ch, tq), F32)
        for h in range(IDX_HEADS):
            r = jnp.dot(kc, qiT_ref[h], preferred_element_type=F32)
            acc = acc + jnp.maximum(r, 0.0) * wT_ref[h]
        spos = r0 + lax.broadcasted_iota(I32, (ch, tq), 0)
        sc_ref[pl.ds(r0, ch), :] = jnp.where(spos <= tpos, acc, -jnp.inf)
        return carry
    lax.fori_loop(0, n_score, score_chunk, 0)

    def fill_chunk(c, carry):
        sc_ref[pl.ds(pl.multiple_of(c * cb, cb), cb), :] = jnp.full((cb, tq), -jnp.inf, F32)
        return carry
    lax.fori_loop(n_count, seq // cb, fill_chunk, 0)

    def count(pred):
        def body(c, part):
            r0 = pl.multiple_of(c * cb, cb)
            m = jnp.where(pred(sc_ref[pl.ds(r0, cb), :], r0), 1, 0)
            return part + jnp.sum(m.reshape(cb // 8, 8, tq), axis=0)
        part = lax.fori_loop(0, n_count, body, jnp.zeros((8, tq), I32))
        return jnp.sum(part, axis=0, keepdims=True)

    def count_ge(cand_key):
        cand = _key_to_float(cand_key)
        return count(lambda blk, r0: blk >= cand)

    t0 = jnp.where(count_ge(jnp.zeros((1, tq), I32)) >= nsel, 0, INT_MIN).astype(I32)

    def bit_step(b, t):
        cand = t + jnp.left_shift(jnp.int32(1), 30 - b)
        return jnp.where(count_ge(cand) >= nsel, cand, t)
    thr = _key_to_float(lax.fori_loop(0, 31, bit_step, t0))
    thr_ref[...] = thr

    tie = (count(lambda blk, r0: blk >= thr) > nsel) & (thr > -jnp.inf)

    @pl.when(jnp.max(tie.astype(I32)) > 0)
    def _break_ties():
        need = nsel - count(lambda blk, r0: blk > thr)

        def eq_below(j):
            return count(lambda blk, r0: (blk == thr) & (r0 + lax.broadcasted_iota(I32, (cb, tq), 0) < j))

        def jbit(b, j):
            test = j + jnp.left_shift(jnp.int32(1), (seq.bit_length() - 2) - b)
            return jnp.where(eq_below(test) < need, test, j)
        jlast = lax.fori_loop(0, seq.bit_length() - 1, jbit, jnp.zeros((1, tq), I32))

        def demote(c, carry):
            r0 = pl.multiple_of(c * cb, cb)
            blk = sc_ref[pl.ds(r0, cb), :]
            row = r0 + lax.broadcasted_iota(I32, (cb, tq), 0)
            sc_ref[pl.ds(r0, cb), :] = jnp.where(tie & (blk == thr) & (row > jlast), -jnp.inf, blk)
            return carry
        lax.fori_loop(0, n_count, demote, 0)


def _indexer(kidx, qiT, wT, tq, nsel):
    s = kidx.shape[0]
    return pl.pallas_call(
        functools.partial(_indexer_kernel, seq=s, tq=tq, nsel=nsel),
        out_shape=(jax.ShapeDtypeStruct((s, s), F32), jax.ShapeDtypeStruct((1, s), F32)),
        grid=(s // tq,),
        in_specs=[_const_spec((s, IDX_HEAD_DIM)),
                  pl.BlockSpec((IDX_HEADS, IDX_HEAD_DIM, tq), lambda i: (0, 0, i)),
                  pl.BlockSpec((IDX_HEADS, 1, tq), lambda i: (0, 0, i))],
        out_specs=(pl.BlockSpec((s, tq), lambda i: (0, i)),
                   pl.BlockSpec((1, tq), lambda i: (0, i))),
        compiler_params=_cparams(("arbitrary",), 40),
        name="indexer",
    )(kidx, qiT, wT)


def _attn_kernel(qT_ref, k_ref, vT_ref, keys_ref, thr_ref, z_ref, sl_ref, o_ref,
                 acc_ref, m_ref, dm_ref, lg_ref, p_ref, *, tq, tk, nk):
    qi = pl.program_id(0)
    kj = pl.program_id(1)
    last = (qi * tq + tq - 1) // tk

    @pl.when(kj == 0)
    def _init():
        acc_ref[...] = jnp.zeros(acc_ref.shape, F32)
        m_ref[...] = jnp.full(m_ref.shape, M_INIT, F32)

    @pl.when(kj <= last)
    def _compute():
        spos = kj * tk + lax.broadcasted_iota(I32, (tk, tq), 0)
        tpos = qi * tq + lax.broadcasted_iota(I32, (tk, tq), 1)
        sel = (keys_ref[...] >= thr_ref[...]) & (spos <= tpos)
        dm_ref[...] = jnp.where(sel, (tpos - spos).astype(F32), MASK_DIST)

        def group(g, carry):
            def logits(u):
                h = g * HEAD_GROUP + u
                qh = qT_ref[h]
                slope = sl_ref[h]
                part = jnp.full((8, tq), M_INIT, F32)
                for c in range(tk // ATTN_ROWS):
                    rows = pl.ds(c * ATTN_ROWS, ATTN_ROWS)
                    s = jnp.dot(k_ref[h, rows, :], qh, preferred_element_type=F32)
                    lg = s - slope * dm_ref[rows, :]
                    lg_ref[u % 2, rows, :] = lg
                    part = jnp.maximum(part, jnp.max(lg.reshape(ATTN_ROWS // 8, 8, tq), axis=0))
                m_old = m_ref[g, u]
                return m_old, jnp.maximum(m_old, jnp.max(part, axis=0, keepdims=True))

            def probs(u, m_old, m_new):
                for c in range(tk // ATTN_ROWS):
                    rows = pl.ds(c * ATTN_ROWS, ATTN_ROWS)
                    p_ref[u % 2, rows, :] = jnp.exp2(lg_ref[u % 2, rows, :] - m_new).astype(BF16)
                m_ref[g, u] = m_new
                return jnp.exp2(m_old - m_new)

            def values(u, alpha):
                h = g * HEAD_GROUP + u
                acc_ref[g, u] = alpha * acc_ref[g, u] + jnp.dot(vT_ref[h], p_ref[u % 2], preferred_element_type=F32)

            stats = logits(0)
            alpha_prev = None
            for u in range(HEAD_GROUP):
                stats_next = logits(u + 1) if u + 1 < HEAD_GROUP else None
                alpha = probs(u, *stats)
                if u >= 1:
                    values(u - 1, alpha_prev)
                stats, alpha_prev = stats_next, alpha
            values(HEAD_GROUP - 1, alpha_prev)
            return carry
        lax.fori_loop(0, ATTN_HEADS // HEAD_GROUP, group, 0)

    @pl.when(kj == nk - 1)
    def _finish():
        for h in range(ATTN_HEADS):
            g, u = divmod(h, HEAD_GROUP)
            cols = slice(h * ATTN_HEAD_DIM, (h + 1) * ATTN_HEAD_DIM)
            acc = acc_ref[g, u]
            o = (acc[:ATTN_HEAD_DIM] * (1.0 / acc[ATTN_HEAD_DIM:ATTN_HEAD_DIM + 1])).T
            o_ref[:, cols] = (o * _silu(z_ref[:, cols])).astype(BF16)


def _attention(qT, k, vT, keys, thr, proj, slopes, tq, tk):
    s = k.shape[1]
    nk = s // tk
    ng = ATTN_HEADS // HEAD_GROUP

    def kv_blk(qi, kj):
        return jnp.minimum(kj, (qi * tq + tq - 1) // tk)
    return pl.pallas_call(
        functools.partial(_attn_kernel, tq=tq, tk=tk, nk=nk),
        out_shape=jax.ShapeDtypeStruct((s, ATTN_WIDTH), BF16),
        grid=(s // tq, nk),
        in_specs=[pl.BlockSpec((ATTN_HEADS, ATTN_HEAD_DIM, tq), lambda qi, kj: (0, 0, qi)),
                  pl.BlockSpec((ATTN_HEADS, tk, ATTN_HEAD_DIM), lambda qi, kj: (0, kv_blk(qi, kj), 0)),
                  pl.BlockSpec((ATTN_HEADS, V_ROWS, tk), lambda qi, kj: (0, 0, kv_blk(qi, kj))),
                  pl.BlockSpec((tk, tq), lambda qi, kj: (kv_blk(qi, kj), qi)),
                  pl.BlockSpec((1, tq), lambda qi, kj: (0, qi)),
                  pl.BlockSpec((tq, ATTN_WIDTH), lambda qi, kj: (qi, P_OFFSETS["z_attn"] // ATTN_WIDTH)),
                  _const_spec((ATTN_HEADS, 1, tq))],
        out_specs=pl.BlockSpec((tq, ATTN_WIDTH), lambda qi, kj: (qi, 0)),
        scratch_shapes=[pltpu.VMEM((ng, HEAD_GROUP, V_ROWS, tq), F32),
                        pltpu.VMEM((ng, HEAD_GROUP, 1, tq), F32),
                        pltpu.VMEM((tk, tq), F32),
                        pltpu.VMEM((2, tk, tq), F32),
                        pltpu.VMEM((2, tk, tq), BF16)],
        compiler_params=_cparams(("arbitrary", "arbitrary"), 56),
        name="attn",
    )(qT, k, vT, keys, thr, proj, slopes)


def _conv_kernel(x_ref, prev_ref, w_ref, b_ref, q_ref, k_ref, *, kscale):
    i = pl.program_id(0)
    x = x_ref[...]
    prev = jnp.where(i > 0, prev_ref[...], 0.0)
    head = jnp.concatenate([prev, x[:8]], axis=0)
    y = b_ref[...]
    yh = b_ref[...]
    for j in range(CONV_WIDTH):
        d = CONV_WIDTH - 1 - j
        xs = x if d == 0 else pltpu.roll(x, d, 0)
        hs = head if d == 0 else pltpu.roll(head, d, 0)
        y = y + xs * w_ref[j:j + 1, :]
        yh = yh + hs[8:] * w_ref[j:j + 1, :]
    y = jnp.concatenate([yh, y[8:]], axis=0)
    y = _silu(y)
    half = y.shape[1] // 2
    q_ref[...] = y[:, :half].astype(BF16)
    k_ref[...] = (y[:, half:] * kscale).astype(BF16)


def _conv(proj, conv_w, conv_b, tm):
    s = proj.shape[0]
    c = 2 * MLSTM_QK_WIDTH
    cb = P_OFFSETS["qk_m"] // c
    return pl.pallas_call(
        functools.partial(_conv_kernel, kscale=MLSTM_QK_DIM ** -0.5),
        out_shape=(jax.ShapeDtypeStruct((s, MLSTM_QK_WIDTH), BF16), jax.ShapeDtypeStruct((s, MLSTM_QK_WIDTH), BF16)),
        grid=(s // tm,),
        in_specs=[pl.BlockSpec((tm, c), lambda i: (i, cb)),
                  pl.BlockSpec((8, c), lambda i: (jnp.maximum(i * (tm // 8) - 1, 0), cb)),
                  _const_spec((CONV_WIDTH, c)),
                  _const_spec((1, c))],
        out_specs=(pl.BlockSpec((tm, MLSTM_QK_WIDTH), lambda i: (i, 0)),
                   pl.BlockSpec((tm, MLSTM_QK_WIDTH), lambda i: (i, 0))),
        compiler_params=_cparams(("arbitrary",), 40),
        name="conv",
    )(proj, proj, conv_w, conv_b)


def _softcap(x):
    return GATE_SOFTCAP * jnp.tanh(x / GATE_SOFTCAP)


def _mlstm_kernel(q_ref, k_ref, v_ref, og_ref, z_ref, sm_ref, g_ref, out_ref, c_ref, n_ref, m_ref, *, chunk):
    hd = pl.program_id(0)
    ci = pl.program_id(1)
    L = chunk

    @pl.when(ci == 0)
    def _init():
        c_ref[...] = jnp.zeros(c_ref.shape, F32)
        n_ref[...] = jnp.zeros(n_ref.shape, F32)
        m_ref[...] = jnp.zeros(m_ref.shape, F32)

    sm = sm_ref[...]
    lane = lax.broadcasted_iota(I32, sm.shape, 1)
    ig_col = _softcap(jnp.sum(jnp.where(lane == SM_I + hd, sm, 0.0), axis=1, keepdims=True))
    fg_col = _softcap(jnp.sum(jnp.where(lane == SM_F + hd, sm, 0.0), axis=1, keepdims=True))
    logf_col = jnp.minimum(fg_col, 0.0) - jnp.log1p(jnp.exp(-jnp.abs(fg_col)))

    r_i = lax.broadcasted_iota(I32, (L, L), 0)
    c_i = lax.broadcasted_iota(I32, (L, L), 1)
    eye = r_i == c_i
    tril = r_i >= c_i
    logf_row = jnp.sum(jnp.where(eye, logf_col, 0.0), axis=0, keepdims=True)
    ig_row = jnp.sum(jnp.where(eye, ig_col, 0.0), axis=0, keepdims=True)
    b_col = jnp.sum(jnp.where(tril, logf_row, 0.0), axis=1, keepdims=True)
    b_row = jnp.sum(jnp.where(r_i <= c_i, logf_col, 0.0), axis=0, keepdims=True)
    dmat = jnp.where(tril, b_col - b_row + ig_row, -jnp.inf)
    m_prev = m_ref[...]
    m_inter = b_col + m_prev
    m_t = jnp.maximum(m_inter, jnp.max(dmat, axis=1, keepdims=True))

    qc = q_ref[...]
    kc = k_ref[...]
    vc = v_ref[...].astype(BF16)
    nt = (((1,), (1,)), ((), ()))
    s = lax.dot_general(qc, kc, nt, preferred_element_type=F32) * jnp.exp(dmat - m_t)
    inter = jnp.exp(m_inter - m_t)
    num = (jnp.dot(s.astype(BF16), vc, preferred_element_type=F32)
           + inter * jnp.dot(qc, c_ref[...].astype(BF16), preferred_element_type=F32))
    qn = jnp.sum(qc.astype(F32) * n_ref[...], axis=1, keepdims=True)
    den = jnp.sum(s, axis=1, keepdims=True) + inter * qn
    hh = num / jnp.maximum(jnp.abs(den), jnp.exp(-m_t))

    g_last = b_col[L - 1:L, :]
    m_new = m_t[L - 1:L, :]
    wgt = jnp.exp(g_last - b_col + ig_col - m_new)
    decay = jnp.exp(g_last + m_prev - m_new)
    wk = wgt * kc.astype(F32)
    tn = (((0,), (0,)), ((), ()))
    c_ref[...] = decay * c_ref[...] + lax.dot_general(wk.astype(BF16), vc, tn, preferred_element_type=F32)
    n_ref[...] = decay * n_ref[...] + jnp.sum(wk, axis=0, keepdims=True)
    m_ref[...] = m_new

    hn = hh * lax.rsqrt(jnp.mean(hh * hh, axis=-1, keepdims=True) + NORM_EPS) * g_ref[0]
    out_ref[...] = (hn * _sigmoid(og_ref[...]) * _silu(z_ref[...])).astype(BF16)


def _mlstm(qm, km, proj, g_mh3, chunk):
    s = qm.shape[0]
    dk, dv = MLSTM_QK_DIM, MLSTM_V_DIM
    vb, ob, zb = (P_OFFSETS[n] // dv for n in ("v_m", "o_m", "z_m"))
    return pl.pallas_call(
        functools.partial(_mlstm_kernel, chunk=chunk),
        out_shape=jax.ShapeDtypeStruct((s, MLSTM_WIDTH), BF16),
        grid=(MLSTM_HEADS, s // chunk),
        in_specs=[pl.BlockSpec((chunk, dk), lambda h, c: (c, h)),
                  pl.BlockSpec((chunk, dk), lambda h, c: (c, h)),
                  pl.BlockSpec((chunk, dv), lambda h, c: (c, vb + h)),
                  pl.BlockSpec((chunk, dv), lambda h, c: (c, ob + h)),
                  pl.BlockSpec((chunk, dv), lambda h, c: (c, zb + h)),
                  pl.BlockSpec((chunk, SMALL_W), lambda h, c: (c, SMALL_OFF // SMALL_W)),
                  pl.BlockSpec((1, 1, dv), lambda h, c: (h, 0, 0))],
        out_specs=pl.BlockSpec((chunk, dv), lambda h, c: (c, h)),
        scratch_shapes=[pltpu.VMEM((dk, dv), F32), pltpu.VMEM((1, dk), F32), pltpu.VMEM((1, 1), F32)],
        compiler_params=_cparams(("arbitrary", "arbitrary"), 32),
        name="mlstm",
    )(qm, km, proj, proj, proj, proj, g_mh3)


def _merge_kernel(a1_ref, a2_ref, w1_ref, w2_ref, ga_ref, gm_ref, o_ref):
    y1 = jnp.dot(a1_ref[...], w1_ref[...], preferred_element_type=F32)
    y2 = jnp.dot(a2_ref[...], w2_ref[...], preferred_element_type=F32)
    o_ref[...] = (_sigmoid(ga_ref[...]) * y1 + _sigmoid(gm_ref[...]) * y2).astype(BF16)


def _merge(a1, a2, w1, w2, proj):
    s, d = a1.shape
    tm = min(512, s)
    tn = 512
    gab, gmb = P_OFFSETS["g_attn"] // tn, P_OFFSETS["g_mlstm"] // tn
    return pl.pallas_call(
        _merge_kernel,
        out_shape=jax.ShapeDtypeStruct((s, D_MODEL), BF16),
        grid=(s // tm, D_MODEL // tn),
        in_specs=[pl.BlockSpec((tm, d), lambda i, j: (i, 0)),
                  pl.BlockSpec((tm, d), lambda i, j: (i, 0)),
                  pl.BlockSpec((d, tn), lambda i, j: (0, j)),
                  pl.BlockSpec((d, tn), lambda i, j: (0, j)),
                  pl.BlockSpec((tm, tn), lambda i, j: (i, gab + j)),
                  pl.BlockSpec((tm, tn), lambda i, j: (i, gmb + j))],
        out_specs=pl.BlockSpec((tm, tn), lambda i, j: (i, j)),
        compiler_params=_cparams(("arbitrary", "arbitrary"), 48),
        name="merge",
    )(a1, a2, w1, w2, proj, proj)


def _final_kernel(mg_ref, w_ref, x_ref, gate_ref, lg_ref, lb_ref, o_ref, buf_ref, *, tn, nn):
    j = pl.program_id(1)
    buf_ref[j] = jnp.dot(mg_ref[...], w_ref[...], preferred_element_type=F32)

    @pl.when(j == nn - 1)
    def _norm():
        d = nn * tn
        ssum = 0.0
        for jj in range(nn):
            cols = slice(jj * tn, (jj + 1) * tn)
            r = DEEPNORM_ALPHA * x_ref[:, cols] + gate_ref[:, cols] * buf_ref[jj]
            buf_ref[jj] = r
            ssum = ssum + jnp.sum(r, axis=-1, keepdims=True)
        mu = ssum / d
        vsum = 0.0
        for jj in range(nn):
            vsum = vsum + jnp.sum(jnp.square(buf_ref[jj] - mu), axis=-1, keepdims=True)
        inv = lax.rsqrt(vsum / d + NORM_EPS)
        for jj in range(nn):
            cols = slice(jj * tn, (jj + 1) * tn)
            o_ref[:, cols] = (buf_ref[jj] - mu) * inv * lg_ref[:, cols] + lb_ref[:, cols]


def _final(merged, w_out, x2, mod, ln_g, ln_b):
    s, d = x2.shape
    tm = min(512, s)
    tn = 512
    nn = d // tn
    return pl.pallas_call(
        functools.partial(_final_kernel, tn=tn, nn=nn),
        out_shape=jax.ShapeDtypeStruct((s, d), F32),
        grid=(s // tm, nn),
        in_specs=[pl.BlockSpec((tm, d), lambda i, j: (i, 0)),
                  pl.BlockSpec((d, tn), lambda i, j: (0, j)),
                  pl.BlockSpec((tm, d), lambda i, j: (i, 0), pipeline_mode=pl.Buffered(1)),
                  pl.BlockSpec((1, d), lambda i, j: (0, 2)),
                  pl.BlockSpec((1, d), lambda i, j: (0, 0)),
                  pl.BlockSpec((1, d), lambda i, j: (0, 0))],
        out_specs=pl.BlockSpec((tm, d), lambda i, j: (i, 0), pipeline_mode=pl.Buffered(1)),
        scratch_shapes=[pltpu.VMEM((nn, tm, tn), F32)],
        compiler_params=_cparams(("arbitrary", "arbitrary"), 56),
        name="final",
    )(merged, w_out, x2, mod, ln_g, ln_b)


RG_TN = 512
RG_LANES = 128
RG_NBLK = RG_TN // RG_LANES + 1


def _regroup_table():
    blk, sh = [], []
    for j in range(P_TOTAL // RG_TN):
        oc = j * RG_TN
        if oc >= SMALL_OFF:
            blk.append(0)
            sh.append(-1)
            continue
        seg = next(n for n in P_ORDER if P_OFFSETS[n] <= oc < P_OFFSETS[n] + IN_WIDTH_OF[n])
        c0 = IN_OFFSETS[seg] + oc - P_OFFSETS[seg]
        blk.append(c0 // RG_LANES)
        sh.append(c0 % RG_LANES)
    return blk, sh


def _regroup_kernel(tbl_ref, *refs, shifts):
    src, small_ref, o_ref = refs[:RG_NBLK], refs[RG_NBLK], refs[RG_NBLK + 1]
    sh = tbl_ref[1, pl.program_id(0)]
    for s in shifts:
        @pl.when(sh == s)
        def _shifted(s=s):
            x = jnp.concatenate([b[...] for b in (src if s else src[:-1])], axis=1)
            o_ref[...] = x[:, s:s + RG_TN].astype(BF16)

    @pl.when(sh < 0)
    def _narrow():
        o_ref[...] = small_ref[...]


def _regroup_w(w_in):
    d = w_in.shape[0]
    tm = 2048
    blk, sh = _regroup_table()
    small = jnp.concatenate(
        [w_in[:, IN_OFFSETS[n]:IN_OFFSETS[n] + IN_WIDTH_OF[n]] for n in ("k_idx", "w_idx", "i_m", "f_m")]
        + [jnp.zeros((d, RG_TN - (P_USED - SMALL_OFF)), w_in.dtype)], axis=1).astype(BF16)
    grid_spec = pltpu.PrefetchScalarGridSpec(
        num_scalar_prefetch=1,
        grid=(P_TOTAL // RG_TN, d // tm),
        in_specs=[pl.BlockSpec((tm, RG_LANES), functools.partial(lambda j, i, tbl, k: (i, tbl[0, j] + k), k=k))
                  for k in range(RG_NBLK)]
        + [pl.BlockSpec((tm, RG_TN), lambda j, i, tbl: (i, 0))],
        out_specs=pl.BlockSpec((tm, RG_TN), lambda j, i, tbl: (i, j)),
    )
    return pl.pallas_call(
        functools.partial(_regroup_kernel, shifts=tuple(sorted(set(s for s in sh if s >= 0)))),
        out_shape=jax.ShapeDtypeStruct((d, P_TOTAL), BF16),
        grid_spec=grid_spec,
        compiler_params=_cparams(("arbitrary", "arbitrary"), 40),
        name="regroup",
    )(jnp.asarray([blk, sh], I32), *([w_in] * RG_NBLK), small)


def _regroup_cols(a, pad_to):
    parts = [a[..., IN_OFFSETS[n]:IN_OFFSETS[n] + IN_WIDTH_OF[n]] for n in P_ORDER]
    parts.append(jnp.zeros(a.shape[:-1] + (pad_to - P_USED,), a.dtype))
    return jnp.concatenate(parts, axis=-1)


def _layer(x2, c, w_ada, b_ada, w_in, b_in, g_q, g_kv, w_uq, w_iq, w_uk, w_uv, g_kidx, b_kidx, conv_w, conv_b, g_mh,
           w_attn_out, w_mlstm_out, w_out, ln_g, ln_b):
    s, d = x2.shape
    assert d == D_MODEL and s % 1024 == 0, (s, d)
    tq, tk = 256, 512
    nsel = min(TOPK_MAX, s // 4)

    w_cat = _regroup_w(w_in)
    b_cat = _regroup_cols(b_in, P_TOTAL).reshape(1, P_TOTAL)
    w_uqT = w_uq.T.astype(BF16)
    w_iqT = w_iq.T.astype(BF16)
    w_ukT = w_uk.reshape(ATTN_WIDTH, KV_LORA_RANK).T.astype(BF16)
    w_uvT = w_uv.transpose(0, 2, 1).reshape(ATTN_WIDTH, KV_LORA_RANK).astype(BF16)
    slopes = jnp.exp2(-8.0 * jnp.arange(1, ATTN_HEADS + 1, dtype=F32) / ATTN_HEADS)
    slopes = jnp.broadcast_to((slopes * LOG2E)[:, None, None], (ATTN_HEADS, 1, tq))

    mod = _ada(c.reshape(d, 1), w_ada, b_ada.reshape(1, -1))
    u = _modulate(x2, mod)
    proj = _proj(u, w_cat, b_cat)

    qT, qiT = _qpath(proj, g_q.reshape(1, -1), w_uqT, w_iqT, tq)
    k, vT, kidx, widx = _kvpath(proj, g_kv.reshape(1, -1), g_kidx.reshape(1, -1), b_kidx.reshape(1, -1), w_ukT, w_uvT, tq)
    wT = widx.T.reshape(IDX_HEADS, 1, s)
    keys, thr = _indexer(kidx, qiT, wT, tq, nsel)
    a_attn = _attention(qT, k, vT, keys, thr, proj, slopes, tq, tk)

    qm, km = _conv(proj, conv_w, conv_b.reshape(1, -1), tq)
    a_mlstm = _mlstm(qm, km, proj, g_mh.reshape(MLSTM_HEADS, 1, MLSTM_V_DIM), MLSTM_CHUNK)

    merged = _merge(a_attn, a_mlstm, w_attn_out.astype(BF16), w_mlstm_out.astype(BF16), proj)
    return _final(merged, w_out.astype(BF16), x2, mod, ln_g.reshape(1, -1), ln_b.reshape(1, -1))


def kernel(x, c, w_ada, b_ada, w_in, b_in, g_q, g_kv, w_uq, w_iq, w_uk, w_uv, g_kidx, b_kidx, conv_w, conv_b, g_mh,
           w_attn_out, w_mlstm_out, w_out, ln_g, ln_b):
    bsz, seq, d = x.shape
    assert bsz == 1 and w_ada.shape[0] == 1, "single batch, single layer"
    out = _layer(x.reshape(seq, d), c, w_ada[0], b_ada[0], w_in[0], b_in[0], g_q[0], g_kv[0], w_uq[0], w_iq[0],
                 w_uk[0], w_uv[0], g_kidx[0], b_kidx[0], conv_w[0], conv_b[0], g_mh[0], w_attn_out[0],
                 w_mlstm_out[0], w_out[0], ln_g[0], ln_b[0])
    return out.reshape(bsz, seq, d)
```

```python
import functools

import jax
import jax.numpy as jnp
from jax import lax
from jax.experimental import pallas as pl
from jax.experimental.pallas import tpu as pltpu

F32 = jnp.float32
BF16 = jnp.bfloat16
I32 = jnp.int32

D_MODEL = 4096
ATTN_HEADS = 32
ATTN_HEAD_DIM = 128
ATTN_WIDTH = ATTN_HEADS * ATTN_HEAD_DIM
Q_LORA_RANK = 1024
KV_LORA_RANK = 512
IDX_HEADS = 32
IDX_HEAD_DIM = 64
TOPK_MAX = 256
MLSTM_HEADS = 8
MLSTM_QK_DIM = (D_MODEL // 2) // MLSTM_HEADS
MLSTM_V_DIM = D_MODEL // MLSTM_HEADS
MLSTM_QK_WIDTH = MLSTM_HEADS * MLSTM_QK_DIM
MLSTM_WIDTH = MLSTM_HEADS * MLSTM_V_DIM
MLSTM_CHUNK = 256
CONV_WIDTH = 4
GATE_SOFTCAP = 15.0
DEEPNORM_ALPHA = 2.0 ** 0.25
NORM_EPS = 1e-6

IN_WIDTHS = (Q_LORA_RANK, KV_LORA_RANK, IDX_HEAD_DIM, IDX_HEADS, ATTN_WIDTH, 2 * MLSTM_QK_WIDTH, MLSTM_WIDTH,
             MLSTM_WIDTH, MLSTM_HEADS, MLSTM_HEADS, MLSTM_WIDTH, D_MODEL, D_MODEL)
IN_NAMES = ("q_lat", "kv_lat", "k_idx", "w_idx", "z_attn", "qk_m", "v_m", "o_m", "i_m", "f_m", "z_m", "g_attn", "g_mlstm")
IN_OFFSETS = {n: sum(IN_WIDTHS[:i]) for i, n in enumerate(IN_NAMES)}
IN_WIDTH_OF = dict(zip(IN_NAMES, IN_WIDTHS))

P_ORDER = ("z_attn", "qk_m", "v_m", "o_m", "z_m", "g_attn", "g_mlstm", "q_lat", "kv_lat", "k_idx", "w_idx", "i_m", "f_m")
P_OFFSETS = {}
_off = 0
for _n in P_ORDER:
    P_OFFSETS[_n] = _off
    _off += IN_WIDTH_OF[_n]
P_USED = _off
PROJ_TN = 1024
P_TOTAL = -(-P_USED // PROJ_TN) * PROJ_TN
SMALL_W = 128
SMALL_OFF = P_OFFSETS["k_idx"]
SM_WIDX = IDX_HEAD_DIM
SM_I = SM_WIDX + IDX_HEADS
SM_F = SM_I + MLSTM_HEADS

VMEM_CAP_BYTES = 60 * 1024 * 1024

MASK_DIST = 1e30
M_INIT = -1e20
LOG2E = 1.4426950408889634
V_ONES = 16
V_ROWS = ATTN_HEAD_DIM + V_ONES
ATTN_ROWS = 128
HEAD_GROUP = 8
INT_MIN = -2 ** 31
KEY_NEG_INF = INT_MIN + 0x7FFFFF


def _cparams(sem, vmem_mb):
    return pltpu.CompilerParams(dimension_semantics=sem, vmem_limit_bytes=min(vmem_mb * 1024 * 1024, VMEM_CAP_BYTES))


def _sigmoid(x):
    return jax.nn.sigmoid(x)


def _silu(x):
    return x * jax.nn.sigmoid(x)


def _const_spec(shape):
    nd = len(shape)
    return pl.BlockSpec(shape, lambda *_: (0,) * nd, pipeline_mode=pl.Buffered(1))


def _ada_kernel(c_ref, w_ref, b_ref, o_ref):
    c = c_ref[...]
    o_ref[...] = jnp.sum(w_ref[...] * _silu(c), axis=0, keepdims=True) + b_ref[...]


def _ada(c_col, w_ada, b_ada):
    d, n = w_ada.shape
    tn = 512
    return pl.pallas_call(
        _ada_kernel,
        out_shape=jax.ShapeDtypeStruct((1, n), F32),
        grid=(n // tn,),
        in_specs=[pl.BlockSpec((d, 1), lambda j: (0, 0)),
                  pl.BlockSpec((d, tn), lambda j: (0, j)),
                  pl.BlockSpec((1, tn), lambda j: (0, j))],
        out_specs=pl.BlockSpec((1, tn), lambda j: (0, j)),
        compiler_params=_cparams(("arbitrary",), 32),
        name="ada",
    )(c_col, w_ada, b_ada)


def _modulate_kernel(x_ref, shift_ref, scale_ref, u_ref):
    u_ref[...] = (x_ref[...] * (1.0 + scale_ref[...]) + shift_ref[...]).astype(BF16)


def _modulate(x2, mod):
    s, d = x2.shape
    tm = min(512, s)
    return pl.pallas_call(
        _modulate_kernel,
        out_shape=jax.ShapeDtypeStruct((s, d), BF16),
        grid=(s // tm,),
        in_specs=[pl.BlockSpec((tm, d), lambda i: (i, 0)),
                  pl.BlockSpec((1, d), lambda i: (0, 0)),
                  pl.BlockSpec((1, d), lambda i: (0, 1))],
        out_specs=pl.BlockSpec((tm, d), lambda i: (i, 0)),
        compiler_params=_cparams(("arbitrary",), 40),
        name="modulate",
    )(x2, mod, mod)


def _proj_kernel(u_ref, w_ref, b_ref, o_ref):
    nt = (((1,), (1,)), ((), ()))
    o_ref[...] = lax.dot_general(u_ref[...], w_ref[...], nt, preferred_element_type=F32) + b_ref[...]


def _proj(u, w_catT, b_cat):
    s, d = u.shape
    n = w_catT.shape[0]
    tm = min(1024, s)
    tn = PROJ_TN
    return pl.pallas_call(
        _proj_kernel,
        out_shape=jax.ShapeDtypeStruct((s, n), F32),
        grid=(n // tn, s // tm),
        in_specs=[pl.BlockSpec((tm, d), lambda j, i: (i, 0)),
                  pl.BlockSpec((tn, d), lambda j, i: (j, 0)),
                  pl.BlockSpec((1, tn), lambda j, i: (0, j))],
        out_specs=pl.BlockSpec((tm, tn), lambda j, i: (i, j)),
        compiler_params=_cparams(("arbitrary", "arbitrary"), 56),
        name="proj",
    )(u, w_catT, b_cat)


def _qpath_kernel(ql_ref, g_ref, wuq_ref, wiq_ref, qT_ref, qiT_ref, *, scale):
    x = ql_ref[...]
    cq = (x * lax.rsqrt(jnp.mean(x * x, axis=-1, keepdims=True) + NORM_EPS) * g_ref[...]).astype(BF16)
    nt = (((1,), (1,)), ((), ()))
    qT = lax.dot_general(wuq_ref[...], cq, nt, preferred_element_type=F32)
    qT_ref[...] = (qT * scale).reshape(qT_ref.shape).astype(BF16)
    qiT = lax.dot_general(wiq_ref[...], cq, nt, preferred_element_type=F32)
    qiT_ref[...] = qiT.reshape(qiT_ref.shape).astype(BF16)


def _qpath(proj, g_q, w_uqT, w_iqT, tq):
    s = proj.shape[0]
    r = Q_LORA_RANK
    return pl.pallas_call(
        functools.partial(_qpath_kernel, scale=ATTN_HEAD_DIM ** -0.5 * LOG2E),
        out_shape=(jax.ShapeDtypeStruct((ATTN_HEADS, ATTN_HEAD_DIM, s), BF16),
                   jax.ShapeDtypeStruct((IDX_HEADS, IDX_HEAD_DIM, s), BF16)),
        grid=(s // tq,),
        in_specs=[pl.BlockSpec((tq, r), lambda i: (i, P_OFFSETS["q_lat"] // r)),
                  _const_spec((1, r)),
                  _const_spec(w_uqT.shape),
                  _const_spec(w_iqT.shape)],
        out_specs=(pl.BlockSpec((ATTN_HEADS, ATTN_HEAD_DIM, tq), lambda i: (0, 0, i)),
                   pl.BlockSpec((IDX_HEADS, IDX_HEAD_DIM, tq), lambda i: (0, 0, i))),
        compiler_params=_cparams(("arbitrary",), 48),
        name="qpath",
    )(proj, g_q, w_uqT, w_iqT)


def _kvpath_kernel(kvl_ref, sm_ref, gkv_ref, gk_ref, bk_ref, wuk_ref, wuv_ref, k_ref, vT_ref, kidx_ref, widx_ref, *, wscale):
    x = kvl_ref[...]
    ckv = (x * lax.rsqrt(jnp.mean(x * x, axis=-1, keepdims=True) + NORM_EPS) * gkv_ref[...]).astype(BF16)
    kfull = jnp.dot(ckv, wuk_ref[...], preferred_element_type=F32)
    for h in range(ATTN_HEADS):
        k_ref[h] = kfull[:, h * ATTN_HEAD_DIM:(h + 1) * ATTN_HEAD_DIM].astype(BF16)
    nt = (((1,), (1,)), ((), ()))
    vT = lax.dot_general(wuv_ref[...], ckv, nt, preferred_element_type=F32)
    vT_ref[:, :ATTN_HEAD_DIM, :] = vT.reshape(ATTN_HEADS, ATTN_HEAD_DIM, -1).astype(BF16)
    vT_ref[:, ATTN_HEAD_DIM:, :] = jnp.ones((ATTN_HEADS, V_ONES, vT_ref.shape[2]), BF16)
    sm = sm_ref[...]
    ki = sm[:, :IDX_HEAD_DIM]
    mu = jnp.mean(ki, axis=-1, keepdims=True)
    var = jnp.mean(jnp.square(ki - mu), axis=-1, keepdims=True)
    kidx_ref[...] = ((ki - mu) * lax.rsqrt(var + NORM_EPS) * gk_ref[...] + bk_ref[...]).astype(BF16)
    widx_ref[...] = sm[:, SM_WIDX:SM_WIDX + IDX_HEADS] * wscale


def _kvpath(proj, g_kv, g_kidx, b_kidx, w_ukT, w_uvT, tm):
    s = proj.shape[0]
    r = KV_LORA_RANK
    return pl.pallas_call(
        functools.partial(_kvpath_kernel, wscale=IDX_HEADS ** -0.5 * IDX_HEAD_DIM ** -0.5),
        out_shape=(jax.ShapeDtypeStruct((ATTN_HEADS, s, ATTN_HEAD_DIM), BF16),
                   jax.ShapeDtypeStruct((ATTN_HEADS, V_ROWS, s), BF16),
                   jax.ShapeDtypeStruct((s, IDX_HEAD_DIM), BF16),
                   jax.ShapeDtypeStruct((s, IDX_HEADS), F32)),
        grid=(s // tm,),
        in_specs=[pl.BlockSpec((tm, r), lambda i: (i, P_OFFSETS["kv_lat"] // r)),
                  pl.BlockSpec((tm, SMALL_W), lambda i: (i, SMALL_OFF // SMALL_W)),
                  _const_spec((1, r)),
                  _const_spec((1, IDX_HEAD_DIM)),
                  _const_spec((1, IDX_HEAD_DIM)),
                  _const_spec(w_ukT.shape),
                  _const_spec(w_uvT.shape)],
        out_specs=(pl.BlockSpec((ATTN_HEADS, tm, ATTN_HEAD_DIM), lambda i: (0, i, 0)),
                   pl.BlockSpec((ATTN_HEADS, V_ROWS, tm), lambda i: (0, 0, i)),
                   pl.BlockSpec((tm, IDX_HEAD_DIM), lambda i: (i, 0)),
                   pl.BlockSpec((tm, IDX_HEADS), lambda i: (i, 0))),
        compiler_params=_cparams(("arbitrary",), 48),
        name="kvpath",
    )(proj, proj, g_kv, g_kidx, b_kidx, w_ukT, w_uvT)


def _key_to_float(key):
    bits = jnp.where(key >= 0, key, key ^ 0x7FFFFFFF)
    return jnp.where(key < KEY_NEG_INF, -jnp.inf, pltpu.bitcast(bits, F32))


def _indexer_kernel(kidx_ref, qiT_ref, wT_ref, sc_ref, thr_ref, *, seq, tq, nsel):
    i = pl.program_id(0)
    ch = 128
    cb = tq
    n_score = (i + 1) * (tq // ch)
    n_count = i + 1
    tpos = i * tq + lax.broadcasted_iota(I32, (ch, tq), 1)

    def score_chunk(c, carry):
        r0 = pl.multiple_of(c * ch, ch)
        kc = kidx_ref[pl.ds(r0, ch), :]

        acc = jnp.zeros((ch, tq), F32)
        for h in range(IDX_HEADS):
            r = jnp.dot(kc, qiT_ref[h], preferred_element_type=F32)
            acc = acc + jnp.maximum(r, 0.0) * wT_ref[h]
        spos = r0 + lax.broadcasted_iota(I32, (ch, tq), 0)
        sc_ref[pl.ds(r0, ch), :] = jnp.where(spos <= tpos, acc, -jnp.inf)
        return carry
    lax.fori_loop(0, n_score, score_chunk, 0)

    def fill_chunk(c, carry):
        sc_ref[pl.ds(pl.multiple_of(c * cb, cb), cb), :] = jnp.full((cb, tq), -jnp.inf, F32)
        return carry
    lax.fori_loop(n_count, seq // cb, fill_chunk, 0)

    def count(pred):
        def body(c, part):
            r0 = pl.multiple_of(c * cb, cb)
            m = jnp.where(pred(sc_ref[pl.ds(r0, cb), :], r0), 1, 0)
            return part + jnp.sum(m.reshape(cb // 8, 8, tq), axis=0)
        part = lax.fori_loop(0, n_count, body, jnp.zeros((8, tq), I32))
        return jnp.sum(part, axis=0, keepdims=True)

    def count_ge(cand_key):
        cand = _key_to_float(cand_key)
        return count(lambda blk, r0: blk >= cand)

    t0 = jnp.where(count_ge(jnp.zeros((1, tq), I32)) >= nsel, 0, INT_MIN).astype(I32)

    def bit_step(b, t):
        cand = t + jnp.left_shift(jnp.int32(1), 30 - b)
        return jnp.where(count_ge(cand) >= nsel, cand, t)
    thr = _key_to_float(lax.fori_loop(0, 31, bit_step, t0))
    thr_ref[...] = thr

    tie = (count(lambda blk, r0: blk >= thr) > nsel) & (thr > -jnp.inf)

    @pl.when(jnp.max(tie.astype(I32)) > 0)
    def _break_ties():
        need = nsel - count(lambda blk, r0: blk > thr)

        def eq_below(j):
            return count(lambda blk, r0: (blk == thr) & (r0 + lax.broadcasted_iota(I32, (cb, tq), 0) < j))

        def jbit(b, j):
            test = j + jnp.left_shift(jnp.int32(1), (seq.bit_length() - 2) - b)
            return jnp.where(eq_below(test) < need, test, j)
        jlast = lax.fori_loop(0, seq.bit_length() - 1, jbit, jnp.zeros((1, tq), I32))

        def demote(c, carry):
            r0 = pl.multiple_of(c * cb, cb)
            blk = sc_ref[pl.ds(r0, cb), :]
            row = r0 + lax.broadcasted_iota(I32, (cb, tq), 0)
            sc_ref[pl.ds(r0, cb), :] = jnp.where(tie & (blk == thr) & (row > jlast), -jnp.inf, blk)
            return carry
        lax.fori_loop(0, n_count, demote, 0)


def _indexer(kidx, qiT, wT, tq, nsel):
    s = kidx.shape[0]
    return pl.pallas_call(
        functools.partial(_indexer_kernel, seq=s, tq=tq, nsel=nsel),
        out_shape=(jax.ShapeDtypeStruct((s, s), F32), jax.ShapeDtypeStruct((1, s), F32)),
        grid=(s // tq,),
        in_specs=[_const_spec((s, IDX_HEAD_DIM)),
                  pl.BlockSpec((IDX_HEADS, IDX_HEAD_DIM, tq), lambda i: (0, 0, i)),
                  pl.BlockSpec((IDX_HEADS, 1, tq), lambda i: (0, 0, i))],
        out_specs=(pl.BlockSpec((s, tq), lambda i: (0, i)),
                   pl.BlockSpec((1, tq), lambda i: (0, i))),
        compiler_params=_cparams(("arbitrary",), 40),
        name="indexer",
    )(kidx, qiT, wT)


def _attn_kernel(qT_ref, k_ref, vT_ref, keys_ref, thr_ref, z_ref, sl_ref, o_ref,
                 acc_ref, m_ref, dm_ref, lg_ref, p_ref, *, tq, tk, nk):
    qi = pl.program_id(0)
    kj = pl.program_id(1)
    last = (qi * tq + tq - 1) // tk

    @pl.when(kj == 0)
    def _init():
        acc_ref[...] = jnp.zeros(acc_ref.shape, F32)
        m_ref[...] = jnp.full(m_ref.shape, M_INIT, F32)

    @pl.when(kj <= last)
    def _compute():
        spos = kj * tk + lax.broadcasted_iota(I32, (tk, tq), 0)
        tpos = qi * tq + lax.broadcasted_iota(I32, (tk, tq), 1)
        sel = (keys_ref[...] >= thr_ref[...]) & (spos <= tpos)
        dm_ref[...] = jnp.where(sel, (tpos - spos).astype(F32), MASK_DIST)

        def group(g, carry):
            def logits(u):
                h = g * HEAD_GROUP + u
                qh = qT_ref[h]
                slope = sl_ref[h]
                part = jnp.full((8, tq), M_INIT, F32)
                for c in range(tk // ATTN_ROWS):
                    rows = pl.ds(c * ATTN_ROWS, ATTN_ROWS)
                    s = jnp.dot(k_ref[h, rows, :], qh, preferred_element_type=F32)
                    lg = s - slope * dm_ref[rows, :]
                    lg_ref[u % 2, rows, :] = lg
                    part = jnp.maximum(part, jnp.max(lg.reshape(ATTN_ROWS // 8, 8, tq), axis=0))
                m_old = m_ref[g, u]
                return m_old, jnp.maximum(m_old, jnp.max(part, axis=0, keepdims=True))

            def probs(u, m_old, m_new):
                for c in range(tk // ATTN_ROWS):
                    rows = pl.ds(c * ATTN_ROWS, ATTN_ROWS)
                    p_ref[u % 2, rows, :] = jnp.exp2(lg_ref[u % 2, rows, :] - m_new).astype(BF16)
                m_ref[g, u] = m_new
                return jnp.exp2(m_old - m_new)

            def values(u, alpha):
                h = g * HEAD_GROUP + u
                acc_ref[g, u] = alpha * acc_ref[g, u] + jnp.dot(vT_ref[h], p_ref[u % 2], preferred_element_type=F32)

            stats = logits(0)
            alpha_prev = None
            for u in range(HEAD_GROUP):
                stats_next = logits(u + 1) if u + 1 < HEAD_GROUP else None
                alpha = probs(u, *stats)
                if u >= 1:
                    values(u - 1, alpha_prev)
                stats, alpha_prev = stats_next, alpha
            values(HEAD_GROUP - 1, alpha_prev)
            return carry
        lax.fori_loop(0, ATTN_HEADS // HEAD_GROUP, group, 0)

    @pl.when(kj == nk - 1)
    def _finish():
        for h in range(ATTN_HEADS):
            g, u = divmod(h, HEAD_GROUP)
            cols = slice(h * ATTN_HEAD_DIM, (h + 1) * ATTN_HEAD_DIM)
            acc = acc_ref[g, u]
            o = (acc[:ATTN_HEAD_DIM] * (1.0 / acc[ATTN_HEAD_DIM:ATTN_HEAD_DIM + 1])).T
            o_ref[:, cols] = (o * _silu(z_ref[:, cols])).astype(BF16)


def _attention(qT, k, vT, keys, thr, proj, slopes, tq, tk):
    s = k.shape[1]
    nk = s // tk
    ng = ATTN_HEADS // HEAD_GROUP

    def kv_blk(qi, kj):
        return jnp.minimum(kj, (qi * tq + tq - 1) // tk)
    return pl.pallas_call(
        functools.partial(_attn_kernel, tq=tq, tk=tk, nk=nk),
        out_shape=jax.ShapeDtypeStruct((s, ATTN_WIDTH), BF16),
        grid=(s // tq, nk),
        in_specs=[pl.BlockSpec((ATTN_HEADS, ATTN_HEAD_DIM, tq), lambda qi, kj: (0, 0, qi)),
                  pl.BlockSpec((ATTN_HEADS, tk, ATTN_HEAD_DIM), lambda qi, kj: (0, kv_blk(qi, kj), 0)),
                  pl.BlockSpec((ATTN_HEADS, V_ROWS, tk), lambda qi, kj: (0, 0, kv_blk(qi, kj))),
                  pl.BlockSpec((tk, tq), lambda qi, kj: (kv_blk(qi, kj), qi)),
                  pl.BlockSpec((1, tq), lambda qi, kj: (0, qi)),
                  pl.BlockSpec((tq, ATTN_WIDTH), lambda qi, kj: (qi, P_OFFSETS["z_attn"] // ATTN_WIDTH)),
                  _const_spec((ATTN_HEADS, 1, tq))],
        out_specs=pl.BlockSpec((tq, ATTN_WIDTH), lambda qi, kj: (qi, 0)),
        scratch_shapes=[pltpu.VMEM((ng, HEAD_GROUP, V_ROWS, tq), F32),
                        pltpu.VMEM((ng, HEAD_GROUP, 1, tq), F32),
                        pltpu.VMEM((tk, tq), F32),
                        pltpu.VMEM((2, tk, tq), F32),
                        pltpu.VMEM((2, tk, tq), BF16)],
        compiler_params=_cparams(("arbitrary", "arbitrary"), 56),
        name="attn",
    )(qT, k, vT, keys, thr, proj, slopes)


def _conv_kernel(x_ref, prev_ref, w_ref, b_ref, q_ref, k_ref, *, kscale):
    i = pl.program_id(0)
    x = x_ref[...]
    prev = jnp.where(i > 0, prev_ref[...], 0.0)
    head = jnp.concatenate([prev, x[:8]], axis=0)
    y = b_ref[...]
    yh = b_ref[...]
    for j in range(CONV_WIDTH):
        d = CONV_WIDTH - 1 - j
        xs = x if d == 0 else pltpu.roll(x, d, 0)
        hs = head if d == 0 else pltpu.roll(head, d, 0)
        y = y + xs * w_ref[j:j + 1, :]
        yh = yh + hs[8:] * w_ref[j:j + 1, :]
    y = jnp.concatenate([yh, y[8:]], axis=0)
    y = _silu(y)
    half = y.shape[1] // 2
    q_ref[...] = y[:, :half].astype(BF16)
    k_ref[...] = (y[:, half:] * kscale).astype(BF16)


def _conv(proj, conv_w, conv_b, tm):
    s = proj.shape[0]
    c = 2 * MLSTM_QK_WIDTH
    cb = P_OFFSETS["qk_m"] // c
    return pl.pallas_call(
        functools.partial(_conv_kernel, kscale=MLSTM_QK_DIM ** -0.5),
        out_shape=(jax.ShapeDtypeStruct((s, MLSTM_QK_WIDTH), BF16), jax.ShapeDtypeStruct((s, MLSTM_QK_WIDTH), BF16)),
        grid=(s // tm,),
        in_specs=[pl.BlockSpec((tm, c), lambda i: (i, cb)),
                  pl.BlockSpec((8, c), lambda i: (jnp.maximum(i * (tm // 8) - 1, 0), cb)),
                  _const_spec((CONV_WIDTH, c)),
                  _const_spec((1, c))],
        out_specs=(pl.BlockSpec((tm, MLSTM_QK_WIDTH), lambda i: (i, 0)),
                   pl.BlockSpec((tm, MLSTM_QK_WIDTH), lambda i: (i, 0))),
        compiler_params=_cparams(("arbitrary",), 40),
        name="conv",
    )(proj, proj, conv_w, conv_b)


def _softcap(x):
    return GATE_SOFTCAP * jnp.tanh(x / GATE_SOFTCAP)


def _mlstm_kernel(q_ref, k_ref, v_ref, og_ref, z_ref, sm_ref, g_ref, out_ref, c_ref, n_ref, m_ref, *, chunk):
    hd = pl.program_id(0)
    ci = pl.program_id(1)
    L = chunk

    @pl.when(ci == 0)
    def _init():
        c_ref[...] = jnp.zeros(c_ref.shape, F32)
        n_ref[...] = jnp.zeros(n_ref.shape, F32)
        m_ref[...] = jnp.zeros(m_ref.shape, F32)

    sm = sm_ref[...]
    lane = lax.broadcasted_iota(I32, sm.shape, 1)
    ig_col = _softcap(jnp.sum(jnp.where(lane == SM_I + hd, sm, 0.0), axis=1, keepdims=True))
    fg_col = _softcap(jnp.sum(jnp.where(lane == SM_F + hd, sm, 0.0), axis=1, keepdims=True))
    logf_col = jnp.minimum(fg_col, 0.0) - jnp.log1p(jnp.exp(-jnp.abs(fg_col)))

    r_i = lax.broadcasted_iota(I32, (L, L), 0)
    c_i = lax.broadcasted_iota(I32, (L, L), 1)
    eye = r_i == c_i
    tril = r_i >= c_i
    logf_row = jnp.sum(jnp.where(eye, logf_col, 0.0), axis=0, keepdims=True)
    ig_row = jnp.sum(jnp.where(eye, ig_col, 0.0), axis=0, keepdims=True)
    b_col = jnp.sum(jnp.where(tril, logf_row, 0.0), axis=1, keepdims=True)
    b_row = jnp.sum(jnp.where(r_i <= c_i, logf_col, 0.0), axis=0, keepdims=True)
    dmat = jnp.where(tril, b_col - b_row + ig_row, -jnp.inf)
    m_prev = m_ref[...]
    m_inter = b_col + m_prev
    m_t = jnp.maximum(m_inter, jnp.max(dmat, axis=1, keepdims=True))

    qc = q_ref[...]
    kc = k_ref[...]
    vc = v_ref[...].astype(BF16)
    nt = (((1,), (1,)), ((), ()))
    s = lax.dot_general(qc, kc, nt, preferred_element_type=F32) * jnp.exp(dmat - m_t)
    inter = jnp.exp(m_inter - m_t)
    num = (jnp.dot(s.astype(BF16), vc, preferred_element_type=F32)
           + inter * jnp.dot(qc, c_ref[...].astype(BF16), preferred_element_type=F32))
    qn = jnp.sum(qc.astype(F32) * n_ref[...], axis=1, keepdims=True)
    den = jnp.sum(s, axis=1, keepdims=True) + inter * qn
    hh = num / jnp.maximum(jnp.abs(den), jnp.exp(-m_t))

    g_last = b_col[L - 1:L, :]
    m_new = m_t[L - 1:L, :]
    wgt = jnp.exp(g_last - b_col + ig_col - m_new)
    decay = jnp.exp(g_last + m_prev - m_new)
    wk = wgt * kc.astype(F32)
    tn = (((0,), (0,)), ((), ()))
    c_ref[...] = decay * c_ref[...] + lax.dot_general(wk.astype(BF16), vc, tn, preferred_element_type=F32)
    n_ref[...] = decay * n_ref[...] + jnp.sum(wk, axis=0, keepdims=True)
    m_ref[...] = m_new

    hn = hh * lax.rsqrt(jnp.mean(hh * hh, axis=-1, keepdims=True) + NORM_EPS) * g_ref[0]
    out_ref[...] = (hn * _sigmoid(og_ref[...]) * _silu(z_ref[...])).astype(BF16)


def _mlstm(qm, km, proj, g_mh3, chunk):
    s = qm.shape[0]
    dk, dv = MLSTM_QK_DIM, MLSTM_V_DIM
    vb, ob, zb = (P_OFFSETS[n] // dv for n in ("v_m", "o_m", "z_m"))
    return pl.pallas_call(
        functools.partial(_mlstm_kernel, chunk=chunk),
        out_shape=jax.ShapeDtypeStruct((s, MLSTM_WIDTH), BF16),
        grid=(MLSTM_HEADS, s // chunk),
        in_specs=[pl.BlockSpec((chunk, dk), lambda h, c: (c, h)),
                  pl.BlockSpec((chunk, dk), lambda h, c: (c, h)),
                  pl.BlockSpec((chunk, dv), lambda h, c: (c, vb + h)),
                  pl.BlockSpec((chunk, dv), lambda h, c: (c, ob + h)),
                  pl.BlockSpec((chunk, dv), lambda h, c: (c, zb + h)),
                  pl.BlockSpec((chunk, SMALL_W), lambda h, c: (c, SMALL_OFF // SMALL_W)),
                  pl.BlockSpec((1, 1, dv), lambda h, c: (h, 0, 0))],
        out_specs=pl.BlockSpec((chunk, dv), lambda h, c: (c, h)),
        scratch_shapes=[pltpu.VMEM((dk, dv), F32), pltpu.VMEM((1, dk), F32), pltpu.VMEM((1, 1), F32)],
        compiler_params=_cparams(("arbitrary", "arbitrary"), 32),
        name="mlstm",
    )(qm, km, proj, proj, proj, proj, g_mh3)


def _merge_kernel(a1_ref, a2_ref, w1_ref, w2_ref, ga_ref, gm_ref, o_ref):
    y1 = jnp.dot(a1_ref[...], w1_ref[...], preferred_element_type=F32)
    y2 = jnp.dot(a2_ref[...], w2_ref[...], preferred_element_type=F32)
    o_ref[...] = (_sigmoid(ga_ref[...]) * y1 + _sigmoid(gm_ref[...]) * y2).astype(BF16)


def _merge(a1, a2, w1, w2, proj):
    s, d = a1.shape
    tm = min(512, s)
    tn = 512
    gab, gmb = P_OFFSETS["g_attn"] // tn, P_OFFSETS["g_mlstm"] // tn
    return pl.pallas_call(
        _merge_kernel,
        out_shape=jax.ShapeDtypeStruct((s, D_MODEL), BF16),
        grid=(s // tm, D_MODEL // tn),
        in_specs=[pl.BlockSpec((tm, d), lambda i, j: (i, 0)),
                  pl.BlockSpec((tm, d), lambda i, j: (i, 0)),
                  pl.BlockSpec((d, tn), lambda i, j: (0, j)),
                  pl.BlockSpec((d, tn), lambda i, j: (0, j)),
                  pl.BlockSpec((tm, tn), lambda i, j: (i, gab + j)),
                  pl.BlockSpec((tm, tn), lambda i, j: (i, gmb + j))],
        out_specs=pl.BlockSpec((tm, tn), lambda i, j: (i, j)),
        compiler_params=_cparams(("arbitrary", "arbitrary"), 48),
        name="merge",
    )(a1, a2, w1, w2, proj, proj)


def _final_kernel(mg_ref, w_ref, x_ref, gate_ref, lg_ref, lb_ref, o_ref, buf_ref, *, tn, nn):
    j = pl.program_id(1)
    buf_ref[j] = jnp.dot(mg_ref[...], w_ref[...], preferred_element_type=F32)

    @pl.when(j == nn - 1)
    def _norm():
        d = nn * tn
        ssum = 0.0
        for jj in range(nn):
            cols = slice(jj * tn, (jj + 1) * tn)
            r = DEEPNORM_ALPHA * x_ref[:, cols] + gate_ref[:, cols] * buf_ref[jj]
            buf_ref[jj] = r
            ssum = ssum + jnp.sum(r, axis=-1, keepdims=True)
        mu = ssum / d
        vsum = 0.0
        for jj in range(nn):
            vsum = vsum + jnp.sum(jnp.square(buf_ref[jj] - mu), axis=-1, keepdims=True)
        inv = lax.rsqrt(vsum / d + NORM_EPS)
        for jj in range(nn):
            cols = slice(jj * tn, (jj + 1) * tn)
            o_ref[:, cols] = (buf_ref[jj] - mu) * inv * lg_ref[:, cols] + lb_ref[:, cols]


def _final(merged, w_out, x2, mod, ln_g, ln_b):
    s, d = x2.shape
    tm = min(512, s)
    tn = 512
    nn = d // tn
    return pl.pallas_call(
        functools.partial(_final_kernel, tn=tn, nn=nn),
        out_shape=jax.ShapeDtypeStruct((s, d), F32),
        grid=(s // tm, nn),
        in_specs=[pl.BlockSpec((tm, d), lambda i, j: (i, 0)),
                  pl.BlockSpec((d, tn), lambda i, j: (0, j)),
                  pl.BlockSpec((tm, d), lambda i, j: (i, 0), pipeline_mode=pl.Buffered(1)),
                  pl.BlockSpec((1, d), lambda i, j: (0, 2)),
                  pl.BlockSpec((1, d), lambda i, j: (0, 0)),
                  pl.BlockSpec((1, d), lambda i, j: (0, 0))],
        out_specs=pl.BlockSpec((tm, d), lambda i, j: (i, 0), pipeline_mode=pl.Buffered(1)),
        scratch_shapes=[pltpu.VMEM((nn, tm, tn), F32)],
        compiler_params=_cparams(("arbitrary", "arbitrary"), 56),
        name="final",
    )(merged, w_out, x2, mod, ln_g, ln_b)


RG_TN = 512
F32_SUBLANES = 8
NARROW_A = ("k_idx", "w_idx")
NARROW_B = ("i_m", "f_m")


def _regroup_kernel(tbl_ref, main_ref, na_ref, nb_ref, o_ref, *, n_a, n_b):
    @pl.when(tbl_ref[pl.program_id(0)] >= 0)
    def _wide():
        o_ref[...] = main_ref[...].astype(BF16)

    @pl.when(tbl_ref[pl.program_id(0)] < 0)
    def _narrow():
        o_ref[:n_a, :] = na_ref[...].astype(BF16)
        o_ref[n_a:n_a + n_b, :] = nb_ref[...].astype(BF16)
        o_ref[n_a + n_b:, :] = jnp.zeros((o_ref.shape[0] - n_a - n_b, o_ref.shape[1]), BF16)


def _regroup_w(w_inT):
    d = w_inT.shape[1]
    starts = []
    for j in range(P_TOTAL // RG_TN):
        oc = j * RG_TN
        if oc >= SMALL_OFF:
            starts.append(-1)
            continue
        seg = next(n for n in P_ORDER if P_OFFSETS[n] <= oc < P_OFFSETS[n] + IN_WIDTH_OF[n])
        start = IN_OFFSETS[seg] + oc - P_OFFSETS[seg]
        assert start % F32_SUBLANES == 0, (seg, start)
        starts.append(start // F32_SUBLANES)
    n_a = sum(IN_WIDTH_OF[n] for n in NARROW_A)
    n_b = sum(IN_WIDTH_OF[n] for n in NARROW_B)
    off_a, off_b = IN_OFFSETS[NARROW_A[0]], IN_OFFSETS[NARROW_B[0]]
    grid_spec = pltpu.PrefetchScalarGridSpec(
        num_scalar_prefetch=1,
        grid=(P_TOTAL // RG_TN,),
        in_specs=[pl.BlockSpec((pl.Element(RG_TN), pl.Element(d)), lambda j, tbl: (jnp.maximum(tbl[j], 0) * F32_SUBLANES, 0)),
                  pl.BlockSpec((pl.Element(n_a), pl.Element(d)), lambda j, tbl: (off_a, 0)),
                  pl.BlockSpec((pl.Element(n_b), pl.Element(d)), lambda j, tbl: (off_b, 0))],
        out_specs=pl.BlockSpec((RG_TN, d), lambda j, tbl: (j, 0)),
    )
    return pl.pallas_call(
        functools.partial(_regroup_kernel, n_a=n_a, n_b=n_b),
        out_shape=jax.ShapeDtypeStruct((P_TOTAL, d), BF16),
        grid_spec=grid_spec,
        compiler_params=_cparams(("arbitrary",), 40),
        name="regroup",
    )(jnp.asarray(starts, I32), w_inT, w_inT, w_inT)


def _regroup_cols(a, pad_to):
    parts = [a[..., IN_OFFSETS[n]:IN_OFFSETS[n] + IN_WIDTH_OF[n]] for n in P_ORDER]
    parts.append(jnp.zeros(a.shape[:-1] + (pad_to - P_USED,), a.dtype))
    return jnp.concatenate(parts, axis=-1)


def _layer(x2, c, w_ada, b_ada, w_in, b_in, g_q, g_kv, w_uq, w_iq, w_uk, w_uv, g_kidx, b_kidx, conv_w, conv_b, g_mh,
           w_attn_out, w_mlstm_out, w_out, ln_g, ln_b):
    s, d = x2.shape
    assert d == D_MODEL and s % 1024 == 0, (s, d)
    tq, tk = 256, 512
    nsel = min(TOPK_MAX, s // 4)

    w_cat = _regroup_w(w_in.T)
    b_cat = _regroup_cols(b_in, P_TOTAL).reshape(1, P_TOTAL)
    w_uqT = w_uq.T.astype(BF16)
    w_iqT = w_iq.T.astype(BF16)
    w_ukT = w_uk.reshape(ATTN_WIDTH, KV_LORA_RANK).T.astype(BF16)
    w_uvT = w_uv.transpose(0, 2, 1).reshape(ATTN_WIDTH, KV_LORA_RANK).astype(BF16)
    slopes = jnp.exp2(-8.0 * jnp.arange(1, ATTN_HEADS + 1, dtype=F32) / ATTN_HEADS)
    slopes = jnp.broadcast_to((slopes * LOG2E)[:, None, None], (ATTN_HEADS, 1, tq))

    mod = _ada(c.reshape(d, 1), w_ada, b_ada.reshape(1, -1))
    u = _modulate(x2, mod)
    proj = _proj(u, w_cat, b_cat)

    qT, qiT = _qpath(proj, g_q.reshape(1, -1), w_uqT, w_iqT, tq)
    k, vT, kidx, widx = _kvpath(proj, g_kv.reshape(1, -1), g_kidx.reshape(1, -1), b_kidx.reshape(1, -1), w_ukT, w_uvT, tq)
    wT = widx.T.reshape(IDX_HEADS, 1, s)
    keys, thr = _indexer(kidx, qiT, wT, tq, nsel)
    a_attn = _attention(qT, k, vT, keys, thr, proj, slopes, tq, tk)

    qm, km = _conv(proj, conv_w, conv_b.reshape(1, -1), tq)
    a_mlstm = _mlstm(qm, km, proj, g_mh.reshape(MLSTM_HEADS, 1, MLSTM_V_DIM), MLSTM_CHUNK)

    merged = _merge(a_attn, a_mlstm, w_attn_out.astype(BF16), w_mlstm_out.astype(BF16), proj)
    return _final(merged, w_out.astype(BF16), x2, mod, ln_g.reshape(1, -1), ln_b.reshape(1, -1))


def kernel(x, c, w_ada, b_ada, w_in, b_in, g_q, g_kv, w_uq, w_iq, w_uk, w_uv, g_kidx, b_kidx, conv_w, conv_b, g_mh,
           w_attn_out, w_mlstm_out, w_out, ln_g, ln_b):
    bsz, seq, d = x.shape
    assert bsz == 1 and w_ada.shape[0] == 1, "single batch, single layer"
    out = _layer(x.reshape(seq, d), c, w_ada[0], b_ada[0], w_in[0], b_in[0], g_q[0], g_kv[0], w_uq[0], w_iq[0],
                 w_uk[0], w_uv[0], g_kidx[0], b_kidx[0], conv_w[0], conv_b[0], g_mh[0], w_attn_out[0],
                 w_mlstm_out[0], w_out[0], ln_g[0], ln_b[0])
    return out.reshape(bsz, seq, d)
```

```python
import functools

import jax
import jax.numpy as jnp
from jax import lax
from jax.experimental import pallas as pl
from jax.experimental.pallas import tpu as pltpu

F32 = jnp.float32
BF16 = jnp.bfloat16
I32 = jnp.int32

D_MODEL = 4096
ATTN_HEADS = 32
ATTN_HEAD_DIM = 128
ATTN_WIDTH = ATTN_HEADS * ATTN_HEAD_DIM
Q_LORA_RANK = 1024
KV_LORA_RANK = 512
IDX_HEADS = 32
IDX_HEAD_DIM = 64
TOPK_MAX = 256
MLSTM_HEADS = 8
MLSTM_QK_DIM = (D_MODEL // 2) // MLSTM_HEADS
MLSTM_V_DIM = D_MODEL // MLSTM_HEADS
MLSTM_QK_WIDTH = MLSTM_HEADS * MLSTM_QK_DIM
MLSTM_WIDTH = MLSTM_HEADS * MLSTM_V_DIM
MLSTM_CHUNK = 256
CONV_WIDTH = 4
GATE_SOFTCAP = 15.0
DEEPNORM_ALPHA = 2.0 ** 0.25
NORM_EPS = 1e-6

IN_WIDTHS = (Q_LORA_RANK, KV_LORA_RANK, IDX_HEAD_DIM, IDX_HEADS, ATTN_WIDTH, 2 * MLSTM_QK_WIDTH, MLSTM_WIDTH,
             MLSTM_WIDTH, MLSTM_HEADS, MLSTM_HEADS, MLSTM_WIDTH, D_MODEL, D_MODEL)
IN_NAMES = ("q_lat", "kv_lat", "k_idx", "w_idx", "z_attn", "qk_m", "v_m", "o_m", "i_m", "f_m", "z_m", "g_attn", "g_mlstm")
IN_OFFSETS = {n: sum(IN_WIDTHS[:i]) for i, n in enumerate(IN_NAMES)}
IN_WIDTH_OF = dict(zip(IN_NAMES, IN_WIDTHS))

P_ORDER = ("z_attn", "qk_m", "v_m", "o_m", "z_m", "g_attn", "g_mlstm", "q_lat", "kv_lat", "k_idx", "w_idx", "i_m", "f_m")
P_OFFSETS = {}
_off = 0
for _n in P_ORDER:
    P_OFFSETS[_n] = _off
    _off += IN_WIDTH_OF[_n]
P_USED = _off
PROJ_TN = 1024
P_TOTAL = -(-P_USED // PROJ_TN) * PROJ_TN
SMALL_W = 128
SMALL_OFF = P_OFFSETS["k_idx"]
SM_WIDX = IDX_HEAD_DIM
SM_I = SM_WIDX + IDX_HEADS
SM_F = SM_I + MLSTM_HEADS

VMEM_CAP_BYTES = 60 * 1024 * 1024

MASK_DIST = 1e30
M_INIT = -1e20
LOG2E = 1.4426950408889634
V_ONES = 16
V_ROWS = ATTN_HEAD_DIM + V_ONES
ATTN_ROWS = 256
HEAD_GROUP = 32
INT_MIN = -2 ** 31
KEY_NEG_INF = INT_MIN + 0x7FFFFF


def _cparams(sem, vmem_mb):
    return pltpu.CompilerParams(dimension_semantics=sem, vmem_limit_bytes=min(vmem_mb * 1024 * 1024, VMEM_CAP_BYTES))


def _sigmoid(x):
    return jax.nn.sigmoid(x)


def _silu(x):
    return x * jax.nn.sigmoid(x)


def _const_spec(shape):
    nd = len(shape)
    return pl.BlockSpec(shape, lambda *_: (0,) * nd, pipeline_mode=pl.Buffered(1))


def _ada_kernel(c_ref, w_ref, b_ref, o_ref):
    c = c_ref[...]
    o_ref[...] = jnp.sum(w_ref[...] * _silu(c), axis=0, keepdims=True) + b_ref[...]


def _ada(c_col, w_ada, b_ada):
    d, n = w_ada.shape
    tn = 512
    return pl.pallas_call(
        _ada_kernel,
        out_shape=jax.ShapeDtypeStruct((1, n), F32),
        grid=(n // tn,),
        in_specs=[pl.BlockSpec((d, 1), lambda j: (0, 0)),
                  pl.BlockSpec((d, tn), lambda j: (0, j)),
                  pl.BlockSpec((1, tn), lambda j: (0, j))],
        out_specs=pl.BlockSpec((1, tn), lambda j: (0, j)),
        compiler_params=_cparams(("arbitrary",), 32),
        name="ada",
    )(c_col, w_ada, b_ada)


def _modulate_kernel(x_ref, shift_ref, scale_ref, u_ref):
    u_ref[...] = (x_ref[...] * (1.0 + scale_ref[...]) + shift_ref[...]).astype(BF16)


def _modulate(x2, mod):
    s, d = x2.shape
    tm = min(512, s)
    return pl.pallas_call(
        _modulate_kernel,
        out_shape=jax.ShapeDtypeStruct((s, d), BF16),
        grid=(s // tm,),
        in_specs=[pl.BlockSpec((tm, d), lambda i: (i, 0)),
                  pl.BlockSpec((1, d), lambda i: (0, 0)),
                  pl.BlockSpec((1, d), lambda i: (0, 1))],
        out_specs=pl.BlockSpec((tm, d), lambda i: (i, 0)),
        compiler_params=_cparams(("arbitrary",), 40),
        name="modulate",
    )(x2, mod, mod)


def _proj_kernel(u_ref, w_ref, b_ref, o_ref):
    nt = (((1,), (1,)), ((), ()))
    o_ref[...] = lax.dot_general(u_ref[...], w_ref[...], nt, preferred_element_type=F32) + b_ref[...]


def _proj(u, w_catT, b_cat):
    s, d = u.shape
    n = w_catT.shape[0]
    tm = min(1024, s)
    tn = PROJ_TN
    return pl.pallas_call(
        _proj_kernel,
        out_shape=jax.ShapeDtypeStruct((s, n), F32),
        grid=(n // tn, s // tm),
        in_specs=[pl.BlockSpec((tm, d), lambda j, i: (i, 0)),
                  pl.BlockSpec((tn, d), lambda j, i: (j, 0)),
                  pl.BlockSpec((1, tn), lambda j, i: (0, j))],
        out_specs=pl.BlockSpec((tm, tn), lambda j, i: (i, j)),
        compiler_params=_cparams(("arbitrary", "arbitrary"), 56),
        name="proj",
    )(u, w_catT, b_cat)


def _qpath_kernel(ql_ref, g_ref, wuq_ref, wiq_ref, qT_ref, qiT_ref, *, scale):
    x = ql_ref[...]
    cq = (x * lax.rsqrt(jnp.mean(x * x, axis=-1, keepdims=True) + NORM_EPS) * g_ref[...]).astype(BF16)
    nt = (((1,), (1,)), ((), ()))
    qT = lax.dot_general(wuq_ref[...], cq, nt, preferred_element_type=F32)
    qT_ref[...] = (qT * scale).reshape(qT_ref.shape).astype(BF16)
    qiT = lax.dot_general(wiq_ref[...], cq, nt, preferred_element_type=F32)
    qiT_ref[...] = qiT.reshape(qiT_ref.shape).astype(BF16)


def _qpath(proj, g_q, w_uqT, w_iqT, tq):
    s = proj.shape[0]
    r = Q_LORA_RANK
    return pl.pallas_call(
        functools.partial(_qpath_kernel, scale=ATTN_HEAD_DIM ** -0.5 * LOG2E),
        out_shape=(jax.ShapeDtypeStruct((ATTN_HEADS, ATTN_HEAD_DIM, s), BF16),
                   jax.ShapeDtypeStruct((IDX_HEADS, IDX_HEAD_DIM, s), BF16)),
        grid=(s // tq,),
        in_specs=[pl.BlockSpec((tq, r), lambda i: (i, P_OFFSETS["q_lat"] // r)),
                  _const_spec((1, r)),
                  _const_spec(w_uqT.shape),
                  _const_spec(w_iqT.shape)],
        out_specs=(pl.BlockSpec((ATTN_HEADS, ATTN_HEAD_DIM, tq), lambda i: (0, 0, i)),
                   pl.BlockSpec((IDX_HEADS, IDX_HEAD_DIM, tq), lambda i: (0, 0, i))),
        compiler_params=_cparams(("arbitrary",), 48),
        name="qpath",
    )(proj, g_q, w_uqT, w_iqT)


def _kvpath_kernel(kvl_ref, sm_ref, gkv_ref, gk_ref, bk_ref, wuk_ref, wuv_ref, k_ref, vT_ref, kidx_ref, widx_ref, *, wscale):
    x = kvl_ref[...]
    ckv = (x * lax.rsqrt(jnp.mean(x * x, axis=-1, keepdims=True) + NORM_EPS) * gkv_ref[...]).astype(BF16)
    kfull = jnp.dot(ckv, wuk_ref[...], preferred_element_type=F32)
    for h in range(ATTN_HEADS):
        k_ref[h] = kfull[:, h * ATTN_HEAD_DIM:(h + 1) * ATTN_HEAD_DIM].astype(BF16)
    nt = (((1,), (1,)), ((), ()))
    vT = lax.dot_general(wuv_ref[...], ckv, nt, preferred_element_type=F32)
    vT_ref[:, :ATTN_HEAD_DIM, :] = vT.reshape(ATTN_HEADS, ATTN_HEAD_DIM, -1).astype(BF16)
    vT_ref[:, ATTN_HEAD_DIM:, :] = jnp.ones((ATTN_HEADS, V_ONES, vT_ref.shape[2]), BF16)
    sm = sm_ref[...]
    ki = sm[:, :IDX_HEAD_DIM]
    mu = jnp.mean(ki, axis=-1, keepdims=True)
    var = jnp.mean(jnp.square(ki - mu), axis=-1, keepdims=True)
    kidx_ref[...] = ((ki - mu) * lax.rsqrt(var + NORM_EPS) * gk_ref[...] + bk_ref[...]).astype(BF16)
    widx_ref[...] = sm[:, SM_WIDX:SM_WIDX + IDX_HEADS] * wscale


def _kvpath(proj, g_kv, g_kidx, b_kidx, w_ukT, w_uvT, tm):
    s = proj.shape[0]
    r = KV_LORA_RANK
    return pl.pallas_call(
        functools.partial(_kvpath_kernel, wscale=IDX_HEADS ** -0.5 * IDX_HEAD_DIM ** -0.5),
        out_shape=(jax.ShapeDtypeStruct((ATTN_HEADS, s, ATTN_HEAD_DIM), BF16),
                   jax.ShapeDtypeStruct((ATTN_HEADS, V_ROWS, s), BF16),
                   jax.ShapeDtypeStruct((s, IDX_HEAD_DIM), BF16),
                   jax.ShapeDtypeStruct((s, IDX_HEADS), F32)),
        grid=(s // tm,),
        in_specs=[pl.BlockSpec((tm, r), lambda i: (i, P_OFFSETS["kv_lat"] // r)),
                  pl.BlockSpec((tm, SMALL_W), lambda i: (i, SMALL_OFF // SMALL_W)),
                  _const_spec((1, r)),
                  _const_spec((1, IDX_HEAD_DIM)),
                  _const_spec((1, IDX_HEAD_DIM)),
                  _const_spec(w_ukT.shape),
                  _const_spec(w_uvT.shape)],
        out_specs=(pl.BlockSpec((ATTN_HEADS, tm, ATTN_HEAD_DIM), lambda i: (0, i, 0)),
                   pl.BlockSpec((ATTN_HEADS, V_ROWS, tm), lambda i: (0, 0, i)),
                   pl.BlockSpec((tm, IDX_HEAD_DIM), lambda i: (i, 0)),
                   pl.BlockSpec((tm, IDX_HEADS), lambda i: (i, 0))),
        compiler_params=_cparams(("arbitrary",), 48),
        name="kvpath",
    )(proj, proj, g_kv, g_kidx, b_kidx, w_ukT, w_uvT)


def _key_to_float(key):
    bits = jnp.where(key >= 0, key, key ^ 0x7FFFFFFF)
    return jnp.where(key < KEY_NEG_INF, -jnp.inf, pltpu.bitcast(bits, F32))


def _indexer_kernel(kidx_ref, qiT_ref, wT_ref, sc_ref, thr_ref, *, seq, tq, nsel):
    i = pl.program_id(0)
    ch = 128
    cb = tq
    n_score = (i + 1) * (tq // ch)
    n_count = i + 1
    tpos = i * tq + lax.broadcasted_iota(I32, (ch, tq), 1)

    def score_chunk(c, carry):
        r0 = pl.multiple_of(c * ch, ch)
        kc = kidx_ref[pl.ds(r0, ch), :]

        acc = jnp.zeros((ch, tq), F32)
        for h in range(IDX_HEADS):
            r = jnp.dot(kc, qiT_ref[h], preferred_element_type=F32)
            acc = acc + jnp.maximum(r, 0.0) * wT_ref[h]
        spos = r0 + lax.broadcasted_iota(I32, (ch, tq), 0)
        sc_ref[pl.ds(r0, ch), :] = jnp.where(spos <= tpos, acc, -jnp.inf)
        return carry
    lax.fori_loop(0, n_score, score_chunk, 0)

    def fill_chunk(c, carry):
        sc_ref[pl.ds(pl.multiple_of(c * cb, cb), cb), :] = jnp.full((cb, tq), -jnp.inf, F32)
        return carry
    lax.fori_loop(n_count, seq // cb, fill_chunk, 0)

    def count(pred):
        def body(c, part):
            r0 = pl.multiple_of(c * cb, cb)
            m = jnp.where(pred(sc_ref[pl.ds(r0, cb), :], r0), 1, 0)
            return part + jnp.sum(m.reshape(cb // 8, 8, tq), axis=0)
        part = lax.fori_loop(0, n_count, body, jnp.zeros((8, tq), I32))
        return jnp.sum(part, axis=0, keepdims=True)

    def count_ge(cand_key):
        cand = _key_to_float(cand_key)
        return count(lambda blk, r0: blk >= cand)

    t0 = jnp.where(count_ge(jnp.zeros((1, tq), I32)) >= nsel, 0, INT_MIN).astype(I32)

    def bit_step(b, t):
        cand = t + jnp.left_shift(jnp.int32(1), 30 - b)
        return jnp.where(count_ge(cand) >= nsel, cand, t)
    thr = _key_to_float(lax.fori_loop(0, 31, bit_step, t0))
    thr_ref[...] = thr

    tie = (count(lambda blk, r0: blk >= thr) > nsel) & (thr > -jnp.inf)

    @pl.when(jnp.max(tie.astype(I32)) > 0)
    def _break_ties():
        need = nsel - count(lambda blk, r0: blk > thr)

        def eq_below(j):
            return count(lambda blk, r0: (blk == thr) & (r0 + lax.broadcasted_iota(I32, (cb, tq), 0) < j))

        def jbit(b, j):
            test = j + jnp.left_shift(jnp.int32(1), (seq.bit_length() - 2) - b)
            return jnp.where(eq_below(test) < need, test, j)
        jlast = lax.fori_loop(0, seq.bit_length() - 1, jbit, jnp.zeros((1, tq), I32))

        def demote(c, carry):
            r0 = pl.multiple_of(c * cb, cb)
            blk = sc_ref[pl.ds(r0, cb), :]
            row = r0 + lax.broadcasted_iota(I32, (cb, tq), 0)
            sc_ref[pl.ds(r0, cb), :] = jnp.where(tie & (blk == thr) & (row > jlast), -jnp.inf, blk)
            return carry
        lax.fori_loop(0, n_count, demote, 0)


def _indexer(kidx, qiT, wT, tq, nsel):
    s = kidx.shape[0]
    return pl.pallas_call(
        functools.partial(_indexer_kernel, seq=s, tq=tq, nsel=nsel),
        out_shape=(jax.ShapeDtypeStruct((s, s), F32), jax.ShapeDtypeStruct((1, s), F32)),
        grid=(s // tq,),
        in_specs=[_const_spec((s, IDX_HEAD_DIM)),
                  pl.BlockSpec((IDX_HEADS, IDX_HEAD_DIM, tq), lambda i: (0, 0, i)),
                  pl.BlockSpec((IDX_HEADS, 1, tq), lambda i: (0, 0, i))],
        out_specs=(pl.BlockSpec((s, tq), lambda i: (0, i)),
                   pl.BlockSpec((1, tq), lambda i: (0, i))),
        compiler_params=_cparams(("arbitrary",), 40),
        name="indexer",
    )(kidx, qiT, wT)


def _attn_kernel(qT_ref, k_ref, vT_ref, keys_ref, thr_ref, z_ref, sl_ref, o_ref,
                 acc_ref, m_ref, dm_ref, lg_ref, p_ref, *, tq, tk, nk):
    qi = pl.program_id(0)
    kj = pl.program_id(1)
    last = (qi * tq + tq - 1) // tk

    @pl.when(kj == 0)
    def _init():
        acc_ref[...] = jnp.zeros(acc_ref.shape, F32)
        m_ref[...] = jnp.full(m_ref.shape, M_INIT, F32)

    @pl.when(kj <= last)
    def _compute():
        spos = kj * tk + lax.broadcasted_iota(I32, (tk, tq), 0)
        tpos = qi * tq + lax.broadcasted_iota(I32, (tk, tq), 1)
        sel = (keys_ref[...] >= thr_ref[...]) & (spos <= tpos)
        dm_ref[...] = jnp.where(sel, (tpos - spos).astype(F32), MASK_DIST)

        def group(g, carry):
            def logits(u):
                h = g * HEAD_GROUP + u
                qh = qT_ref[h]
                slope = sl_ref[h]
                part = jnp.full((8, tq), M_INIT, F32)
                for c in range(tk // ATTN_ROWS):
                    rows = pl.ds(c * ATTN_ROWS, ATTN_ROWS)
                    s = jnp.dot(k_ref[h, rows, :], qh, preferred_element_type=F32)
                    lg = s - slope * dm_ref[rows, :]
                    lg_ref[u % 2, rows, :] = lg
                    part = jnp.maximum(part, jnp.max(lg.reshape(ATTN_ROWS // 8, 8, tq), axis=0))
                m_old = m_ref[g, u]
                return m_old, jnp.maximum(m_old, jnp.max(part, axis=0, keepdims=True))

            def probs(u, m_old, m_new):
                for c in range(tk // ATTN_ROWS):
                    rows = pl.ds(c * ATTN_ROWS, ATTN_ROWS)
                    p_ref[u % 2, rows, :] = jnp.exp2(lg_ref[u % 2, rows, :] - m_new).astype(BF16)
                m_ref[g, u] = m_new
                return jnp.exp2(m_old - m_new)

            def values(u, alpha):
                h = g * HEAD_GROUP + u
                acc_ref[g, u] = alpha * acc_ref[g, u] + jnp.dot(vT_ref[h], p_ref[u % 2], preferred_element_type=F32)

            stats = logits(0)
            alpha_prev = None
            for u in range(HEAD_GROUP):
                stats_next = logits(u + 1) if u + 1 < HEAD_GROUP else None
                alpha = probs(u, *stats)
                if u >= 1:
                    values(u - 1, alpha_prev)
                stats, alpha_prev = stats_next, alpha
            values(HEAD_GROUP - 1, alpha_prev)
            return carry
        lax.fori_loop(0, ATTN_HEADS // HEAD_GROUP, group, 0)

    @pl.when(kj == nk - 1)
    def _finish():
        for h in range(ATTN_HEADS):
            g, u = divmod(h, HEAD_GROUP)
            cols = slice(h * ATTN_HEAD_DIM, (h + 1) * ATTN_HEAD_DIM)
            acc = acc_ref[g, u]
            o = (acc[:ATTN_HEAD_DIM] * (1.0 / acc[ATTN_HEAD_DIM:ATTN_HEAD_DIM + 1])).T
            o_ref[:, cols] = (o * _silu(z_ref[:, cols])).astype(BF16)


def _attention(qT, k, vT, keys, thr, proj, slopes, tq, tk):
    s = k.shape[1]
    nk = s // tk
    ng = ATTN_HEADS // HEAD_GROUP

    def kv_blk(qi, kj):
        return jnp.minimum(kj, (qi * tq + tq - 1) // tk)
    return pl.pallas_call(
        functools.partial(_attn_kernel, tq=tq, tk=tk, nk=nk),
        out_shape=jax.ShapeDtypeStruct((s, ATTN_WIDTH), BF16),
        grid=(s // tq, nk),
        in_specs=[pl.BlockSpec((ATTN_HEADS, ATTN_HEAD_DIM, tq), lambda qi, kj: (0, 0, qi)),
                  pl.BlockSpec((ATTN_HEADS, tk, ATTN_HEAD_DIM), lambda qi, kj: (0, kv_blk(qi, kj), 0)),
                  pl.BlockSpec((ATTN_HEADS, V_ROWS, tk), lambda qi, kj: (0, 0, kv_blk(qi, kj))),
                  pl.BlockSpec((tk, tq), lambda qi, kj: (kv_blk(qi, kj), qi)),
                  pl.BlockSpec((1, tq), lambda qi, kj: (0, qi)),
                  pl.BlockSpec((tq, ATTN_WIDTH), lambda qi, kj: (qi, P_OFFSETS["z_attn"] // ATTN_WIDTH)),
                  _const_spec((ATTN_HEADS, 1, tq))],
        out_specs=pl.BlockSpec((tq, ATTN_WIDTH), lambda qi, kj: (qi, 0)),
        scratch_shapes=[pltpu.VMEM((ng, HEAD_GROUP, V_ROWS, tq), F32),
                        pltpu.VMEM((ng, HEAD_GROUP, 1, tq), F32),
                        pltpu.VMEM((tk, tq), F32),
                        pltpu.VMEM((2, tk, tq), F32),
                        pltpu.VMEM((2, tk, tq), BF16)],
        compiler_params=_cparams(("arbitrary", "arbitrary"), 56),
        name="attn",
    )(qT, k, vT, keys, thr, proj, slopes)


def _conv_kernel(x_ref, prev_ref, w_ref, b_ref, q_ref, k_ref, *, kscale):
    i = pl.program_id(0)
    x = x_ref[...]
    prev = jnp.where(i > 0, prev_ref[...], 0.0)
    head = jnp.concatenate([prev, x[:8]], axis=0)
    y = b_ref[...]
    yh = b_ref[...]
    for j in range(CONV_WIDTH):
        d = CONV_WIDTH - 1 - j
        xs = x if d == 0 else pltpu.roll(x, d, 0)
        hs = head if d == 0 else pltpu.roll(head, d, 0)
        y = y + xs * w_ref[j:j + 1, :]
        yh = yh + hs[8:] * w_ref[j:j + 1, :]
    y = jnp.concatenate([yh, y[8:]], axis=0)
    y = _silu(y)
    half = y.shape[1] // 2
    q_ref[...] = y[:, :half].astype(BF16)
    k_ref[...] = (y[:, half:] * kscale).astype(BF16)


def _conv(proj, conv_w, conv_b, tm):
    s = proj.shape[0]
    c = 2 * MLSTM_QK_WIDTH
    cb = P_OFFSETS["qk_m"] // c
    return pl.pallas_call(
        functools.partial(_conv_kernel, kscale=MLSTM_QK_DIM ** -0.5),
        out_shape=(jax.ShapeDtypeStruct((s, MLSTM_QK_WIDTH), BF16), jax.ShapeDtypeStruct((s, MLSTM_QK_WIDTH), BF16)),
        grid=(s // tm,),
        in_specs=[pl.BlockSpec((tm, c), lambda i: (i, cb)),
                  pl.BlockSpec((8, c), lambda i: (jnp.maximum(i * (tm // 8) - 1, 0), cb)),
                  _const_spec((CONV_WIDTH, c)),
                  _const_spec((1, c))],
        out_specs=(pl.BlockSpec((tm, MLSTM_QK_WIDTH), lambda i: (i, 0)),
                   pl.BlockSpec((tm, MLSTM_QK_WIDTH), lambda i: (i, 0))),
        compiler_params=_cparams(("arbitrary",), 40),
        name="conv",
    )(proj, proj, conv_w, conv_b)


def _softcap(x):
    return GATE_SOFTCAP * jnp.tanh(x / GATE_SOFTCAP)


def _mlstm_kernel(q_ref, k_ref, v_ref, og_ref, z_ref, sm_ref, g_ref, out_ref, c_ref, n_ref, m_ref, *, chunk):
    hd = pl.program_id(0)
    ci = pl.program_id(1)
    L = chunk

    @pl.when(ci == 0)
    def _init():
        c_ref[...] = jnp.zeros(c_ref.shape, F32)
        n_ref[...] = jnp.zeros(n_ref.shape, F32)
        m_ref[...] = jnp.zeros(m_ref.shape, F32)

    sm = sm_ref[...]
    lane = lax.broadcasted_iota(I32, sm.shape, 1)
    ig_col = _softcap(jnp.sum(jnp.where(lane == SM_I + hd, sm, 0.0), axis=1, keepdims=True))
    fg_col = _softcap(jnp.sum(jnp.where(lane == SM_F + hd, sm, 0.0), axis=1, keepdims=True))
    logf_col = jnp.minimum(fg_col, 0.0) - jnp.log1p(jnp.exp(-jnp.abs(fg_col)))

    r_i = lax.broadcasted_iota(I32, (L, L), 0)
    c_i = lax.broadcasted_iota(I32, (L, L), 1)
    eye = r_i == c_i
    tril = r_i >= c_i
    logf_row = jnp.sum(jnp.where(eye, logf_col, 0.0), axis=0, keepdims=True)
    ig_row = jnp.sum(jnp.where(eye, ig_col, 0.0), axis=0, keepdims=True)
    b_col = jnp.sum(jnp.where(tril, logf_row, 0.0), axis=1, keepdims=True)
    b_row = jnp.sum(jnp.where(r_i <= c_i, logf_col, 0.0), axis=0, keepdims=True)
    dmat = jnp.where(tril, b_col - b_row + ig_row, -jnp.inf)
    m_prev = m_ref[...]
    m_inter = b_col + m_prev
    m_t = jnp.maximum(m_inter, jnp.max(dmat, axis=1, keepdims=True))

    qc = q_ref[...]
    kc = k_ref[...]
    vc = v_ref[...].astype(BF16)
    nt = (((1,), (1,)), ((), ()))
    s = lax.dot_general(qc, kc, nt, preferred_element_type=F32) * jnp.exp(dmat - m_t)
    inter = jnp.exp(m_inter - m_t)
    num = (jnp.dot(s.astype(BF16), vc, preferred_element_type=F32)
           + inter * jnp.dot(qc, c_ref[...].astype(BF16), preferred_element_type=F32))
    qn = jnp.sum(qc.astype(F32) * n_ref[...], axis=1, keepdims=True)
    den = jnp.sum(s, axis=1, keepdims=True) + inter * qn
    hh = num / jnp.maximum(jnp.abs(den), jnp.exp(-m_t))

    g_last = b_col[L - 1:L, :]
    m_new = m_t[L - 1:L, :]
    wgt = jnp.exp(g_last - b_col + ig_col - m_new)
    decay = jnp.exp(g_last + m_prev - m_new)
    wk = wgt * kc.astype(F32)
    tn = (((0,), (0,)), ((), ()))
    c_ref[...] = decay * c_ref[...] + lax.dot_general(wk.astype(BF16), vc, tn, preferred_element_type=F32)
    n_ref[...] = decay * n_ref[...] + jnp.sum(wk, axis=0, keepdims=True)
    m_ref[...] = m_new

    hn = hh * lax.rsqrt(jnp.mean(hh * hh, axis=-1, keepdims=True) + NORM_EPS) * g_ref[0]
    out_ref[...] = (hn * _sigmoid(og_ref[...]) * _silu(z_ref[...])).astype(BF16)


def _mlstm(qm, km, proj, g_mh3, chunk):
    s = qm.shape[0]
    dk, dv = MLSTM_QK_DIM, MLSTM_V_DIM
    vb, ob, zb = (P_OFFSETS[n] // dv for n in ("v_m", "o_m", "z_m"))
    return pl.pallas_call(
        functools.partial(_mlstm_kernel, chunk=chunk),
        out_shape=jax.ShapeDtypeStruct((s, MLSTM_WIDTH), BF16),
        grid=(MLSTM_HEADS, s // chunk),
        in_specs=[pl.BlockSpec((chunk, dk), lambda h, c: (c, h)),
                  pl.BlockSpec((chunk, dk), lambda h, c: (c, h)),
                  pl.BlockSpec((chunk, dv), lambda h, c: (c, vb + h)),
                  pl.BlockSpec((chunk, dv), lambda h, c: (c, ob + h)),
                  pl.BlockSpec((chunk, dv), lambda h, c: (c, zb + h)),
                  pl.BlockSpec((chunk, SMALL_W), lambda h, c: (c, SMALL_OFF // SMALL_W)),
                  pl.BlockSpec((1, 1, dv), lambda h, c: (h, 0, 0))],
        out_specs=pl.BlockSpec((chunk, dv), lambda h, c: (c, h)),
        scratch_shapes=[pltpu.VMEM((dk, dv), F32), pltpu.VMEM((1, dk), F32), pltpu.VMEM((1, 1), F32)],
        compiler_params=_cparams(("arbitrary", "arbitrary"), 32),
        name="mlstm",
    )(qm, km, proj, proj, proj, proj, g_mh3)


def _merge_kernel(a1_ref, a2_ref, w1_ref, w2_ref, ga_ref, gm_ref, o_ref):
    y1 = jnp.dot(a1_ref[...], w1_ref[...], preferred_element_type=F32)
    y2 = jnp.dot(a2_ref[...], w2_ref[...], preferred_element_type=F32)
    o_ref[...] = (_sigmoid(ga_ref[...]) * y1 + _sigmoid(gm_ref[...]) * y2).astype(BF16)


def _merge(a1, a2, w1, w2, proj):
    s, d = a1.shape
    tm = min(512, s)
    tn = 512
    gab, gmb = P_OFFSETS["g_attn"] // tn, P_OFFSETS["g_mlstm"] // tn
    return pl.pallas_call(
        _merge_kernel,
        out_shape=jax.ShapeDtypeStruct((s, D_MODEL), BF16),
        grid=(s // tm, D_MODEL // tn),
        in_specs=[pl.BlockSpec((tm, d), lambda i, j: (i, 0)),
                  pl.BlockSpec((tm, d), lambda i, j: (i, 0)),
                  pl.BlockSpec((d, tn), lambda i, j: (0, j)),
                  pl.BlockSpec((d, tn), lambda i, j: (0, j)),
                  pl.BlockSpec((tm, tn), lambda i, j: (i, gab + j)),
                  pl.BlockSpec((tm, tn), lambda i, j: (i, gmb + j))],
        out_specs=pl.BlockSpec((tm, tn), lambda i, j: (i, j)),
        compiler_params=_cparams(("arbitrary", "arbitrary"), 48),
        name="merge",
    )(a1, a2, w1, w2, proj, proj)


def _final_kernel(mg_ref, w_ref, x_ref, gate_ref, lg_ref, lb_ref, o_ref, buf_ref, *, tn, nn):
    j = pl.program_id(1)
    buf_ref[j] = jnp.dot(mg_ref[...], w_ref[...], preferred_element_type=F32)

    @pl.when(j == nn - 1)
    def _norm():
        d = nn * tn
        ssum = 0.0
        for jj in range(nn):
            cols = slice(jj * tn, (jj + 1) * tn)
            r = DEEPNORM_ALPHA * x_ref[:, cols] + gate_ref[:, cols] * buf_ref[jj]
            buf_ref[jj] = r
            ssum = ssum + jnp.sum(r, axis=-1, keepdims=True)
        mu = ssum / d
        vsum = 0.0
        for jj in range(nn):
            vsum = vsum + jnp.sum(jnp.square(buf_ref[jj] - mu), axis=-1, keepdims=True)
        inv = lax.rsqrt(vsum / d + NORM_EPS)
        for jj in range(nn):
            cols = slice(jj * tn, (jj + 1) * tn)
            o_ref[:, cols] = (buf_ref[jj] - mu) * inv * lg_ref[:, cols] + lb_ref[:, cols]


def _final(merged, w_out, x2, mod, ln_g, ln_b):
    s, d = x2.shape
    tm = min(512, s)
    tn = 512
    nn = d // tn
    return pl.pallas_call(
        functools.partial(_final_kernel, tn=tn, nn=nn),
        out_shape=jax.ShapeDtypeStruct((s, d), F32),
        grid=(s // tm, nn),
        in_specs=[pl.BlockSpec((tm, d), lambda i, j: (i, 0)),
                  pl.BlockSpec((d, tn), lambda i, j: (0, j)),
                  pl.BlockSpec((tm, d), lambda i, j: (i, 0), pipeline_mode=pl.Buffered(1)),
                  pl.BlockSpec((1, d), lambda i, j: (0, 2)),
                  pl.BlockSpec((1, d), lambda i, j: (0, 0)),
                  pl.BlockSpec((1, d), lambda i, j: (0, 0))],
        out_specs=pl.BlockSpec((tm, d), lambda i, j: (i, 0), pipeline_mode=pl.Buffered(1)),
        scratch_shapes=[pltpu.VMEM((nn, tm, tn), F32)],
        compiler_params=_cparams(("arbitrary", "arbitrary"), 56),
        name="final",
    )(merged, w_out, x2, mod, ln_g, ln_b)


RG_TN = 512
F32_SUBLANES = 8
NARROW_A = ("k_idx", "w_idx")
NARROW_B = ("i_m", "f_m")


def _regroup_kernel(tbl_ref, main_ref, na_ref, nb_ref, o_ref, *, n_a, n_b):
    @pl.when(tbl_ref[pl.program_id(0)] >= 0)
    def _wide():
        o_ref[...] = main_ref[...].astype(BF16)

    @pl.when(tbl_ref[pl.program_id(0)] < 0)
    def _narrow():
        o_ref[:n_a, :] = na_ref[...].astype(BF16)
        o_ref[n_a:n_a + n_b, :] = nb_ref[...].astype(BF16)
        o_ref[n_a + n_b:, :] = jnp.zeros((o_ref.shape[0] - n_a - n_b, o_ref.shape[1]), BF16)


def _regroup_w(w_inT):
    d = w_inT.shape[1]
    starts = []
    for j in range(P_TOTAL // RG_TN):
        oc = j * RG_TN
        if oc >= SMALL_OFF:
            starts.append(-1)
            continue
        seg = next(n for n in P_ORDER if P_OFFSETS[n] <= oc < P_OFFSETS[n] + IN_WIDTH_OF[n])
        start = IN_OFFSETS[seg] + oc - P_OFFSETS[seg]
        assert start % F32_SUBLANES == 0, (seg, start)
        starts.append(start // F32_SUBLANES)
    n_a = sum(IN_WIDTH_OF[n] for n in NARROW_A)
    n_b = sum(IN_WIDTH_OF[n] for n in NARROW_B)
    off_a, off_b = IN_OFFSETS[NARROW_A[0]], IN_OFFSETS[NARROW_B[0]]
    grid_spec = pltpu.PrefetchScalarGridSpec(
        num_scalar_prefetch=1,
        grid=(P_TOTAL // RG_TN,),
        in_specs=[pl.BlockSpec((pl.Element(RG_TN), pl.Element(d)), lambda j, tbl: (jnp.maximum(tbl[j], 0) * F32_SUBLANES, 0)),
                  pl.BlockSpec((pl.Element(n_a), pl.Element(d)), lambda j, tbl: (off_a, 0)),
                  pl.BlockSpec((pl.Element(n_b), pl.Element(d)), lambda j, tbl: (off_b, 0))],
        out_specs=pl.BlockSpec((RG_TN, d), lambda j, tbl: (j, 0)),
    )
    return pl.pallas_call(
        functools.partial(_regroup_kernel, n_a=n_a, n_b=n_b),
        out_shape=jax.ShapeDtypeStruct((P_TOTAL, d), BF16),
        grid_spec=grid_spec,
        compiler_params=_cparams(("arbitrary",), 40),
        name="regroup",
    )(jnp.asarray(starts, I32), w_inT, w_inT, w_inT)


def _regroup_cols(a, pad_to):
    parts = [a[..., IN_OFFSETS[n]:IN_OFFSETS[n] + IN_WIDTH_OF[n]] for n in P_ORDER]
    parts.append(jnp.zeros(a.shape[:-1] + (pad_to - P_USED,), a.dtype))
    return jnp.concatenate(parts, axis=-1)


def _layer(x2, c, w_ada, b_ada, w_in, b_in, g_q, g_kv, w_uq, w_iq, w_uk, w_uv, g_kidx, b_kidx, conv_w, conv_b, g_mh,
           w_attn_out, w_mlstm_out, w_out, ln_g, ln_b):
    s, d = x2.shape
    assert d == D_MODEL and s % 1024 == 0, (s, d)
    tq, tk = 256, 512
    nsel = min(TOPK_MAX, s // 4)

    w_cat = _regroup_w(w_in.T)
    b_cat = _regroup_cols(b_in, P_TOTAL).reshape(1, P_TOTAL)
    w_uqT = w_uq.T.astype(BF16)
    w_iqT = w_iq.T.astype(BF16)
    w_ukT = w_uk.reshape(ATTN_WIDTH, KV_LORA_RANK).T.astype(BF16)
    w_uvT = w_uv.transpose(0, 2, 1).reshape(ATTN_WIDTH, KV_LORA_RANK).astype(BF16)
    slopes = jnp.exp2(-8.0 * jnp.arange(1, ATTN_HEADS + 1, dtype=F32) / ATTN_HEADS)
    slopes = jnp.broadcast_to((slopes * LOG2E)[:, None, None], (ATTN_HEADS, 1, tq))

    mod = _ada(c.reshape(d, 1), w_ada, b_ada.reshape(1, -1))
    u = _modulate(x2, mod)
    proj = _proj(u, w_cat, b_cat)

    qT, qiT = _qpath(proj, g_q.reshape(1, -1), w_uqT, w_iqT, tq)
    k, vT, kidx, widx = _kvpath(proj, g_kv.reshape(1, -1), g_kidx.reshape(1, -1), b_kidx.reshape(1, -1), w_ukT, w_uvT, tq)
    wT = widx.T.reshape(IDX_HEADS, 1, s)
    keys, thr = _indexer(kidx, qiT, wT, tq, nsel)
    a_attn = _attention(qT, k, vT, keys, thr, proj, slopes, tq, tk)

    qm, km = _conv(proj, conv_w, conv_b.reshape(1, -1), tq)
    a_mlstm = _mlstm(qm, km, proj, g_mh.reshape(MLSTM_HEADS, 1, MLSTM_V_DIM), MLSTM_CHUNK)

    merged = _merge(a_attn, a_mlstm, w_attn_out.astype(BF16), w_mlstm_out.astype(BF16), proj)
    return _final(merged, w_out.astype(BF16), x2, mod, ln_g.reshape(1, -1), ln_b.reshape(1, -1))


def kernel(x, c, w_ada, b_ada, w_in, b_in, g_q, g_kv, w_uq, w_iq, w_uk, w_uv, g_kidx, b_kidx, conv_w, conv_b, g_mh,
           w_attn_out, w_mlstm_out, w_out, ln_g, ln_b):
    bsz, seq, d = x.shape
    assert bsz == 1 and w_ada.shape[0] == 1, "single batch, single layer"
    out = _layer(x.reshape(seq, d), c, w_ada[0], b_ada[0], w_in[0], b_in[0], g_q[0], g_kv[0], w_uq[0], w_iq[0],
                 w_uk[0], w_uv[0], g_kidx[0], b_kidx[0], conv_w[0], conv_b[0], g_mh[0], w_attn_out[0],
                 w_mlstm_out[0], w_out[0], ln_g[0], ln_b[0])
    return out.reshape(bsz, seq, d)
```

```python
import functools

import jax
import jax.numpy as jnp
from jax import lax
from jax.experimental import pallas as pl
from jax.experimental.pallas import tpu as pltpu

F32 = jnp.float32
BF16 = jnp.bfloat16
I32 = jnp.int32

D_MODEL = 4096
ATTN_HEADS = 32
ATTN_HEAD_DIM = 128
ATTN_WIDTH = ATTN_HEADS * ATTN_HEAD_DIM
Q_LORA_RANK = 1024
KV_LORA_RANK = 512
IDX_HEADS = 32
IDX_HEAD_DIM = 64
TOPK_MAX = 256
MLSTM_HEADS = 8
MLSTM_QK_DIM = (D_MODEL // 2) // MLSTM_HEADS
MLSTM_V_DIM = D_MODEL // MLSTM_HEADS
MLSTM_QK_WIDTH = MLSTM_HEADS * MLSTM_QK_DIM
MLSTM_WIDTH = MLSTM_HEADS * MLSTM_V_DIM
MLSTM_CHUNK = 256
MLSTM_GROUP = 2
CONV_WIDTH = 4
GATE_SOFTCAP = 15.0
DEEPNORM_ALPHA = 2.0 ** 0.25
NORM_EPS = 1e-6

IN_WIDTHS = (Q_LORA_RANK, KV_LORA_RANK, IDX_HEAD_DIM, IDX_HEADS, ATTN_WIDTH, 2 * MLSTM_QK_WIDTH, MLSTM_WIDTH,
             MLSTM_WIDTH, MLSTM_HEADS, MLSTM_HEADS, MLSTM_WIDTH, D_MODEL, D_MODEL)
IN_NAMES = ("q_lat", "kv_lat", "k_idx", "w_idx", "z_attn", "qk_m", "v_m", "o_m", "i_m", "f_m", "z_m", "g_attn", "g_mlstm")
IN_OFFSETS = {n: sum(IN_WIDTHS[:i]) for i, n in enumerate(IN_NAMES)}
IN_WIDTH_OF = dict(zip(IN_NAMES, IN_WIDTHS))

P_ORDER = ("z_attn", "qk_m", "v_m", "o_m", "z_m", "g_attn", "g_mlstm", "q_lat", "kv_lat", "k_idx", "w_idx", "i_m", "f_m")
P_OFFSETS = {}
_off = 0
for _n in P_ORDER:
    P_OFFSETS[_n] = _off
    _off += IN_WIDTH_OF[_n]
P_USED = _off
PROJ_TN = 1024
P_TOTAL = -(-P_USED // PROJ_TN) * PROJ_TN
SMALL_W = 128
SMALL_OFF = P_OFFSETS["k_idx"]
SM_WIDX = IDX_HEAD_DIM
SM_I = SM_WIDX + IDX_HEADS
SM_F = SM_I + MLSTM_HEADS

VMEM_CAP_BYTES = 60 * 1024 * 1024

MASK_BIAS = -1e30
M_INIT = -1e20
LOG2E = 1.4426950408889634
V_ONES = 16
V_ROWS = ATTN_HEAD_DIM + V_ONES
ATTN_ROWS = 256
HEAD_GROUP = 32
INT_MIN = -2 ** 31
KEY_NEG_INF = INT_MIN + 0x7FFFFF


def _cparams(sem, vmem_mb):
    return pltpu.CompilerParams(dimension_semantics=sem, vmem_limit_bytes=min(vmem_mb * 1024 * 1024, VMEM_CAP_BYTES))


def _sigmoid(x):
    return jax.nn.sigmoid(x)


def _silu(x):
    return x * jax.nn.sigmoid(x)


def _const_spec(shape):
    nd = len(shape)
    return pl.BlockSpec(shape, lambda *_: (0,) * nd, pipeline_mode=pl.Buffered(1))


def _ada_kernel(c_ref, w_ref, b_ref, o_ref):
    c = c_ref[...]
    o_ref[...] = jnp.sum(w_ref[...] * _silu(c), axis=0, keepdims=True) + b_ref[...]


def _ada(c_col, w_ada, b_ada):
    d, n = w_ada.shape
    tn = 512
    return pl.pallas_call(
        _ada_kernel,
        out_shape=jax.ShapeDtypeStruct((1, n), F32),
        grid=(n // tn,),
        in_specs=[pl.BlockSpec((d, 1), lambda j: (0, 0)),
                  pl.BlockSpec((d, tn), lambda j: (0, j)),
                  pl.BlockSpec((1, tn), lambda j: (0, j))],
        out_specs=pl.BlockSpec((1, tn), lambda j: (0, j)),
        compiler_params=_cparams(("arbitrary",), 32),
        name="ada",
    )(c_col, w_ada, b_ada)


def _modulate_kernel(x_ref, shift_ref, scale_ref, u_ref):
    u_ref[...] = (x_ref[...] * (1.0 + scale_ref[...]) + shift_ref[...]).astype(BF16)


def _modulate(x2, mod):
    s, d = x2.shape
    tm = min(512, s)
    return pl.pallas_call(
        _modulate_kernel,
        out_shape=jax.ShapeDtypeStruct((s, d), BF16),
        grid=(s // tm,),
        in_specs=[pl.BlockSpec((tm, d), lambda i: (i, 0)),
                  pl.BlockSpec((1, d), lambda i: (0, 0)),
                  pl.BlockSpec((1, d), lambda i: (0, 1))],
        out_specs=pl.BlockSpec((tm, d), lambda i: (i, 0)),
        compiler_params=_cparams(("arbitrary",), 40),
        name="modulate",
    )(x2, mod, mod)


def _proj_kernel(u_ref, w_ref, b_ref, o_ref):
    nt = (((1,), (1,)), ((), ()))
    o_ref[...] = lax.dot_general(u_ref[...], w_ref[...], nt, preferred_element_type=F32) + b_ref[...]


def _proj(u, w_catT, b_cat):
    s, d = u.shape
    n = w_catT.shape[0]
    tm = min(1024, s)
    tn = PROJ_TN
    return pl.pallas_call(
        _proj_kernel,
        out_shape=jax.ShapeDtypeStruct((s, n), F32),
        grid=(n // tn, s // tm),
        in_specs=[pl.BlockSpec((tm, d), lambda j, i: (i, 0)),
                  pl.BlockSpec((tn, d), lambda j, i: (j, 0)),
                  pl.BlockSpec((1, tn), lambda j, i: (0, j))],
        out_specs=pl.BlockSpec((tm, tn), lambda j, i: (i, j)),
        compiler_params=_cparams(("arbitrary", "arbitrary"), 56),
        name="proj",
    )(u, w_catT, b_cat)


def _qpath_kernel(ql_ref, g_ref, wuq_ref, wiq_ref, qT_ref, qiT_ref, *, scale):
    x = ql_ref[...]
    cq = (x * lax.rsqrt(jnp.mean(x * x, axis=-1, keepdims=True) + NORM_EPS) * g_ref[...]).astype(BF16)
    nt = (((1,), (1,)), ((), ()))
    qT = lax.dot_general(wuq_ref[...], cq, nt, preferred_element_type=F32)
    qT_ref[...] = (qT * scale).reshape(qT_ref.shape).astype(BF16)
    qiT = lax.dot_general(wiq_ref[...], cq, nt, preferred_element_type=F32)
    qiT_ref[...] = qiT.reshape(qiT_ref.shape).astype(BF16)


def _qpath(proj, g_q, w_uqT, w_iqT, tq):
    s = proj.shape[0]
    r = Q_LORA_RANK
    return pl.pallas_call(
        functools.partial(_qpath_kernel, scale=ATTN_HEAD_DIM ** -0.5 * LOG2E),
        out_shape=(jax.ShapeDtypeStruct((ATTN_HEADS, ATTN_HEAD_DIM, s), BF16),
                   jax.ShapeDtypeStruct((IDX_HEADS, IDX_HEAD_DIM, s), BF16)),
        grid=(s // tq,),
        in_specs=[pl.BlockSpec((tq, r), lambda i: (i, P_OFFSETS["q_lat"] // r)),
                  _const_spec((1, r)),
                  _const_spec(w_uqT.shape),
                  _const_spec(w_iqT.shape)],
        out_specs=(pl.BlockSpec((ATTN_HEADS, ATTN_HEAD_DIM, tq), lambda i: (0, 0, i)),
                   pl.BlockSpec((IDX_HEADS, IDX_HEAD_DIM, tq), lambda i: (0, 0, i))),
        compiler_params=_cparams(("arbitrary",), 48),
        name="qpath",
    )(proj, g_q, w_uqT, w_iqT)


def _kvpath_kernel(kvl_ref, sm_ref, gkv_ref, gk_ref, bk_ref, wuk_ref, wuv_ref, k_ref, vT_ref, kidx_ref, widx_ref, *, wscale):
    x = kvl_ref[...]
    ckv = (x * lax.rsqrt(jnp.mean(x * x, axis=-1, keepdims=True) + NORM_EPS) * gkv_ref[...]).astype(BF16)
    kfull = jnp.dot(ckv, wuk_ref[...], preferred_element_type=F32)
    for h in range(ATTN_HEADS):
        k_ref[h] = kfull[:, h * ATTN_HEAD_DIM:(h + 1) * ATTN_HEAD_DIM].astype(BF16)
    nt = (((1,), (1,)), ((), ()))
    vT = lax.dot_general(wuv_ref[...], ckv, nt, preferred_element_type=F32)
    vT_ref[:, :ATTN_HEAD_DIM, :] = vT.reshape(ATTN_HEADS, ATTN_HEAD_DIM, -1).astype(BF16)
    vT_ref[:, ATTN_HEAD_DIM:, :] = jnp.ones((ATTN_HEADS, V_ONES, vT_ref.shape[2]), BF16)
    sm = sm_ref[...]
    ki = sm[:, :IDX_HEAD_DIM]
    mu = jnp.mean(ki, axis=-1, keepdims=True)
    var = jnp.mean(jnp.square(ki - mu), axis=-1, keepdims=True)
    kidx_ref[...] = ((ki - mu) * lax.rsqrt(var + NORM_EPS) * gk_ref[...] + bk_ref[...]).astype(BF16)
    widx_ref[...] = sm[:, SM_WIDX:SM_WIDX + IDX_HEADS] * wscale


def _kvpath(proj, g_kv, g_kidx, b_kidx, w_ukT, w_uvT, tm):
    s = proj.shape[0]
    r = KV_LORA_RANK
    return pl.pallas_call(
        functools.partial(_kvpath_kernel, wscale=IDX_HEADS ** -0.5 * IDX_HEAD_DIM ** -0.5),
        out_shape=(jax.ShapeDtypeStruct((ATTN_HEADS, s, ATTN_HEAD_DIM), BF16),
                   jax.ShapeDtypeStruct((ATTN_HEADS, V_ROWS, s), BF16),
                   jax.ShapeDtypeStruct((s, IDX_HEAD_DIM), BF16),
                   jax.ShapeDtypeStruct((s, IDX_HEADS), F32)),
        grid=(s // tm,),
        in_specs=[pl.BlockSpec((tm, r), lambda i: (i, P_OFFSETS["kv_lat"] // r)),
                  pl.BlockSpec((tm, SMALL_W), lambda i: (i, SMALL_OFF // SMALL_W)),
                  _const_spec((1, r)),
                  _const_spec((1, IDX_HEAD_DIM)),
                  _const_spec((1, IDX_HEAD_DIM)),
                  _const_spec(w_ukT.shape),
                  _const_spec(w_uvT.shape)],
        out_specs=(pl.BlockSpec((ATTN_HEADS, tm, ATTN_HEAD_DIM), lambda i: (0, i, 0)),
                   pl.BlockSpec((ATTN_HEADS, V_ROWS, tm), lambda i: (0, 0, i)),
                   pl.BlockSpec((tm, IDX_HEAD_DIM), lambda i: (i, 0)),
                   pl.BlockSpec((tm, IDX_HEADS), lambda i: (i, 0))),
        compiler_params=_cparams(("arbitrary",), 48),
        name="kvpath",
    )(proj, proj, g_kv, g_kidx, b_kidx, w_ukT, w_uvT)


def _key_to_float(key):
    bits = jnp.where(key >= 0, key, key ^ 0x7FFFFFFF)
    return jnp.where(key < KEY_NEG_INF, -jnp.inf, pltpu.bitcast(bits, F32))


def _indexer_kernel(kidx_ref, qiT_ref, wT_ref, sc_ref, thr_ref, *, seq, tq, nsel):
    i = pl.program_id(0)
    ch = 128
    cb = tq
    n_score = (i + 1) * (tq // ch)
    n_count = i + 1
    tpos = i * tq + lax.broadcasted_iota(I32, (ch, tq), 1)

    def score_chunk(c, carry):
        r0 = pl.multiple_of(c * ch, ch)
        kc = kidx_ref[pl.ds(r0, ch), :]

        acc = jnp.zeros((ch, tq), F32)
        for h in range(IDX_HEADS):
            r = jnp.dot(kc, qiT_ref[h], preferred_element_type=F32)
            acc = acc + jnp.maximum(r, 0.0) * wT_ref[h]
        spos = r0 + lax.broadcasted_iota(I32, (ch, tq), 0)
        sc_ref[pl.ds(r0, ch), :] = jnp.where(spos <= tpos, acc, -jnp.inf)
        return carry
    lax.fori_loop(0, n_score, score_chunk, 0)

    def fill_chunk(c, carry):
        sc_ref[pl.ds(pl.multiple_of(c * cb, cb), cb), :] = jnp.full((cb, tq), -jnp.inf, F32)
        return carry
    lax.fori_loop(n_count, seq // cb, fill_chunk, 0)

    def count(pred):
        def body(c, part):
            r0 = pl.multiple_of(c * cb, cb)
            m = jnp.where(pred(sc_ref[pl.ds(r0, cb), :], r0), 1, 0)
            return part + jnp.sum(m.reshape(cb // 8, 8, tq), axis=0)
        part = lax.fori_loop(0, n_count, body, jnp.zeros((8, tq), I32))
        return jnp.sum(part, axis=0, keepdims=True)

    def count_ge(cand_key):
        cand = _key_to_float(cand_key)
        return count(lambda blk, r0: blk >= cand)

    t0 = jnp.where(count_ge(jnp.zeros((1, tq), I32)) >= nsel, 0, INT_MIN).astype(I32)

    def bit_step(b, t):
        cand = t + jnp.left_shift(jnp.int32(1), 30 - b)
        return jnp.where(count_ge(cand) >= nsel, cand, t)
    thr = _key_to_float(lax.fori_loop(0, 31, bit_step, t0))
    thr_ref[...] = thr

    tie = (count(lambda blk, r0: blk >= thr) > nsel) & (thr > -jnp.inf)

    @pl.when(jnp.max(tie.astype(I32)) > 0)
    def _break_ties():
        need = nsel - count(lambda blk, r0: blk > thr)

        def eq_below(j):
            return count(lambda blk, r0: (blk == thr) & (r0 + lax.broadcasted_iota(I32, (cb, tq), 0) < j))

        def jbit(b, j):
            test = j + jnp.left_shift(jnp.int32(1), (seq.bit_length() - 2) - b)
            return jnp.where(eq_below(test) < need, test, j)
        jlast = lax.fori_loop(0, seq.bit_length() - 1, jbit, jnp.zeros((1, tq), I32))

        def demote(c, carry):
            r0 = pl.multiple_of(c * cb, cb)
            blk = sc_ref[pl.ds(r0, cb), :]
            row = r0 + lax.broadcasted_iota(I32, (cb, tq), 0)
            sc_ref[pl.ds(r0, cb), :] = jnp.where(tie & (blk == thr) & (row > jlast), -jnp.inf, blk)
            return carry
        lax.fori_loop(0, n_count, demote, 0)


def _indexer(kidx, qiT, wT, tq, nsel):
    s = kidx.shape[0]
    return pl.pallas_call(
        functools.partial(_indexer_kernel, seq=s, tq=tq, nsel=nsel),
        out_shape=(jax.ShapeDtypeStruct((s, s), F32), jax.ShapeDtypeStruct((1, s), F32)),
        grid=(s // tq,),
        in_specs=[_const_spec((s, IDX_HEAD_DIM)),
                  pl.BlockSpec((IDX_HEADS, IDX_HEAD_DIM, tq), lambda i: (0, 0, i)),
                  pl.BlockSpec((IDX_HEADS, 1, tq), lambda i: (0, 0, i))],
        out_specs=(pl.BlockSpec((s, tq), lambda i: (0, i)),
                   pl.BlockSpec((1, tq), lambda i: (0, i))),
        compiler_params=_cparams(("arbitrary",), 40),
        name="indexer",
    )(kidx, qiT, wT)


def _attn_kernel(qT_ref, k_ref, vT_ref, keys_ref, thr_ref, z_ref, sl_ref, kf_ref, qf_ref, o_ref,
                 acc_ref, m_ref, mb_ref, lg_ref, p_ref, *, tq, tk, nk):
    qi = pl.program_id(0)
    kj = pl.program_id(1)
    last = (qi * tq + tq - 1) // tk

    @pl.when(kj == 0)
    def _init():
        acc_ref[...] = jnp.zeros(acc_ref.shape, F32)
        m_ref[...] = jnp.full(m_ref.shape, M_INIT, F32)

    @pl.when(kj <= last)
    def _compute():
        spos = kj * tk + lax.broadcasted_iota(I32, (tk, tq), 0)
        tpos = qi * tq + lax.broadcasted_iota(I32, (tk, tq), 1)
        sel = (keys_ref[...] >= thr_ref[...]) & (spos <= tpos)
        mb_ref[...] = jnp.where(sel, 0.0, MASK_BIAS)
        tile_off = (kj * tk - qi * tq).astype(F32)

        def group(g, carry):
            def logits(u):
                h = g * HEAD_GROUP + u
                qh = jnp.concatenate([qT_ref[h], qf_ref[h]], axis=0)
                part = jnp.full((8, tq), M_INIT, F32)
                for c in range(tk // ATTN_ROWS):
                    rows = pl.ds(c * ATTN_ROWS, ATTN_ROWS)
                    kh = jnp.concatenate([k_ref[h, rows, :], kf_ref[rows, :]], axis=1)
                    lg = jnp.dot(kh, qh, preferred_element_type=F32) + mb_ref[rows, :]
                    lg_ref[u % 2, rows, :] = lg
                    part = jnp.maximum(part, jnp.max(lg.reshape(ATTN_ROWS // 8, 8, tq), axis=0))
                shift = sl_ref[h] * tile_off
                m_old = m_ref[g, u]
                return m_old, jnp.maximum(m_old, jnp.max(part, axis=0, keepdims=True) + shift), shift

            def probs(u, m_old, m_new, shift):
                m_tile = m_new - shift
                for c in range(tk // ATTN_ROWS):
                    rows = pl.ds(c * ATTN_ROWS, ATTN_ROWS)
                    p_ref[u % 2, rows, :] = jnp.exp2(lg_ref[u % 2, rows, :] - m_tile).astype(BF16)
                m_ref[g, u] = m_new
                return jnp.exp2(m_old - m_new)

            def values(u, alpha):
                h = g * HEAD_GROUP + u
                acc_ref[g, u] = alpha * acc_ref[g, u] + jnp.dot(vT_ref[h], p_ref[u % 2], preferred_element_type=F32)

            stats = logits(0)
            alpha_prev = None
            for u in range(HEAD_GROUP):
                stats_next = logits(u + 1) if u + 1 < HEAD_GROUP else None
                alpha = probs(u, *stats)
                if u >= 1:
                    values(u - 1, alpha_prev)
                stats, alpha_prev = stats_next, alpha
            values(HEAD_GROUP - 1, alpha_prev)
            return carry
        lax.fori_loop(0, ATTN_HEADS // HEAD_GROUP, group, 0)

    @pl.when(kj == nk - 1)
    def _finish():
        for h in range(ATTN_HEADS):
            g, u = divmod(h, HEAD_GROUP)
            cols = slice(h * ATTN_HEAD_DIM, (h + 1) * ATTN_HEAD_DIM)
            acc = acc_ref[g, u]
            o = (acc[:ATTN_HEAD_DIM] * (1.0 / acc[ATTN_HEAD_DIM:ATTN_HEAD_DIM + 1])).T
            o_ref[:, cols] = (o * _silu(z_ref[:, cols])).astype(BF16)


def _alibi_features(tq, tk):
    sigma = jnp.exp2(-8.0 * jnp.arange(1, ATTN_HEADS + 1, dtype=F32) / ATTN_HEADS) * LOG2E
    s1 = sigma.astype(BF16)
    s2 = (sigma - s1.astype(F32)).astype(BF16)
    s3 = (sigma - s1.astype(F32) - s2.astype(F32)).astype(BF16)
    pieces = jnp.stack([s1, s2, s3, s1, s2, s3], axis=1)
    qf = jnp.zeros((ATTN_HEADS, ATTN_HEAD_DIM, tq), BF16)
    qf = qf.at[:, :6, :].set(jnp.broadcast_to(pieces[:, :, None], (ATTN_HEADS, 6, tq)))
    r = jnp.arange(tk, dtype=I32)
    r_hi = ((r // 256) * 256).astype(BF16)
    r_lo = (r % 256).astype(BF16)
    kf = jnp.zeros((tk, ATTN_HEAD_DIM), BF16).at[:, :6].set(jnp.stack([r_hi, r_hi, r_hi, r_lo, r_lo, r_lo], axis=1))
    return jnp.broadcast_to(sigma[:, None, None], (ATTN_HEADS, 1, tq)), kf, qf


def _attention(qT, k, vT, keys, thr, proj, tq, tk):
    s = k.shape[1]
    nk = s // tk
    ng = ATTN_HEADS // HEAD_GROUP

    def kv_blk(qi, kj):
        return jnp.minimum(kj, (qi * tq + tq - 1) // tk)
    return pl.pallas_call(
        functools.partial(_attn_kernel, tq=tq, tk=tk, nk=nk),
        out_shape=jax.ShapeDtypeStruct((s, ATTN_WIDTH), BF16),
        grid=(s // tq, nk),
        in_specs=[pl.BlockSpec((ATTN_HEADS, ATTN_HEAD_DIM, tq), lambda qi, kj: (0, 0, qi)),
                  pl.BlockSpec((ATTN_HEADS, tk, ATTN_HEAD_DIM), lambda qi, kj: (0, kv_blk(qi, kj), 0)),
                  pl.BlockSpec((ATTN_HEADS, V_ROWS, tk), lambda qi, kj: (0, 0, kv_blk(qi, kj))),
                  pl.BlockSpec((tk, tq), lambda qi, kj: (kv_blk(qi, kj), qi)),
                  pl.BlockSpec((1, tq), lambda qi, kj: (0, qi)),
                  pl.BlockSpec((tq, ATTN_WIDTH), lambda qi, kj: (qi, P_OFFSETS["z_attn"] // ATTN_WIDTH)),
                  _const_spec((ATTN_HEADS, 1, tq)),
                  _const_spec((tk, ATTN_HEAD_DIM)),
                  _const_spec((ATTN_HEADS, ATTN_HEAD_DIM, tq))],
        out_specs=pl.BlockSpec((tq, ATTN_WIDTH), lambda qi, kj: (qi, 0)),
        scratch_shapes=[pltpu.VMEM((ng, HEAD_GROUP, V_ROWS, tq), F32),
                        pltpu.VMEM((ng, HEAD_GROUP, 1, tq), F32),
                        pltpu.VMEM((tk, tq), F32),
                        pltpu.VMEM((2, tk, tq), F32),
                        pltpu.VMEM((2, tk, tq), BF16)],
        compiler_params=_cparams(("arbitrary", "arbitrary"), 56),
        name="attn",
    )(qT, k, vT, keys, thr, proj, *_alibi_features(tq, tk))


def _conv_kernel(x_ref, prev_ref, w_ref, b_ref, q_ref, k_ref, *, kscale):
    i = pl.program_id(0)
    x = x_ref[...]
    prev = jnp.where(i > 0, prev_ref[...], 0.0)
    head = jnp.concatenate([prev, x[:8]], axis=0)
    y = b_ref[...]
    yh = b_ref[...]
    for j in range(CONV_WIDTH):
        d = CONV_WIDTH - 1 - j
        xs = x if d == 0 else pltpu.roll(x, d, 0)
        hs = head if d == 0 else pltpu.roll(head, d, 0)
        y = y + xs * w_ref[j:j + 1, :]
        yh = yh + hs[8:] * w_ref[j:j + 1, :]
    y = jnp.concatenate([yh, y[8:]], axis=0)
    y = _silu(y)
    half = y.shape[1] // 2
    q_ref[...] = y[:, :half].astype(BF16)
    k_ref[...] = (y[:, half:] * kscale).astype(BF16)


def _conv(proj, conv_w, conv_b, tm):
    s = proj.shape[0]
    c = 2 * MLSTM_QK_WIDTH
    cb = P_OFFSETS["qk_m"] // c
    return pl.pallas_call(
        functools.partial(_conv_kernel, kscale=MLSTM_QK_DIM ** -0.5),
        out_shape=(jax.ShapeDtypeStruct((s, MLSTM_QK_WIDTH), BF16), jax.ShapeDtypeStruct((s, MLSTM_QK_WIDTH), BF16)),
        grid=(s // tm,),
        in_specs=[pl.BlockSpec((tm, c), lambda i: (i, cb)),
                  pl.BlockSpec((8, c), lambda i: (jnp.maximum(i * (tm // 8) - 1, 0), cb)),
                  _const_spec((CONV_WIDTH, c)),
                  _const_spec((1, c))],
        out_specs=(pl.BlockSpec((tm, MLSTM_QK_WIDTH), lambda i: (i, 0)),
                   pl.BlockSpec((tm, MLSTM_QK_WIDTH), lambda i: (i, 0))),
        compiler_params=_cparams(("arbitrary",), 40),
        name="conv",
    )(proj, proj, conv_w, conv_b)


def _softcap(x):
    return GATE_SOFTCAP * jnp.tanh(x / GATE_SOFTCAP)


def _mlstm_kernel(q_ref, k_ref, v_ref, og_ref, z_ref, sm_ref, g_ref, out_ref, c_ref, n_ref, m_ref, *, chunk):
    ci = pl.program_id(1)
    L = chunk
    dk, dv = MLSTM_QK_DIM, MLSTM_V_DIM

    @pl.when(ci == 0)
    def _init():
        c_ref[...] = jnp.zeros(c_ref.shape, F32)
        n_ref[...] = jnp.zeros(n_ref.shape, F32)
        m_ref[...] = jnp.zeros(m_ref.shape, F32)

    sm = sm_ref[...]
    lane = lax.broadcasted_iota(I32, sm.shape, 1)
    r_i = lax.broadcasted_iota(I32, (L, L), 0)
    c_i = lax.broadcasted_iota(I32, (L, L), 1)
    eye = r_i == c_i
    tril = r_i >= c_i
    nt = (((1,), (1,)), ((), ()))
    tn = (((0,), (0,)), ((), ()))

    for j in range(MLSTM_GROUP):
        hd = pl.program_id(0) * MLSTM_GROUP + j
        ig_col = _softcap(jnp.sum(jnp.where(lane == SM_I + hd, sm, 0.0), axis=1, keepdims=True))
        fg_col = _softcap(jnp.sum(jnp.where(lane == SM_F + hd, sm, 0.0), axis=1, keepdims=True))
        logf_col = jnp.minimum(fg_col, 0.0) - jnp.log1p(jnp.exp(-jnp.abs(fg_col)))
        logf_row = jnp.sum(jnp.where(eye, logf_col, 0.0), axis=0, keepdims=True)
        ig_row = jnp.sum(jnp.where(eye, ig_col, 0.0), axis=0, keepdims=True)
        b_col = jnp.sum(jnp.where(tril, logf_row, 0.0), axis=1, keepdims=True)
        b_row = jnp.sum(jnp.where(r_i <= c_i, logf_col, 0.0), axis=0, keepdims=True)
        dmat = jnp.where(tril, b_col - b_row + ig_row, -jnp.inf)
        m_prev = m_ref[j]
        m_inter = b_col + m_prev
        m_t = jnp.maximum(m_inter, jnp.max(dmat, axis=1, keepdims=True))

        qc = q_ref[:, j * dk:(j + 1) * dk]
        kc = k_ref[:, j * dk:(j + 1) * dk]
        vc = v_ref[:, j * dv:(j + 1) * dv].astype(BF16)
        s = lax.dot_general(qc, kc, nt, preferred_element_type=F32) * jnp.exp(dmat - m_t)
        inter = jnp.exp(m_inter - m_t)
        num = (jnp.dot(s.astype(BF16), vc, preferred_element_type=F32)
               + inter * jnp.dot(qc, c_ref[j].astype(BF16), preferred_element_type=F32))
        qn = jnp.sum(qc.astype(F32) * n_ref[j], axis=1, keepdims=True)
        den = jnp.sum(s, axis=1, keepdims=True) + inter * qn
        hh = num / jnp.maximum(jnp.abs(den), jnp.exp(-m_t))

        g_last = b_col[L - 1:L, :]
        m_new = m_t[L - 1:L, :]
        wgt = jnp.exp(g_last - b_col + ig_col - m_new)
        decay = jnp.exp(g_last + m_prev - m_new)
        wk = wgt * kc.astype(F32)
        c_ref[j] = decay * c_ref[j] + lax.dot_general(wk.astype(BF16), vc, tn, preferred_element_type=F32)
        n_ref[j] = decay * n_ref[j] + jnp.sum(wk, axis=0, keepdims=True)
        m_ref[j] = m_new

        hn = hh * lax.rsqrt(jnp.mean(hh * hh, axis=-1, keepdims=True) + NORM_EPS) * g_ref[j]
        cols = slice(j * dv, (j + 1) * dv)
        out_ref[:, cols] = (hn * _sigmoid(og_ref[:, cols]) * _silu(z_ref[:, cols])).astype(BF16)


def _mlstm(qm, km, proj, g_mh3, chunk):
    s = qm.shape[0]
    gdk, gdv = MLSTM_GROUP * MLSTM_QK_DIM, MLSTM_GROUP * MLSTM_V_DIM
    vb, ob, zb = (P_OFFSETS[n] // gdv for n in ("v_m", "o_m", "z_m"))
    return pl.pallas_call(
        functools.partial(_mlstm_kernel, chunk=chunk),
        out_shape=jax.ShapeDtypeStruct((s, MLSTM_WIDTH), BF16),
        grid=(MLSTM_HEADS // MLSTM_GROUP, s // chunk),
        in_specs=[pl.BlockSpec((chunk, gdk), lambda h, c: (c, h)),
                  pl.BlockSpec((chunk, gdk), lambda h, c: (c, h)),
                  pl.BlockSpec((chunk, gdv), lambda h, c: (c, vb + h)),
                  pl.BlockSpec((chunk, gdv), lambda h, c: (c, ob + h)),
                  pl.BlockSpec((chunk, gdv), lambda h, c: (c, zb + h)),
                  pl.BlockSpec((chunk, SMALL_W), lambda h, c: (c, SMALL_OFF // SMALL_W)),
                  pl.BlockSpec((MLSTM_GROUP, 1, MLSTM_V_DIM), lambda h, c: (h, 0, 0))],
        out_specs=pl.BlockSpec((chunk, gdv), lambda h, c: (c, h)),
        scratch_shapes=[pltpu.VMEM((MLSTM_GROUP, MLSTM_QK_DIM, MLSTM_V_DIM), F32),
                        pltpu.VMEM((MLSTM_GROUP, 1, MLSTM_QK_DIM), F32),
                        pltpu.VMEM((MLSTM_GROUP, 1, 1), F32)],
        compiler_params=_cparams(("arbitrary", "arbitrary"), 32),
        name="mlstm",
    )(qm, km, proj, proj, proj, proj, g_mh3)


def _merge_kernel(a1_ref, a2_ref, w1_ref, w2_ref, ga_ref, gm_ref, o_ref):
    y1 = jnp.dot(a1_ref[...], w1_ref[...], preferred_element_type=F32)
    y2 = jnp.dot(a2_ref[...], w2_ref[...], preferred_element_type=F32)
    o_ref[...] = (_sigmoid(ga_ref[...]) * y1 + _sigmoid(gm_ref[...]) * y2).astype(BF16)


def _merge(a1, a2, w1, w2, proj):
    s, d = a1.shape
    tm = min(512, s)
    tn = 512
    gab, gmb = P_OFFSETS["g_attn"] // tn, P_OFFSETS["g_mlstm"] // tn
    return pl.pallas_call(
        _merge_kernel,
        out_shape=jax.ShapeDtypeStruct((s, D_MODEL), BF16),
        grid=(s // tm, D_MODEL // tn),
        in_specs=[pl.BlockSpec((tm, d), lambda i, j: (i, 0)),
                  pl.BlockSpec((tm, d), lambda i, j: (i, 0)),
                  pl.BlockSpec((d, tn), lambda i, j: (0, j)),
                  pl.BlockSpec((d, tn), lambda i, j: (0, j)),
                  pl.BlockSpec((tm, tn), lambda i, j: (i, gab + j)),
                  pl.BlockSpec((tm, tn), lambda i, j: (i, gmb + j))],
        out_specs=pl.BlockSpec((tm, tn), lambda i, j: (i, j)),
        compiler_params=_cparams(("arbitrary", "arbitrary"), 48),
        name="merge",
    )(a1, a2, w1, w2, proj, proj)


def _final_kernel(mg_ref, w_ref, x_ref, gate_ref, lg_ref, lb_ref, o_ref, buf_ref, *, tn, nn):
    j = pl.program_id(1)
    buf_ref[j] = jnp.dot(mg_ref[...], w_ref[...], preferred_element_type=F32)

    @pl.when(j == nn - 1)
    def _norm():
        d = nn * tn
        ssum = 0.0
        for jj in range(nn):
            cols = slice(jj * tn, (jj + 1) * tn)
            r = DEEPNORM_ALPHA * x_ref[:, cols] + gate_ref[:, cols] * buf_ref[jj]
            buf_ref[jj] = r
            ssum = ssum + jnp.sum(r, axis=-1, keepdims=True)
        mu = ssum / d
        vsum = 0.0
        for jj in range(nn):
            vsum = vsum + jnp.sum(jnp.square(buf_ref[jj] - mu), axis=-1, keepdims=True)
        inv = lax.rsqrt(vsum / d + NORM_EPS)
        for jj in range(nn):
            cols = slice(jj * tn, (jj + 1) * tn)
            o_ref[:, cols] = (buf_ref[jj] - mu) * inv * lg_ref[:, cols] + lb_ref[:, cols]


def _final(merged, w_out, x2, mod, ln_g, ln_b):
    s, d = x2.shape
    tm = min(512, s)
    tn = 512
    nn = d // tn
    return pl.pallas_call(
        functools.partial(_final_kernel, tn=tn, nn=nn),
        out_shape=jax.ShapeDtypeStruct((s, d), F32),
        grid=(s // tm, nn),
        in_specs=[pl.BlockSpec((tm, d), lambda i, j: (i, 0)),
                  pl.BlockSpec((d, tn), lambda i, j: (0, j)),
                  pl.BlockSpec((tm, d), lambda i, j: (i, 0), pipeline_mode=pl.Buffered(1)),
                  pl.BlockSpec((1, d), lambda i, j: (0, 2)),
                  pl.BlockSpec((1, d), lambda i, j: (0, 0)),
                  pl.BlockSpec((1, d), lambda i, j: (0, 0))],
        out_specs=pl.BlockSpec((tm, d), lambda i, j: (i, 0), pipeline_mode=pl.Buffered(1)),
        scratch_shapes=[pltpu.VMEM((nn, tm, tn), F32)],
        compiler_params=_cparams(("arbitrary", "arbitrary"), 56),
        name="final",
    )(merged, w_out, x2, mod, ln_g, ln_b)


RG_TN = 512
F32_SUBLANES = 8
NARROW_A = ("k_idx", "w_idx")
NARROW_B = ("i_m", "f_m")


def _regroup_kernel(tbl_ref, main_ref, na_ref, nb_ref, o_ref, *, n_a, n_b):
    @pl.when(tbl_ref[pl.program_id(0)] >= 0)
    def _wide():
        o_ref[...] = main_ref[...].astype(BF16)

    @pl.when(tbl_ref[pl.program_id(0)] < 0)
    def _narrow():
        o_ref[:n_a, :] = na_ref[...].astype(BF16)
        o_ref[n_a:n_a + n_b, :] = nb_ref[...].astype(BF16)
        o_ref[n_a + n_b:, :] = jnp.zeros((o_ref.shape[0] - n_a - n_b, o_ref.shape[1]), BF16)


def _regroup_w(w_inT):
    d = w_inT.shape[1]
    starts = []
    for j in range(P_TOTAL // RG_TN):
        oc = j * RG_TN
        if oc >= SMALL_OFF:
            starts.append(-1)
            continue
        seg = next(n for n in P_ORDER if P_OFFSETS[n] <= oc < P_OFFSETS[n] + IN_WIDTH_OF[n])
        start = IN_OFFSETS[seg] + oc - P_OFFSETS[seg]
        assert start % F32_SUBLANES == 0, (seg, start)
        starts.append(start // F32_SUBLANES)
    n_a = sum(IN_WIDTH_OF[n] for n in NARROW_A)
    n_b = sum(IN_WIDTH_OF[n] for n in NARROW_B)
    off_a, off_b = IN_OFFSETS[NARROW_A[0]], IN_OFFSETS[NARROW_B[0]]
    grid_spec = pltpu.PrefetchScalarGridSpec(
        num_scalar_prefetch=1,
        grid=(P_TOTAL // RG_TN,),
        in_specs=[pl.BlockSpec((pl.Element(RG_TN), pl.Element(d)), lambda j, tbl: (jnp.maximum(tbl[j], 0) * F32_SUBLANES, 0)),
                  pl.BlockSpec((pl.Element(n_a), pl.Element(d)), lambda j, tbl: (off_a, 0)),
                  pl.BlockSpec((pl.Element(n_b), pl.Element(d)), lambda j, tbl: (off_b, 0))],
        out_specs=pl.BlockSpec((RG_TN, d), lambda j, tbl: (j, 0)),
    )
    return pl.pallas_call(
        functools.partial(_regroup_kernel, n_a=n_a, n_b=n_b),
        out_shape=jax.ShapeDtypeStruct((P_TOTAL, d), BF16),
        grid_spec=grid_spec,
        compiler_params=_cparams(("arbitrary",), 40),
        name="regroup",
    )(jnp.asarray(starts, I32), w_inT, w_inT, w_inT)


def _regroup_cols(a, pad_to):
    parts = [a[..., IN_OFFSETS[n]:IN_OFFSETS[n] + IN_WIDTH_OF[n]] for n in P_ORDER]
    parts.append(jnp.zeros(a.shape[:-1] + (pad_to - P_USED,), a.dtype))
    return jnp.concatenate(parts, axis=-1)


def _layer(x2, c, w_ada, b_ada, w_in, b_in, g_q, g_kv, w_uq, w_iq, w_uk, w_uv, g_kidx, b_kidx, conv_w, conv_b, g_mh,
           w_attn_out, w_mlstm_out, w_out, ln_g, ln_b):
    s, d = x2.shape
    assert d == D_MODEL and s % 1024 == 0, (s, d)
    tq, tk = 256, 512
    nsel = min(TOPK_MAX, s // 4)

    w_cat = _regroup_w(w_in.T)
    b_cat = _regroup_cols(b_in, P_TOTAL).reshape(1, P_TOTAL)
    w_uqT = w_uq.T.astype(BF16)
    w_iqT = w_iq.T.astype(BF16)
    w_ukT = w_uk.reshape(ATTN_WIDTH, KV_LORA_RANK).T.astype(BF16)
    w_uvT = w_uv.transpose(0, 2, 1).reshape(ATTN_WIDTH, KV_LORA_RANK).astype(BF16)

    mod = _ada(c.reshape(d, 1), w_ada, b_ada.reshape(1, -1))
    u = _modulate(x2, mod)
    proj = _proj(u, w_cat, b_cat)

    qT, qiT = _qpath(proj, g_q.reshape(1, -1), w_uqT, w_iqT, tq)
    k, vT, kidx, widx = _kvpath(proj, g_kv.reshape(1, -1), g_kidx.reshape(1, -1), b_kidx.reshape(1, -1), w_ukT, w_uvT, tq)
    wT = widx.T.reshape(IDX_HEADS, 1, s)
    keys, thr = _indexer(kidx, qiT, wT, tq, nsel)
    a_attn = _attention(qT, k, vT, keys, thr, proj, tq, tk)

    qm, km = _conv(proj, conv_w, conv_b.reshape(1, -1), tq)
    a_mlstm = _mlstm(qm, km, proj, g_mh.reshape(MLSTM_HEADS, 1, MLSTM_V_DIM), MLSTM_CHUNK)

    merged = _merge(a_attn, a_mlstm, w_attn_out.astype(BF16), w_mlstm_out.astype(BF16), proj)
    return _final(merged, w_out.astype(BF16), x2, mod, ln_g.reshape(1, -1), ln_b.reshape(1, -1))


def kernel(x, c, w_ada, b_ada, w_in, b_in, g_q, g_kv, w_uq, w_iq, w_uk, w_uv, g_kidx, b_kidx, conv_w, conv_b, g_mh,
           w_attn_out, w_mlstm_out, w_out, ln_g, ln_b):
    bsz, seq, d = x.shape
    assert bsz == 1 and w_ada.shape[0] == 1, "single batch, single layer"
    out = _layer(x.reshape(seq, d), c, w_ada[0], b_ada[0], w_in[0], b_in[0], g_q[0], g_kv[0], w_uq[0], w_iq[0],
                 w_uk[0], w_uv[0], g_kidx[0], b_kidx[0], conv_w[0], conv_b[0], g_mh[0], w_attn_out[0],
                 w_mlstm_out[0], w_out[0], ln_g[0], ln_b[0])
    return out.reshape(bsz, seq, d)
```

```python
import functools

import jax
import jax.numpy as jnp
from jax import lax
from jax.experimental import pallas as pl
from jax.experimental.pallas import tpu as pltpu

F32 = jnp.float32
BF16 = jnp.bfloat16
I32 = jnp.int32

D_MODEL = 4096
ATTN_HEADS = 32
ATTN_HEAD_DIM = 128
ATTN_WIDTH = ATTN_HEADS * ATTN_HEAD_DIM
Q_LORA_RANK = 1024
KV_LORA_RANK = 512
IDX_HEADS = 32
IDX_HEAD_DIM = 64
TOPK_MAX = 256
MLSTM_HEADS = 8
MLSTM_QK_DIM = (D_MODEL // 2) // MLSTM_HEADS
MLSTM_V_DIM = D_MODEL // MLSTM_HEADS
MLSTM_QK_WIDTH = MLSTM_HEADS * MLSTM_QK_DIM
MLSTM_WIDTH = MLSTM_HEADS * MLSTM_V_DIM
MLSTM_CHUNK = 256
MLSTM_GROUP = 2
CONV_WIDTH = 4
GATE_SOFTCAP = 15.0
DEEPNORM_ALPHA = 2.0 ** 0.25
NORM_EPS = 1e-6

IN_WIDTHS = (Q_LORA_RANK, KV_LORA_RANK, IDX_HEAD_DIM, IDX_HEADS, ATTN_WIDTH, 2 * MLSTM_QK_WIDTH, MLSTM_WIDTH,
             MLSTM_WIDTH, MLSTM_HEADS, MLSTM_HEADS, MLSTM_WIDTH, D_MODEL, D_MODEL)
IN_NAMES = ("q_lat", "kv_lat", "k_idx", "w_idx", "z_attn", "qk_m", "v_m", "o_m", "i_m", "f_m", "z_m", "g_attn", "g_mlstm")
IN_OFFSETS = {n: sum(IN_WIDTHS[:i]) for i, n in enumerate(IN_NAMES)}
IN_WIDTH_OF = dict(zip(IN_NAMES, IN_WIDTHS))

P_ORDER = ("z_attn", "qk_m", "v_m", "o_m", "z_m", "g_attn", "g_mlstm", "q_lat", "kv_lat", "k_idx", "w_idx", "i_m", "f_m")
P_OFFSETS = {}
_off = 0
for _n in P_ORDER:
    P_OFFSETS[_n] = _off
    _off += IN_WIDTH_OF[_n]
P_USED = _off
PROJ_TN = 1024
P_TOTAL = -(-P_USED // PROJ_TN) * PROJ_TN
SMALL_W = 128
SMALL_OFF = P_OFFSETS["k_idx"]
SM_WIDX = IDX_HEAD_DIM
SM_I = SM_WIDX + IDX_HEADS
SM_F = SM_I + MLSTM_HEADS

VMEM_CAP_BYTES = 60 * 1024 * 1024

MASK_BIAS = -1e30
M_INIT = -1e20
LOG2E = 1.4426950408889634
V_ONES = 16
V_ROWS = ATTN_HEAD_DIM + V_ONES
ATTN_ROWS = 256
HEAD_GROUP = 32
INT_MIN = -2 ** 31
KEY_NEG_INF = INT_MIN + 0x7FFFFF


def _cparams(sem, vmem_mb):
    return pltpu.CompilerParams(dimension_semantics=sem, vmem_limit_bytes=min(vmem_mb * 1024 * 1024, VMEM_CAP_BYTES))


def _sigmoid(x):
    return jax.nn.sigmoid(x)


def _silu(x):
    return x * jax.nn.sigmoid(x)


def _const_spec(shape):
    nd = len(shape)
    return pl.BlockSpec(shape, lambda *_: (0,) * nd, pipeline_mode=pl.Buffered(1))


def _ada_kernel(c_ref, w_ref, b_ref, o_ref):
    c = c_ref[...]
    o_ref[...] = jnp.sum(w_ref[...] * _silu(c), axis=0, keepdims=True) + b_ref[...]


def _ada(c_col, w_ada, b_ada):
    d, n = w_ada.shape
    tn = 512
    return pl.pallas_call(
        _ada_kernel,
        out_shape=jax.ShapeDtypeStruct((1, n), F32),
        grid=(n // tn,),
        in_specs=[pl.BlockSpec((d, 1), lambda j: (0, 0)),
                  pl.BlockSpec((d, tn), lambda j: (0, j)),
                  pl.BlockSpec((1, tn), lambda j: (0, j))],
        out_specs=pl.BlockSpec((1, tn), lambda j: (0, j)),
        compiler_params=_cparams(("arbitrary",), 32),
        name="ada",
    )(c_col, w_ada, b_ada)


def _modulate_kernel(x_ref, shift_ref, scale_ref, u_ref):
    u_ref[...] = (x_ref[...] * (1.0 + scale_ref[...]) + shift_ref[...]).astype(BF16)


def _modulate(x2, mod):
    s, d = x2.shape
    tm = min(512, s)
    return pl.pallas_call(
        _modulate_kernel,
        out_shape=jax.ShapeDtypeStruct((s, d), BF16),
        grid=(s // tm,),
        in_specs=[pl.BlockSpec((tm, d), lambda i: (i, 0)),
                  pl.BlockSpec((1, d), lambda i: (0, 0)),
                  pl.BlockSpec((1, d), lambda i: (0, 1))],
        out_specs=pl.BlockSpec((tm, d), lambda i: (i, 0)),
        compiler_params=_cparams(("arbitrary",), 40),
        name="modulate",
    )(x2, mod, mod)


def _proj_kernel(u_ref, w_ref, b_ref, o_ref):
    nt = (((1,), (1,)), ((), ()))
    o_ref[...] = lax.dot_general(u_ref[...], w_ref[...], nt, preferred_element_type=F32) + b_ref[...]


def _proj(u, w_catT, b_cat):
    s, d = u.shape
    n = w_catT.shape[0]
    tm = min(1024, s)
    tn = PROJ_TN
    return pl.pallas_call(
        _proj_kernel,
        out_shape=jax.ShapeDtypeStruct((s, n), F32),
        grid=(n // tn, s // tm),
        in_specs=[pl.BlockSpec((tm, d), lambda j, i: (i, 0)),
                  pl.BlockSpec((tn, d), lambda j, i: (j, 0)),
                  pl.BlockSpec((1, tn), lambda j, i: (0, j))],
        out_specs=pl.BlockSpec((tm, tn), lambda j, i: (i, j)),
        compiler_params=_cparams(("arbitrary", "arbitrary"), 56),
        name="proj",
    )(u, w_catT, b_cat)


def _qpath_kernel(ql_ref, g_ref, wuq_ref, wiq_ref, qT_ref, qiT_ref, *, scale):
    x = ql_ref[...]
    cq = (x * lax.rsqrt(jnp.mean(x * x, axis=-1, keepdims=True) + NORM_EPS) * g_ref[...]).astype(BF16)
    nt = (((1,), (1,)), ((), ()))
    qT = lax.dot_general(wuq_ref[...], cq, nt, preferred_element_type=F32)
    qT_ref[...] = (qT * scale).reshape(qT_ref.shape).astype(BF16)
    qiT = lax.dot_general(wiq_ref[...], cq, nt, preferred_element_type=F32)
    qiT_ref[...] = qiT.reshape(qiT_ref.shape).astype(BF16)


def _qpath(proj, g_q, w_uqT, w_iqT, tq):
    s = proj.shape[0]
    r = Q_LORA_RANK
    return pl.pallas_call(
        functools.partial(_qpath_kernel, scale=ATTN_HEAD_DIM ** -0.5 * LOG2E),
        out_shape=(jax.ShapeDtypeStruct((ATTN_HEADS, ATTN_HEAD_DIM, s), BF16),
                   jax.ShapeDtypeStruct((IDX_HEADS, IDX_HEAD_DIM, s), BF16)),
        grid=(s // tq,),
        in_specs=[pl.BlockSpec((tq, r), lambda i: (i, P_OFFSETS["q_lat"] // r)),
                  _const_spec((1, r)),
                  _const_spec(w_uqT.shape),
                  _const_spec(w_iqT.shape)],
        out_specs=(pl.BlockSpec((ATTN_HEADS, ATTN_HEAD_DIM, tq), lambda i: (0, 0, i)),
                   pl.BlockSpec((IDX_HEADS, IDX_HEAD_DIM, tq), lambda i: (0, 0, i))),
        compiler_params=_cparams(("arbitrary",), 48),
        name="qpath",
    )(proj, g_q, w_uqT, w_iqT)


def _kvpath_kernel(kvl_ref, sm_ref, gkv_ref, gk_ref, bk_ref, wuk_ref, wuv_ref, k_ref, vT_ref, kidx_ref, widx_ref, *, wscale):
    x = kvl_ref[...]
    ckv = (x * lax.rsqrt(jnp.mean(x * x, axis=-1, keepdims=True) + NORM_EPS) * gkv_ref[...]).astype(BF16)
    kfull = jnp.dot(ckv, wuk_ref[...], preferred_element_type=F32)
    for h in range(ATTN_HEADS):
        k_ref[h] = kfull[:, h * ATTN_HEAD_DIM:(h + 1) * ATTN_HEAD_DIM].astype(BF16)
    nt = (((1,), (1,)), ((), ()))
    vT = lax.dot_general(wuv_ref[...], ckv, nt, preferred_element_type=F32)
    vT_ref[:, :ATTN_HEAD_DIM, :] = vT.reshape(ATTN_HEADS, ATTN_HEAD_DIM, -1).astype(BF16)
    vT_ref[:, ATTN_HEAD_DIM:, :] = jnp.ones((ATTN_HEADS, V_ONES, vT_ref.shape[2]), BF16)
    sm = sm_ref[...]
    ki = sm[:, :IDX_HEAD_DIM]
    mu = jnp.mean(ki, axis=-1, keepdims=True)
    var = jnp.mean(jnp.square(ki - mu), axis=-1, keepdims=True)
    kidx_ref[...] = ((ki - mu) * lax.rsqrt(var + NORM_EPS) * gk_ref[...] + bk_ref[...]).astype(BF16)
    widx_ref[...] = sm[:, SM_WIDX:SM_WIDX + IDX_HEADS] * wscale


def _kvpath(proj, g_kv, g_kidx, b_kidx, w_ukT, w_uvT, tm):
    s = proj.shape[0]
    r = KV_LORA_RANK
    return pl.pallas_call(
        functools.partial(_kvpath_kernel, wscale=IDX_HEADS ** -0.5 * IDX_HEAD_DIM ** -0.5),
        out_shape=(jax.ShapeDtypeStruct((ATTN_HEADS, s, ATTN_HEAD_DIM), BF16),
                   jax.ShapeDtypeStruct((ATTN_HEADS, V_ROWS, s), BF16),
                   jax.ShapeDtypeStruct((s, IDX_HEAD_DIM), BF16),
                   jax.ShapeDtypeStruct((s, IDX_HEADS), F32)),
        grid=(s // tm,),
        in_specs=[pl.BlockSpec((tm, r), lambda i: (i, P_OFFSETS["kv_lat"] // r)),
                  pl.BlockSpec((tm, SMALL_W), lambda i: (i, SMALL_OFF // SMALL_W)),
                  _const_spec((1, r)),
                  _const_spec((1, IDX_HEAD_DIM)),
                  _const_spec((1, IDX_HEAD_DIM)),
                  _const_spec(w_ukT.shape),
                  _const_spec(w_uvT.shape)],
        out_specs=(pl.BlockSpec((ATTN_HEADS, tm, ATTN_HEAD_DIM), lambda i: (0, i, 0)),
                   pl.BlockSpec((ATTN_HEADS, V_ROWS, tm), lambda i: (0, 0, i)),
                   pl.BlockSpec((tm, IDX_HEAD_DIM), lambda i: (i, 0)),
                   pl.BlockSpec((tm, IDX_HEADS), lambda i: (i, 0))),
        compiler_params=_cparams(("arbitrary",), 48),
        name="kvpath",
    )(proj, proj, g_kv, g_kidx, b_kidx, w_ukT, w_uvT)


def _key_to_float(key):
    bits = jnp.where(key >= 0, key, key ^ 0x7FFFFFFF)
    return jnp.where(key < KEY_NEG_INF, -jnp.inf, pltpu.bitcast(bits, F32))


def _indexer_kernel(kidx_ref, qiT_ref, wT_ref, sc_ref, thr_ref, *, seq, tq, nsel):
    i = pl.program_id(0)
    ch = 128
    cb = tq
    n_score = (i + 1) * (tq // ch)
    n_count = i + 1
    tpos = i * tq + lax.broadcasted_iota(I32, (ch, tq), 1)

    def score_chunk(c, carry):
        r0 = pl.multiple_of(c * ch, ch)
        kc = kidx_ref[pl.ds(r0, ch), :]

        acc = jnp.zeros((ch, tq), F32)
        for h in range(IDX_HEADS):
            r = jnp.dot(kc, qiT_ref[h], preferred_element_type=F32)
            acc = acc + jnp.maximum(r, 0.0) * wT_ref[h]
        spos = r0 + lax.broadcasted_iota(I32, (ch, tq), 0)
        sc_ref[pl.ds(r0, ch), :] = jnp.where(spos <= tpos, acc, -jnp.inf)
        return carry
    lax.fori_loop(0, n_score, score_chunk, 0)

    def fill_chunk(c, carry):
        sc_ref[pl.ds(pl.multiple_of(c * cb, cb), cb), :] = jnp.full((cb, tq), -jnp.inf, F32)
        return carry
    lax.fori_loop(n_count, seq // cb, fill_chunk, 0)

    def count(pred):
        def body(c, part):
            r0 = pl.multiple_of(c * cb, cb)
            m = jnp.where(pred(sc_ref[pl.ds(r0, cb), :], r0), 1, 0)
            return part + jnp.sum(m.reshape(cb // 8, 8, tq), axis=0)
        part = lax.fori_loop(0, n_count, body, jnp.zeros((8, tq), I32))
        return jnp.sum(part, axis=0, keepdims=True)

    def count_ge(cand_key):
        cand = _key_to_float(cand_key)
        return count(lambda blk, r0: blk >= cand)

    t0 = jnp.where(count_ge(jnp.zeros((1, tq), I32)) >= nsel, 0, INT_MIN).astype(I32)

    def bit_step(b, t):
        cand = t + jnp.left_shift(jnp.int32(1), 30 - b)
        return jnp.where(count_ge(cand) >= nsel, cand, t)
    thr = _key_to_float(lax.fori_loop(0, 31, bit_step, t0))
    thr_ref[...] = thr

    tie = (count(lambda blk, r0: blk >= thr) > nsel) & (thr > -jnp.inf)

    @pl.when(jnp.max(tie.astype(I32)) > 0)
    def _break_ties():
        need = nsel - count(lambda blk, r0: blk > thr)

        def eq_below(j):
            return count(lambda blk, r0: (blk == thr) & (r0 + lax.broadcasted_iota(I32, (cb, tq), 0) < j))

        def jbit(b, j):
            test = j + jnp.left_shift(jnp.int32(1), (seq.bit_length() - 2) - b)
            return jnp.where(eq_below(test) < need, test, j)
        jlast = lax.fori_loop(0, seq.bit_length() - 1, jbit, jnp.zeros((1, tq), I32))

        def demote(c, carry):
            r0 = pl.multiple_of(c * cb, cb)
            blk = sc_ref[pl.ds(r0, cb), :]
            row = r0 + lax.broadcasted_iota(I32, (cb, tq), 0)
            sc_ref[pl.ds(r0, cb), :] = jnp.where(tie & (blk == thr) & (row > jlast), -jnp.inf, blk)
            return carry
        lax.fori_loop(0, n_count, demote, 0)


def _indexer(kidx, qiT, wT, tq, nsel):
    s = kidx.shape[0]
    return pl.pallas_call(
        functools.partial(_indexer_kernel, seq=s, tq=tq, nsel=nsel),
        out_shape=(jax.ShapeDtypeStruct((s, s), F32), jax.ShapeDtypeStruct((1, s), F32)),
        grid=(s // tq,),
        in_specs=[_const_spec((s, IDX_HEAD_DIM)),
                  pl.BlockSpec((IDX_HEADS, IDX_HEAD_DIM, tq), lambda i: (0, 0, i)),
                  pl.BlockSpec((IDX_HEADS, 1, tq), lambda i: (0, 0, i))],
        out_specs=(pl.BlockSpec((s, tq), lambda i: (0, i)),
                   pl.BlockSpec((1, tq), lambda i: (0, i))),
        compiler_params=_cparams(("arbitrary",), 40),
        name="indexer",
    )(kidx, qiT, wT)


def _attn_kernel(tiles_ref, qT_ref, k_ref, vT_ref, keys_ref, thr_ref, z_ref, sl_ref, kf_ref, qf_ref, o_ref,
                 acc_ref, m_ref, mb_ref, lg_ref, p_ref, *, tq, tk):
    qi = tiles_ref[0, pl.program_id(0)]
    kj = tiles_ref[1, pl.program_id(0)]

    @pl.when(kj == 0)
    def _init():
        acc_ref[...] = jnp.zeros(acc_ref.shape, F32)
        m_ref[...] = jnp.full(m_ref.shape, M_INIT, F32)

    def _compute():
        spos = kj * tk + lax.broadcasted_iota(I32, (tk, tq), 0)
        tpos = qi * tq + lax.broadcasted_iota(I32, (tk, tq), 1)
        sel = (keys_ref[...] >= thr_ref[...]) & (spos <= tpos)
        mb_ref[...] = jnp.where(sel, 0.0, MASK_BIAS)
        tile_off = (kj * tk - qi * tq).astype(F32)

        def group(g, carry):
            def logits(u):
                h = g * HEAD_GROUP + u
                qh = jnp.concatenate([qT_ref[h], qf_ref[h]], axis=0)
                part = jnp.full((8, tq), M_INIT, F32)
                for c in range(tk // ATTN_ROWS):
                    rows = pl.ds(c * ATTN_ROWS, ATTN_ROWS)
                    kh = jnp.concatenate([k_ref[h, rows, :], kf_ref[rows, :]], axis=1)
                    lg = jnp.dot(kh, qh, preferred_element_type=F32) + mb_ref[rows, :]
                    lg_ref[u % 2, rows, :] = lg
                    part = jnp.maximum(part, jnp.max(lg.reshape(ATTN_ROWS // 8, 8, tq), axis=0))
                shift = sl_ref[h] * tile_off
                m_old = m_ref[g, u]
                return m_old, jnp.maximum(m_old, jnp.max(part, axis=0, keepdims=True) + shift), shift

            def probs(u, m_old, m_new, shift):
                m_tile = m_new - shift
                for c in range(tk // ATTN_ROWS):
                    rows = pl.ds(c * ATTN_ROWS, ATTN_ROWS)
                    p_ref[u % 2, rows, :] = jnp.exp2(lg_ref[u % 2, rows, :] - m_tile).astype(BF16)
                m_ref[g, u] = m_new
                return jnp.exp2(m_old - m_new)

            def values(u, alpha):
                h = g * HEAD_GROUP + u
                acc_ref[g, u] = alpha * acc_ref[g, u] + jnp.dot(vT_ref[h], p_ref[u % 2], preferred_element_type=F32)

            stats = logits(0)
            alpha_prev = None
            for u in range(HEAD_GROUP):
                stats_next = logits(u + 1) if u + 1 < HEAD_GROUP else None
                alpha = probs(u, *stats)
                if u >= 1:
                    values(u - 1, alpha_prev)
                stats, alpha_prev = stats_next, alpha
            values(HEAD_GROUP - 1, alpha_prev)
            return carry
        lax.fori_loop(0, ATTN_HEADS // HEAD_GROUP, group, 0)
    _compute()

    @pl.when(kj == (qi * tq + tq - 1) // tk)
    def _finish():
        for h in range(ATTN_HEADS):
            g, u = divmod(h, HEAD_GROUP)
            cols = slice(h * ATTN_HEAD_DIM, (h + 1) * ATTN_HEAD_DIM)
            acc = acc_ref[g, u]
            o = (acc[:ATTN_HEAD_DIM] * (1.0 / acc[ATTN_HEAD_DIM:ATTN_HEAD_DIM + 1])).T
            o_ref[:, cols] = (o * _silu(z_ref[:, cols])).astype(BF16)


def _alibi_features(tq, tk):
    sigma = jnp.exp2(-8.0 * jnp.arange(1, ATTN_HEADS + 1, dtype=F32) / ATTN_HEADS) * LOG2E
    s1 = sigma.astype(BF16)
    s2 = (sigma - s1.astype(F32)).astype(BF16)
    s3 = (sigma - s1.astype(F32) - s2.astype(F32)).astype(BF16)
    pieces = jnp.stack([s1, s2, s3, s1, s2, s3], axis=1)
    qf = jnp.zeros((ATTN_HEADS, ATTN_HEAD_DIM, tq), BF16)
    qf = qf.at[:, :6, :].set(jnp.broadcast_to(pieces[:, :, None], (ATTN_HEADS, 6, tq)))
    r = jnp.arange(tk, dtype=I32)
    r_hi = ((r // 256) * 256).astype(BF16)
    r_lo = (r % 256).astype(BF16)
    kf = jnp.zeros((tk, ATTN_HEAD_DIM), BF16).at[:, :6].set(jnp.stack([r_hi, r_hi, r_hi, r_lo, r_lo, r_lo], axis=1))
    return jnp.broadcast_to(sigma[:, None, None], (ATTN_HEADS, 1, tq)), kf, qf


def _attention(qT, k, vT, keys, thr, proj, tq, tk):
    s = k.shape[1]
    ng = ATTN_HEADS // HEAD_GROUP
    tiles = [(qi, kj) for qi in range(s // tq) for kj in range((qi * tq + tq - 1) // tk + 1)]
    const3 = lambda shape: pl.BlockSpec(shape, lambda i, t: (0, 0, 0), pipeline_mode=pl.Buffered(1))
    grid_spec = pltpu.PrefetchScalarGridSpec(
        num_scalar_prefetch=1,
        grid=(len(tiles),),
        in_specs=[pl.BlockSpec((ATTN_HEADS, ATTN_HEAD_DIM, tq), lambda i, t: (0, 0, t[0, i])),
                  pl.BlockSpec((ATTN_HEADS, tk, ATTN_HEAD_DIM), lambda i, t: (0, t[1, i], 0)),
                  pl.BlockSpec((ATTN_HEADS, V_ROWS, tk), lambda i, t: (0, 0, t[1, i])),
                  pl.BlockSpec((tk, tq), lambda i, t: (t[1, i], t[0, i])),
                  pl.BlockSpec((1, tq), lambda i, t: (0, t[0, i])),
                  pl.BlockSpec((tq, ATTN_WIDTH), lambda i, t: (t[0, i], P_OFFSETS["z_attn"] // ATTN_WIDTH)),
                  const3((ATTN_HEADS, 1, tq)),
                  pl.BlockSpec((tk, ATTN_HEAD_DIM), lambda i, t: (0, 0), pipeline_mode=pl.Buffered(1)),
                  const3((ATTN_HEADS, ATTN_HEAD_DIM, tq))],
        out_specs=pl.BlockSpec((tq, ATTN_WIDTH), lambda i, t: (t[0, i], 0)),
        scratch_shapes=[pltpu.VMEM((ng, HEAD_GROUP, V_ROWS, tq), F32),
                        pltpu.VMEM((ng, HEAD_GROUP, 1, tq), F32),
                        pltpu.VMEM((tk, tq), F32),
                        pltpu.VMEM((2, tk, tq), F32),
                        pltpu.VMEM((2, tk, tq), BF16)],
    )
    return pl.pallas_call(
        functools.partial(_attn_kernel, tq=tq, tk=tk),
        out_shape=jax.ShapeDtypeStruct((s, ATTN_WIDTH), BF16),
        grid_spec=grid_spec,
        compiler_params=_cparams(("arbitrary",), 56),
        name="attn",
    )(jnp.asarray(tiles, I32).T, qT, k, vT, keys, thr, proj, *_alibi_features(tq, tk))


def _conv_kernel(x_ref, prev_ref, w_ref, b_ref, q_ref, k_ref, *, kscale):
    i = pl.program_id(0)
    x = x_ref[...]
    prev = jnp.where(i > 0, prev_ref[...], 0.0)
    head = jnp.concatenate([prev, x[:8]], axis=0)
    y = b_ref[...]
    yh = b_ref[...]
    for j in range(CONV_WIDTH):
        d = CONV_WIDTH - 1 - j
        xs = x if d == 0 else pltpu.roll(x, d, 0)
        hs = head if d == 0 else pltpu.roll(head, d, 0)
        y = y + xs * w_ref[j:j + 1, :]
        yh = yh + hs[8:] * w_ref[j:j + 1, :]
    y = jnp.concatenate([yh, y[8:]], axis=0)
    y = _silu(y)
    half = y.shape[1] // 2
    q_ref[...] = y[:, :half].astype(BF16)
    k_ref[...] = (y[:, half:] * kscale).astype(BF16)


def _conv(proj, conv_w, conv_b, tm):
    s = proj.shape[0]
    c = 2 * MLSTM_QK_WIDTH
    cb = P_OFFSETS["qk_m"] // c
    return pl.pallas_call(
        functools.partial(_conv_kernel, kscale=MLSTM_QK_DIM ** -0.5),
        out_shape=(jax.ShapeDtypeStruct((s, MLSTM_QK_WIDTH), BF16), jax.ShapeDtypeStruct((s, MLSTM_QK_WIDTH), BF16)),
        grid=(s // tm,),
        in_specs=[pl.BlockSpec((tm, c), lambda i: (i, cb)),
                  pl.BlockSpec((8, c), lambda i: (jnp.maximum(i * (tm // 8) - 1, 0), cb)),
                  _const_spec((CONV_WIDTH, c)),
                  _const_spec((1, c))],
        out_specs=(pl.BlockSpec((tm, MLSTM_QK_WIDTH), lambda i: (i, 0)),
                   pl.BlockSpec((tm, MLSTM_QK_WIDTH), lambda i: (i, 0))),
        compiler_params=_cparams(("arbitrary",), 40),
        name="conv",
    )(proj, proj, conv_w, conv_b)


def _softcap(x):
    return GATE_SOFTCAP * jnp.tanh(x / GATE_SOFTCAP)


def _mlstm_kernel(q_ref, k_ref, v_ref, og_ref, z_ref, sm_ref, g_ref, out_ref, c_ref, n_ref, m_ref, *, chunk):
    ci = pl.program_id(1)
    L = chunk
    dk, dv = MLSTM_QK_DIM, MLSTM_V_DIM

    @pl.when(ci == 0)
    def _init():
        c_ref[...] = jnp.zeros(c_ref.shape, F32)
        n_ref[...] = jnp.zeros(n_ref.shape, F32)
        m_ref[...] = jnp.zeros(m_ref.shape, F32)

    sm = sm_ref[...]
    lane = lax.broadcasted_iota(I32, sm.shape, 1)
    r_i = lax.broadcasted_iota(I32, (L, L), 0)
    c_i = lax.broadcasted_iota(I32, (L, L), 1)
    eye = r_i == c_i
    tril = r_i >= c_i
    nt = (((1,), (1,)), ((), ()))
    tn = (((0,), (0,)), ((), ()))

    for j in range(MLSTM_GROUP):
        hd = pl.program_id(0) * MLSTM_GROUP + j
        ig_col = _softcap(jnp.sum(jnp.where(lane == SM_I + hd, sm, 0.0), axis=1, keepdims=True))
        fg_col = _softcap(jnp.sum(jnp.where(lane == SM_F + hd, sm, 0.0), axis=1, keepdims=True))
        logf_col = jnp.minimum(fg_col, 0.0) - jnp.log1p(jnp.exp(-jnp.abs(fg_col)))
        logf_row = jnp.sum(jnp.where(eye, logf_col, 0.0), axis=0, keepdims=True)
        ig_row = jnp.sum(jnp.where(eye, ig_col, 0.0), axis=0, keepdims=True)
        b_col = jnp.sum(jnp.where(tril, logf_row, 0.0), axis=1, keepdims=True)
        b_row = jnp.sum(jnp.where(r_i <= c_i, logf_col, 0.0), axis=0, keepdims=True)
        dmat = jnp.where(tril, b_col - b_row + ig_row, -jnp.inf)
        m_prev = m_ref[j]
        m_inter = b_col + m_prev
        m_t = jnp.maximum(m_inter, jnp.max(dmat, axis=1, keepdims=True))

        qc = q_ref[:, j * dk:(j + 1) * dk]
        kc = k_ref[:, j * dk:(j + 1) * dk]
        vc = v_ref[:, j * dv:(j + 1) * dv].astype(BF16)
        s = lax.dot_general(qc, kc, nt, preferred_element_type=F32) * jnp.exp(dmat - m_t)
        inter = jnp.exp(m_inter - m_t)
        num = (jnp.dot(s.astype(BF16), vc, preferred_element_type=F32)
               + inter * jnp.dot(qc, c_ref[j].astype(BF16), preferred_element_type=F32))
        qn = jnp.sum(qc.astype(F32) * n_ref[j], axis=1, keepdims=True)
        den = jnp.sum(s, axis=1, keepdims=True) + inter * qn
        hh = num / jnp.maximum(jnp.abs(den), jnp.exp(-m_t))

        g_last = b_col[L - 1:L, :]
        m_new = m_t[L - 1:L, :]
        wgt = jnp.exp(g_last - b_col + ig_col - m_new)
        decay = jnp.exp(g_last + m_prev - m_new)
        wk = wgt * kc.astype(F32)
        c_ref[j] = decay * c_ref[j] + lax.dot_general(wk.astype(BF16), vc, tn, preferred_element_type=F32)
        n_ref[j] = decay * n_ref[j] + jnp.sum(wk, axis=0, keepdims=True)
        m_ref[j] = m_new

        hn = hh * lax.rsqrt(jnp.mean(hh * hh, axis=-1, keepdims=True) + NORM_EPS) * g_ref[j]
        cols = slice(j * dv, (j + 1) * dv)
        out_ref[:, cols] = (hn * _sigmoid(og_ref[:, cols]) * _silu(z_ref[:, cols])).astype(BF16)


def _mlstm(qm, km, proj, g_mh3, chunk):
    s = qm.shape[0]
    gdk, gdv = MLSTM_GROUP * MLSTM_QK_DIM, MLSTM_GROUP * MLSTM_V_DIM
    vb, ob, zb = (P_OFFSETS[n] // gdv for n in ("v_m", "o_m", "z_m"))
    return pl.pallas_call(
        functools.partial(_mlstm_kernel, chunk=chunk),
        out_shape=jax.ShapeDtypeStruct((s, MLSTM_WIDTH), BF16),
        grid=(MLSTM_HEADS // MLSTM_GROUP, s // chunk),
        in_specs=[pl.BlockSpec((chunk, gdk), lambda h, c: (c, h)),
                  pl.BlockSpec((chunk, gdk), lambda h, c: (c, h)),
                  pl.BlockSpec((chunk, gdv), lambda h, c: (c, vb + h)),
                  pl.BlockSpec((chunk, gdv), lambda h, c: (c, ob + h)),
                  pl.BlockSpec((chunk, gdv), lambda h, c: (c, zb + h)),
                  pl.BlockSpec((chunk, SMALL_W), lambda h, c: (c, SMALL_OFF // SMALL_W)),
                  pl.BlockSpec((MLSTM_GROUP, 1, MLSTM_V_DIM), lambda h, c: (h, 0, 0))],
        out_specs=pl.BlockSpec((chunk, gdv), lambda h, c: (c, h)),
        scratch_shapes=[pltpu.VMEM((MLSTM_GROUP, MLSTM_QK_DIM, MLSTM_V_DIM), F32),
                        pltpu.VMEM((MLSTM_GROUP, 1, MLSTM_QK_DIM), F32),
                        pltpu.VMEM((MLSTM_GROUP, 1, 1), F32)],
        compiler_params=_cparams(("arbitrary", "arbitrary"), 32),
        name="mlstm",
    )(qm, km, proj, proj, proj, proj, g_mh3)


def _merge_kernel(a1_ref, a2_ref, w1_ref, w2_ref, ga_ref, gm_ref, o_ref):
    y1 = jnp.dot(a1_ref[...], w1_ref[...], preferred_element_type=F32)
    y2 = jnp.dot(a2_ref[...], w2_ref[...], preferred_element_type=F32)
    o_ref[...] = (_sigmoid(ga_ref[...]) * y1 + _sigmoid(gm_ref[...]) * y2).astype(BF16)


def _merge(a1, a2, w1, w2, proj):
    s, d = a1.shape
    tm = min(512, s)
    tn = 512
    gab, gmb = P_OFFSETS["g_attn"] // tn, P_OFFSETS["g_mlstm"] // tn
    return pl.pallas_call(
        _merge_kernel,
        out_shape=jax.ShapeDtypeStruct((s, D_MODEL), BF16),
        grid=(s // tm, D_MODEL // tn),
        in_specs=[pl.BlockSpec((tm, d), lambda i, j: (i, 0)),
                  pl.BlockSpec((tm, d), lambda i, j: (i, 0)),
                  pl.BlockSpec((d, tn), lambda i, j: (0, j)),
                  pl.BlockSpec((d, tn), lambda i, j: (0, j)),
                  pl.BlockSpec((tm, tn), lambda i, j: (i, gab + j)),
                  pl.BlockSpec((tm, tn), lambda i, j: (i, gmb + j))],
        out_specs=pl.BlockSpec((tm, tn), lambda i, j: (i, j)),
        compiler_params=_cparams(("arbitrary", "arbitrary"), 48),
        name="merge",
    )(a1, a2, w1, w2, proj, proj)


def _final_kernel(mg_ref, w_ref, x_ref, gate_ref, lg_ref, lb_ref, o_ref, buf_ref, *, tn, nn):
    j = pl.program_id(1)
    buf_ref[j] = jnp.dot(mg_ref[...], w_ref[...], preferred_element_type=F32)

    @pl.when(j == nn - 1)
    def _norm():
        d = nn * tn
        ssum = 0.0
        for jj in range(nn):
            cols = slice(jj * tn, (jj + 1) * tn)
            r = DEEPNORM_ALPHA * x_ref[:, cols] + gate_ref[:, cols] * buf_ref[jj]
            buf_ref[jj] = r
            ssum = ssum + jnp.sum(r, axis=-1, keepdims=True)
        mu = ssum / d
        vsum = 0.0
        for jj in range(nn):
            vsum = vsum + jnp.sum(jnp.square(buf_ref[jj] - mu), axis=-1, keepdims=True)
        inv = lax.rsqrt(vsum / d + NORM_EPS)
        for jj in range(nn):
            cols = slice(jj * tn, (jj + 1) * tn)
            o_ref[:, cols] = (buf_ref[jj] - mu) * inv * lg_ref[:, cols] + lb_ref[:, cols]


def _final(merged, w_out, x2, mod, ln_g, ln_b):
    s, d = x2.shape
    tm = min(512, s)
    tn = 512
    nn = d // tn
    return pl.pallas_call(
        functools.partial(_final_kernel, tn=tn, nn=nn),
        out_shape=jax.ShapeDtypeStruct((s, d), F32),
        grid=(s // tm, nn),
        in_specs=[pl.BlockSpec((tm, d), lambda i, j: (i, 0)),
                  pl.BlockSpec((d, tn), lambda i, j: (0, j)),
                  pl.BlockSpec((tm, d), lambda i, j: (i, 0), pipeline_mode=pl.Buffered(1)),
                  pl.BlockSpec((1, d), lambda i, j: (0, 2)),
                  pl.BlockSpec((1, d), lambda i, j: (0, 0)),
                  pl.BlockSpec((1, d), lambda i, j: (0, 0))],
        out_specs=pl.BlockSpec((tm, d), lambda i, j: (i, 0), pipeline_mode=pl.Buffered(1)),
        scratch_shapes=[pltpu.VMEM((nn, tm, tn), F32)],
        compiler_params=_cparams(("arbitrary", "arbitrary"), 56),
        name="final",
    )(merged, w_out, x2, mod, ln_g, ln_b)


RG_TN = 512
F32_SUBLANES = 8
NARROW_A = ("k_idx", "w_idx")
NARROW_B = ("i_m", "f_m")


def _regroup_kernel(tbl_ref, main_ref, na_ref, nb_ref, o_ref, *, n_a, n_b):
    @pl.when(tbl_ref[pl.program_id(0)] >= 0)
    def _wide():
        o_ref[...] = main_ref[...].astype(BF16)

    @pl.when(tbl_ref[pl.program_id(0)] < 0)
    def _narrow():
        o_ref[:n_a, :] = na_ref[...].astype(BF16)
        o_ref[n_a:n_a + n_b, :] = nb_ref[...].astype(BF16)
        o_ref[n_a + n_b:, :] = jnp.zeros((o_ref.shape[0] - n_a - n_b, o_ref.shape[1]), BF16)


def _regroup_w(w_inT):
    d = w_inT.shape[1]
    starts = []
    for j in range(P_TOTAL // RG_TN):
        oc = j * RG_TN
        if oc >= SMALL_OFF:
            starts.append(-1)
            continue
        seg = next(n for n in P_ORDER if P_OFFSETS[n] <= oc < P_OFFSETS[n] + IN_WIDTH_OF[n])
        start = IN_OFFSETS[seg] + oc - P_OFFSETS[seg]
        assert start % F32_SUBLANES == 0, (seg, start)
        starts.append(start // F32_SUBLANES)
    n_a = sum(IN_WIDTH_OF[n] for n in NARROW_A)
    n_b = sum(IN_WIDTH_OF[n] for n in NARROW_B)
    off_a, off_b = IN_OFFSETS[NARROW_A[0]], IN_OFFSETS[NARROW_B[0]]
    grid_spec = pltpu.PrefetchScalarGridSpec(
        num_scalar_prefetch=1,
        grid=(P_TOTAL // RG_TN,),
        in_specs=[pl.BlockSpec((pl.Element(RG_TN), pl.Element(d)), lambda j, tbl: (jnp.maximum(tbl[j], 0) * F32_SUBLANES, 0)),
                  pl.BlockSpec((pl.Element(n_a), pl.Element(d)), lambda j, tbl: (off_a, 0)),
                  pl.BlockSpec((pl.Element(n_b), pl.Element(d)), lambda j, tbl: (off_b, 0))],
        out_specs=pl.BlockSpec((RG_TN, d), lambda j, tbl: (j, 0)),
    )
    return pl.pallas_call(
        functools.partial(_regroup_kernel, n_a=n_a, n_b=n_b),
        out_shape=jax.ShapeDtypeStruct((P_TOTAL, d), BF16),
        grid_spec=grid_spec,
        compiler_params=_cparams(("arbitrary",), 40),
        name="regroup",
    )(jnp.asarray(starts, I32), w_inT, w_inT, w_inT)


def _regroup_cols(a, pad_to):
    parts = [a[..., IN_OFFSETS[n]:IN_OFFSETS[n] + IN_WIDTH_OF[n]] for n in P_ORDER]
    parts.append(jnp.zeros(a.shape[:-1] + (pad_to - P_USED,), a.dtype))
    return jnp.concatenate(parts, axis=-1)


def _layer(x2, c, w_ada, b_ada, w_in, b_in, g_q, g_kv, w_uq, w_iq, w_uk, w_uv, g_kidx, b_kidx, conv_w, conv_b, g_mh,
           w_attn_out, w_mlstm_out, w_out, ln_g, ln_b):
    s, d = x2.shape
    assert d == D_MODEL and s % 1024 == 0, (s, d)
    tq, tk = 256, 512
    nsel = min(TOPK_MAX, s // 4)

    w_cat = _regroup_w(w_in.T)
    b_cat = _regroup_cols(b_in, P_TOTAL).reshape(1, P_TOTAL)
    w_uqT = w_uq.T.astype(BF16)
    w_iqT = w_iq.T.astype(BF16)
    w_ukT = w_uk.reshape(ATTN_WIDTH, KV_LORA_RANK).T.astype(BF16)
    w_uvT = w_uv.transpose(0, 2, 1).reshape(ATTN_WIDTH, KV_LORA_RANK).astype(BF16)

    mod = _ada(c.reshape(d, 1), w_ada, b_ada.reshape(1, -1))
    u = _modulate(x2, mod)
    proj = _proj(u, w_cat, b_cat)

    qT, qiT = _qpath(proj, g_q.reshape(1, -1), w_uqT, w_iqT, tq)
    k, vT, kidx, widx = _kvpath(proj, g_kv.reshape(1, -1), g_kidx.reshape(1, -1), b_kidx.reshape(1, -1), w_ukT, w_uvT, tq)
    wT = widx.T.reshape(IDX_HEADS, 1, s)
    keys, thr = _indexer(kidx, qiT, wT, tq, nsel)
    a_attn = _attention(qT, k, vT, keys, thr, proj, tq, tk)

    qm, km = _conv(proj, conv_w, conv_b.reshape(1, -1), tq)
    a_mlstm = _mlstm(qm, km, proj, g_mh.reshape(MLSTM_HEADS, 1, MLSTM_V_DIM), MLSTM_CHUNK)

    merged = _merge(a_attn, a_mlstm, w_attn_out.astype(BF16), w_mlstm_out.astype(BF16), proj)
    return _final(merged, w_out.astype(BF16), x2, mod, ln_g.reshape(1, -1), ln_b.reshape(1, -1))


def kernel(x, c, w_ada, b_ada, w_in, b_in, g_q, g_kv, w_uq, w_iq, w_uk, w_uv, g_kidx, b_kidx, conv_w, conv_b, g_mh,
           w_attn_out, w_mlstm_out, w_out, ln_g, ln_b):
    bsz, seq, d = x.shape
    assert bsz == 1 and w_ada.shape[0] == 1, "single batch, single layer"
    out = _layer(x.reshape(seq, d), c, w_ada[0], b_ada[0], w_in[0], b_in[0], g_q[0], g_kv[0], w_uq[0], w_iq[0],
                 w_uk[0], w_uv[0], g_kidx[0], b_kidx[0], conv_w[0], conv_b[0], g_mh[0], w_attn_out[0],
                 w_mlstm_out[0], w_out[0], ln_g[0], ln_b[0])
    return out.reshape(bsz, seq, d)
```

```python
import functools

import jax
import jax.numpy as jnp
from jax import lax
from jax.experimental import pallas as pl
from jax.experimental.pallas import tpu as pltpu

F32 = jnp.float32
BF16 = jnp.bfloat16
I32 = jnp.int32

D_MODEL = 4096
ATTN_HEADS = 32
ATTN_HEAD_DIM = 128
ATTN_WIDTH = ATTN_HEADS * ATTN_HEAD_DIM
Q_LORA_RANK = 1024
KV_LORA_RANK = 512
IDX_HEADS = 32
IDX_HEAD_DIM = 64
TOPK_MAX = 256
MLSTM_HEADS = 8
MLSTM_QK_DIM = (D_MODEL // 2) // MLSTM_HEADS
MLSTM_V_DIM = D_MODEL // MLSTM_HEADS
MLSTM_QK_WIDTH = MLSTM_HEADS * MLSTM_QK_DIM
MLSTM_WIDTH = MLSTM_HEADS * MLSTM_V_DIM
MLSTM_CHUNK = 256
MLSTM_GROUP = 2
CONV_WIDTH = 4
GATE_SOFTCAP = 15.0
DEEPNORM_ALPHA = 2.0 ** 0.25
NORM_EPS = 1e-6

IN_WIDTHS = (Q_LORA_RANK, KV_LORA_RANK, IDX_HEAD_DIM, IDX_HEADS, ATTN_WIDTH, 2 * MLSTM_QK_WIDTH, MLSTM_WIDTH,
             MLSTM_WIDTH, MLSTM_HEADS, MLSTM_HEADS, MLSTM_WIDTH, D_MODEL, D_MODEL)
IN_NAMES = ("q_lat", "kv_lat", "k_idx", "w_idx", "z_attn", "qk_m", "v_m", "o_m", "i_m", "f_m", "z_m", "g_attn", "g_mlstm")
IN_OFFSETS = {n: sum(IN_WIDTHS[:i]) for i, n in enumerate(IN_NAMES)}
IN_WIDTH_OF = dict(zip(IN_NAMES, IN_WIDTHS))

P_ORDER = ("z_attn", "qk_m", "v_m", "o_m", "z_m", "g_attn", "g_mlstm", "q_lat", "kv_lat", "k_idx", "w_idx", "i_m", "f_m")
P_OFFSETS = {}
_off = 0
for _n in P_ORDER:
    P_OFFSETS[_n] = _off
    _off += IN_WIDTH_OF[_n]
P_USED = _off
PROJ_TN = 1024
P_TOTAL = -(-P_USED // PROJ_TN) * PROJ_TN
SMALL_W = 128
SMALL_OFF = P_OFFSETS["k_idx"]
P_MAIN = P_OFFSETS["kv_lat"]
P_TAIL = P_TOTAL - P_MAIN
assert P_MAIN % PROJ_TN == 0 and P_TAIL % PROJ_TN == 0
SM_WIDX = IDX_HEAD_DIM
SM_I = SM_WIDX + IDX_HEADS
SM_F = SM_I + MLSTM_HEADS

VMEM_CAP_BYTES = 60 * 1024 * 1024

MASK_BIAS = -1e30
M_INIT = -1e20
LOG2E = 1.4426950408889634
V_ONES = 16
V_ROWS = ATTN_HEAD_DIM + V_ONES
ATTN_ROWS = 256
HEAD_GROUP = 32
INT_MIN = -2 ** 31
KEY_NEG_INF = INT_MIN + 0x7FFFFF


def _cparams(sem, vmem_mb):
    return pltpu.CompilerParams(dimension_semantics=sem, vmem_limit_bytes=min(vmem_mb * 1024 * 1024, VMEM_CAP_BYTES))


def _sigmoid(x):
    return jax.nn.sigmoid(x)


def _silu(x):
    return x * jax.nn.sigmoid(x)


def _const_spec(shape):
    nd = len(shape)
    return pl.BlockSpec(shape, lambda *_: (0,) * nd, pipeline_mode=pl.Buffered(1))


def _ada_kernel(c_ref, w_ref, b_ref, o_ref):
    c = c_ref[...]
    o_ref[...] = jnp.sum(w_ref[...] * _silu(c), axis=0, keepdims=True) + b_ref[...]


def _ada(c_col, w_ada, b_ada):
    d, n = w_ada.shape
    tn = 512
    return pl.pallas_call(
        _ada_kernel,
        out_shape=jax.ShapeDtypeStruct((1, n), F32),
        grid=(n // tn,),
        in_specs=[pl.BlockSpec((d, 1), lambda j: (0, 0)),
                  pl.BlockSpec((d, tn), lambda j: (0, j)),
                  pl.BlockSpec((1, tn), lambda j: (0, j))],
        out_specs=pl.BlockSpec((1, tn), lambda j: (0, j)),
        compiler_params=_cparams(("arbitrary",), 32),
        name="ada",
    )(c_col, w_ada, b_ada)


def _modulate_kernel(x_ref, shift_ref, scale_ref, u_ref):
    u_ref[...] = (x_ref[...] * (1.0 + scale_ref[...]) + shift_ref[...]).astype(BF16)


def _modulate(x2, mod):
    s, d = x2.shape
    tm = min(512, s)
    return pl.pallas_call(
        _modulate_kernel,
        out_shape=jax.ShapeDtypeStruct((s, d), BF16),
        grid=(s // tm,),
        in_specs=[pl.BlockSpec((tm, d), lambda i: (i, 0)),
                  pl.BlockSpec((1, d), lambda i: (0, 0)),
                  pl.BlockSpec((1, d), lambda i: (0, 1))],
        out_specs=pl.BlockSpec((tm, d), lambda i: (i, 0)),
        compiler_params=_cparams(("arbitrary",), 40),
        name="modulate",
    )(x2, mod, mod)


def _proj_kernel(u_ref, w_ref, b_ref, o_ref):
    nt = (((1,), (1,)), ((), ()))
    o_ref[...] = lax.dot_general(u_ref[...], w_ref[...], nt, preferred_element_type=F32) + b_ref[...]


def _proj(u, w_catT, b_cat):
    s, d = u.shape
    n = w_catT.shape[0]
    tm = min(1024, s)
    tn = PROJ_TN
    return pl.pallas_call(
        _proj_kernel,
        out_shape=jax.ShapeDtypeStruct((s, n), F32),
        grid=(n // tn, s // tm),
        in_specs=[pl.BlockSpec((tm, d), lambda j, i: (i, 0)),
                  pl.BlockSpec((tn, d), lambda j, i: (j, 0)),
                  pl.BlockSpec((1, tn), lambda j, i: (0, j))],
        out_specs=pl.BlockSpec((tm, tn), lambda j, i: (i, j)),
        compiler_params=_cparams(("arbitrary", "arbitrary"), 56),
        name="proj",
    )(u, w_catT, b_cat)


PM_CHUNK = 128


def _proj_main_kernel(starts_ref, u_ref, b_ref, w_hbm, o_ref, wbf_ref, st_ref, sem, *, n_m):
    j = pl.program_id(0)
    i = pl.program_id(1)
    nj = pl.num_programs(0)
    step = j * n_m + i
    cpt = PROJ_TN // PM_CHUNK
    cps = cpt // n_m
    cpw = RG_TN // PM_CHUNK

    def chunk_copy(tile, c, slot):
        win = starts_ref[tile * (PROJ_TN // RG_TN) + c // cpw]
        row0 = pl.multiple_of(win * F32_SUBLANES + (c % cpw) * PM_CHUNK, F32_SUBLANES)
        return pltpu.make_async_copy(w_hbm.at[pl.ds(row0, PM_CHUNK), :], st_ref.at[slot], sem.at[slot])

    def cast_chunk(tile, c, slot):
        rows = pl.ds(pl.multiple_of(c * PM_CHUNK, PM_CHUNK), PM_CHUNK)
        wbf_ref[tile % 2, rows, :] = st_ref[slot].astype(BF16)

    def group(g):
        tile = g // n_m + 1
        return [(tile, (g % n_m) * cps + e, (g % 2) * cps + e) for e in range(cps)]

    @pl.when(step == 0)
    def _first_tile():
        for c in range(cpt):
            cp = chunk_copy(0, c, 0)
            cp.start()
            cp.wait()
            cast_chunk(0, c, 0)

        @pl.when(nj > 1)
        def _():
            for tile, c, slot in group(0):
                chunk_copy(tile, c, slot).start()

    @pl.when((step + 1) // n_m + 1 < nj)
    def _prefetch():
        for tile, c, slot in group(step + 1):
            chunk_copy(tile, c, slot).start()

    @pl.when(j + 1 < nj)
    def _stage_next_tile():
        for tile, c, slot in group(step):
            chunk_copy(tile, c, slot).wait()
            cast_chunk(tile, c, slot)

    nt = (((1,), (1,)), ((), ()))
    o_ref[...] = lax.dot_general(u_ref[...], wbf_ref[j % 2], nt, preferred_element_type=F32) + b_ref[...]


def _proj_main(u, w_inT, b_main):
    s, d = u.shape
    n = b_main.shape[1]
    tm = min(1024, s)
    tn = PROJ_TN
    n_m = s // tm
    cps = (tn // PM_CHUNK) // n_m
    assert cps * n_m * PM_CHUNK == tn, (s, tm)
    grid_spec = pltpu.PrefetchScalarGridSpec(
        num_scalar_prefetch=1,
        grid=(n // tn, n_m),
        in_specs=[pl.BlockSpec((tm, d), lambda j, i, t: (i, 0)),
                  pl.BlockSpec((1, tn), lambda j, i, t: (0, j)),
                  pl.BlockSpec(memory_space=pl.ANY)],
        out_specs=pl.BlockSpec((tm, tn), lambda j, i, t: (i, j)),
        scratch_shapes=[pltpu.VMEM((2, tn, d), BF16),
                        pltpu.VMEM((2 * cps, PM_CHUNK, d), F32),
                        pltpu.SemaphoreType.DMA((2 * cps,))],
    )
    return pl.pallas_call(
        functools.partial(_proj_main_kernel, n_m=n_m),
        out_shape=jax.ShapeDtypeStruct((s, n), F32),
        grid_spec=grid_spec,
        compiler_params=_cparams(("arbitrary", "arbitrary"), 56),
        name="proj_main",
    )(jnp.asarray(_window_starts(0, n), I32), u, b_main, w_inT)


def _qpath_kernel(ql_ref, g_ref, wuq_ref, wiq_ref, qT_ref, qiT_ref, *, scale):
    x = ql_ref[...]
    cq = (x * lax.rsqrt(jnp.mean(x * x, axis=-1, keepdims=True) + NORM_EPS) * g_ref[...]).astype(BF16)
    nt = (((1,), (1,)), ((), ()))
    qT = lax.dot_general(wuq_ref[...], cq, nt, preferred_element_type=F32)
    qT_ref[...] = (qT * scale).reshape(qT_ref.shape).astype(BF16)
    qiT = lax.dot_general(wiq_ref[...], cq, nt, preferred_element_type=F32)
    qiT_ref[...] = qiT.reshape(qiT_ref.shape).astype(BF16)


def _qpath(proj, g_q, w_uqT, w_iqT, tq):
    s = proj.shape[0]
    r = Q_LORA_RANK
    return pl.pallas_call(
        functools.partial(_qpath_kernel, scale=ATTN_HEAD_DIM ** -0.5 * LOG2E),
        out_shape=(jax.ShapeDtypeStruct((ATTN_HEADS, ATTN_HEAD_DIM, s), BF16),
                   jax.ShapeDtypeStruct((IDX_HEADS, IDX_HEAD_DIM, s), BF16)),
        grid=(s // tq,),
        in_specs=[pl.BlockSpec((tq, r), lambda i: (i, P_OFFSETS["q_lat"] // r)),
                  _const_spec((1, r)),
                  _const_spec(w_uqT.shape),
                  _const_spec(w_iqT.shape)],
        out_specs=(pl.BlockSpec((ATTN_HEADS, ATTN_HEAD_DIM, tq), lambda i: (0, 0, i)),
                   pl.BlockSpec((IDX_HEADS, IDX_HEAD_DIM, tq), lambda i: (0, 0, i))),
        compiler_params=_cparams(("arbitrary",), 48),
        name="qpath",
    )(proj, g_q, w_uqT, w_iqT)


def _kvpath_kernel(kvl_ref, sm_ref, gkv_ref, gk_ref, bk_ref, wuk_ref, wuv_ref, k_ref, vT_ref, kidx_ref, widx_ref, *, wscale):
    x = kvl_ref[...]
    ckv = (x * lax.rsqrt(jnp.mean(x * x, axis=-1, keepdims=True) + NORM_EPS) * gkv_ref[...]).astype(BF16)
    kfull = jnp.dot(ckv, wuk_ref[...], preferred_element_type=F32)
    for h in range(ATTN_HEADS):
        k_ref[h] = kfull[:, h * ATTN_HEAD_DIM:(h + 1) * ATTN_HEAD_DIM].astype(BF16)
    nt = (((1,), (1,)), ((), ()))
    vT = lax.dot_general(wuv_ref[...], ckv, nt, preferred_element_type=F32)
    vT_ref[:, :ATTN_HEAD_DIM, :] = vT.reshape(ATTN_HEADS, ATTN_HEAD_DIM, -1).astype(BF16)
    vT_ref[:, ATTN_HEAD_DIM:, :] = jnp.ones((ATTN_HEADS, V_ONES, vT_ref.shape[2]), BF16)
    sm = sm_ref[...]
    ki = sm[:, :IDX_HEAD_DIM]
    mu = jnp.mean(ki, axis=-1, keepdims=True)
    var = jnp.mean(jnp.square(ki - mu), axis=-1, keepdims=True)
    kidx_ref[...] = ((ki - mu) * lax.rsqrt(var + NORM_EPS) * gk_ref[...] + bk_ref[...]).astype(BF16)
    widx_ref[...] = sm[:, SM_WIDX:SM_WIDX + IDX_HEADS] * wscale


def _kvpath(proj_tail, g_kv, g_kidx, b_kidx, w_ukT, w_uvT, tm):
    s = proj_tail.shape[0]
    r = KV_LORA_RANK
    return pl.pallas_call(
        functools.partial(_kvpath_kernel, wscale=IDX_HEADS ** -0.5 * IDX_HEAD_DIM ** -0.5),
        out_shape=(jax.ShapeDtypeStruct((ATTN_HEADS, s, ATTN_HEAD_DIM), BF16),
                   jax.ShapeDtypeStruct((ATTN_HEADS, V_ROWS, s), BF16),
                   jax.ShapeDtypeStruct((s, IDX_HEAD_DIM), BF16),
                   jax.ShapeDtypeStruct((s, IDX_HEADS), F32)),
        grid=(s // tm,),
        in_specs=[pl.BlockSpec((tm, r), lambda i: (i, (P_OFFSETS["kv_lat"] - P_MAIN) // r)),
                  pl.BlockSpec((tm, SMALL_W), lambda i: (i, (SMALL_OFF - P_MAIN) // SMALL_W)),
                  _const_spec((1, r)),
                  _const_spec((1, IDX_HEAD_DIM)),
                  _const_spec((1, IDX_HEAD_DIM)),
                  _const_spec(w_ukT.shape),
                  _const_spec(w_uvT.shape)],
        out_specs=(pl.BlockSpec((ATTN_HEADS, tm, ATTN_HEAD_DIM), lambda i: (0, i, 0)),
                   pl.BlockSpec((ATTN_HEADS, V_ROWS, tm), lambda i: (0, 0, i)),
                   pl.BlockSpec((tm, IDX_HEAD_DIM), lambda i: (i, 0)),
                   pl.BlockSpec((tm, IDX_HEADS), lambda i: (i, 0))),
        compiler_params=_cparams(("arbitrary",), 48),
        name="kvpath",
    )(proj_tail, proj_tail, g_kv, g_kidx, b_kidx, w_ukT, w_uvT)


def _key_to_float(key):
    bits = jnp.where(key >= 0, key, key ^ 0x7FFFFFFF)
    return jnp.where(key < KEY_NEG_INF, -jnp.inf, pltpu.bitcast(bits, F32))


def _indexer_kernel(kidx_ref, qiT_ref, wT_ref, sc_ref, thr_ref, *, seq, tq, nsel):
    i = pl.program_id(0)
    ch = 128
    cb = tq
    n_score = (i + 1) * (tq // ch)
    n_count = i + 1
    tpos = i * tq + lax.broadcasted_iota(I32, (ch, tq), 1)

    def score_chunk(c, carry):
        r0 = pl.multiple_of(c * ch, ch)
        kc = kidx_ref[pl.ds(r0, ch), :]

        acc = jnp.zeros((ch, tq), F32)
        for h in range(IDX_HEADS):
            r = jnp.dot(kc, qiT_ref[h], preferred_element_type=F32)
            acc = acc + jnp.maximum(r, 0.0) * wT_ref[h]
        spos = r0 + lax.broadcasted_iota(I32, (ch, tq), 0)
        sc_ref[pl.ds(r0, ch), :] = jnp.where(spos <= tpos, acc, -jnp.inf)
        return carry
    lax.fori_loop(0, n_score, score_chunk, 0)

    def fill_chunk(c, carry):
        sc_ref[pl.ds(pl.multiple_of(c * cb, cb), cb), :] = jnp.full((cb, tq), -jnp.inf, F32)
        return carry
    lax.fori_loop(n_count, seq // cb, fill_chunk, 0)

    def count(pred):
        def body(c, part):
            r0 = pl.multiple_of(c * cb, cb)
            m = jnp.where(pred(sc_ref[pl.ds(r0, cb), :], r0), 1, 0)
            return part + jnp.sum(m.reshape(cb // 8, 8, tq), axis=0)
        part = lax.fori_loop(0, n_count, body, jnp.zeros((8, tq), I32))
        return jnp.sum(part, axis=0, keepdims=True)

    def count_ge(cand_key):
        cand = _key_to_float(cand_key)
        return count(lambda blk, r0: blk >= cand)

    t0 = jnp.where(count_ge(jnp.zeros((1, tq), I32)) >= nsel, 0, INT_MIN).astype(I32)

    def bit_step(b, t):
        cand = t + jnp.left_shift(jnp.int32(1), 30 - b)
        return jnp.where(count_ge(cand) >= nsel, cand, t)
    thr = _key_to_float(lax.fori_loop(0, 31, bit_step, t0))
    thr_ref[...] = thr

    tie = (count(lambda blk, r0: blk >= thr) > nsel) & (thr > -jnp.inf)

    @pl.when(jnp.max(tie.astype(I32)) > 0)
    def _break_ties():
        need = nsel - count(lambda blk, r0: blk > thr)

        def eq_below(j):
            return count(lambda blk, r0: (blk == thr) & (r0 + lax.broadcasted_iota(I32, (cb, tq), 0) < j))

        def jbit(b, j):
            test = j + jnp.left_shift(jnp.int32(1), (seq.bit_length() - 2) - b)
            return jnp.where(eq_below(test) < need, test, j)
        jlast = lax.fori_loop(0, seq.bit_length() - 1, jbit, jnp.zeros((1, tq), I32))

        def demote(c, carry):
            r0 = pl.multiple_of(c * cb, cb)
            blk = sc_ref[pl.ds(r0, cb), :]
            row = r0 + lax.broadcasted_iota(I32, (cb, tq), 0)
            sc_ref[pl.ds(r0, cb), :] = jnp.where(tie & (blk == thr) & (row > jlast), -jnp.inf, blk)
            return carry
        lax.fori_loop(0, n_count, demote, 0)


def _indexer(kidx, qiT, wT, tq, nsel):
    s = kidx.shape[0]
    return pl.pallas_call(
        functools.partial(_indexer_kernel, seq=s, tq=tq, nsel=nsel),
        out_shape=(jax.ShapeDtypeStruct((s, s), F32), jax.ShapeDtypeStruct((1, s), F32)),
        grid=(s // tq,),
        in_specs=[_const_spec((s, IDX_HEAD_DIM)),
                  pl.BlockSpec((IDX_HEADS, IDX_HEAD_DIM, tq), lambda i: (0, 0, i)),
                  pl.BlockSpec((IDX_HEADS, 1, tq), lambda i: (0, 0, i))],
        out_specs=(pl.BlockSpec((s, tq), lambda i: (0, i)),
                   pl.BlockSpec((1, tq), lambda i: (0, i))),
        compiler_params=_cparams(("arbitrary",), 40),
        name="indexer",
    )(kidx, qiT, wT)


def _attn_kernel(tiles_ref, qT_ref, k_ref, vT_ref, keys_ref, thr_ref, z_ref, sl_ref, kf_ref, qf_ref, o_ref,
                 acc_ref, m_ref, mb_ref, lg_ref, p_ref, *, tq, tk):
    qi = tiles_ref[0, pl.program_id(0)]
    kj = tiles_ref[1, pl.program_id(0)]

    @pl.when(kj == 0)
    def _init():
        acc_ref[...] = jnp.zeros(acc_ref.shape, F32)
        m_ref[...] = jnp.full(m_ref.shape, M_INIT, F32)

    def _compute():
        spos = kj * tk + lax.broadcasted_iota(I32, (tk, tq), 0)
        tpos = qi * tq + lax.broadcasted_iota(I32, (tk, tq), 1)
        sel = (keys_ref[...] >= thr_ref[...]) & (spos <= tpos)
        mb_ref[...] = jnp.where(sel, 0.0, MASK_BIAS)
        tile_off = (kj * tk - qi * tq).astype(F32)

        def group(g, carry):
            def logits(u):
                h = g * HEAD_GROUP + u
                qh = jnp.concatenate([qT_ref[h], qf_ref[h]], axis=0)
                part = jnp.full((8, tq), M_INIT, F32)
                for c in range(tk // ATTN_ROWS):
                    rows = pl.ds(c * ATTN_ROWS, ATTN_ROWS)
                    kh = jnp.concatenate([k_ref[h, rows, :], kf_ref[rows, :]], axis=1)
                    lg = jnp.dot(kh, qh, preferred_element_type=F32) + mb_ref[rows, :]
                    lg_ref[u % 2, rows, :] = lg
                    part = jnp.maximum(part, jnp.max(lg.reshape(ATTN_ROWS // 8, 8, tq), axis=0))
                shift = sl_ref[h] * tile_off
                m_old = m_ref[g, u]
                return m_old, jnp.maximum(m_old, jnp.max(part, axis=0, keepdims=True) + shift), shift

            def probs(u, m_old, m_new, shift):
                m_tile = m_new - shift
                for c in range(tk // ATTN_ROWS):
                    rows = pl.ds(c * ATTN_ROWS, ATTN_ROWS)
                    p_ref[u % 2, rows, :] = jnp.exp2(lg_ref[u % 2, rows, :] - m_tile).astype(BF16)
                m_ref[g, u] = m_new
                return jnp.exp2(m_old - m_new)

            def values(u, alpha):
                h = g * HEAD_GROUP + u
                acc_ref[g, u] = alpha * acc_ref[g, u] + jnp.dot(vT_ref[h], p_ref[u % 2], preferred_element_type=F32)

            stats = logits(0)
            alpha_prev = None
            for u in range(HEAD_GROUP):
                stats_next = logits(u + 1) if u + 1 < HEAD_GROUP else None
                alpha = probs(u, *stats)
                if u >= 1:
                    values(u - 1, alpha_prev)
                stats, alpha_prev = stats_next, alpha
            values(HEAD_GROUP - 1, alpha_prev)
            return carry
        lax.fori_loop(0, ATTN_HEADS // HEAD_GROUP, group, 0)
    _compute()

    @pl.when(kj == (qi * tq + tq - 1) // tk)
    def _finish():
        for h in range(ATTN_HEADS):
            g, u = divmod(h, HEAD_GROUP)
            cols = slice(h * ATTN_HEAD_DIM, (h + 1) * ATTN_HEAD_DIM)
            acc = acc_ref[g, u]
            o = (acc[:ATTN_HEAD_DIM] * (1.0 / acc[ATTN_HEAD_DIM:ATTN_HEAD_DIM + 1])).T
            o_ref[:, cols] = (o * _silu(z_ref[:, cols])).astype(BF16)


def _alibi_features(tq, tk):
    sigma = jnp.exp2(-8.0 * jnp.arange(1, ATTN_HEADS + 1, dtype=F32) / ATTN_HEADS) * LOG2E
    s1 = sigma.astype(BF16)
    s2 = (sigma - s1.astype(F32)).astype(BF16)
    s3 = (sigma - s1.astype(F32) - s2.astype(F32)).astype(BF16)
    pieces = jnp.stack([s1, s2, s3, s1, s2, s3], axis=1)
    qf = jnp.zeros((ATTN_HEADS, ATTN_HEAD_DIM, tq), BF16)
    qf = qf.at[:, :6, :].set(jnp.broadcast_to(pieces[:, :, None], (ATTN_HEADS, 6, tq)))
    r = jnp.arange(tk, dtype=I32)
    r_hi = ((r // 256) * 256).astype(BF16)
    r_lo = (r % 256).astype(BF16)
    kf = jnp.zeros((tk, ATTN_HEAD_DIM), BF16).at[:, :6].set(jnp.stack([r_hi, r_hi, r_hi, r_lo, r_lo, r_lo], axis=1))
    return jnp.broadcast_to(sigma[:, None, None], (ATTN_HEADS, 1, tq)), kf, qf


def _attention(qT, k, vT, keys, thr, proj, tq, tk):
    s = k.shape[1]
    ng = ATTN_HEADS // HEAD_GROUP
    tiles = [(qi, kj) for qi in range(s // tq) for kj in range((qi * tq + tq - 1) // tk + 1)]
    const3 = lambda shape: pl.BlockSpec(shape, lambda i, t: (0, 0, 0), pipeline_mode=pl.Buffered(1))
    grid_spec = pltpu.PrefetchScalarGridSpec(
        num_scalar_prefetch=1,
        grid=(len(tiles),),
        in_specs=[pl.BlockSpec((ATTN_HEADS, ATTN_HEAD_DIM, tq), lambda i, t: (0, 0, t[0, i])),
                  pl.BlockSpec((ATTN_HEADS, tk, ATTN_HEAD_DIM), lambda i, t: (0, t[1, i], 0)),
                  pl.BlockSpec((ATTN_HEADS, V_ROWS, tk), lambda i, t: (0, 0, t[1, i])),
                  pl.BlockSpec((tk, tq), lambda i, t: (t[1, i], t[0, i])),
                  pl.BlockSpec((1, tq), lambda i, t: (0, t[0, i])),
                  pl.BlockSpec((tq, ATTN_WIDTH), lambda i, t: (t[0, i], P_OFFSETS["z_attn"] // ATTN_WIDTH)),
                  const3((ATTN_HEADS, 1, tq)),
                  pl.BlockSpec((tk, ATTN_HEAD_DIM), lambda i, t: (0, 0), pipeline_mode=pl.Buffered(1)),
                  const3((ATTN_HEADS, ATTN_HEAD_DIM, tq))],
        out_specs=pl.BlockSpec((tq, ATTN_WIDTH), lambda i, t: (t[0, i], 0)),
        scratch_shapes=[pltpu.VMEM((ng, HEAD_GROUP, V_ROWS, tq), F32),
                        pltpu.VMEM((ng, HEAD_GROUP, 1, tq), F32),
                        pltpu.VMEM((tk, tq), F32),
                        pltpu.VMEM((2, tk, tq), F32),
                        pltpu.VMEM((2, tk, tq), BF16)],
    )
    return pl.pallas_call(
        functools.partial(_attn_kernel, tq=tq, tk=tk),
        out_shape=jax.ShapeDtypeStruct((s, ATTN_WIDTH), BF16),
        grid_spec=grid_spec,
        compiler_params=_cparams(("arbitrary",), 56),
        name="attn",
    )(jnp.asarray(tiles, I32).T, qT, k, vT, keys, thr, proj, *_alibi_features(tq, tk))


def _conv_kernel(x_ref, prev_ref, w_ref, b_ref, q_ref, k_ref, *, kscale):
    i = pl.program_id(0)
    x = x_ref[...]
    prev = jnp.where(i > 0, prev_ref[...], 0.0)
    head = jnp.concatenate([prev, x[:8]], axis=0)
    y = b_ref[...]
    yh = b_ref[...]
    for j in range(CONV_WIDTH):
        d = CONV_WIDTH - 1 - j
        xs = x if d == 0 else pltpu.roll(x, d, 0)
        hs = head if d == 0 else pltpu.roll(head, d, 0)
        y = y + xs * w_ref[j:j + 1, :]
        yh = yh + hs[8:] * w_ref[j:j + 1, :]
    y = jnp.concatenate([yh, y[8:]], axis=0)
    y = _silu(y)
    half = y.shape[1] // 2
    q_ref[...] = y[:, :half].astype(BF16)
    k_ref[...] = (y[:, half:] * kscale).astype(BF16)


def _conv(proj, conv_w, conv_b, tm):
    s = proj.shape[0]
    c = 2 * MLSTM_QK_WIDTH
    cb = P_OFFSETS["qk_m"] // c
    return pl.pallas_call(
        functools.partial(_conv_kernel, kscale=MLSTM_QK_DIM ** -0.5),
        out_shape=(jax.ShapeDtypeStruct((s, MLSTM_QK_WIDTH), BF16), jax.ShapeDtypeStruct((s, MLSTM_QK_WIDTH), BF16)),
        grid=(s // tm,),
        in_specs=[pl.BlockSpec((tm, c), lambda i: (i, cb)),
                  pl.BlockSpec((8, c), lambda i: (jnp.maximum(i * (tm // 8) - 1, 0), cb)),
                  _const_spec((CONV_WIDTH, c)),
                  _const_spec((1, c))],
        out_specs=(pl.BlockSpec((tm, MLSTM_QK_WIDTH), lambda i: (i, 0)),
                   pl.BlockSpec((tm, MLSTM_QK_WIDTH), lambda i: (i, 0))),
        compiler_params=_cparams(("arbitrary",), 40),
        name="conv",
    )(proj, proj, conv_w, conv_b)


def _softcap(x):
    return GATE_SOFTCAP * jnp.tanh(x / GATE_SOFTCAP)


def _mlstm_kernel(q_ref, k_ref, v_ref, og_ref, z_ref, sm_ref, g_ref, out_ref, c_ref, n_ref, m_ref, *, chunk):
    ci = pl.program_id(1)
    L = chunk
    dk, dv = MLSTM_QK_DIM, MLSTM_V_DIM

    @pl.when(ci == 0)
    def _init():
        c_ref[...] = jnp.zeros(c_ref.shape, F32)
        n_ref[...] = jnp.zeros(n_ref.shape, F32)
        m_ref[...] = jnp.zeros(m_ref.shape, F32)

    sm = sm_ref[...]
    lane = lax.broadcasted_iota(I32, sm.shape, 1)
    r_i = lax.broadcasted_iota(I32, (L, L), 0)
    c_i = lax.broadcasted_iota(I32, (L, L), 1)
    eye = r_i == c_i
    tril = r_i >= c_i
    nt = (((1,), (1,)), ((), ()))
    tn = (((0,), (0,)), ((), ()))

    for j in range(MLSTM_GROUP):
        hd = pl.program_id(0) * MLSTM_GROUP + j
        ig_col = _softcap(jnp.sum(jnp.where(lane == SM_I + hd, sm, 0.0), axis=1, keepdims=True))
        fg_col = _softcap(jnp.sum(jnp.where(lane == SM_F + hd, sm, 0.0), axis=1, keepdims=True))
        logf_col = jnp.minimum(fg_col, 0.0) - jnp.log1p(jnp.exp(-jnp.abs(fg_col)))
        logf_row = jnp.sum(jnp.where(eye, logf_col, 0.0), axis=0, keepdims=True)
        ig_row = jnp.sum(jnp.where(eye, ig_col, 0.0), axis=0, keepdims=True)
        b_col = jnp.sum(jnp.where(tril, logf_row, 0.0), axis=1, keepdims=True)
        b_row = jnp.sum(jnp.where(r_i <= c_i, logf_col, 0.0), axis=0, keepdims=True)
        dmat = jnp.where(tril, b_col - b_row + ig_row, -jnp.inf)
        m_prev = m_ref[j]
        m_inter = b_col + m_prev
        m_t = jnp.maximum(m_inter, jnp.max(dmat, axis=1, keepdims=True))

        qc = q_ref[:, j * dk:(j + 1) * dk]
        kc = k_ref[:, j * dk:(j + 1) * dk]
        vc = v_ref[:, j * dv:(j + 1) * dv].astype(BF16)
        s = lax.dot_general(qc, kc, nt, preferred_element_type=F32) * jnp.exp(dmat - m_t)
        inter = jnp.exp(m_inter - m_t)
        num = (jnp.dot(s.astype(BF16), vc, preferred_element_type=F32)
               + inter * jnp.dot(qc, c_ref[j].astype(BF16), preferred_element_type=F32))
        qn = jnp.sum(qc.astype(F32) * n_ref[j], axis=1, keepdims=True)
        den = jnp.sum(s, axis=1, keepdims=True) + inter * qn
        hh = num / jnp.maximum(jnp.abs(den), jnp.exp(-m_t))

        g_last = b_col[L - 1:L, :]
        m_new = m_t[L - 1:L, :]
        wgt = jnp.exp(g_last - b_col + ig_col - m_new)
        decay = jnp.exp(g_last + m_prev - m_new)
        wk = wgt * kc.astype(F32)
        c_ref[j] = decay * c_ref[j] + lax.dot_general(wk.astype(BF16), vc, tn, preferred_element_type=F32)
        n_ref[j] = decay * n_ref[j] + jnp.sum(wk, axis=0, keepdims=True)
        m_ref[j] = m_new

        hn = hh * lax.rsqrt(jnp.mean(hh * hh, axis=-1, keepdims=True) + NORM_EPS) * g_ref[j]
        cols = slice(j * dv, (j + 1) * dv)
        out_ref[:, cols] = (hn * _sigmoid(og_ref[:, cols]) * _silu(z_ref[:, cols])).astype(BF16)


def _mlstm(qm, km, proj, proj_tail, g_mh3, chunk):
    s = qm.shape[0]
    gdk, gdv = MLSTM_GROUP * MLSTM_QK_DIM, MLSTM_GROUP * MLSTM_V_DIM
    vb, ob, zb = (P_OFFSETS[n] // gdv for n in ("v_m", "o_m", "z_m"))
    return pl.pallas_call(
        functools.partial(_mlstm_kernel, chunk=chunk),
        out_shape=jax.ShapeDtypeStruct((s, MLSTM_WIDTH), BF16),
        grid=(MLSTM_HEADS // MLSTM_GROUP, s // chunk),
        in_specs=[pl.BlockSpec((chunk, gdk), lambda h, c: (c, h)),
                  pl.BlockSpec((chunk, gdk), lambda h, c: (c, h)),
                  pl.BlockSpec((chunk, gdv), lambda h, c: (c, vb + h)),
                  pl.BlockSpec((chunk, gdv), lambda h, c: (c, ob + h)),
                  pl.BlockSpec((chunk, gdv), lambda h, c: (c, zb + h)),
                  pl.BlockSpec((chunk, SMALL_W), lambda h, c: (c, (SMALL_OFF - P_MAIN) // SMALL_W)),
                  pl.BlockSpec((MLSTM_GROUP, 1, MLSTM_V_DIM), lambda h, c: (h, 0, 0))],
        out_specs=pl.BlockSpec((chunk, gdv), lambda h, c: (c, h)),
        scratch_shapes=[pltpu.VMEM((MLSTM_GROUP, MLSTM_QK_DIM, MLSTM_V_DIM), F32),
                        pltpu.VMEM((MLSTM_GROUP, 1, MLSTM_QK_DIM), F32),
                        pltpu.VMEM((MLSTM_GROUP, 1, 1), F32)],
        compiler_params=_cparams(("arbitrary", "arbitrary"), 32),
        name="mlstm",
    )(qm, km, proj, proj, proj, proj_tail, g_mh3)


def _merge_kernel(a1_ref, a2_ref, w1_ref, w2_ref, ga_ref, gm_ref, o_ref):
    y1 = jnp.dot(a1_ref[...], w1_ref[...], preferred_element_type=F32)
    y2 = jnp.dot(a2_ref[...], w2_ref[...], preferred_element_type=F32)
    o_ref[...] = (_sigmoid(ga_ref[...]) * y1 + _sigmoid(gm_ref[...]) * y2).astype(BF16)


def _merge(a1, a2, w1, w2, proj):
    s, d = a1.shape
    tm = min(1024, s)
    tn = 512
    gab, gmb = P_OFFSETS["g_attn"] // tn, P_OFFSETS["g_mlstm"] // tn
    once = pl.Buffered(1)
    return pl.pallas_call(
        _merge_kernel,
        out_shape=jax.ShapeDtypeStruct((s, D_MODEL), BF16),
        grid=(s // tm, D_MODEL // tn),
        in_specs=[pl.BlockSpec((tm, d), lambda i, j: (i, 0), pipeline_mode=once),
                  pl.BlockSpec((tm, d), lambda i, j: (i, 0), pipeline_mode=once),
                  pl.BlockSpec((d, tn), lambda i, j: (0, j)),
                  pl.BlockSpec((d, tn), lambda i, j: (0, j)),
                  pl.BlockSpec((tm, tn), lambda i, j: (i, gab + j)),
                  pl.BlockSpec((tm, tn), lambda i, j: (i, gmb + j))],
        out_specs=pl.BlockSpec((tm, tn), lambda i, j: (i, j)),
        compiler_params=_cparams(("arbitrary", "arbitrary"), 48),
        name="merge",
    )(a1, a2, w1, w2, proj, proj)


def _final_kernel(mg_ref, w_ref, x_ref, gate_ref, lg_ref, lb_ref, o_ref, buf_ref, *, tn, nn):
    j = pl.program_id(1)
    buf_ref[j] = jnp.dot(mg_ref[...], w_ref[...], preferred_element_type=F32)

    @pl.when(j == nn - 1)
    def _norm():
        d = nn * tn
        ssum = 0.0
        for jj in range(nn):
            cols = slice(jj * tn, (jj + 1) * tn)
            r = DEEPNORM_ALPHA * x_ref[:, cols] + gate_ref[:, cols] * buf_ref[jj]
            buf_ref[jj] = r
            ssum = ssum + jnp.sum(r, axis=-1, keepdims=True)
        mu = ssum / d
        vsum = 0.0
        for jj in range(nn):
            vsum = vsum + jnp.sum(jnp.square(buf_ref[jj] - mu), axis=-1, keepdims=True)
        inv = lax.rsqrt(vsum / d + NORM_EPS)
        for jj in range(nn):
            cols = slice(jj * tn, (jj + 1) * tn)
            o_ref[:, cols] = (buf_ref[jj] - mu) * inv * lg_ref[:, cols] + lb_ref[:, cols]


def _final(merged, w_out, x2, mod, ln_g, ln_b):
    s, d = x2.shape
    tm = min(512, s)
    tn = 512
    nn = d // tn
    return pl.pallas_call(
        functools.partial(_final_kernel, tn=tn, nn=nn),
        out_shape=jax.ShapeDtypeStruct((s, d), F32),
        grid=(s // tm, nn),
        in_specs=[pl.BlockSpec((tm, d), lambda i, j: (i, 0)),
                  pl.BlockSpec((d, tn), lambda i, j: (0, j)),
                  pl.BlockSpec((tm, d), lambda i, j: (i, 0), pipeline_mode=pl.Buffered(1)),
                  pl.BlockSpec((1, d), lambda i, j: (0, 2)),
                  pl.BlockSpec((1, d), lambda i, j: (0, 0)),
                  pl.BlockSpec((1, d), lambda i, j: (0, 0))],
        out_specs=pl.BlockSpec((tm, d), lambda i, j: (i, 0), pipeline_mode=pl.Buffered(1)),
        scratch_shapes=[pltpu.VMEM((nn, tm, tn), F32)],
        compiler_params=_cparams(("arbitrary", "arbitrary"), 56),
        name="final",
    )(merged, w_out, x2, mod, ln_g, ln_b)


RG_TN = 512
F32_SUBLANES = 8
NARROW_A = ("k_idx", "w_idx")
NARROW_B = ("i_m", "f_m")


def _regroup_kernel(tbl_ref, main_ref, na_ref, nb_ref, o_ref, *, n_a, n_b):
    @pl.when(tbl_ref[pl.program_id(0)] >= 0)
    def _wide():
        o_ref[...] = main_ref[...].astype(BF16)

    @pl.when(tbl_ref[pl.program_id(0)] < 0)
    def _narrow():
        o_ref[:n_a, :] = na_ref[...].astype(BF16)
        o_ref[n_a:n_a + n_b, :] = nb_ref[...].astype(BF16)
        o_ref[n_a + n_b:, :] = jnp.zeros((o_ref.shape[0] - n_a - n_b, o_ref.shape[1]), BF16)


def _window_starts(first_col, n_cols):
    starts = []
    for oc in range(first_col, first_col + n_cols, RG_TN):
        if oc >= SMALL_OFF:
            starts.append(-1)
            continue
        seg = next(n for n in P_ORDER if P_OFFSETS[n] <= oc < P_OFFSETS[n] + IN_WIDTH_OF[n])
        start = IN_OFFSETS[seg] + oc - P_OFFSETS[seg]
        assert start % F32_SUBLANES == 0, (seg, start)
        starts.append(start // F32_SUBLANES)
    return starts


def _regroup_w(w_inT, first_col, n_cols):
    d = w_inT.shape[1]
    starts = _window_starts(first_col, n_cols)
    n_a = sum(IN_WIDTH_OF[n] for n in NARROW_A)
    n_b = sum(IN_WIDTH_OF[n] for n in NARROW_B)
    off_a, off_b = IN_OFFSETS[NARROW_A[0]], IN_OFFSETS[NARROW_B[0]]
    grid_spec = pltpu.PrefetchScalarGridSpec(
        num_scalar_prefetch=1,
        grid=(n_cols // RG_TN,),
        in_specs=[pl.BlockSpec((pl.Element(RG_TN), pl.Element(d)), lambda j, tbl: (jnp.maximum(tbl[j], 0) * F32_SUBLANES, 0)),
                  pl.BlockSpec((pl.Element(n_a), pl.Element(d)), lambda j, tbl: (off_a, 0)),
                  pl.BlockSpec((pl.Element(n_b), pl.Element(d)), lambda j, tbl: (off_b, 0))],
        out_specs=pl.BlockSpec((RG_TN, d), lambda j, tbl: (j, 0)),
    )
    return pl.pallas_call(
        functools.partial(_regroup_kernel, n_a=n_a, n_b=n_b),
        out_shape=jax.ShapeDtypeStruct((n_cols, d), BF16),
        grid_spec=grid_spec,
        compiler_params=_cparams(("arbitrary",), 40),
        name="regroup",
    )(jnp.asarray(starts, I32), w_inT, w_inT, w_inT)


def _regroup_cols(a, pad_to):
    parts = [a[..., IN_OFFSETS[n]:IN_OFFSETS[n] + IN_WIDTH_OF[n]] for n in P_ORDER]
    parts.append(jnp.zeros(a.shape[:-1] + (pad_to - P_USED,), a.dtype))
    return jnp.concatenate(parts, axis=-1)


def _layer(x2, c, w_ada, b_ada, w_in, b_in, g_q, g_kv, w_uq, w_iq, w_uk, w_uv, g_kidx, b_kidx, conv_w, conv_b, g_mh,
           w_attn_out, w_mlstm_out, w_out, ln_g, ln_b):
    s, d = x2.shape
    assert d == D_MODEL and s % 1024 == 0, (s, d)
    tq, tk = 256, 512
    nsel = min(TOPK_MAX, s // 4)

    w_inT = w_in.T
    w_tail = _regroup_w(w_inT, P_MAIN, P_TAIL)
    b_cat = _regroup_cols(b_in, P_TOTAL).reshape(1, P_TOTAL)
    w_uqT = w_uq.T.astype(BF16)
    w_iqT = w_iq.T.astype(BF16)
    w_ukT = w_uk.reshape(ATTN_WIDTH, KV_LORA_RANK).T.astype(BF16)
    w_uvT = w_uv.transpose(0, 2, 1).reshape(ATTN_WIDTH, KV_LORA_RANK).astype(BF16)

    mod = _ada(c.reshape(d, 1), w_ada, b_ada.reshape(1, -1))
    u = _modulate(x2, mod)
    proj = _proj_main(u, w_inT, b_cat[:, :P_MAIN])
    proj_tail = _proj(u, w_tail, b_cat[:, P_MAIN:])

    qT, qiT = _qpath(proj, g_q.reshape(1, -1), w_uqT, w_iqT, tq)
    k, vT, kidx, widx = _kvpath(proj_tail, g_kv.reshape(1, -1), g_kidx.reshape(1, -1), b_kidx.reshape(1, -1), w_ukT, w_uvT, tq)
    wT = widx.T.reshape(IDX_HEADS, 1, s)
    keys, thr = _indexer(kidx, qiT, wT, tq, nsel)
    a_attn = _attention(qT, k, vT, keys, thr, proj, tq, tk)

    qm, km = _conv(proj, conv_w, conv_b.reshape(1, -1), tq)
    a_mlstm = _mlstm(qm, km, proj, proj_tail, g_mh.reshape(MLSTM_HEADS, 1, MLSTM_V_DIM), MLSTM_CHUNK)

    merged = _merge(a_attn, a_mlstm, w_attn_out.astype(BF16), w_mlstm_out.astype(BF16), proj)
    return _final(merged, w_out.astype(BF16), x2, mod, ln_g.reshape(1, -1), ln_b.reshape(1, -1))


def kernel(x, c, w_ada, b_ada, w_in, b_in, g_q, g_kv, w_uq, w_iq, w_uk, w_uv, g_kidx, b_kidx, conv_w, conv_b, g_mh,
           w_attn_out, w_mlstm_out, w_out, ln_g, ln_b):
    bsz, seq, d = x.shape
    assert bsz == 1 and w_ada.shape[0] == 1, "single batch, single layer"
    out = _layer(x.reshape(seq, d), c, w_ada[0], b_ada[0], w_in[0], b_in[0], g_q[0], g_kv[0], w_uq[0], w_iq[0],
                 w_uk[0], w_uv[0], g_kidx[0], b_kidx[0], conv_w[0], conv_b[0], g_mh[0], w_attn_out[0],
                 w_mlstm_out[0], w_out[0], ln_g[0], ln_b[0])
    return out.reshape(bsz, seq, d)
```

```python
import functools

import jax
import jax.numpy as jnp
from jax import lax
from jax.experimental import pallas as pl
from jax.experimental.pallas import tpu as pltpu

F32 = jnp.float32
BF16 = jnp.bfloat16
I32 = jnp.int32

D_MODEL = 4096
ATTN_HEADS = 32
ATTN_HEAD_DIM = 128
ATTN_WIDTH = ATTN_HEADS * ATTN_HEAD_DIM
Q_LORA_RANK = 1024
KV_LORA_RANK = 512
IDX_HEADS = 32
IDX_HEAD_DIM = 64
TOPK_MAX = 256
MLSTM_HEADS = 8
MLSTM_QK_DIM = (D_MODEL // 2) // MLSTM_HEADS
MLSTM_V_DIM = D_MODEL // MLSTM_HEADS
MLSTM_QK_WIDTH = MLSTM_HEADS * MLSTM_QK_DIM
MLSTM_WIDTH = MLSTM_HEADS * MLSTM_V_DIM
MLSTM_CHUNK = 256
MLSTM_GROUP = 2
CONV_WIDTH = 4
GATE_SOFTCAP = 15.0
DEEPNORM_ALPHA = 2.0 ** 0.25
NORM_EPS = 1e-6

IN_WIDTHS = (Q_LORA_RANK, KV_LORA_RANK, IDX_HEAD_DIM, IDX_HEADS, ATTN_WIDTH, 2 * MLSTM_QK_WIDTH, MLSTM_WIDTH,
             MLSTM_WIDTH, MLSTM_HEADS, MLSTM_HEADS, MLSTM_WIDTH, D_MODEL, D_MODEL)
IN_NAMES = ("q_lat", "kv_lat", "k_idx", "w_idx", "z_attn", "qk_m", "v_m", "o_m", "i_m", "f_m", "z_m", "g_attn", "g_mlstm")
IN_OFFSETS = {n: sum(IN_WIDTHS[:i]) for i, n in enumerate(IN_NAMES)}
IN_WIDTH_OF = dict(zip(IN_NAMES, IN_WIDTHS))

P_ORDER = ("z_attn", "qk_m", "v_m", "o_m", "z_m", "g_attn", "g_mlstm", "q_lat", "kv_lat", "k_idx", "w_idx", "i_m", "f_m")
P_OFFSETS = {}
_off = 0
for _n in P_ORDER:
    P_OFFSETS[_n] = _off
    _off += IN_WIDTH_OF[_n]
P_USED = _off
PROJ_TN = 1024
P_TOTAL = -(-P_USED // PROJ_TN) * PROJ_TN
SMALL_W = 128
SMALL_OFF = P_OFFSETS["k_idx"]
P_MAIN = P_OFFSETS["kv_lat"]
P_TAIL = P_TOTAL - P_MAIN
assert P_MAIN % PROJ_TN == 0 and P_TAIL % PROJ_TN == 0
SM_WIDX = IDX_HEAD_DIM
SM_I = SM_WIDX + IDX_HEADS
SM_F = SM_I + MLSTM_HEADS

VMEM_CAP_BYTES = 60 * 1024 * 1024

LOG2E = 1.4426950408889634
V_ONES = 16
V_ROWS = ATTN_HEAD_DIM + V_ONES
ATTN_ROWS = 256
HEAD_GROUP = 32
INT_MIN = -2 ** 31
KEY_NEG_INF = INT_MIN + 0x7FFFFF


def _cparams(sem, vmem_mb):
    return pltpu.CompilerParams(dimension_semantics=sem, vmem_limit_bytes=min(vmem_mb * 1024 * 1024, VMEM_CAP_BYTES))


def _sigmoid(x):
    return jax.nn.sigmoid(x)


def _silu(x):
    return x * jax.nn.sigmoid(x)


def _const_spec(shape):
    nd = len(shape)
    return pl.BlockSpec(shape, lambda *_: (0,) * nd, pipeline_mode=pl.Buffered(1))


def _ada_kernel(c_ref, w_ref, b_ref, o_ref):
    c = c_ref[...]
    o_ref[...] = jnp.sum(w_ref[...] * _silu(c), axis=0, keepdims=True) + b_ref[...]


def _ada(c_col, w_ada, b_ada):
    d, n = w_ada.shape
    tn = 512
    return pl.pallas_call(
        _ada_kernel,
        out_shape=jax.ShapeDtypeStruct((1, n), F32),
        grid=(n // tn,),
        in_specs=[pl.BlockSpec((d, 1), lambda j: (0, 0)),
                  pl.BlockSpec((d, tn), lambda j: (0, j)),
                  pl.BlockSpec((1, tn), lambda j: (0, j))],
        out_specs=pl.BlockSpec((1, tn), lambda j: (0, j)),
        compiler_params=_cparams(("arbitrary",), 32),
        name="ada",
    )(c_col, w_ada, b_ada)


def _modulate_kernel(x_ref, shift_ref, scale_ref, u_ref):
    u_ref[...] = (x_ref[...] * (1.0 + scale_ref[...]) + shift_ref[...]).astype(BF16)


def _modulate(x2, mod):
    s, d = x2.shape
    tm = min(512, s)
    return pl.pallas_call(
        _modulate_kernel,
        out_shape=jax.ShapeDtypeStruct((s, d), BF16),
        grid=(s // tm,),
        in_specs=[pl.BlockSpec((tm, d), lambda i: (i, 0)),
                  pl.BlockSpec((1, d), lambda i: (0, 0)),
                  pl.BlockSpec((1, d), lambda i: (0, 1))],
        out_specs=pl.BlockSpec((tm, d), lambda i: (i, 0)),
        compiler_params=_cparams(("arbitrary",), 40),
        name="modulate",
    )(x2, mod, mod)


def _proj_kernel(u_ref, w_ref, b_ref, o_ref):
    nt = (((1,), (1,)), ((), ()))
    o_ref[...] = lax.dot_general(u_ref[...], w_ref[...], nt, preferred_element_type=F32) + b_ref[...]


def _proj(u, w_catT, b_cat):
    s, d = u.shape
    n = w_catT.shape[0]
    tm = min(1024, s)
    tn = PROJ_TN
    return pl.pallas_call(
        _proj_kernel,
        out_shape=jax.ShapeDtypeStruct((s, n), F32),
        grid=(n // tn, s // tm),
        in_specs=[pl.BlockSpec((tm, d), lambda j, i: (i, 0)),
                  pl.BlockSpec((tn, d), lambda j, i: (j, 0)),
                  pl.BlockSpec((1, tn), lambda j, i: (0, j))],
        out_specs=pl.BlockSpec((tm, tn), lambda j, i: (i, j)),
        compiler_params=_cparams(("arbitrary", "arbitrary"), 56),
        name="proj",
    )(u, w_catT, b_cat)


PM_CHUNK = 128


def _proj_main_kernel(starts_ref, u_ref, b_ref, w_hbm, o_ref, wbf_ref, st_ref, sem, *, n_m):
    j = pl.program_id(0)
    i = pl.program_id(1)
    nj = pl.num_programs(0)
    step = j * n_m + i
    cpt = PROJ_TN // PM_CHUNK
    cps = cpt // n_m
    cpw = RG_TN // PM_CHUNK

    def chunk_copy(tile, c, slot):
        win = starts_ref[tile * (PROJ_TN // RG_TN) + c // cpw]
        row0 = pl.multiple_of(win * F32_SUBLANES + (c % cpw) * PM_CHUNK, F32_SUBLANES)
        return pltpu.make_async_copy(w_hbm.at[pl.ds(row0, PM_CHUNK), :], st_ref.at[slot], sem.at[slot])

    def cast_chunk(tile, c, slot):
        rows = pl.ds(pl.multiple_of(c * PM_CHUNK, PM_CHUNK), PM_CHUNK)
        wbf_ref[tile % 2, rows, :] = st_ref[slot].astype(BF16)

    def group(g):
        tile = g // n_m + 1
        return [(tile, (g % n_m) * cps + e, (g % 2) * cps + e) for e in range(cps)]

    @pl.when(step == 0)
    def _first_tile():
        for c in range(cpt):
            cp = chunk_copy(0, c, 0)
            cp.start()
            cp.wait()
            cast_chunk(0, c, 0)

        @pl.when(nj > 1)
        def _():
            for tile, c, slot in group(0):
                chunk_copy(tile, c, slot).start()

    @pl.when((step + 1) // n_m + 1 < nj)
    def _prefetch():
        for tile, c, slot in group(step + 1):
            chunk_copy(tile, c, slot).start()

    @pl.when(j + 1 < nj)
    def _stage_next_tile():
        for tile, c, slot in group(step):
            chunk_copy(tile, c, slot).wait()
            cast_chunk(tile, c, slot)

    nt = (((1,), (1,)), ((), ()))
    o_ref[...] = lax.dot_general(u_ref[...], wbf_ref[j % 2], nt, preferred_element_type=F32) + b_ref[...]


def _proj_main(u, w_inT, b_main):
    s, d = u.shape
    n = b_main.shape[1]
    tm = min(1024, s)
    tn = PROJ_TN
    n_m = s // tm
    cps = (tn // PM_CHUNK) // n_m
    assert cps * n_m * PM_CHUNK == tn, (s, tm)
    grid_spec = pltpu.PrefetchScalarGridSpec(
        num_scalar_prefetch=1,
        grid=(n // tn, n_m),
        in_specs=[pl.BlockSpec((tm, d), lambda j, i, t: (i, 0)),
                  pl.BlockSpec((1, tn), lambda j, i, t: (0, j)),
                  pl.BlockSpec(memory_space=pl.ANY)],
        out_specs=pl.BlockSpec((tm, tn), lambda j, i, t: (i, j)),
        scratch_shapes=[pltpu.VMEM((2, tn, d), BF16),
                        pltpu.VMEM((2 * cps, PM_CHUNK, d), F32),
                        pltpu.SemaphoreType.DMA((2 * cps,))],
    )
    return pl.pallas_call(
        functools.partial(_proj_main_kernel, n_m=n_m),
        out_shape=jax.ShapeDtypeStruct((s, n), F32),
        grid_spec=grid_spec,
        compiler_params=_cparams(("arbitrary", "arbitrary"), 56),
        name="proj_main",
    )(jnp.asarray(_window_starts(0, n), I32), u, b_main, w_inT)


def _qpath_kernel(ql_ref, g_ref, wuq_ref, wiq_ref, qT_ref, qiT_ref, *, scale):
    x = ql_ref[...]
    cq = (x * lax.rsqrt(jnp.mean(x * x, axis=-1, keepdims=True) + NORM_EPS) * g_ref[...]).astype(BF16)
    nt = (((1,), (1,)), ((), ()))
    qT = lax.dot_general(wuq_ref[...], cq, nt, preferred_element_type=F32)
    qT_ref[...] = (qT * scale).reshape(qT_ref.shape).astype(BF16)
    qiT = lax.dot_general(wiq_ref[...], cq, nt, preferred_element_type=F32)
    qiT_ref[...] = qiT.reshape(qiT_ref.shape).astype(BF16)


def _qpath(proj, g_q, w_uqT, w_iqT, tq):
    s = proj.shape[0]
    r = Q_LORA_RANK
    return pl.pallas_call(
        functools.partial(_qpath_kernel, scale=ATTN_HEAD_DIM ** -0.5 * LOG2E),
        out_shape=(jax.ShapeDtypeStruct((ATTN_HEADS, ATTN_HEAD_DIM, s), BF16),
                   jax.ShapeDtypeStruct((IDX_HEADS, IDX_HEAD_DIM, s), BF16)),
        grid=(s // tq,),
        in_specs=[pl.BlockSpec((tq, r), lambda i: (i, P_OFFSETS["q_lat"] // r)),
                  _const_spec((1, r)),
                  _const_spec(w_uqT.shape),
                  _const_spec(w_iqT.shape)],
        out_specs=(pl.BlockSpec((ATTN_HEADS, ATTN_HEAD_DIM, tq), lambda i: (0, 0, i)),
                   pl.BlockSpec((IDX_HEADS, IDX_HEAD_DIM, tq), lambda i: (0, 0, i))),
        compiler_params=_cparams(("arbitrary",), 48),
        name="qpath",
    )(proj, g_q, w_uqT, w_iqT)


def _kvpath_kernel(kvl_ref, sm_ref, gkv_ref, gk_ref, bk_ref, wuk_ref, wuv_ref, k_ref, vT_ref, kidx_ref, widx_ref, *, wscale):
    x = kvl_ref[...]
    ckv = (x * lax.rsqrt(jnp.mean(x * x, axis=-1, keepdims=True) + NORM_EPS) * gkv_ref[...]).astype(BF16)
    kfull = jnp.dot(ckv, wuk_ref[...], preferred_element_type=F32)
    for h in range(ATTN_HEADS):
        k_ref[h] = kfull[:, h * ATTN_HEAD_DIM:(h + 1) * ATTN_HEAD_DIM].astype(BF16)
    nt = (((1,), (1,)), ((), ()))
    vT = lax.dot_general(wuv_ref[...], ckv, nt, preferred_element_type=F32)
    vT_ref[:, :ATTN_HEAD_DIM, :] = vT.reshape(ATTN_HEADS, ATTN_HEAD_DIM, -1).astype(BF16)
    vT_ref[:, ATTN_HEAD_DIM:, :] = jnp.ones((ATTN_HEADS, V_ONES, vT_ref.shape[2]), BF16)
    sm = sm_ref[...]
    ki = sm[:, :IDX_HEAD_DIM]
    mu = jnp.mean(ki, axis=-1, keepdims=True)
    var = jnp.mean(jnp.square(ki - mu), axis=-1, keepdims=True)
    kidx_ref[...] = ((ki - mu) * lax.rsqrt(var + NORM_EPS) * gk_ref[...] + bk_ref[...]).astype(BF16)
    widx_ref[...] = sm[:, SM_WIDX:SM_WIDX + IDX_HEADS] * wscale


def _kvpath(proj_tail, g_kv, g_kidx, b_kidx, w_ukT, w_uvT, tm):
    s = proj_tail.shape[0]
    r = KV_LORA_RANK
    return pl.pallas_call(
        functools.partial(_kvpath_kernel, wscale=IDX_HEADS ** -0.5 * IDX_HEAD_DIM ** -0.5),
        out_shape=(jax.ShapeDtypeStruct((ATTN_HEADS, s, ATTN_HEAD_DIM), BF16),
                   jax.ShapeDtypeStruct((ATTN_HEADS, V_ROWS, s), BF16),
                   jax.ShapeDtypeStruct((s, IDX_HEAD_DIM), BF16),
                   jax.ShapeDtypeStruct((s, IDX_HEADS), F32)),
        grid=(s // tm,),
        in_specs=[pl.BlockSpec((tm, r), lambda i: (i, (P_OFFSETS["kv_lat"] - P_MAIN) // r)),
                  pl.BlockSpec((tm, SMALL_W), lambda i: (i, (SMALL_OFF - P_MAIN) // SMALL_W)),
                  _const_spec((1, r)),
                  _const_spec((1, IDX_HEAD_DIM)),
                  _const_spec((1, IDX_HEAD_DIM)),
                  _const_spec(w_ukT.shape),
                  _const_spec(w_uvT.shape)],
        out_specs=(pl.BlockSpec((ATTN_HEADS, tm, ATTN_HEAD_DIM), lambda i: (0, i, 0)),
                   pl.BlockSpec((ATTN_HEADS, V_ROWS, tm), lambda i: (0, 0, i)),
                   pl.BlockSpec((tm, IDX_HEAD_DIM), lambda i: (i, 0)),
                   pl.BlockSpec((tm, IDX_HEADS), lambda i: (i, 0))),
        compiler_params=_cparams(("arbitrary",), 48),
        name="kvpath",
    )(proj_tail, proj_tail, g_kv, g_kidx, b_kidx, w_ukT, w_uvT)


def _key_to_float(key):
    bits = jnp.where(key >= 0, key, key ^ 0x7FFFFFFF)
    return jnp.where(key < KEY_NEG_INF, -jnp.inf, pltpu.bitcast(bits, F32))


def _indexer_kernel(kidx_ref, qiT_ref, wT_ref, sc_ref, thr_ref, *, seq, tq, nsel):
    i = pl.program_id(0)
    ch = 128
    cb = tq
    n_score = (i + 1) * (tq // ch)
    n_count = i + 1
    tpos = i * tq + lax.broadcasted_iota(I32, (ch, tq), 1)

    def score_chunk(c, carry):
        r0 = pl.multiple_of(c * ch, ch)
        kc = kidx_ref[pl.ds(r0, ch), :]

        acc = jnp.zeros((ch, tq), F32)
        for h in range(IDX_HEADS):
            r = jnp.dot(kc, qiT_ref[h], preferred_element_type=F32)
            acc = acc + jnp.maximum(r, 0.0) * wT_ref[h]
        spos = r0 + lax.broadcasted_iota(I32, (ch, tq), 0)
        sc_ref[pl.ds(r0, ch), :] = jnp.where(spos <= tpos, acc, -jnp.inf)
        return carry
    lax.fori_loop(0, n_score, score_chunk, 0)

    def fill_chunk(c, carry):
        sc_ref[pl.ds(pl.multiple_of(c * cb, cb), cb), :] = jnp.full((cb, tq), -jnp.inf, F32)
        return carry
    lax.fori_loop(n_count, seq // cb, fill_chunk, 0)

    def count(pred):
        def body(c, part):
            r0 = pl.multiple_of(c * cb, cb)
            m = jnp.where(pred(sc_ref[pl.ds(r0, cb), :], r0), 1, 0)
            return part + jnp.sum(m.reshape(cb // 8, 8, tq), axis=0)
        part = lax.fori_loop(0, n_count, body, jnp.zeros((8, tq), I32))
        return jnp.sum(part, axis=0, keepdims=True)

    def count_ge(cand_key):
        cand = _key_to_float(cand_key)
        return count(lambda blk, r0: blk >= cand)

    t0 = jnp.where(count_ge(jnp.zeros((1, tq), I32)) >= nsel, 0, INT_MIN).astype(I32)

    def bit_step(b, t):
        cand = t + jnp.left_shift(jnp.int32(1), 30 - b)
        return jnp.where(count_ge(cand) >= nsel, cand, t)
    thr = _key_to_float(lax.fori_loop(0, 31, bit_step, t0))
    thr_ref[...] = thr

    tie = (count(lambda blk, r0: blk >= thr) > nsel) & (thr > -jnp.inf)

    @pl.when(jnp.max(tie.astype(I32)) > 0)
    def _break_ties():
        need = nsel - count(lambda blk, r0: blk > thr)

        def eq_below(j):
            return count(lambda blk, r0: (blk == thr) & (r0 + lax.broadcasted_iota(I32, (cb, tq), 0) < j))

        def jbit(b, j):
            test = j + jnp.left_shift(jnp.int32(1), (seq.bit_length() - 2) - b)
            return jnp.where(eq_below(test) < need, test, j)
        jlast = lax.fori_loop(0, seq.bit_length() - 1, jbit, jnp.zeros((1, tq), I32))

        def demote(c, carry):
            r0 = pl.multiple_of(c * cb, cb)
            blk = sc_ref[pl.ds(r0, cb), :]
            row = r0 + lax.broadcasted_iota(I32, (cb, tq), 0)
            sc_ref[pl.ds(r0, cb), :] = jnp.where(tie & (blk == thr) & (row > jlast), -jnp.inf, blk)
            return carry
        lax.fori_loop(0, n_count, demote, 0)


def _indexer(kidx, qiT, wT, tq, nsel):
    s = kidx.shape[0]
    return pl.pallas_call(
        functools.partial(_indexer_kernel, seq=s, tq=tq, nsel=nsel),
        out_shape=(jax.ShapeDtypeStruct((s, s), F32), jax.ShapeDtypeStruct((1, s), F32)),
        grid=(s // tq,),
        in_specs=[_const_spec((s, IDX_HEAD_DIM)),
                  pl.BlockSpec((IDX_HEADS, IDX_HEAD_DIM, tq), lambda i: (0, 0, i)),
                  pl.BlockSpec((IDX_HEADS, 1, tq), lambda i: (0, 0, i))],
        out_specs=(pl.BlockSpec((s, tq), lambda i: (0, i)),
                   pl.BlockSpec((1, tq), lambda i: (0, i))),
        compiler_params=_cparams(("arbitrary",), 40),
        name="indexer",
    )(kidx, qiT, wT)


def _attn_kernel(tiles_ref, qT_ref, k_ref, vT_ref, keys_ref, thr_ref, z_ref, sl_ref, kf_ref, qf_ref, o_ref,
                 acc_ref, m_ref, mb_ref, lg_ref, p_ref, *, tq, tk):
    qi = tiles_ref[0, pl.program_id(0)]
    kj = tiles_ref[1, pl.program_id(0)]

    @pl.when(kj == 0)
    def _init():
        acc_ref[...] = jnp.zeros(acc_ref.shape, F32)
        m_ref[...] = jnp.full(m_ref.shape, -jnp.inf, F32)

    def _compute():
        spos = kj * tk + lax.broadcasted_iota(I32, (tk, tq), 0)
        tpos = qi * tq + lax.broadcasted_iota(I32, (tk, tq), 1)
        sel = (keys_ref[...] >= thr_ref[...]) & (spos <= tpos)
        mb_ref[...] = jnp.where(sel, 0.0, -jnp.inf)
        tile_off = (kj * tk - qi * tq).astype(F32)

        def group(g, carry):
            def logits(u):
                h = g * HEAD_GROUP + u
                qh = jnp.concatenate([qT_ref[h], qf_ref[h]], axis=0)
                part = jnp.full((8, tq), -jnp.inf, F32)
                for c in range(tk // ATTN_ROWS):
                    rows = pl.ds(c * ATTN_ROWS, ATTN_ROWS)
                    kh = jnp.concatenate([k_ref[h, rows, :], kf_ref[rows, :]], axis=1)
                    lg = jnp.dot(kh, qh, preferred_element_type=F32) + mb_ref[rows, :]
                    lg_ref[u % 2, rows, :] = lg
                    part = jnp.maximum(part, jnp.max(lg.reshape(ATTN_ROWS // 8, 8, tq), axis=0))
                shift = sl_ref[h] * tile_off
                m_old = m_ref[g, u]
                return m_old, jnp.maximum(m_old, jnp.max(part, axis=0, keepdims=True) + shift), shift

            def probs(u, m_old, m_new, shift):
                m_safe = jnp.where(m_new == -jnp.inf, 0.0, m_new)
                m_tile = m_safe - shift
                for c in range(tk // ATTN_ROWS):
                    rows = pl.ds(c * ATTN_ROWS, ATTN_ROWS)
                    p_ref[u % 2, rows, :] = jnp.exp2(lg_ref[u % 2, rows, :] - m_tile).astype(BF16)
                m_ref[g, u] = m_new
                return jnp.exp2(m_old - m_safe)

            def values(u, alpha):
                h = g * HEAD_GROUP + u
                acc_ref[g, u] = alpha * acc_ref[g, u] + jnp.dot(vT_ref[h], p_ref[u % 2], preferred_element_type=F32)

            stats = logits(0)
            alpha_prev = None
            for u in range(HEAD_GROUP):
                stats_next = logits(u + 1) if u + 1 < HEAD_GROUP else None
                alpha = probs(u, *stats)
                if u >= 1:
                    values(u - 1, alpha_prev)
                stats, alpha_prev = stats_next, alpha
            values(HEAD_GROUP - 1, alpha_prev)
            return carry
        lax.fori_loop(0, ATTN_HEADS // HEAD_GROUP, group, 0)
    _compute()

    @pl.when(kj == (qi * tq + tq - 1) // tk)
    def _finish():
        for h in range(ATTN_HEADS):
            g, u = divmod(h, HEAD_GROUP)
            cols = slice(h * ATTN_HEAD_DIM, (h + 1) * ATTN_HEAD_DIM)
            acc = acc_ref[g, u]
            o = (acc[:ATTN_HEAD_DIM] * (1.0 / acc[ATTN_HEAD_DIM:ATTN_HEAD_DIM + 1])).T
            o_ref[:, cols] = (o * _silu(z_ref[:, cols])).astype(BF16)


def _alibi_features(tq, tk):
    sigma = jnp.exp2(-8.0 * jnp.arange(1, ATTN_HEADS + 1, dtype=F32) / ATTN_HEADS) * LOG2E
    s1 = sigma.astype(BF16)
    s2 = (sigma - s1.astype(F32)).astype(BF16)
    s3 = (sigma - s1.astype(F32) - s2.astype(F32)).astype(BF16)
    pieces = jnp.stack([s1, s2, s3, s1, s2, s3], axis=1)
    qf = jnp.zeros((ATTN_HEADS, ATTN_HEAD_DIM, tq), BF16)
    qf = qf.at[:, :6, :].set(jnp.broadcast_to(pieces[:, :, None], (ATTN_HEADS, 6, tq)))
    r = jnp.arange(tk, dtype=I32)
    r_hi = ((r // 256) * 256).astype(BF16)
    r_lo = (r % 256).astype(BF16)
    kf = jnp.zeros((tk, ATTN_HEAD_DIM), BF16).at[:, :6].set(jnp.stack([r_hi, r_hi, r_hi, r_lo, r_lo, r_lo], axis=1))
    return jnp.broadcast_to(sigma[:, None, None], (ATTN_HEADS, 1, tq)), kf, qf


def _attention(qT, k, vT, keys, thr, proj, tq, tk):
    s = k.shape[1]
    ng = ATTN_HEADS // HEAD_GROUP
    tiles = [(qi, kj) for qi in range(s // tq) for kj in range((qi * tq + tq - 1) // tk + 1)]
    const3 = lambda shape: pl.BlockSpec(shape, lambda i, t: (0, 0, 0), pipeline_mode=pl.Buffered(1))
    grid_spec = pltpu.PrefetchScalarGridSpec(
        num_scalar_prefetch=1,
        grid=(len(tiles),),
        in_specs=[pl.BlockSpec((ATTN_HEADS, ATTN_HEAD_DIM, tq), lambda i, t: (0, 0, t[0, i])),
                  pl.BlockSpec((ATTN_HEADS, tk, ATTN_HEAD_DIM), lambda i, t: (0, t[1, i], 0)),
                  pl.BlockSpec((ATTN_HEADS, V_ROWS, tk), lambda i, t: (0, 0, t[1, i])),
                  pl.BlockSpec((tk, tq), lambda i, t: (t[1, i], t[0, i])),
                  pl.BlockSpec((1, tq), lambda i, t: (0, t[0, i])),
                  pl.BlockSpec((tq, ATTN_WIDTH), lambda i, t: (t[0, i], P_OFFSETS["z_attn"] // ATTN_WIDTH)),
                  const3((ATTN_HEADS, 1, tq)),
                  pl.BlockSpec((tk, ATTN_HEAD_DIM), lambda i, t: (0, 0), pipeline_mode=pl.Buffered(1)),
                  const3((ATTN_HEADS, ATTN_HEAD_DIM, tq))],
        out_specs=pl.BlockSpec((tq, ATTN_WIDTH), lambda i, t: (t[0, i], 0)),
        scratch_shapes=[pltpu.VMEM((ng, HEAD_GROUP, V_ROWS, tq), F32),
                        pltpu.VMEM((ng, HEAD_GROUP, 1, tq), F32),
                        pltpu.VMEM((tk, tq), F32),
                        pltpu.VMEM((2, tk, tq), F32),
                        pltpu.VMEM((2, tk, tq), BF16)],
    )
    return pl.pallas_call(
        functools.partial(_attn_kernel, tq=tq, tk=tk),
        out_shape=jax.ShapeDtypeStruct((s, ATTN_WIDTH), BF16),
        grid_spec=grid_spec,
        compiler_params=_cparams(("arbitrary",), 56),
        name="attn",
    )(jnp.asarray(tiles, I32).T, qT, k, vT, keys, thr, proj, *_alibi_features(tq, tk))


def _conv_kernel(x_ref, prev_ref, w_ref, b_ref, q_ref, k_ref, *, kscale):
    i = pl.program_id(0)
    x = x_ref[...]
    prev = jnp.where(i > 0, prev_ref[...], 0.0)
    head = jnp.concatenate([prev, x[:8]], axis=0)
    y = b_ref[...]
    yh = b_ref[...]
    for j in range(CONV_WIDTH):
        d = CONV_WIDTH - 1 - j
        xs = x if d == 0 else pltpu.roll(x, d, 0)
        hs = head if d == 0 else pltpu.roll(head, d, 0)
        y = y + xs * w_ref[j:j + 1, :]
        yh = yh + hs[8:] * w_ref[j:j + 1, :]
    y = jnp.concatenate([yh, y[8:]], axis=0)
    y = _silu(y)
    half = y.shape[1] // 2
    q_ref[...] = y[:, :half].astype(BF16)
    k_ref[...] = (y[:, half:] * kscale).astype(BF16)


def _conv(proj, conv_w, conv_b, tm):
    s = proj.shape[0]
    c = 2 * MLSTM_QK_WIDTH
    cb = P_OFFSETS["qk_m"] // c
    return pl.pallas_call(
        functools.partial(_conv_kernel, kscale=MLSTM_QK_DIM ** -0.5),
        out_shape=(jax.ShapeDtypeStruct((s, MLSTM_QK_WIDTH), BF16), jax.ShapeDtypeStruct((s, MLSTM_QK_WIDTH), BF16)),
        grid=(s // tm,),
        in_specs=[pl.BlockSpec((tm, c), lambda i: (i, cb)),
                  pl.BlockSpec((8, c), lambda i: (jnp.maximum(i * (tm // 8) - 1, 0), cb)),
                  _const_spec((CONV_WIDTH, c)),
                  _const_spec((1, c))],
        out_specs=(pl.BlockSpec((tm, MLSTM_QK_WIDTH), lambda i: (i, 0)),
                   pl.BlockSpec((tm, MLSTM_QK_WIDTH), lambda i: (i, 0))),
        compiler_params=_cparams(("arbitrary",), 40),
        name="conv",
    )(proj, proj, conv_w, conv_b)


def _softcap(x):
    return GATE_SOFTCAP * jnp.tanh(x / GATE_SOFTCAP)


def _mlstm_kernel(q_ref, k_ref, v_ref, og_ref, z_ref, sm_ref, g_ref, out_ref, c_ref, n_ref, m_ref, *, chunk):
    ci = pl.program_id(1)
    L = chunk
    dk, dv = MLSTM_QK_DIM, MLSTM_V_DIM

    @pl.when(ci == 0)
    def _init():
        c_ref[...] = jnp.zeros(c_ref.shape, F32)
        n_ref[...] = jnp.zeros(n_ref.shape, F32)
        m_ref[...] = jnp.zeros(m_ref.shape, F32)

    sm = sm_ref[...]
    lane = lax.broadcasted_iota(I32, sm.shape, 1)
    r_i = lax.broadcasted_iota(I32, (L, L), 0)
    c_i = lax.broadcasted_iota(I32, (L, L), 1)
    eye = r_i == c_i
    tril = r_i >= c_i
    nt = (((1,), (1,)), ((), ()))
    tn = (((0,), (0,)), ((), ()))

    for j in range(MLSTM_GROUP):
        hd = pl.program_id(0) * MLSTM_GROUP + j
        ig_col = _softcap(jnp.sum(jnp.where(lane == SM_I + hd, sm, 0.0), axis=1, keepdims=True))
        fg_col = _softcap(jnp.sum(jnp.where(lane == SM_F + hd, sm, 0.0), axis=1, keepdims=True))
        logf_col = jnp.minimum(fg_col, 0.0) - jnp.log1p(jnp.exp(-jnp.abs(fg_col)))
        logf_row = jnp.sum(jnp.where(eye, logf_col, 0.0), axis=0, keepdims=True)
        ig_row = jnp.sum(jnp.where(eye, ig_col, 0.0), axis=0, keepdims=True)
        b_col = jnp.sum(jnp.where(tril, logf_row, 0.0), axis=1, keepdims=True)
        b_row = jnp.sum(jnp.where(r_i <= c_i, logf_col, 0.0), axis=0, keepdims=True)
        dmat = jnp.where(tril, b_col - b_row + ig_row, -jnp.inf)
        m_prev = m_ref[j]
        m_inter = b_col + m_prev
        m_t = jnp.maximum(m_inter, jnp.max(dmat, axis=1, keepdims=True))

        qc = q_ref[:, j * dk:(j + 1) * dk]
        kc = k_ref[:, j * dk:(j + 1) * dk]
        vc = v_ref[:, j * dv:(j + 1) * dv].astype(BF16)
        s = lax.dot_general(qc, kc, nt, preferred_element_type=F32) * jnp.exp(dmat - m_t)
        inter = jnp.exp(m_inter - m_t)
        num = (jnp.dot(s.astype(BF16), vc, preferred_element_type=F32)
               + inter * jnp.dot(qc, c_ref[j].astype(BF16), preferred_element_type=F32))
        qn = jnp.sum(qc.astype(F32) * n_ref[j], axis=1, keepdims=True)
        den = jnp.sum(s, axis=1, keepdims=True) + inter * qn
        hh = num / jnp.maximum(jnp.abs(den), jnp.exp(-m_t))

        g_last = b_col[L - 1:L, :]
        m_new = m_t[L - 1:L, :]
        wgt = jnp.exp(g_last - b_col + ig_col - m_new)
        decay = jnp.exp(g_last + m_prev - m_new)
        wk = wgt * kc.astype(F32)
        c_ref[j] = decay * c_ref[j] + lax.dot_general(wk.astype(BF16), vc, tn, preferred_element_type=F32)
        n_ref[j] = decay * n_ref[j] + jnp.sum(wk, axis=0, keepdims=True)
        m_ref[j] = m_new

        hn = hh * lax.rsqrt(jnp.mean(hh * hh, axis=-1, keepdims=True) + NORM_EPS) * g_ref[j]
        cols = slice(j * dv, (j + 1) * dv)
        out_ref[:, cols] = (hn * _sigmoid(og_ref[:, cols]) * _silu(z_ref[:, cols])).astype(BF16)


def _mlstm(qm, km, proj, proj_tail, g_mh3, chunk):
    s = qm.shape[0]
    gdk, gdv = MLSTM_GROUP * MLSTM_QK_DIM, MLSTM_GROUP * MLSTM_V_DIM
    vb, ob, zb = (P_OFFSETS[n] // gdv for n in ("v_m", "o_m", "z_m"))
    return pl.pallas_call(
        functools.partial(_mlstm_kernel, chunk=chunk),
        out_shape=jax.ShapeDtypeStruct((s, MLSTM_WIDTH), BF16),
        grid=(MLSTM_HEADS // MLSTM_GROUP, s // chunk),
        in_specs=[pl.BlockSpec((chunk, gdk), lambda h, c: (c, h)),
                  pl.BlockSpec((chunk, gdk), lambda h, c: (c, h)),
                  pl.BlockSpec((chunk, gdv), lambda h, c: (c, vb + h)),
                  pl.BlockSpec((chunk, gdv), lambda h, c: (c, ob + h)),
                  pl.BlockSpec((chunk, gdv), lambda h, c: (c, zb + h)),
                  pl.BlockSpec((chunk, SMALL_W), lambda h, c: (c, (SMALL_OFF - P_MAIN) // SMALL_W)),
                  pl.BlockSpec((MLSTM_GROUP, 1, MLSTM_V_DIM), lambda h, c: (h, 0, 0))],
        out_specs=pl.BlockSpec((chunk, gdv), lambda h, c: (c, h)),
        scratch_shapes=[pltpu.VMEM((MLSTM_GROUP, MLSTM_QK_DIM, MLSTM_V_DIM), F32),
                        pltpu.VMEM((MLSTM_GROUP, 1, MLSTM_QK_DIM), F32),
                        pltpu.VMEM((MLSTM_GROUP, 1, 1), F32)],
        compiler_params=_cparams(("arbitrary", "arbitrary"), 32),
        name="mlstm",
    )(qm, km, proj, proj, proj, proj_tail, g_mh3)


def _merge_kernel(a1_ref, a2_ref, w1_ref, w2_ref, ga_ref, gm_ref, o_ref):
    y1 = jnp.dot(a1_ref[...], w1_ref[...], preferred_element_type=F32)
    y2 = jnp.dot(a2_ref[...], w2_ref[...], preferred_element_type=F32)
    o_ref[...] = (_sigmoid(ga_ref[...]) * y1 + _sigmoid(gm_ref[...]) * y2).astype(BF16)


def _merge(a1, a2, w1, w2, proj):
    s, d = a1.shape
    tm = min(512, s)
    tn = 512
    gab, gmb = P_OFFSETS["g_attn"] // tn, P_OFFSETS["g_mlstm"] // tn
    return pl.pallas_call(
        _merge_kernel,
        out_shape=jax.ShapeDtypeStruct((s, D_MODEL), BF16),
        grid=(s // tm, D_MODEL // tn),
        in_specs=[pl.BlockSpec((tm, d), lambda i, j: (i, 0)),
                  pl.BlockSpec((tm, d), lambda i, j: (i, 0)),
                  pl.BlockSpec((d, tn), lambda i, j: (0, j)),
                  pl.BlockSpec((d, tn), lambda i, j: (0, j)),
                  pl.BlockSpec((tm, tn), lambda i, j: (i, gab + j)),
                  pl.BlockSpec((tm, tn), lambda i, j: (i, gmb + j))],
        out_specs=pl.BlockSpec((tm, tn), lambda i, j: (i, j)),
        compiler_params=_cparams(("arbitrary", "arbitrary"), 48),
        name="merge",
    )(a1, a2, w1, w2, proj, proj)


def _final_kernel(mg_ref, w_ref, x_ref, gate_ref, lg_ref, lb_ref, o_ref, buf_ref, *, tn, nn):
    j = pl.program_id(1)
    buf_ref[j] = jnp.dot(mg_ref[...], w_ref[...], preferred_element_type=F32)

    @pl.when(j == nn - 1)
    def _norm():
        d = nn * tn
        ssum = 0.0
        for jj in range(nn):
            cols = slice(jj * tn, (jj + 1) * tn)
            r = DEEPNORM_ALPHA * x_ref[:, cols] + gate_ref[:, cols] * buf_ref[jj]
            buf_ref[jj] = r
            ssum = ssum + jnp.sum(r, axis=-1, keepdims=True)
        mu = ssum / d
        vsum = 0.0
        for jj in range(nn):
            vsum = vsum + jnp.sum(jnp.square(buf_ref[jj] - mu), axis=-1, keepdims=True)
        inv = lax.rsqrt(vsum / d + NORM_EPS)
        for jj in range(nn):
            cols = slice(jj * tn, (jj + 1) * tn)
            o_ref[:, cols] = (buf_ref[jj] - mu) * inv * lg_ref[:, cols] + lb_ref[:, cols]


def _final(merged, w_out, x2, mod, ln_g, ln_b):
    s, d = x2.shape
    tm = min(512, s)
    tn = 512
    nn = d // tn
    return pl.pallas_call(
        functools.partial(_final_kernel, tn=tn, nn=nn),
        out_shape=jax.ShapeDtypeStruct((s, d), F32),
        grid=(s // tm, nn),
        in_specs=[pl.BlockSpec((tm, d), lambda i, j: (i, 0)),
                  pl.BlockSpec((d, tn), lambda i, j: (0, j)),
                  pl.BlockSpec((tm, d), lambda i, j: (i, 0), pipeline_mode=pl.Buffered(1)),
                  pl.BlockSpec((1, d), lambda i, j: (0, 2)),
                  pl.BlockSpec((1, d), lambda i, j: (0, 0)),
                  pl.BlockSpec((1, d), lambda i, j: (0, 0))],
        out_specs=pl.BlockSpec((tm, d), lambda i, j: (i, 0), pipeline_mode=pl.Buffered(1)),
        scratch_shapes=[pltpu.VMEM((nn, tm, tn), F32)],
        compiler_params=_cparams(("arbitrary", "arbitrary"), 56),
        name="final",
    )(merged, w_out, x2, mod, ln_g, ln_b)


RG_TN = 512
F32_SUBLANES = 8
NARROW_A = ("k_idx", "w_idx")
NARROW_B = ("i_m", "f_m")


def _regroup_kernel(tbl_ref, main_ref, na_ref, nb_ref, o_ref, *, n_a, n_b):
    @pl.when(tbl_ref[pl.program_id(0)] >= 0)
    def _wide():
        o_ref[...] = main_ref[...].astype(BF16)

    @pl.when(tbl_ref[pl.program_id(0)] < 0)
    def _narrow():
        o_ref[:n_a, :] = na_ref[...].astype(BF16)
        o_ref[n_a:n_a + n_b, :] = nb_ref[...].astype(BF16)
        o_ref[n_a + n_b:, :] = jnp.zeros((o_ref.shape[0] - n_a - n_b, o_ref.shape[1]), BF16)


def _window_starts(first_col, n_cols):
    starts = []
    for oc in range(first_col, first_col + n_cols, RG_TN):
        if oc >= SMALL_OFF:
            starts.append(-1)
            continue
        seg = next(n for n in P_ORDER if P_OFFSETS[n] <= oc < P_OFFSETS[n] + IN_WIDTH_OF[n])
        start = IN_OFFSETS[seg] + oc - P_OFFSETS[seg]
        assert start % F32_SUBLANES == 0, (seg, start)
        starts.append(start // F32_SUBLANES)
    return starts


def _regroup_w(w_inT, first_col, n_cols):
    d = w_inT.shape[1]
    starts = _window_starts(first_col, n_cols)
    n_a = sum(IN_WIDTH_OF[n] for n in NARROW_A)
    n_b = sum(IN_WIDTH_OF[n] for n in NARROW_B)
    off_a, off_b = IN_OFFSETS[NARROW_A[0]], IN_OFFSETS[NARROW_B[0]]
    grid_spec = pltpu.PrefetchScalarGridSpec(
        num_scalar_prefetch=1,
        grid=(n_cols // RG_TN,),
        in_specs=[pl.BlockSpec((pl.Element(RG_TN), pl.Element(d)), lambda j, tbl: (jnp.maximum(tbl[j], 0) * F32_SUBLANES, 0)),
                  pl.BlockSpec((pl.Element(n_a), pl.Element(d)), lambda j, tbl: (off_a, 0)),
                  pl.BlockSpec((pl.Element(n_b), pl.Element(d)), lambda j, tbl: (off_b, 0))],
        out_specs=pl.BlockSpec((RG_TN, d), lambda j, tbl: (j, 0)),
    )
    return pl.pallas_call(
        functools.partial(_regroup_kernel, n_a=n_a, n_b=n_b),
        out_shape=jax.ShapeDtypeStruct((n_cols, d), BF16),
        grid_spec=grid_spec,
        compiler_params=_cparams(("arbitrary",), 40),
        name="regroup",
    )(jnp.asarray(starts, I32), w_inT, w_inT, w_inT)


def _regroup_cols(a, pad_to):
    parts = [a[..., IN_OFFSETS[n]:IN_OFFSETS[n] + IN_WIDTH_OF[n]] for n in P_ORDER]
    parts.append(jnp.zeros(a.shape[:-1] + (pad_to - P_USED,), a.dtype))
    return jnp.concatenate(parts, axis=-1)


def _layer(x2, c, w_ada, b_ada, w_in, b_in, g_q, g_kv, w_uq, w_iq, w_uk, w_uv, g_kidx, b_kidx, conv_w, conv_b, g_mh,
           w_attn_out, w_mlstm_out, w_out, ln_g, ln_b):
    s, d = x2.shape
    assert d == D_MODEL and s % 1024 == 0, (s, d)
    tq, tk = 256, 512
    nsel = min(TOPK_MAX, s // 4)

    w_inT = w_in.T
    w_tail = _regroup_w(w_inT, P_MAIN, P_TAIL)
    b_cat = _regroup_cols(b_in, P_TOTAL).reshape(1, P_TOTAL)
    w_uqT = w_uq.T.astype(BF16)
    w_iqT = w_iq.T.astype(BF16)
    w_ukT = w_uk.reshape(ATTN_WIDTH, KV_LORA_RANK).T.astype(BF16)
    w_uvT = w_uv.transpose(0, 2, 1).reshape(ATTN_WIDTH, KV_LORA_RANK).astype(BF16)

    mod = _ada(c.reshape(d, 1), w_ada, b_ada.reshape(1, -1))
    u = _modulate(x2, mod)
    proj = _proj_main(u, w_inT, b_cat[:, :P_MAIN])
    proj_tail = _proj(u, w_tail, b_cat[:, P_MAIN:])

    qT, qiT = _qpath(proj, g_q.reshape(1, -1), w_uqT, w_iqT, tq)
    k, vT, kidx, widx = _kvpath(proj_tail, g_kv.reshape(1, -1), g_kidx.reshape(1, -1), b_kidx.reshape(1, -1), w_ukT, w_uvT, tq)
    wT = widx.T.reshape(IDX_HEADS, 1, s)
    keys, thr = _indexer(kidx, qiT, wT, tq, nsel)
    a_attn = _attention(qT, k, vT, keys, thr, proj, tq, tk)

    qm, km = _conv(proj, conv_w, conv_b.reshape(1, -1), tq)
    a_mlstm = _mlstm(qm, km, proj, proj_tail, g_mh.reshape(MLSTM_HEADS, 1, MLSTM_V_DIM), MLSTM_CHUNK)

    merged = _merge(a_attn, a_mlstm, w_attn_out.astype(BF16), w_mlstm_out.astype(BF16), proj)
    return _final(merged, w_out.astype(BF16), x2, mod, ln_g.reshape(1, -1), ln_b.reshape(1, -1))


def kernel(x, c, w_ada, b_ada, w_in, b_in, g_q, g_kv, w_uq, w_iq, w_uk, w_uv, g_kidx, b_kidx, conv_w, conv_b, g_mh,
           w_attn_out, w_mlstm_out, w_out, ln_g, ln_b):
    bsz, seq, d = x.shape
    assert bsz == 1 and w_ada.shape[0] == 1, "single batch, single layer"
    out = _layer(x.reshape(seq, d), c, w_ada[0], b_ada[0], w_in[0], b_in[0], g_q[0], g_kv[0], w_uq[0], w_iq[0],
                 w_uk[0], w_uv[0], g_kidx[0], b_kidx[0], conv_w[0], conv_b[0], g_mh[0], w_attn_out[0],
                 w_mlstm_out[0], w_out[0], ln_g[0], ln_b[0])
    return out.reshape(bsz, seq, d)
```

```python
import functools

import jax
import jax.numpy as jnp
from jax import lax
from jax.experimental import pallas as pl
from jax.experimental.pallas import tpu as pltpu

F32 = jnp.float32
BF16 = jnp.bfloat16
I32 = jnp.int32

D_MODEL = 4096
ATTN_HEADS = 32
ATTN_HEAD_DIM = 128
ATTN_WIDTH = ATTN_HEADS * ATTN_HEAD_DIM
Q_LORA_RANK = 1024
KV_LORA_RANK = 512
IDX_HEADS = 32
IDX_HEAD_DIM = 64
TOPK_MAX = 256
MLSTM_HEADS = 8
MLSTM_QK_DIM = (D_MODEL // 2) // MLSTM_HEADS
MLSTM_V_DIM = D_MODEL // MLSTM_HEADS
MLSTM_QK_WIDTH = MLSTM_HEADS * MLSTM_QK_DIM
MLSTM_WIDTH = MLSTM_HEADS * MLSTM_V_DIM
MLSTM_CHUNK = 256
MLSTM_GROUP = 2
CONV_WIDTH = 4
GATE_SOFTCAP = 15.0
DEEPNORM_ALPHA = 2.0 ** 0.25
NORM_EPS = 1e-6

IN_WIDTHS = (Q_LORA_RANK, KV_LORA_RANK, IDX_HEAD_DIM, IDX_HEADS, ATTN_WIDTH, 2 * MLSTM_QK_WIDTH, MLSTM_WIDTH,
             MLSTM_WIDTH, MLSTM_HEADS, MLSTM_HEADS, MLSTM_WIDTH, D_MODEL, D_MODEL)
IN_NAMES = ("q_lat", "kv_lat", "k_idx", "w_idx", "z_attn", "qk_m", "v_m", "o_m", "i_m", "f_m", "z_m", "g_attn", "g_mlstm")
IN_OFFSETS = {n: sum(IN_WIDTHS[:i]) for i, n in enumerate(IN_NAMES)}
IN_WIDTH_OF = dict(zip(IN_NAMES, IN_WIDTHS))

P_ORDER = ("z_attn", "qk_m", "v_m", "o_m", "z_m", "g_attn", "g_mlstm", "q_lat", "kv_lat", "k_idx", "w_idx", "i_m", "f_m")
P_OFFSETS = {}
_off = 0
for _n in P_ORDER:
    P_OFFSETS[_n] = _off
    _off += IN_WIDTH_OF[_n]
P_USED = _off
PROJ_TN = 1024
P_TOTAL = -(-P_USED // PROJ_TN) * PROJ_TN
SMALL_W = 128
SMALL_OFF = P_OFFSETS["k_idx"]
P_MAIN = P_OFFSETS["kv_lat"]
P_TAIL = P_TOTAL - P_MAIN
assert P_MAIN % PROJ_TN == 0 and P_TAIL % PROJ_TN == 0
SM_WIDX = IDX_HEAD_DIM
SM_I = SM_WIDX + IDX_HEADS
SM_F = SM_I + MLSTM_HEADS

VMEM_CAP_BYTES = 60 * 1024 * 1024

LOG2E = 1.4426950408889634
V_ONES = 16
V_ROWS = ATTN_HEAD_DIM + V_ONES
ATTN_ROWS = 256
HEAD_GROUP = 32
INT_MIN = -2 ** 31
KEY_NEG_INF = INT_MIN + 0x7FFFFF


def _cparams(sem, vmem_mb):
    return pltpu.CompilerParams(dimension_semantics=sem, vmem_limit_bytes=min(vmem_mb * 1024 * 1024, VMEM_CAP_BYTES))


def _sigmoid(x):
    return jax.nn.sigmoid(x)


def _silu(x):
    return x * jax.nn.sigmoid(x)


def _const_spec(shape):
    nd = len(shape)
    return pl.BlockSpec(shape, lambda *_: (0,) * nd, pipeline_mode=pl.Buffered(1))


def _ada_kernel(c_ref, w_ref, b_ref, o_ref):
    c = c_ref[...]
    o_ref[...] = jnp.sum(w_ref[...] * _silu(c), axis=0, keepdims=True) + b_ref[...]


def _ada(c_col, w_ada, b_ada):
    d, n = w_ada.shape
    tn = 512
    return pl.pallas_call(
        _ada_kernel,
        out_shape=jax.ShapeDtypeStruct((1, n), F32),
        grid=(n // tn,),
        in_specs=[pl.BlockSpec((d, 1), lambda j: (0, 0)),
                  pl.BlockSpec((d, tn), lambda j: (0, j)),
                  pl.BlockSpec((1, tn), lambda j: (0, j))],
        out_specs=pl.BlockSpec((1, tn), lambda j: (0, j)),
        compiler_params=_cparams(("arbitrary",), 32),
        name="ada",
    )(c_col, w_ada, b_ada)


def _modulate_kernel(x_ref, shift_ref, scale_ref, u_ref):
    u_ref[...] = (x_ref[...] * (1.0 + scale_ref[...]) + shift_ref[...]).astype(BF16)


def _modulate(x2, mod):
    s, d = x2.shape
    tm = min(512, s)
    return pl.pallas_call(
        _modulate_kernel,
        out_shape=jax.ShapeDtypeStruct((s, d), BF16),
        grid=(s // tm,),
        in_specs=[pl.BlockSpec((tm, d), lambda i: (i, 0)),
                  pl.BlockSpec((1, d), lambda i: (0, 0)),
                  pl.BlockSpec((1, d), lambda i: (0, 1))],
        out_specs=pl.BlockSpec((tm, d), lambda i: (i, 0)),
        compiler_params=_cparams(("arbitrary",), 40),
        name="modulate",
    )(x2, mod, mod)


def _proj_kernel(u_ref, w_ref, b_ref, o_ref):
    nt = (((1,), (1,)), ((), ()))
    o_ref[...] = lax.dot_general(u_ref[...], w_ref[...], nt, preferred_element_type=F32) + b_ref[...]


def _proj(u, w_catT, b_cat):
    s, d = u.shape
    n = w_catT.shape[0]
    tm = min(1024, s)
    tn = PROJ_TN
    return pl.pallas_call(
        _proj_kernel,
        out_shape=jax.ShapeDtypeStruct((s, n), F32),
        grid=(n // tn, s // tm),
        in_specs=[pl.BlockSpec((tm, d), lambda j, i: (i, 0)),
                  pl.BlockSpec((tn, d), lambda j, i: (j, 0)),
                  pl.BlockSpec((1, tn), lambda j, i: (0, j))],
        out_specs=pl.BlockSpec((tm, tn), lambda j, i: (i, j)),
        compiler_params=_cparams(("arbitrary", "arbitrary"), 56),
        name="proj",
    )(u, w_catT, b_cat)


PM_CHUNK = 128


def _proj_main_kernel(starts_ref, u_ref, b_ref, w_hbm, o_ref, wbf_ref, st_ref, sem, *, n_m):
    j = pl.program_id(0)
    i = pl.program_id(1)
    nj = pl.num_programs(0)
    step = j * n_m + i
    cpt = PROJ_TN // PM_CHUNK
    cps = cpt // n_m
    cpw = RG_TN // PM_CHUNK

    def chunk_copy(tile, c, slot):
        win = starts_ref[tile * (PROJ_TN // RG_TN) + c // cpw]
        row0 = pl.multiple_of(win * F32_SUBLANES + (c % cpw) * PM_CHUNK, F32_SUBLANES)
        return pltpu.make_async_copy(w_hbm.at[pl.ds(row0, PM_CHUNK), :], st_ref.at[slot], sem.at[slot])

    def cast_chunk(tile, c, slot):
        rows = pl.ds(pl.multiple_of(c * PM_CHUNK, PM_CHUNK), PM_CHUNK)
        wbf_ref[tile % 2, rows, :] = st_ref[slot].astype(BF16)

    def group(g):
        tile = g // n_m + 1
        return [(tile, (g % n_m) * cps + e, (g % 2) * cps + e) for e in range(cps)]

    @pl.when(step == 0)
    def _first_tile():
        for c in range(cpt):
            cp = chunk_copy(0, c, 0)
            cp.start()
            cp.wait()
            cast_chunk(0, c, 0)

        @pl.when(nj > 1)
        def _():
            for tile, c, slot in group(0):
                chunk_copy(tile, c, slot).start()

    @pl.when((step + 1) // n_m + 1 < nj)
    def _prefetch():
        for tile, c, slot in group(step + 1):
            chunk_copy(tile, c, slot).start()

    @pl.when(j + 1 < nj)
    def _stage_next_tile():
        for tile, c, slot in group(step):
            chunk_copy(tile, c, slot).wait()
            cast_chunk(tile, c, slot)

    nt = (((1,), (1,)), ((), ()))
    o_ref[...] = lax.dot_general(u_ref[...], wbf_ref[j % 2], nt, preferred_element_type=F32) + b_ref[...]


def _proj_main(u, w_inT, b_main):
    s, d = u.shape
    n = b_main.shape[1]
    tm = min(1024, s)
    tn = PROJ_TN
    n_m = s // tm
    cps = (tn // PM_CHUNK) // n_m
    assert cps * n_m * PM_CHUNK == tn, (s, tm)
    grid_spec = pltpu.PrefetchScalarGridSpec(
        num_scalar_prefetch=1,
        grid=(n // tn, n_m),
        in_specs=[pl.BlockSpec((tm, d), lambda j, i, t: (i, 0)),
                  pl.BlockSpec((1, tn), lambda j, i, t: (0, j)),
                  pl.BlockSpec(memory_space=pl.ANY)],
        out_specs=pl.BlockSpec((tm, tn), lambda j, i, t: (i, j)),
        scratch_shapes=[pltpu.VMEM((2, tn, d), BF16),
                        pltpu.VMEM((2 * cps, PM_CHUNK, d), F32),
                        pltpu.SemaphoreType.DMA((2 * cps,))],
    )
    return pl.pallas_call(
        functools.partial(_proj_main_kernel, n_m=n_m),
        out_shape=jax.ShapeDtypeStruct((s, n), F32),
        grid_spec=grid_spec,
        compiler_params=_cparams(("arbitrary", "arbitrary"), 56),
        name="proj_main",
    )(jnp.asarray(_window_starts(0, n), I32), u, b_main, w_inT)


def _qpath_kernel(ql_ref, g_ref, wuq_ref, wiq_ref, qT_ref, qiT_ref, *, scale):
    x = ql_ref[...]
    cq = (x * lax.rsqrt(jnp.mean(x * x, axis=-1, keepdims=True) + NORM_EPS) * g_ref[...]).astype(BF16)
    nt = (((1,), (1,)), ((), ()))
    qT = lax.dot_general(wuq_ref[...], cq, nt, preferred_element_type=F32)
    qT_ref[...] = (qT * scale).reshape(qT_ref.shape).astype(BF16)
    qiT = lax.dot_general(wiq_ref[...], cq, nt, preferred_element_type=F32)
    qiT_ref[...] = qiT.reshape(qiT_ref.shape).astype(BF16)


def _qpath(proj, g_q, w_uqT, w_iqT, tq):
    s = proj.shape[0]
    r = Q_LORA_RANK
    return pl.pallas_call(
        functools.partial(_qpath_kernel, scale=ATTN_HEAD_DIM ** -0.5 * LOG2E),
        out_shape=(jax.ShapeDtypeStruct((ATTN_HEADS, ATTN_HEAD_DIM, s), BF16),
                   jax.ShapeDtypeStruct((IDX_HEADS, IDX_HEAD_DIM, s), BF16)),
        grid=(s // tq,),
        in_specs=[pl.BlockSpec((tq, r), lambda i: (i, P_OFFSETS["q_lat"] // r)),
                  _const_spec((1, r)),
                  _const_spec(w_uqT.shape),
                  _const_spec(w_iqT.shape)],
        out_specs=(pl.BlockSpec((ATTN_HEADS, ATTN_HEAD_DIM, tq), lambda i: (0, 0, i)),
                   pl.BlockSpec((IDX_HEADS, IDX_HEAD_DIM, tq), lambda i: (0, 0, i))),
        compiler_params=_cparams(("arbitrary",), 48),
        name="qpath",
    )(proj, g_q, w_uqT, w_iqT)


def _kvpath_kernel(kvl_ref, sm_ref, gkv_ref, gk_ref, bk_ref, wuk_ref, wuv_ref, k_ref, vT_ref, kidx_ref, widx_ref, *, wscale):
    x = kvl_ref[...]
    ckv = (x * lax.rsqrt(jnp.mean(x * x, axis=-1, keepdims=True) + NORM_EPS) * gkv_ref[...]).astype(BF16)
    kfull = jnp.dot(ckv, wuk_ref[...], preferred_element_type=F32)
    for h in range(ATTN_HEADS):
        k_ref[h] = kfull[:, h * ATTN_HEAD_DIM:(h + 1) * ATTN_HEAD_DIM].astype(BF16)
    nt = (((1,), (1,)), ((), ()))
    vT = lax.dot_general(wuv_ref[...], ckv, nt, preferred_element_type=F32)
    vT_ref[:, :ATTN_HEAD_DIM, :] = vT.reshape(ATTN_HEADS, ATTN_HEAD_DIM, -1).astype(BF16)
    vT_ref[:, ATTN_HEAD_DIM:, :] = jnp.ones((ATTN_HEADS, V_ONES, vT_ref.shape[2]), BF16)
    sm = sm_ref[...]
    ki = sm[:, :IDX_HEAD_DIM]
    mu = jnp.mean(ki, axis=-1, keepdims=True)
    var = jnp.mean(jnp.square(ki - mu), axis=-1, keepdims=True)
    kidx_ref[...] = ((ki - mu) * lax.rsqrt(var + NORM_EPS) * gk_ref[...] + bk_ref[...]).astype(BF16)
    widx_ref[...] = sm[:, SM_WIDX:SM_WIDX + IDX_HEADS] * wscale


def _kvpath(proj_tail, g_kv, g_kidx, b_kidx, w_ukT, w_uvT, tm):
    s = proj_tail.shape[0]
    r = KV_LORA_RANK
    return pl.pallas_call(
        functools.partial(_kvpath_kernel, wscale=IDX_HEADS ** -0.5 * IDX_HEAD_DIM ** -0.5),
        out_shape=(jax.ShapeDtypeStruct((ATTN_HEADS, s, ATTN_HEAD_DIM), BF16),
                   jax.ShapeDtypeStruct((ATTN_HEADS, V_ROWS, s), BF16),
                   jax.ShapeDtypeStruct((s, IDX_HEAD_DIM), BF16),
                   jax.ShapeDtypeStruct((s, IDX_HEADS), F32)),
        grid=(s // tm,),
        in_specs=[pl.BlockSpec((tm, r), lambda i: (i, (P_OFFSETS["kv_lat"] - P_MAIN) // r)),
                  pl.BlockSpec((tm, SMALL_W), lambda i: (i, (SMALL_OFF - P_MAIN) // SMALL_W)),
                  _const_spec((1, r)),
                  _const_spec((1, IDX_HEAD_DIM)),
                  _const_spec((1, IDX_HEAD_DIM)),
                  _const_spec(w_ukT.shape),
                  _const_spec(w_uvT.shape)],
        out_specs=(pl.BlockSpec((ATTN_HEADS, tm, ATTN_HEAD_DIM), lambda i: (0, i, 0)),
                   pl.BlockSpec((ATTN_HEADS, V_ROWS, tm), lambda i: (0, 0, i)),
                   pl.BlockSpec((tm, IDX_HEAD_DIM), lambda i: (i, 0)),
                   pl.BlockSpec((tm, IDX_HEADS), lambda i: (i, 0))),
        compiler_params=_cparams(("arbitrary",), 48),
        name="kvpath",
    )(proj_tail, proj_tail, g_kv, g_kidx, b_kidx, w_ukT, w_uvT)


def _key_to_float(key):
    bits = jnp.where(key >= 0, key, key ^ 0x7FFFFFFF)
    return jnp.where(key < KEY_NEG_INF, -jnp.inf, pltpu.bitcast(bits, F32))


def _indexer_kernel(kidx_ref, qiT_ref, wT_ref, sc_ref, thr_ref, *, seq, tq, nsel):
    i = pl.program_id(0)
    ch = 128
    cb = tq
    n_score = (i + 1) * (tq // ch)
    n_count = i + 1
    tpos = i * tq + lax.broadcasted_iota(I32, (ch, tq), 1)

    def score_chunk(c, carry):
        r0 = pl.multiple_of(c * ch, ch)
        kc = kidx_ref[pl.ds(r0, ch), :]

        acc = jnp.zeros((ch, tq), F32)
        for h in range(IDX_HEADS):
            r = jnp.dot(kc, qiT_ref[h], preferred_element_type=F32)
            acc = acc + jnp.maximum(r, 0.0) * wT_ref[h]
        spos = r0 + lax.broadcasted_iota(I32, (ch, tq), 0)
        sc_ref[pl.ds(r0, ch), :] = jnp.where(spos <= tpos, acc, -jnp.inf)
        return carry
    lax.fori_loop(0, n_score, score_chunk, 0)

    def fill_chunk(c, carry):
        sc_ref[pl.ds(pl.multiple_of(c * cb, cb), cb), :] = jnp.full((cb, tq), -jnp.inf, F32)
        return carry
    lax.fori_loop(n_count, seq // cb, fill_chunk, 0)

    def count(pred):
        def body(c, part):
            r0 = pl.multiple_of(c * cb, cb)
            m = jnp.where(pred(sc_ref[pl.ds(r0, cb), :], r0), 1, 0)
            return part + jnp.sum(m.reshape(cb // 8, 8, tq), axis=0)
        part = lax.fori_loop(0, n_count, body, jnp.zeros((8, tq), I32))
        return jnp.sum(part, axis=0, keepdims=True)

    def count_ge(cand_key):
        cand = _key_to_float(cand_key)
        return count(lambda blk, r0: blk >= cand)

    t0 = jnp.where(count_ge(jnp.zeros((1, tq), I32)) >= nsel, 0, INT_MIN).astype(I32)

    def bit_step(b, t):
        cand = t + jnp.left_shift(jnp.int32(1), 30 - b)
        return jnp.where(count_ge(cand) >= nsel, cand, t)
    thr = _key_to_float(lax.fori_loop(0, 31, bit_step, t0))
    thr_ref[...] = thr

    tie = (count(lambda blk, r0: blk >= thr) > nsel) & (thr > -jnp.inf)

    @pl.when(jnp.max(tie.astype(I32)) > 0)
    def _break_ties():
        need = nsel - count(lambda blk, r0: blk > thr)

        def eq_below(j):
            return count(lambda blk, r0: (blk == thr) & (r0 + lax.broadcasted_iota(I32, (cb, tq), 0) < j))

        def jbit(b, j):
            test = j + jnp.left_shift(jnp.int32(1), (seq.bit_length() - 2) - b)
            return jnp.where(eq_below(test) < need, test, j)
        jlast = lax.fori_loop(0, seq.bit_length() - 1, jbit, jnp.zeros((1, tq), I32))

        def demote(c, carry):
            r0 = pl.multiple_of(c * cb, cb)
            blk = sc_ref[pl.ds(r0, cb), :]
            row = r0 + lax.broadcasted_iota(I32, (cb, tq), 0)
            sc_ref[pl.ds(r0, cb), :] = jnp.where(tie & (blk == thr) & (row > jlast), -jnp.inf, blk)
            return carry
        lax.fori_loop(0, n_count, demote, 0)


def _indexer(kidx, qiT, wT, tq, nsel):
    s = kidx.shape[0]
    return pl.pallas_call(
        functools.partial(_indexer_kernel, seq=s, tq=tq, nsel=nsel),
        out_shape=(jax.ShapeDtypeStruct((s, s), F32), jax.ShapeDtypeStruct((1, s), F32)),
        grid=(s // tq,),
        in_specs=[_const_spec((s, IDX_HEAD_DIM)),
                  pl.BlockSpec((IDX_HEADS, IDX_HEAD_DIM, tq), lambda i: (0, 0, i)),
                  pl.BlockSpec((IDX_HEADS, 1, tq), lambda i: (0, 0, i))],
        out_specs=(pl.BlockSpec((s, tq), lambda i: (0, i)),
                   pl.BlockSpec((1, tq), lambda i: (0, i))),
        compiler_params=_cparams(("arbitrary",), 40),
        name="indexer",
    )(kidx, qiT, wT)


def _attn_kernel(tiles_ref, qT_ref, k_ref, vT_ref, keys_ref, thr_ref, z_ref, sl_ref, kf_ref, qf_ref, o_ref,
                 acc_ref, m_ref, mb_ref, lg_ref, p_ref, *, tq, tk):
    qi = tiles_ref[0, pl.program_id(0)]
    kj = tiles_ref[1, pl.program_id(0)]

    @pl.when(kj == 0)
    def _init():
        acc_ref[...] = jnp.zeros(acc_ref.shape, F32)
        m_ref[...] = jnp.full(m_ref.shape, -jnp.inf, F32)

    def _compute():
        spos = kj * tk + lax.broadcasted_iota(I32, (tk, tq), 0)
        tpos = qi * tq + lax.broadcasted_iota(I32, (tk, tq), 1)
        sel = (keys_ref[...] >= thr_ref[...]) & (spos <= tpos)
        mb_ref[...] = jnp.where(sel, 0.0, -jnp.inf)
        tile_off = (kj * tk - qi * tq).astype(F32)

        def group(g, carry):
            def logits(u):
                h = g * HEAD_GROUP + u
                qh = jnp.concatenate([qT_ref[h], qf_ref[h]], axis=0)
                part = jnp.full((8, tq), -jnp.inf, F32)
                for c in range(tk // ATTN_ROWS):
                    rows = pl.ds(c * ATTN_ROWS, ATTN_ROWS)
                    kh = jnp.concatenate([k_ref[h, rows, :], kf_ref[rows, :]], axis=1)
                    lg = jnp.dot(kh, qh, preferred_element_type=F32) + mb_ref[rows, :]
                    lg_ref[u % 2, rows, :] = lg
                    part = jnp.maximum(part, jnp.max(lg.reshape(ATTN_ROWS // 8, 8, tq), axis=0))
                shift = sl_ref[h] * tile_off
                m_old = m_ref[g, u]
                return m_old, jnp.maximum(m_old, jnp.max(part, axis=0, keepdims=True) + shift), shift

            def probs(u, m_old, m_new, shift):
                m_safe = jnp.where(m_new == -jnp.inf, 0.0, m_new)
                m_tile = m_safe - shift
                for c in range(tk // ATTN_ROWS):
                    rows = pl.ds(c * ATTN_ROWS, ATTN_ROWS)
                    p_ref[u % 2, rows, :] = jnp.exp2(lg_ref[u % 2, rows, :] - m_tile).astype(BF16)
                m_ref[g, u] = m_new
                return jnp.exp2(m_old - m_safe)

            def values(u, alpha):
                h = g * HEAD_GROUP + u
                acc_ref[g, u] = alpha * acc_ref[g, u] + jnp.dot(vT_ref[h], p_ref[u % 2], preferred_element_type=F32)

            stats = logits(0)
            alpha_prev = None
            for u in range(HEAD_GROUP):
                stats_next = logits(u + 1) if u + 1 < HEAD_GROUP else None
                alpha = probs(u, *stats)
                if u >= 1:
                    values(u - 1, alpha_prev)
                stats, alpha_prev = stats_next, alpha
            values(HEAD_GROUP - 1, alpha_prev)
            return carry
        lax.fori_loop(0, ATTN_HEADS // HEAD_GROUP, group, 0)
    _compute()

    @pl.when(kj == (qi * tq + tq - 1) // tk)
    def _finish():
        for h in range(ATTN_HEADS):
            g, u = divmod(h, HEAD_GROUP)
            cols = slice(h * ATTN_HEAD_DIM, (h + 1) * ATTN_HEAD_DIM)
            acc = acc_ref[g, u]
            o = (acc[:ATTN_HEAD_DIM] * (1.0 / acc[ATTN_HEAD_DIM:ATTN_HEAD_DIM + 1])).T
            o_ref[:, cols] = (o * _silu(z_ref[:, cols])).astype(BF16)


def _alibi_features(tq, tk):
    sigma = jnp.exp2(-8.0 * jnp.arange(1, ATTN_HEADS + 1, dtype=F32) / ATTN_HEADS) * LOG2E
    s1 = sigma.astype(BF16)
    s2 = (sigma - s1.astype(F32)).astype(BF16)
    s3 = (sigma - s1.astype(F32) - s2.astype(F32)).astype(BF16)
    pieces = jnp.stack([s1, s2, s3, s1, s2, s3], axis=1)
    qf = jnp.zeros((ATTN_HEADS, ATTN_HEAD_DIM, tq), BF16)
    qf = qf.at[:, :6, :].set(jnp.broadcast_to(pieces[:, :, None], (ATTN_HEADS, 6, tq)))
    r = jnp.arange(tk, dtype=I32)
    r_hi = ((r // 256) * 256).astype(BF16)
    r_lo = (r % 256).astype(BF16)
    kf = jnp.zeros((tk, ATTN_HEAD_DIM), BF16).at[:, :6].set(jnp.stack([r_hi, r_hi, r_hi, r_lo, r_lo, r_lo], axis=1))
    return jnp.broadcast_to(sigma[:, None, None], (ATTN_HEADS, 1, tq)), kf, qf


def _attention(qT, k, vT, keys, thr, proj, tq, tk):
    s = k.shape[1]
    ng = ATTN_HEADS // HEAD_GROUP
    tiles = [(qi, kj) for qi in range(s // tq) for kj in range((qi * tq + tq - 1) // tk + 1)]
    const3 = lambda shape: pl.BlockSpec(shape, lambda i, t: (0, 0, 0), pipeline_mode=pl.Buffered(1))
    grid_spec = pltpu.PrefetchScalarGridSpec(
        num_scalar_prefetch=1,
        grid=(len(tiles),),
        in_specs=[pl.BlockSpec((ATTN_HEADS, ATTN_HEAD_DIM, tq), lambda i, t: (0, 0, t[0, i])),
                  pl.BlockSpec((ATTN_HEADS, tk, ATTN_HEAD_DIM), lambda i, t: (0, t[1, i], 0)),
                  pl.BlockSpec((ATTN_HEADS, V_ROWS, tk), lambda i, t: (0, 0, t[1, i])),
                  pl.BlockSpec((tk, tq), lambda i, t: (t[1, i], t[0, i])),
                  pl.BlockSpec((1, tq), lambda i, t: (0, t[0, i])),
                  pl.BlockSpec((tq, ATTN_WIDTH), lambda i, t: (t[0, i], P_OFFSETS["z_attn"] // ATTN_WIDTH)),
                  const3((ATTN_HEADS, 1, tq)),
                  pl.BlockSpec((tk, ATTN_HEAD_DIM), lambda i, t: (0, 0), pipeline_mode=pl.Buffered(1)),
                  const3((ATTN_HEADS, ATTN_HEAD_DIM, tq))],
        out_specs=pl.BlockSpec((tq, ATTN_WIDTH), lambda i, t: (t[0, i], 0)),
        scratch_shapes=[pltpu.VMEM((ng, HEAD_GROUP, V_ROWS, tq), F32),
                        pltpu.VMEM((ng, HEAD_GROUP, 1, tq), F32),
                        pltpu.VMEM((tk, tq), F32),
                        pltpu.VMEM((2, tk, tq), F32),
                        pltpu.VMEM((2, tk, tq), BF16)],
    )
    return pl.pallas_call(
        functools.partial(_attn_kernel, tq=tq, tk=tk),
        out_shape=jax.ShapeDtypeStruct((s, ATTN_WIDTH), BF16),
        grid_spec=grid_spec,
        compiler_params=_cparams(("arbitrary",), 56),
        name="attn",
    )(jnp.asarray(tiles, I32).T, qT, k, vT, keys, thr, proj, *_alibi_features(tq, tk))


def _conv_kernel(x_ref, prev_ref, w_ref, b_ref, q_ref, k_ref, *, kscale):
    i = pl.program_id(0)
    x = x_ref[...]
    prev = jnp.where(i > 0, prev_ref[...], 0.0)
    head = jnp.concatenate([prev, x[:8]], axis=0)
    y = b_ref[...]
    yh = b_ref[...]
    for j in range(CONV_WIDTH):
        d = CONV_WIDTH - 1 - j
        xs = x if d == 0 else pltpu.roll(x, d, 0)
        hs = head if d == 0 else pltpu.roll(head, d, 0)
        y = y + xs * w_ref[j:j + 1, :]
        yh = yh + hs[8:] * w_ref[j:j + 1, :]
    y = jnp.concatenate([yh, y[8:]], axis=0)
    y = _silu(y)
    half = y.shape[1] // 2
    q_ref[...] = y[:, :half].astype(BF16)
    k_ref[...] = (y[:, half:] * kscale).astype(BF16)


def _conv(proj, conv_w, conv_b, tm):
    s = proj.shape[0]
    c = 2 * MLSTM_QK_WIDTH
    cb = P_OFFSETS["qk_m"] // c
    return pl.pallas_call(
        functools.partial(_conv_kernel, kscale=MLSTM_QK_DIM ** -0.5),
        out_shape=(jax.ShapeDtypeStruct((s, MLSTM_QK_WIDTH), BF16), jax.ShapeDtypeStruct((s, MLSTM_QK_WIDTH), BF16)),
        grid=(s // tm,),
        in_specs=[pl.BlockSpec((tm, c), lambda i: (i, cb)),
                  pl.BlockSpec((8, c), lambda i: (jnp.maximum(i * (tm // 8) - 1, 0), cb)),
                  _const_spec((CONV_WIDTH, c)),
                  _const_spec((1, c))],
        out_specs=(pl.BlockSpec((tm, MLSTM_QK_WIDTH), lambda i: (i, 0)),
                   pl.BlockSpec((tm, MLSTM_QK_WIDTH), lambda i: (i, 0))),
        compiler_params=_cparams(("arbitrary",), 40),
        name="conv",
    )(proj, proj, conv_w, conv_b)


def _softcap(x):
    return GATE_SOFTCAP * jnp.tanh(x / GATE_SOFTCAP)


def _mlstm_kernel(q_ref, k_ref, v_ref, og_ref, z_ref, gt_ref, g_ref, out_ref, c_ref, n_ref, m_ref, *, chunk):
    ci = pl.program_id(1)
    L = chunk
    dk, dv = MLSTM_QK_DIM, MLSTM_V_DIM

    @pl.when(ci == 0)
    def _init():
        c_ref[...] = jnp.zeros(c_ref.shape, F32)
        n_ref[...] = jnp.zeros(n_ref.shape, F32)
        m_ref[...] = jnp.zeros(m_ref.shape, F32)

    gt = gt_ref[...]
    sub = lax.broadcasted_iota(I32, gt.shape, 0)
    r_i = lax.broadcasted_iota(I32, (L, L), 0)
    c_i = lax.broadcasted_iota(I32, (L, L), 1)
    eye = r_i == c_i
    tril = r_i >= c_i
    nt = (((1,), (1,)), ((), ()))
    tn = (((0,), (0,)), ((), ()))

    for j in range(MLSTM_GROUP):
        hd = pl.program_id(0) * MLSTM_GROUP + j
        ig_row = _softcap(jnp.sum(jnp.where(sub == hd, gt, 0.0), axis=0, keepdims=True))
        fg_row = _softcap(jnp.sum(jnp.where(sub == MLSTM_HEADS + hd, gt, 0.0), axis=0, keepdims=True))
        logf_row = jnp.minimum(fg_row, 0.0) - jnp.log1p(jnp.exp(-jnp.abs(fg_row)))
        ig_col = jnp.sum(jnp.where(eye, ig_row, 0.0), axis=1, keepdims=True)
        b_col = jnp.sum(jnp.where(tril, logf_row, 0.0), axis=1, keepdims=True)
        b_row = jnp.sum(jnp.where(eye, b_col, 0.0), axis=0, keepdims=True)
        dmat = jnp.where(tril, b_col - b_row + ig_row, -jnp.inf)
        m_prev = m_ref[j]
        m_inter = b_col + m_prev
        m_t = jnp.maximum(m_inter, jnp.max(dmat, axis=1, keepdims=True))

        qc = q_ref[:, j * dk:(j + 1) * dk]
        kc = k_ref[:, j * dk:(j + 1) * dk]
        vc = v_ref[:, j * dv:(j + 1) * dv].astype(BF16)
        s = lax.dot_general(qc, kc, nt, preferred_element_type=F32) * jnp.exp(dmat - m_t)
        inter = jnp.exp(m_inter - m_t)
        num = (jnp.dot(s.astype(BF16), vc, preferred_element_type=F32)
               + inter * jnp.dot(qc, c_ref[j].astype(BF16), preferred_element_type=F32))
        qn = jnp.sum(qc.astype(F32) * n_ref[j], axis=1, keepdims=True)
        den = jnp.sum(s, axis=1, keepdims=True) + inter * qn
        hh = num / jnp.maximum(jnp.abs(den), jnp.exp(-m_t))

        g_last = b_col[L - 1:L, :]
        m_new = m_t[L - 1:L, :]
        wgt = jnp.exp(g_last - b_col + ig_col - m_new)
        decay = jnp.exp(g_last + m_prev - m_new)
        wk = wgt * kc.astype(F32)
        c_ref[j] = decay * c_ref[j] + lax.dot_general(wk.astype(BF16), vc, tn, preferred_element_type=F32)
        n_ref[j] = decay * n_ref[j] + jnp.sum(wk, axis=0, keepdims=True)
        m_ref[j] = m_new

        hn = hh * lax.rsqrt(jnp.mean(hh * hh, axis=-1, keepdims=True) + NORM_EPS) * g_ref[j]
        cols = slice(j * dv, (j + 1) * dv)
        out_ref[:, cols] = (hn * _sigmoid(og_ref[:, cols]) * _silu(z_ref[:, cols])).astype(BF16)


def _mlstm(qm, km, proj, gates_t, g_mh3, chunk):
    s = qm.shape[0]
    gdk, gdv = MLSTM_GROUP * MLSTM_QK_DIM, MLSTM_GROUP * MLSTM_V_DIM
    vb, ob, zb = (P_OFFSETS[n] // gdv for n in ("v_m", "o_m", "z_m"))
    return pl.pallas_call(
        functools.partial(_mlstm_kernel, chunk=chunk),
        out_shape=jax.ShapeDtypeStruct((s, MLSTM_WIDTH), BF16),
        grid=(MLSTM_HEADS // MLSTM_GROUP, s // chunk),
        in_specs=[pl.BlockSpec((chunk, gdk), lambda h, c: (c, h)),
                  pl.BlockSpec((chunk, gdk), lambda h, c: (c, h)),
                  pl.BlockSpec((chunk, gdv), lambda h, c: (c, vb + h)),
                  pl.BlockSpec((chunk, gdv), lambda h, c: (c, ob + h)),
                  pl.BlockSpec((chunk, gdv), lambda h, c: (c, zb + h)),
                  pl.BlockSpec((2 * MLSTM_HEADS, chunk), lambda h, c: (0, c)),
                  pl.BlockSpec((MLSTM_GROUP, 1, MLSTM_V_DIM), lambda h, c: (h, 0, 0))],
        out_specs=pl.BlockSpec((chunk, gdv), lambda h, c: (c, h)),
        scratch_shapes=[pltpu.VMEM((MLSTM_GROUP, MLSTM_QK_DIM, MLSTM_V_DIM), F32),
                        pltpu.VMEM((MLSTM_GROUP, 1, MLSTM_QK_DIM), F32),
                        pltpu.VMEM((MLSTM_GROUP, 1, 1), F32)],
        compiler_params=_cparams(("arbitrary", "arbitrary"), 32),
        name="mlstm",
    )(qm, km, proj, proj, proj, gates_t, g_mh3)


def _merge_kernel(a1_ref, a2_ref, w1_ref, w2_ref, ga_ref, gm_ref, o_ref):
    y1 = jnp.dot(a1_ref[...], w1_ref[...], preferred_element_type=F32)
    y2 = jnp.dot(a2_ref[...], w2_ref[...], preferred_element_type=F32)
    o_ref[...] = (_sigmoid(ga_ref[...]) * y1 + _sigmoid(gm_ref[...]) * y2).astype(BF16)


def _merge(a1, a2, w1, w2, proj):
    s, d = a1.shape
    tm = min(512, s)
    tn = 512
    gab, gmb = P_OFFSETS["g_attn"] // tn, P_OFFSETS["g_mlstm"] // tn
    return pl.pallas_call(
        _merge_kernel,
        out_shape=jax.ShapeDtypeStruct((s, D_MODEL), BF16),
        grid=(s // tm, D_MODEL // tn),
        in_specs=[pl.BlockSpec((tm, d), lambda i, j: (i, 0)),
                  pl.BlockSpec((tm, d), lambda i, j: (i, 0)),
                  pl.BlockSpec((d, tn), lambda i, j: (0, j)),
                  pl.BlockSpec((d, tn), lambda i, j: (0, j)),
                  pl.BlockSpec((tm, tn), lambda i, j: (i, gab + j)),
                  pl.BlockSpec((tm, tn), lambda i, j: (i, gmb + j))],
        out_specs=pl.BlockSpec((tm, tn), lambda i, j: (i, j)),
        compiler_params=_cparams(("arbitrary", "arbitrary"), 48),
        name="merge",
    )(a1, a2, w1, w2, proj, proj)


def _final_kernel(mg_ref, w_ref, x_ref, gate_ref, lg_ref, lb_ref, o_ref, *, tn, nn):
    j = pl.program_id(1)
    y = jnp.dot(mg_ref[...], w_ref[...], preferred_element_type=F32)
    for jj in range(nn):
        @pl.when(j == jj)
        def _store(jj=jj):
            o_ref[:, jj * tn:(jj + 1) * tn] = y

    @pl.when(j == nn - 1)
    def _norm():
        d = nn * tn
        ssum = 0.0
        for jj in range(nn):
            cols = slice(jj * tn, (jj + 1) * tn)
            r = DEEPNORM_ALPHA * x_ref[:, cols] + gate_ref[:, cols] * o_ref[:, cols]
            o_ref[:, cols] = r
            ssum = ssum + jnp.sum(r, axis=-1, keepdims=True)
        mu = ssum / d
        vsum = 0.0
        for jj in range(nn):
            cols = slice(jj * tn, (jj + 1) * tn)
            vsum = vsum + jnp.sum(jnp.square(o_ref[:, cols] - mu), axis=-1, keepdims=True)
        inv = lax.rsqrt(vsum / d + NORM_EPS)
        for jj in range(nn):
            cols = slice(jj * tn, (jj + 1) * tn)
            o_ref[:, cols] = (o_ref[:, cols] - mu) * inv * lg_ref[:, cols] + lb_ref[:, cols]


def _final(merged, w_out, x2, mod, ln_g, ln_b):
    s, d = x2.shape
    tm = min(512, s)
    tn = 512
    nn = d // tn
    return pl.pallas_call(
        functools.partial(_final_kernel, tn=tn, nn=nn),
        out_shape=jax.ShapeDtypeStruct((s, d), F32),
        grid=(s // tm, nn),
        in_specs=[pl.BlockSpec((tm, d), lambda i, j: (i, 0)),
                  pl.BlockSpec((d, tn), lambda i, j: (0, j)),
                  pl.BlockSpec((tm, d), lambda i, j: (i, 0)),
                  pl.BlockSpec((1, d), lambda i, j: (0, 2)),
                  pl.BlockSpec((1, d), lambda i, j: (0, 0)),
                  pl.BlockSpec((1, d), lambda i, j: (0, 0))],
        out_specs=pl.BlockSpec((tm, d), lambda i, j: (i, 0), pipeline_mode=pl.Buffered(1)),
        compiler_params=_cparams(("arbitrary", "arbitrary"), 56),
        name="final",
    )(merged, w_out, x2, mod, ln_g, ln_b)


RG_TN = 512
F32_SUBLANES = 8
NARROW_A = ("k_idx", "w_idx")
NARROW_B = ("i_m", "f_m")


def _regroup_kernel(tbl_ref, main_ref, na_ref, nb_ref, o_ref, *, n_a, n_b):
    @pl.when(tbl_ref[pl.program_id(0)] >= 0)
    def _wide():
        o_ref[...] = main_ref[...].astype(BF16)

    @pl.when(tbl_ref[pl.program_id(0)] < 0)
    def _narrow():
        o_ref[:n_a, :] = na_ref[...].astype(BF16)
        o_ref[n_a:n_a + n_b, :] = nb_ref[...].astype(BF16)
        o_ref[n_a + n_b:, :] = jnp.zeros((o_ref.shape[0] - n_a - n_b, o_ref.shape[1]), BF16)


def _window_starts(first_col, n_cols):
    starts = []
    for oc in range(first_col, first_col + n_cols, RG_TN):
        if oc >= SMALL_OFF:
            starts.append(-1)
            continue
        seg = next(n for n in P_ORDER if P_OFFSETS[n] <= oc < P_OFFSETS[n] + IN_WIDTH_OF[n])
        start = IN_OFFSETS[seg] + oc - P_OFFSETS[seg]
        assert start % F32_SUBLANES == 0, (seg, start)
        starts.append(start // F32_SUBLANES)
    return starts


def _regroup_w(w_inT, first_col, n_cols):
    d = w_inT.shape[1]
    starts = _window_starts(first_col, n_cols)
    n_a = sum(IN_WIDTH_OF[n] for n in NARROW_A)
    n_b = sum(IN_WIDTH_OF[n] for n in NARROW_B)
    off_a, off_b = IN_OFFSETS[NARROW_A[0]], IN_OFFSETS[NARROW_B[0]]
    grid_spec = pltpu.PrefetchScalarGridSpec(
        num_scalar_prefetch=1,
        grid=(n_cols // RG_TN,),
        in_specs=[pl.BlockSpec((pl.Element(RG_TN), pl.Element(d)), lambda j, tbl: (jnp.maximum(tbl[j], 0) * F32_SUBLANES, 0)),
                  pl.BlockSpec((pl.Element(n_a), pl.Element(d)), lambda j, tbl: (off_a, 0)),
                  pl.BlockSpec((pl.Element(n_b), pl.Element(d)), lambda j, tbl: (off_b, 0))],
        out_specs=pl.BlockSpec((RG_TN, d), lambda j, tbl: (j, 0)),
    )
    return pl.pallas_call(
        functools.partial(_regroup_kernel, n_a=n_a, n_b=n_b),
        out_shape=jax.ShapeDtypeStruct((n_cols, d), BF16),
        grid_spec=grid_spec,
        compiler_params=_cparams(("arbitrary",), 40),
        name="regroup",
    )(jnp.asarray(starts, I32), w_inT, w_inT, w_inT)


def _regroup_cols(a, pad_to):
    parts = [a[..., IN_OFFSETS[n]:IN_OFFSETS[n] + IN_WIDTH_OF[n]] for n in P_ORDER]
    parts.append(jnp.zeros(a.shape[:-1] + (pad_to - P_USED,), a.dtype))
    return jnp.concatenate(parts, axis=-1)


def _layer(x2, c, w_ada, b_ada, w_in, b_in, g_q, g_kv, w_uq, w_iq, w_uk, w_uv, g_kidx, b_kidx, conv_w, conv_b, g_mh,
           w_attn_out, w_mlstm_out, w_out, ln_g, ln_b):
    s, d = x2.shape
    assert d == D_MODEL and s % 1024 == 0, (s, d)
    tq, tk = 256, 512
    nsel = min(TOPK_MAX, s // 4)

    w_inT = w_in.T
    w_tail = _regroup_w(w_inT, P_MAIN, P_TAIL)
    b_cat = _regroup_cols(b_in, P_TOTAL).reshape(1, P_TOTAL)
    w_uqT = w_uq.T.astype(BF16)
    w_iqT = w_iq.T.astype(BF16)
    w_ukT = w_uk.reshape(ATTN_WIDTH, KV_LORA_RANK).T.astype(BF16)
    w_uvT = w_uv.transpose(0, 2, 1).reshape(ATTN_WIDTH, KV_LORA_RANK).astype(BF16)

    mod = _ada(c.reshape(d, 1), w_ada, b_ada.reshape(1, -1))
    u = _modulate(x2, mod)
    proj = _proj_main(u, w_inT, b_cat[:, :P_MAIN])
    proj_tail = _proj(u, w_tail, b_cat[:, P_MAIN:])

    qT, qiT = _qpath(proj, g_q.reshape(1, -1), w_uqT, w_iqT, tq)
    k, vT, kidx, widx = _kvpath(proj_tail, g_kv.reshape(1, -1), g_kidx.reshape(1, -1), b_kidx.reshape(1, -1), w_ukT, w_uvT, tq)
    wT = widx.T.reshape(IDX_HEADS, 1, s)
    keys, thr = _indexer(kidx, qiT, wT, tq, nsel)
    a_attn = _attention(qT, k, vT, keys, thr, proj, tq, tk)

    qm, km = _conv(proj, conv_w, conv_b.reshape(1, -1), tq)
    gates_t = proj_tail[:, SMALL_OFF - P_MAIN + SM_I:SMALL_OFF - P_MAIN + SM_F + MLSTM_HEADS].T
    a_mlstm = _mlstm(qm, km, proj, gates_t, g_mh.reshape(MLSTM_HEADS, 1, MLSTM_V_DIM), MLSTM_CHUNK)

    merged = _merge(a_attn, a_mlstm, w_attn_out.astype(BF16), w_mlstm_out.astype(BF16), proj)
    return _final(merged, w_out.astype(BF16), x2, mod, ln_g.reshape(1, -1), ln_b.reshape(1, -1))


def kernel(x, c, w_ada, b_ada, w_in, b_in, g_q, g_kv, w_uq, w_iq, w_uk, w_uv, g_kidx, b_kidx, conv_w, conv_b, g_mh,
           w_attn_out, w_mlstm_out, w_out, ln_g, ln_b):
    bsz, seq, d = x.shape
    assert bsz == 1 and w_ada.shape[0] == 1, "single batch, single layer"
    out = _layer(x.reshape(seq, d), c, w_ada[0], b_ada[0], w_in[0], b_in[0], g_q[0], g_kv[0], w_uq[0], w_iq[0],
                 w_uk[0], w_uv[0], g_kidx[0], b_kidx[0], conv_w[0], conv_b[0], g_mh[0], w_attn_out[0],
                 w_mlstm_out[0], w_out[0], ln_g[0], ln_b[0])
    return out.reshape(bsz, seq, d)
```

```python
import functools

import jax
import jax.numpy as jnp
from jax import lax
from jax.experimental import pallas as pl
from jax.experimental.pallas import tpu as pltpu

F32 = jnp.float32
BF16 = jnp.bfloat16
I32 = jnp.int32

D_MODEL = 4096
ATTN_HEADS = 32
ATTN_HEAD_DIM = 128
ATTN_WIDTH = ATTN_HEADS * ATTN_HEAD_DIM
Q_LORA_RANK = 1024
KV_LORA_RANK = 512
IDX_HEADS = 32
IDX_HEAD_DIM = 64
TOPK_MAX = 256
MLSTM_HEADS = 8
MLSTM_QK_DIM = (D_MODEL // 2) // MLSTM_HEADS
MLSTM_V_DIM = D_MODEL // MLSTM_HEADS
MLSTM_QK_WIDTH = MLSTM_HEADS * MLSTM_QK_DIM
MLSTM_WIDTH = MLSTM_HEADS * MLSTM_V_DIM
MLSTM_CHUNK = 256
MLSTM_GROUP = 2
CONV_WIDTH = 4
GATE_SOFTCAP = 15.0
DEEPNORM_ALPHA = 2.0 ** 0.25
NORM_EPS = 1e-6

IN_WIDTHS = (Q_LORA_RANK, KV_LORA_RANK, IDX_HEAD_DIM, IDX_HEADS, ATTN_WIDTH, 2 * MLSTM_QK_WIDTH, MLSTM_WIDTH,
             MLSTM_WIDTH, MLSTM_HEADS, MLSTM_HEADS, MLSTM_WIDTH, D_MODEL, D_MODEL)
IN_NAMES = ("q_lat", "kv_lat", "k_idx", "w_idx", "z_attn", "qk_m", "v_m", "o_m", "i_m", "f_m", "z_m", "g_attn", "g_mlstm")
IN_OFFSETS = {n: sum(IN_WIDTHS[:i]) for i, n in enumerate(IN_NAMES)}
IN_WIDTH_OF = dict(zip(IN_NAMES, IN_WIDTHS))

P_ORDER = ("z_attn", "v_m", "o_m", "z_m", "g_attn", "g_mlstm", "q_lat", "qk_m", "kv_lat", "k_idx", "w_idx", "i_m", "f_m")
P_OFFSETS = {}
_off = 0
for _n in P_ORDER:
    P_OFFSETS[_n] = _off
    _off += IN_WIDTH_OF[_n]
P_USED = _off
PROJ_TN = 1024
P_TOTAL = -(-P_USED // PROJ_TN) * PROJ_TN
SMALL_W = 128
SMALL_OFF = P_OFFSETS["k_idx"]
P_PLAIN = P_OFFSETS["qk_m"]
P_MAIN = P_OFFSETS["kv_lat"]
P_TAIL = P_TOTAL - P_MAIN
assert P_PLAIN % PROJ_TN == 0 and P_MAIN % PROJ_TN == 0 and P_TAIL % PROJ_TN == 0
SM_WIDX = IDX_HEAD_DIM
SM_I = SM_WIDX + IDX_HEADS
SM_F = SM_I + MLSTM_HEADS

VMEM_CAP_BYTES = 60 * 1024 * 1024
VMEM_MB = dict(ada=32, modulate=40, proj=56, proj_main=56, qpath=48, kvpath=48, indexer=40, attn=56,
               mlstm=32, merge=48, final=56, regroup=40)

TQ = 256
TK = 512
ROW_TILE = 512
COL_TILE = 512
PROJ_TM = 1024
IDX_ROWS = 128

LOG2E = 1.4426950408889634
V_ONES = 16
V_ROWS = ATTN_HEAD_DIM + V_ONES
ATTN_ROWS = 256
HEAD_GROUP = 32
INT_MIN = -2 ** 31
KEY_NEG_INF = INT_MIN + 0x7FFFFF


def _cparams(sem, call):
    return pltpu.CompilerParams(dimension_semantics=sem, vmem_limit_bytes=min(VMEM_MB[call] * 1024 * 1024, VMEM_CAP_BYTES))


def _sigmoid(x):
    return jax.nn.sigmoid(x)


def _silu(x):
    return x * jax.nn.sigmoid(x)


def _const_spec(shape):
    nd = len(shape)
    return pl.BlockSpec(shape, lambda *_: (0,) * nd, pipeline_mode=pl.Buffered(1))


def _ada_kernel(c_ref, w_ref, b_ref, o_ref):
    c = c_ref[...]
    o_ref[...] = jnp.sum(w_ref[...] * _silu(c), axis=0, keepdims=True) + b_ref[...]


def _ada(c_col, w_ada, b_ada):
    d, n = w_ada.shape
    tn = COL_TILE
    return pl.pallas_call(
        _ada_kernel,
        out_shape=jax.ShapeDtypeStruct((1, n), F32),
        grid=(n // tn,),
        in_specs=[pl.BlockSpec((d, 1), lambda j: (0, 0)),
                  pl.BlockSpec((d, tn), lambda j: (0, j)),
                  pl.BlockSpec((1, tn), lambda j: (0, j))],
        out_specs=pl.BlockSpec((1, tn), lambda j: (0, j)),
        compiler_params=_cparams(("arbitrary",), "ada"),
        name="ada",
    )(c_col, w_ada, b_ada)


def _modulate_kernel(x_ref, shift_ref, scale_ref, u_ref):
    u_ref[...] = (x_ref[...] * (1.0 + scale_ref[...]) + shift_ref[...]).astype(BF16)


def _modulate(x2, mod):
    s, d = x2.shape
    tm = min(ROW_TILE, s)
    return pl.pallas_call(
        _modulate_kernel,
        out_shape=jax.ShapeDtypeStruct((s, d), BF16),
        grid=(s // tm,),
        in_specs=[pl.BlockSpec((tm, d), lambda i: (i, 0)),
                  pl.BlockSpec((1, d), lambda i: (0, 0)),
                  pl.BlockSpec((1, d), lambda i: (0, 1))],
        out_specs=pl.BlockSpec((tm, d), lambda i: (i, 0)),
        compiler_params=_cparams(("arbitrary",), "modulate"),
        name="modulate",
    )(x2, mod, mod)


def _proj_kernel(u_ref, w_ref, b_ref, o_ref):
    nt = (((1,), (1,)), ((), ()))
    o_ref[...] = lax.dot_general(u_ref[...], w_ref[...], nt, preferred_element_type=F32) + b_ref[...]


def _proj(u, w_catT, b_cat):
    s, d = u.shape
    n = w_catT.shape[0]
    tm = min(PROJ_TM, s)
    tn = PROJ_TN
    return pl.pallas_call(
        _proj_kernel,
        out_shape=jax.ShapeDtypeStruct((s, n), F32),
        grid=(n // tn, s // tm),
        in_specs=[pl.BlockSpec((tm, d), lambda j, i: (i, 0)),
                  pl.BlockSpec((tn, d), lambda j, i: (j, 0)),
                  pl.BlockSpec((1, tn), lambda j, i: (0, j))],
        out_specs=pl.BlockSpec((tm, tn), lambda j, i: (i, j)),
        compiler_params=_cparams(("arbitrary", "arbitrary"), "proj"),
        name="proj",
    )(u, w_catT, b_cat)


PM_CHUNK = 128


def _proj_main_kernel(starts_ref, u_ref, b_ref, *refs, n_m, conv):
    if conv:
        cw_ref, cb_ref, w_hbm, o_ref, wbf_ref, st_ref, sem, halo_ref = refs
    else:
        w_hbm, o_ref, wbf_ref, st_ref, sem = refs
    j = pl.program_id(0)
    i = pl.program_id(1)
    nj = pl.num_programs(0)
    step = j * n_m + i
    cpt = PROJ_TN // PM_CHUNK
    cps = cpt // n_m
    cpw = RG_TN // PM_CHUNK

    def chunk_copy(tile, c, slot):
        win = starts_ref[tile * (PROJ_TN // RG_TN) + c // cpw]
        row0 = pl.multiple_of(win * F32_SUBLANES + (c % cpw) * PM_CHUNK, F32_SUBLANES)
        return pltpu.make_async_copy(w_hbm.at[pl.ds(row0, PM_CHUNK), :], st_ref.at[slot], sem.at[slot])

    def cast_chunk(tile, c, slot):
        rows = pl.ds(pl.multiple_of(c * PM_CHUNK, PM_CHUNK), PM_CHUNK)
        wbf_ref[tile % 2, rows, :] = st_ref[slot].astype(BF16)

    def group(g):
        tile = g // n_m + 1
        return [(tile, (g % n_m) * cps + e, (g % 2) * cps + e) for e in range(cps)]

    @pl.when(step == 0)
    def _first_tile():
        for c in range(cpt):
            cp = chunk_copy(0, c, 0)
            cp.start()
            cp.wait()
            cast_chunk(0, c, 0)

        @pl.when(nj > 1)
        def _():
            for tile, c, slot in group(0):
                chunk_copy(tile, c, slot).start()

    @pl.when((step + 1) // n_m + 1 < nj)
    def _prefetch():
        for tile, c, slot in group(step + 1):
            chunk_copy(tile, c, slot).start()

    @pl.when(j + 1 < nj)
    def _stage_next_tile():
        for tile, c, slot in group(step):
            chunk_copy(tile, c, slot).wait()
            cast_chunk(tile, c, slot)

    nt = (((1,), (1,)), ((), ()))
    x = lax.dot_general(u_ref[...], wbf_ref[j % 2], nt, preferred_element_type=F32) + b_ref[...]
    if not conv:
        o_ref[...] = x
        return
    prev = jnp.where(i > 0, halo_ref[...], 0.0)
    halo_ref[...] = x[-8:]
    head = jnp.concatenate([prev, x[:8]], axis=0)
    y = cb_ref[...]
    yh = cb_ref[...]
    for tap in range(CONV_WIDTH):
        dly = CONV_WIDTH - 1 - tap
        xs = x if dly == 0 else pltpu.roll(x, dly, 0)
        hs = head if dly == 0 else pltpu.roll(head, dly, 0)
        y = y + xs * cw_ref[tap:tap + 1, :]
        yh = yh + hs[8:] * cw_ref[tap:tap + 1, :]
    y = _silu(jnp.concatenate([yh, y[8:]], axis=0))
    kscale = jnp.where(j * PROJ_TN >= MLSTM_QK_WIDTH, MLSTM_QK_DIM ** -0.5, 1.0)
    o_ref[...] = (y * kscale).astype(BF16)


def _proj_main(u, w_inT, b_main, first_col, conv_wb=None):
    s, d = u.shape
    n = b_main.shape[1]
    conv = conv_wb is not None
    tm = min(PROJ_TM, s)
    tn = PROJ_TN
    n_m = s // tm
    cps = (tn // PM_CHUNK) // n_m
    assert cps * n_m * PM_CHUNK == tn, (s, tm)
    grid_spec = pltpu.PrefetchScalarGridSpec(
        num_scalar_prefetch=1,
        grid=(n // tn, n_m),
        in_specs=[pl.BlockSpec((tm, d), lambda j, i, t: (i, 0)),
                  pl.BlockSpec((1, tn), lambda j, i, t: (0, j))]
        + ([pl.BlockSpec((CONV_WIDTH, tn), lambda j, i, t: (0, j)), pl.BlockSpec((1, tn), lambda j, i, t: (0, j))] if conv else [])
        + [pl.BlockSpec(memory_space=pl.ANY)],
        out_specs=pl.BlockSpec((tm, tn), lambda j, i, t: (i, j)),
        scratch_shapes=[pltpu.VMEM((2, tn, d), BF16),
                        pltpu.VMEM((2 * cps, PM_CHUNK, d), F32),
                        pltpu.SemaphoreType.DMA((2 * cps,))]
        + ([pltpu.VMEM((8, tn), F32)] if conv else []),
    )
    return pl.pallas_call(
        functools.partial(_proj_main_kernel, n_m=n_m, conv=conv),
        out_shape=jax.ShapeDtypeStruct((s, n), BF16 if conv else F32),
        grid_spec=grid_spec,
        compiler_params=_cparams(("arbitrary", "arbitrary"), "proj_main"),
        name="proj_qk" if conv else "proj_main",
    )(jnp.asarray(_window_starts(first_col, n), I32), u, b_main, *(conv_wb or ()), w_inT)


def _qpath_kernel(ql_ref, g_ref, wuq_ref, wiq_ref, qT_ref, qiT_ref, *, scale):
    x = ql_ref[...]
    cq = (x * lax.rsqrt(jnp.mean(x * x, axis=-1, keepdims=True) + NORM_EPS) * g_ref[...]).astype(BF16)
    nt = (((1,), (1,)), ((), ()))
    qT = lax.dot_general(wuq_ref[...], cq, nt, preferred_element_type=F32)
    qT_ref[...] = (qT * scale).reshape(qT_ref.shape).astype(BF16)
    qiT = lax.dot_general(wiq_ref[...], cq, nt, preferred_element_type=F32)
    qiT_ref[...] = qiT.reshape(qiT_ref.shape).astype(BF16)


def _qpath(proj, g_q, w_uqT, w_iqT, tq):
    s = proj.shape[0]
    r = Q_LORA_RANK
    return pl.pallas_call(
        functools.partial(_qpath_kernel, scale=ATTN_HEAD_DIM ** -0.5 * LOG2E),
        out_shape=(jax.ShapeDtypeStruct((ATTN_HEADS, ATTN_HEAD_DIM, s), BF16),
                   jax.ShapeDtypeStruct((IDX_HEADS, IDX_HEAD_DIM, s), BF16)),
        grid=(s // tq,),
        in_specs=[pl.BlockSpec((tq, r), lambda i: (i, P_OFFSETS["q_lat"] // r)),
                  _const_spec((1, r)),
                  _const_spec(w_uqT.shape),
                  _const_spec(w_iqT.shape)],
        out_specs=(pl.BlockSpec((ATTN_HEADS, ATTN_HEAD_DIM, tq), lambda i: (0, 0, i)),
                   pl.BlockSpec((IDX_HEADS, IDX_HEAD_DIM, tq), lambda i: (0, 0, i))),
        compiler_params=_cparams(("arbitrary",), "qpath"),
        name="qpath",
    )(proj, g_q, w_uqT, w_iqT)


def _kvpath_kernel(kvl_ref, sm_ref, gkv_ref, gk_ref, bk_ref, wuk_ref, wuv_ref, k_ref, vT_ref, kidx_ref, widx_ref, *, wscale):
    x = kvl_ref[...]
    ckv = (x * lax.rsqrt(jnp.mean(x * x, axis=-1, keepdims=True) + NORM_EPS) * gkv_ref[...]).astype(BF16)
    kfull = jnp.dot(ckv, wuk_ref[...], preferred_element_type=F32)
    for h in range(ATTN_HEADS):
        k_ref[h] = kfull[:, h * ATTN_HEAD_DIM:(h + 1) * ATTN_HEAD_DIM].astype(BF16)
    nt = (((1,), (1,)), ((), ()))
    vT = lax.dot_general(wuv_ref[...], ckv, nt, preferred_element_type=F32)
    vT_ref[:, :ATTN_HEAD_DIM, :] = vT.reshape(ATTN_HEADS, ATTN_HEAD_DIM, -1).astype(BF16)
    vT_ref[:, ATTN_HEAD_DIM:, :] = jnp.ones((ATTN_HEADS, V_ONES, vT_ref.shape[2]), BF16)
    sm = sm_ref[...]
    ki = sm[:, :IDX_HEAD_DIM]
    mu = jnp.mean(ki, axis=-1, keepdims=True)
    var = jnp.mean(jnp.square(ki - mu), axis=-1, keepdims=True)
    kidx_ref[...] = ((ki - mu) * lax.rsqrt(var + NORM_EPS) * gk_ref[...] + bk_ref[...]).astype(BF16)
    widx_ref[...] = sm[:, SM_WIDX:SM_WIDX + IDX_HEADS] * wscale


def _kvpath(proj_tail, g_kv, g_kidx, b_kidx, w_ukT, w_uvT, tm):
    s = proj_tail.shape[0]
    r = KV_LORA_RANK
    return pl.pallas_call(
        functools.partial(_kvpath_kernel, wscale=IDX_HEADS ** -0.5 * IDX_HEAD_DIM ** -0.5),
        out_shape=(jax.ShapeDtypeStruct((ATTN_HEADS, s, ATTN_HEAD_DIM), BF16),
                   jax.ShapeDtypeStruct((ATTN_HEADS, V_ROWS, s), BF16),
                   jax.ShapeDtypeStruct((s, IDX_HEAD_DIM), BF16),
                   jax.ShapeDtypeStruct((s, IDX_HEADS), F32)),
        grid=(s // tm,),
        in_specs=[pl.BlockSpec((tm, r), lambda i: (i, (P_OFFSETS["kv_lat"] - P_MAIN) // r)),
                  pl.BlockSpec((tm, SMALL_W), lambda i: (i, (SMALL_OFF - P_MAIN) // SMALL_W)),
                  _const_spec((1, r)),
                  _const_spec((1, IDX_HEAD_DIM)),
                  _const_spec((1, IDX_HEAD_DIM)),
                  _const_spec(w_ukT.shape),
                  _const_spec(w_uvT.shape)],
        out_specs=(pl.BlockSpec((ATTN_HEADS, tm, ATTN_HEAD_DIM), lambda i: (0, i, 0)),
                   pl.BlockSpec((ATTN_HEADS, V_ROWS, tm), lambda i: (0, 0, i)),
                   pl.BlockSpec((tm, IDX_HEAD_DIM), lambda i: (i, 0)),
                   pl.BlockSpec((tm, IDX_HEADS), lambda i: (i, 0))),
        compiler_params=_cparams(("arbitrary",), "kvpath"),
        name="kvpath",
    )(proj_tail, proj_tail, g_kv, g_kidx, b_kidx, w_ukT, w_uvT)


def _key_to_float(key):
    bits = jnp.where(key >= 0, key, key ^ 0x7FFFFFFF)
    return jnp.where(key < KEY_NEG_INF, -jnp.inf, pltpu.bitcast(bits, F32))


def _indexer_kernel(kidx_ref, qiT_ref, wT_ref, sc_ref, thr_ref, *, seq, tq, nsel):
    i = pl.program_id(0)
    ch = IDX_ROWS
    cb = tq
    n_score = (i + 1) * (tq // ch)
    n_count = i + 1
    tpos = i * tq + lax.broadcasted_iota(I32, (ch, tq), 1)

    def score_chunk(c, carry):
        r0 = pl.multiple_of(c * ch, ch)
        kc = kidx_ref[pl.ds(r0, ch), :]

        acc = jnp.zeros((ch, tq), F32)
        for h in range(IDX_HEADS):
            r = jnp.dot(kc, qiT_ref[h], preferred_element_type=F32)
            acc = acc + jnp.maximum(r, 0.0) * wT_ref[h]
        spos = r0 + lax.broadcasted_iota(I32, (ch, tq), 0)
        sc_ref[pl.ds(r0, ch), :] = jnp.where(spos <= tpos, acc, -jnp.inf)
        return carry
    lax.fori_loop(0, n_score, score_chunk, 0)

    def fill_chunk(c, carry):
        sc_ref[pl.ds(pl.multiple_of(c * cb, cb), cb), :] = jnp.full((cb, tq), -jnp.inf, F32)
        return carry
    lax.fori_loop(n_count, seq // cb, fill_chunk, 0)

    def count(pred):
        def body(c, part):
            r0 = pl.multiple_of(c * cb, cb)
            m = jnp.where(pred(sc_ref[pl.ds(r0, cb), :], r0), 1, 0)
            return part + jnp.sum(m.reshape(cb // 8, 8, tq), axis=0)
        part = lax.fori_loop(0, n_count, body, jnp.zeros((8, tq), I32))
        return jnp.sum(part, axis=0, keepdims=True)

    def count_ge(cand_key):
        cand = _key_to_float(cand_key)
        return count(lambda blk, r0: blk >= cand)

    t0 = jnp.where(count_ge(jnp.zeros((1, tq), I32)) >= nsel, 0, INT_MIN).astype(I32)

    def bit_step(b, t):
        cand = t + jnp.left_shift(jnp.int32(1), 30 - b)
        return jnp.where(count_ge(cand) >= nsel, cand, t)
    thr = _key_to_float(lax.fori_loop(0, 31, bit_step, t0))
    thr_ref[...] = thr

    tie = (count(lambda blk, r0: blk >= thr) > nsel) & (thr > -jnp.inf)

    @pl.when(jnp.max(tie.astype(I32)) > 0)
    def _break_ties():
        need = nsel - count(lambda blk, r0: blk > thr)

        def eq_below(j):
            return count(lambda blk, r0: (blk == thr) & (r0 + lax.broadcasted_iota(I32, (cb, tq), 0) < j))

        def jbit(b, j):
            test = j + jnp.left_shift(jnp.int32(1), (seq.bit_length() - 2) - b)
            return jnp.where(eq_below(test) < need, test, j)
        jlast = lax.fori_loop(0, seq.bit_length() - 1, jbit, jnp.zeros((1, tq), I32))

        def demote(c, carry):
            r0 = pl.multiple_of(c * cb, cb)
            blk = sc_ref[pl.ds(r0, cb), :]
            row = r0 + lax.broadcasted_iota(I32, (cb, tq), 0)
            sc_ref[pl.ds(r0, cb), :] = jnp.where(tie & (blk == thr) & (row > jlast), -jnp.inf, blk)
            return carry
        lax.fori_loop(0, n_count, demote, 0)


def _indexer(kidx, qiT, wT, tq, nsel):
    s = kidx.shape[0]
    return pl.pallas_call(
        functools.partial(_indexer_kernel, seq=s, tq=tq, nsel=nsel),
        out_shape=(jax.ShapeDtypeStruct((s, s), F32), jax.ShapeDtypeStruct((1, s), F32)),
        grid=(s // tq,),
        in_specs=[_const_spec((s, IDX_HEAD_DIM)),
                  pl.BlockSpec((IDX_HEADS, IDX_HEAD_DIM, tq), lambda i: (0, 0, i)),
                  pl.BlockSpec((IDX_HEADS, 1, tq), lambda i: (0, 0, i))],
        out_specs=(pl.BlockSpec((s, tq), lambda i: (0, i)),
                   pl.BlockSpec((1, tq), lambda i: (0, i))),
        compiler_params=_cparams(("arbitrary",), "indexer"),
        name="indexer",
    )(kidx, qiT, wT)


def _attn_kernel(tiles_ref, qT_ref, k_ref, vT_ref, keys_ref, thr_ref, z_ref, sl_ref, kf_ref, qf_ref, o_ref,
                 acc_ref, m_ref, mb_ref, lg_ref, p_ref, *, tq, tk):
    qi = tiles_ref[0, pl.program_id(0)]
    kj = tiles_ref[1, pl.program_id(0)]

    @pl.when(kj == 0)
    def _init():
        acc_ref[...] = jnp.zeros(acc_ref.shape, F32)
        m_ref[...] = jnp.full(m_ref.shape, -jnp.inf, F32)

    def _compute():
        spos = kj * tk + lax.broadcasted_iota(I32, (tk, tq), 0)
        tpos = qi * tq + lax.broadcasted_iota(I32, (tk, tq), 1)
        sel = (keys_ref[...] >= thr_ref[...]) & (spos <= tpos)
        mb_ref[...] = jnp.where(sel, 0.0, -jnp.inf)
        tile_off = (kj * tk - qi * tq).astype(F32)

        def group(g, carry):
            def logits(u):
                h = g * HEAD_GROUP + u
                qh = jnp.concatenate([qT_ref[h], qf_ref[h]], axis=0)
                part = jnp.full((8, tq), -jnp.inf, F32)
                for c in range(tk // ATTN_ROWS):
                    rows = pl.ds(c * ATTN_ROWS, ATTN_ROWS)
                    kh = jnp.concatenate([k_ref[h, rows, :], kf_ref[rows, :]], axis=1)
                    lg = jnp.dot(kh, qh, preferred_element_type=F32) + mb_ref[rows, :]
                    lg_ref[u % 2, rows, :] = lg
                    part = jnp.maximum(part, jnp.max(lg.reshape(ATTN_ROWS // 8, 8, tq), axis=0))
                shift = sl_ref[h] * tile_off
                m_old = m_ref[g, u]
                return m_old, jnp.maximum(m_old, jnp.max(part, axis=0, keepdims=True) + shift), shift

            def probs(u, m_old, m_new, shift):
                m_safe = jnp.where(m_new == -jnp.inf, 0.0, m_new)
                m_tile = m_safe - shift
                for c in range(tk // ATTN_ROWS):
                    rows = pl.ds(c * ATTN_ROWS, ATTN_ROWS)
                    p_ref[u % 2, rows, :] = jnp.exp2(lg_ref[u % 2, rows, :] - m_tile).astype(BF16)
                m_ref[g, u] = m_new
                return jnp.exp2(m_old - m_safe)

            def values(u, alpha):
                h = g * HEAD_GROUP + u
                acc_ref[g, u] = alpha * acc_ref[g, u] + jnp.dot(vT_ref[h], p_ref[u % 2], preferred_element_type=F32)

            stats = logits(0)
            alpha_prev = None
            for u in range(HEAD_GROUP):
                stats_next = logits(u + 1) if u + 1 < HEAD_GROUP else None
                alpha = probs(u, *stats)
                if u >= 1:
                    values(u - 1, alpha_prev)
                stats, alpha_prev = stats_next, alpha
            values(HEAD_GROUP - 1, alpha_prev)
            return carry
        lax.fori_loop(0, ATTN_HEADS // HEAD_GROUP, group, 0)
    _compute()

    @pl.when(kj == (qi * tq + tq - 1) // tk)
    def _finish():
        for h in range(ATTN_HEADS):
            g, u = divmod(h, HEAD_GROUP)
            cols = slice(h * ATTN_HEAD_DIM, (h + 1) * ATTN_HEAD_DIM)
            acc = acc_ref[g, u]
            o = (acc[:ATTN_HEAD_DIM] * (1.0 / acc[ATTN_HEAD_DIM:ATTN_HEAD_DIM + 1])).T
            o_ref[:, cols] = (o * _silu(z_ref[:, cols])).astype(BF16)


def _alibi_features(tq, tk):
    sigma = jnp.exp2(-8.0 * jnp.arange(1, ATTN_HEADS + 1, dtype=F32) / ATTN_HEADS) * LOG2E
    s1 = sigma.astype(BF16)
    s2 = (sigma - s1.astype(F32)).astype(BF16)
    s3 = (sigma - s1.astype(F32) - s2.astype(F32)).astype(BF16)
    pieces = jnp.stack([s1, s2, s3, s1, s2, s3], axis=1)
    qf = jnp.zeros((ATTN_HEADS, ATTN_HEAD_DIM, tq), BF16)
    qf = qf.at[:, :6, :].set(jnp.broadcast_to(pieces[:, :, None], (ATTN_HEADS, 6, tq)))
    r = jnp.arange(tk, dtype=I32)
    r_hi = ((r // 256) * 256).astype(BF16)
    r_lo = (r % 256).astype(BF16)
    kf = jnp.zeros((tk, ATTN_HEAD_DIM), BF16).at[:, :6].set(jnp.stack([r_hi, r_hi, r_hi, r_lo, r_lo, r_lo], axis=1))
    return jnp.broadcast_to(sigma[:, None, None], (ATTN_HEADS, 1, tq)), kf, qf


def _attention(qT, k, vT, keys, thr, proj, tq, tk):
    s = k.shape[1]
    ng = ATTN_HEADS // HEAD_GROUP
    tiles = [(qi, kj) for qi in range(s // tq) for kj in range((qi * tq + tq - 1) // tk + 1)]
    const3 = lambda shape: pl.BlockSpec(shape, lambda i, t: (0, 0, 0), pipeline_mode=pl.Buffered(1))
    grid_spec = pltpu.PrefetchScalarGridSpec(
        num_scalar_prefetch=1,
        grid=(len(tiles),),
        in_specs=[pl.BlockSpec((ATTN_HEADS, ATTN_HEAD_DIM, tq), lambda i, t: (0, 0, t[0, i])),
                  pl.BlockSpec((ATTN_HEADS, tk, ATTN_HEAD_DIM), lambda i, t: (0, t[1, i], 0)),
                  pl.BlockSpec((ATTN_HEADS, V_ROWS, tk), lambda i, t: (0, 0, t[1, i])),
                  pl.BlockSpec((tk, tq), lambda i, t: (t[1, i], t[0, i])),
                  pl.BlockSpec((1, tq), lambda i, t: (0, t[0, i])),
                  pl.BlockSpec((tq, ATTN_WIDTH), lambda i, t: (t[0, i], P_OFFSETS["z_attn"] // ATTN_WIDTH)),
                  const3((ATTN_HEADS, 1, tq)),
                  pl.BlockSpec((tk, ATTN_HEAD_DIM), lambda i, t: (0, 0), pipeline_mode=pl.Buffered(1)),
                  const3((ATTN_HEADS, ATTN_HEAD_DIM, tq))],
        out_specs=pl.BlockSpec((tq, ATTN_WIDTH), lambda i, t: (t[0, i], 0)),
        scratch_shapes=[pltpu.VMEM((ng, HEAD_GROUP, V_ROWS, tq), F32),
                        pltpu.VMEM((ng, HEAD_GROUP, 1, tq), F32),
                        pltpu.VMEM((tk, tq), F32),
                        pltpu.VMEM((2, tk, tq), F32),
                        pltpu.VMEM((2, tk, tq), BF16)],
    )
    return pl.pallas_call(
        functools.partial(_attn_kernel, tq=tq, tk=tk),
        out_shape=jax.ShapeDtypeStruct((s, ATTN_WIDTH), BF16),
        grid_spec=grid_spec,
        compiler_params=_cparams(("arbitrary",), "attn"),
        name="attn",
    )(jnp.asarray(tiles, I32).T, qT, k, vT, keys, thr, proj, *_alibi_features(tq, tk))


def _softcap(x):
    return GATE_SOFTCAP * jnp.tanh(x / GATE_SOFTCAP)


def _mlstm_kernel(q_ref, k_ref, v_ref, og_ref, z_ref, gt_ref, g_ref, out_ref, c_ref, n_ref, m_ref, *, chunk):
    ci = pl.program_id(1)
    L = chunk
    dk, dv = MLSTM_QK_DIM, MLSTM_V_DIM

    @pl.when(ci == 0)
    def _init():
        c_ref[...] = jnp.zeros(c_ref.shape, F32)
        n_ref[...] = jnp.zeros(n_ref.shape, F32)
        m_ref[...] = jnp.zeros(m_ref.shape, F32)

    gt = gt_ref[...]
    sub = lax.broadcasted_iota(I32, gt.shape, 0)
    r_i = lax.broadcasted_iota(I32, (L, L), 0)
    c_i = lax.broadcasted_iota(I32, (L, L), 1)
    eye = r_i == c_i
    tril = r_i >= c_i
    nt = (((1,), (1,)), ((), ()))
    tn = (((0,), (0,)), ((), ()))

    for j in range(MLSTM_GROUP):
        hd = pl.program_id(0) * MLSTM_GROUP + j
        ig_row = _softcap(jnp.sum(jnp.where(sub == hd, gt, 0.0), axis=0, keepdims=True))
        fg_row = _softcap(jnp.sum(jnp.where(sub == MLSTM_HEADS + hd, gt, 0.0), axis=0, keepdims=True))
        logf_row = jnp.minimum(fg_row, 0.0) - jnp.log1p(jnp.exp(-jnp.abs(fg_row)))
        ig_col = jnp.sum(jnp.where(eye, ig_row, 0.0), axis=1, keepdims=True)
        b_col = jnp.sum(jnp.where(tril, logf_row, 0.0), axis=1, keepdims=True)
        b_row = jnp.sum(jnp.where(eye, b_col, 0.0), axis=0, keepdims=True)
        dmat = jnp.where(tril, b_col - b_row + ig_row, -jnp.inf)
        m_prev = m_ref[j]
        m_inter = b_col + m_prev
        m_t = jnp.maximum(m_inter, jnp.max(dmat, axis=1, keepdims=True))

        qc = q_ref[:, j * dk:(j + 1) * dk]
        kc = k_ref[:, j * dk:(j + 1) * dk]
        vc = v_ref[:, j * dv:(j + 1) * dv].astype(BF16)
        s = lax.dot_general(qc, kc, nt, preferred_element_type=F32) * jnp.exp(dmat - m_t)
        inter = jnp.exp(m_inter - m_t)
        num = (jnp.dot(s.astype(BF16), vc, preferred_element_type=F32)
               + inter * jnp.dot(qc, c_ref[j].astype(BF16), preferred_element_type=F32))
        qn = jnp.sum(qc.astype(F32) * n_ref[j], axis=1, keepdims=True)
        den = jnp.sum(s, axis=1, keepdims=True) + inter * qn
        hh = num / jnp.maximum(jnp.abs(den), jnp.exp(-m_t))

        g_last = b_col[L - 1:L, :]
        m_new = m_t[L - 1:L, :]
        wgt = jnp.exp(g_last - b_col + ig_col - m_new)
        decay = jnp.exp(g_last + m_prev - m_new)
        wk = wgt * kc.astype(F32)
        c_ref[j] = decay * c_ref[j] + lax.dot_general(wk.astype(BF16), vc, tn, preferred_element_type=F32)
        n_ref[j] = decay * n_ref[j] + jnp.sum(wk, axis=0, keepdims=True)
        m_ref[j] = m_new

        hn = hh * lax.rsqrt(jnp.mean(hh * hh, axis=-1, keepdims=True) + NORM_EPS) * g_ref[j]
        cols = slice(j * dv, (j + 1) * dv)
        out_ref[:, cols] = (hn * _sigmoid(og_ref[:, cols]) * _silu(z_ref[:, cols])).astype(BF16)


def _mlstm(qk, proj, gates_t, g_mh3, chunk):
    s = qk.shape[0]
    gdk, gdv = MLSTM_GROUP * MLSTM_QK_DIM, MLSTM_GROUP * MLSTM_V_DIM
    vb, ob, zb = (P_OFFSETS[n] // gdv for n in ("v_m", "o_m", "z_m"))
    return pl.pallas_call(
        functools.partial(_mlstm_kernel, chunk=chunk),
        out_shape=jax.ShapeDtypeStruct((s, MLSTM_WIDTH), BF16),
        grid=(MLSTM_HEADS // MLSTM_GROUP, s // chunk),
        in_specs=[pl.BlockSpec((chunk, gdk), lambda h, c: (c, h)),
                  pl.BlockSpec((chunk, gdk), lambda h, c: (c, MLSTM_QK_WIDTH // gdk + h)),
                  pl.BlockSpec((chunk, gdv), lambda h, c: (c, vb + h)),
                  pl.BlockSpec((chunk, gdv), lambda h, c: (c, ob + h)),
                  pl.BlockSpec((chunk, gdv), lambda h, c: (c, zb + h)),
                  pl.BlockSpec((2 * MLSTM_HEADS, chunk), lambda h, c: (0, c)),
                  pl.BlockSpec((MLSTM_GROUP, 1, MLSTM_V_DIM), lambda h, c: (h, 0, 0))],
        out_specs=pl.BlockSpec((chunk, gdv), lambda h, c: (c, h)),
        scratch_shapes=[pltpu.VMEM((MLSTM_GROUP, MLSTM_QK_DIM, MLSTM_V_DIM), F32),
                        pltpu.VMEM((MLSTM_GROUP, 1, MLSTM_QK_DIM), F32),
                        pltpu.VMEM((MLSTM_GROUP, 1, 1), F32)],
        compiler_params=_cparams(("arbitrary", "arbitrary"), "mlstm"),
        name="mlstm",
    )(qk, qk, proj, proj, proj, gates_t, g_mh3)


def _merge_kernel(a1_ref, a2_ref, w1_ref, w2_ref, ga_ref, gm_ref, o_ref):
    y1 = jnp.dot(a1_ref[...], w1_ref[...], preferred_element_type=F32)
    y2 = jnp.dot(a2_ref[...], w2_ref[...], preferred_element_type=F32)
    o_ref[...] = (_sigmoid(ga_ref[...]) * y1 + _sigmoid(gm_ref[...]) * y2).astype(BF16)


def _merge(a1, a2, w1, w2, proj):
    s, d = a1.shape
    tm = min(ROW_TILE, s)
    tn = COL_TILE
    gab, gmb = P_OFFSETS["g_attn"] // tn, P_OFFSETS["g_mlstm"] // tn
    return pl.pallas_call(
        _merge_kernel,
        out_shape=jax.ShapeDtypeStruct((s, D_MODEL), BF16),
        grid=(s // tm, D_MODEL // tn),
        in_specs=[pl.BlockSpec((tm, d), lambda i, j: (i, 0)),
                  pl.BlockSpec((tm, d), lambda i, j: (i, 0)),
                  pl.BlockSpec((d, tn), lambda i, j: (0, j)),
                  pl.BlockSpec((d, tn), lambda i, j: (0, j)),
                  pl.BlockSpec((tm, tn), lambda i, j: (i, gab + j)),
                  pl.BlockSpec((tm, tn), lambda i, j: (i, gmb + j))],
        out_specs=pl.BlockSpec((tm, tn), lambda i, j: (i, j)),
        compiler_params=_cparams(("arbitrary", "arbitrary"), "merge"),
        name="merge",
    )(a1, a2, w1, w2, proj, proj)


def _final_kernel(mg_ref, w_ref, x_ref, gate_ref, lg_ref, lb_ref, o_ref, *, tn, nn):
    j = pl.program_id(1)
    y = jnp.dot(mg_ref[...], w_ref[...], preferred_element_type=F32)
    for jj in range(nn):
        @pl.when(j == jj)
        def _store(jj=jj):
            o_ref[:, jj * tn:(jj + 1) * tn] = y

    @pl.when(j == nn - 1)
    def _norm():
        d = nn * tn
        ssum = 0.0
        for jj in range(nn):
            cols = slice(jj * tn, (jj + 1) * tn)
            r = DEEPNORM_ALPHA * x_ref[:, cols] + gate_ref[:, cols] * o_ref[:, cols]
            o_ref[:, cols] = r
            ssum = ssum + jnp.sum(r, axis=-1, keepdims=True)
        mu = ssum / d
        vsum = 0.0
        for jj in range(nn):
            cols = slice(jj * tn, (jj + 1) * tn)
            vsum = vsum + jnp.sum(jnp.square(o_ref[:, cols] - mu), axis=-1, keepdims=True)
        inv = lax.rsqrt(vsum / d + NORM_EPS)
        for jj in range(nn):
            cols = slice(jj * tn, (jj + 1) * tn)
            o_ref[:, cols] = (o_ref[:, cols] - mu) * inv * lg_ref[:, cols] + lb_ref[:, cols]


def _final(merged, w_out, x2, mod, ln_g, ln_b):
    s, d = x2.shape
    tm = min(ROW_TILE, s)
    tn = COL_TILE
    nn = d // tn
    return pl.pallas_call(
        functools.partial(_final_kernel, tn=tn, nn=nn),
        out_shape=jax.ShapeDtypeStruct((s, d), F32),
        grid=(s // tm, nn),
        in_specs=[pl.BlockSpec((tm, d), lambda i, j: (i, 0)),
                  pl.BlockSpec((d, tn), lambda i, j: (0, j)),
                  pl.BlockSpec((tm, d), lambda i, j: (i, 0)),
                  pl.BlockSpec((1, d), lambda i, j: (0, 2)),
                  pl.BlockSpec((1, d), lambda i, j: (0, 0)),
                  pl.BlockSpec((1, d), lambda i, j: (0, 0))],
        out_specs=pl.BlockSpec((tm, d), lambda i, j: (i, 0), pipeline_mode=pl.Buffered(1)),
        compiler_params=_cparams(("arbitrary", "arbitrary"), "final"),
        name="final",
    )(merged, w_out, x2, mod, ln_g, ln_b)


RG_TN = 512
F32_SUBLANES = 8
NARROW_A = ("k_idx", "w_idx")
NARROW_B = ("i_m", "f_m")


def _regroup_kernel(tbl_ref, main_ref, na_ref, nb_ref, o_ref, *, n_a, n_b):
    @pl.when(tbl_ref[pl.program_id(0)] >= 0)
    def _wide():
        o_ref[...] = main_ref[...].astype(BF16)

    @pl.when(tbl_ref[pl.program_id(0)] < 0)
    def _narrow():
        o_ref[:n_a, :] = na_ref[...].astype(BF16)
        o_ref[n_a:n_a + n_b, :] = nb_ref[...].astype(BF16)
        o_ref[n_a + n_b:, :] = jnp.zeros((o_ref.shape[0] - n_a - n_b, o_ref.shape[1]), BF16)


def _window_starts(first_col, n_cols):
    starts = []
    for oc in range(first_col, first_col + n_cols, RG_TN):
        if oc >= SMALL_OFF:
            starts.append(-1)
            continue
        seg = next(n for n in P_ORDER if P_OFFSETS[n] <= oc < P_OFFSETS[n] + IN_WIDTH_OF[n])
        start = IN_OFFSETS[seg] + oc - P_OFFSETS[seg]
        assert start % F32_SUBLANES == 0, (seg, start)
        starts.append(start // F32_SUBLANES)
    return starts


def _regroup_w(w_inT, first_col, n_cols):
    d = w_inT.shape[1]
    starts = _window_starts(first_col, n_cols)
    n_a = sum(IN_WIDTH_OF[n] for n in NARROW_A)
    n_b = sum(IN_WIDTH_OF[n] for n in NARROW_B)
    off_a, off_b = IN_OFFSETS[NARROW_A[0]], IN_OFFSETS[NARROW_B[0]]
    grid_spec = pltpu.PrefetchScalarGridSpec(
        num_scalar_prefetch=1,
        grid=(n_cols // RG_TN,),
        in_specs=[pl.BlockSpec((pl.Element(RG_TN), pl.Element(d)), lambda j, tbl: (jnp.maximum(tbl[j], 0) * F32_SUBLANES, 0)),
                  pl.BlockSpec((pl.Element(n_a), pl.Element(d)), lambda j, tbl: (off_a, 0)),
                  pl.BlockSpec((pl.Element(n_b), pl.Element(d)), lambda j, tbl: (off_b, 0))],
        out_specs=pl.BlockSpec((RG_TN, d), lambda j, tbl: (j, 0)),
    )
    return pl.pallas_call(
        functools.partial(_regroup_kernel, n_a=n_a, n_b=n_b),
        out_shape=jax.ShapeDtypeStruct((n_cols, d), BF16),
        grid_spec=grid_spec,
        compiler_params=_cparams(("arbitrary",), "regroup"),
        name="regroup",
    )(jnp.asarray(starts, I32), w_inT, w_inT, w_inT)


def _regroup_cols(a, pad_to):
    parts = [a[..., IN_OFFSETS[n]:IN_OFFSETS[n] + IN_WIDTH_OF[n]] for n in P_ORDER]
    parts.append(jnp.zeros(a.shape[:-1] + (pad_to - P_USED,), a.dtype))
    return jnp.concatenate(parts, axis=-1)


def _layer(x2, c, w_ada, b_ada, w_in, b_in, g_q, g_kv, w_uq, w_iq, w_uk, w_uv, g_kidx, b_kidx, conv_w, conv_b, g_mh,
           w_attn_out, w_mlstm_out, w_out, ln_g, ln_b):
    s, d = x2.shape
    assert d == D_MODEL and s % PROJ_TM == 0, (s, d)
    tq, tk = TQ, TK
    nsel = min(TOPK_MAX, s // 4)

    w_inT = w_in.T
    w_tail = _regroup_w(w_inT, P_MAIN, P_TAIL)
    b_cat = _regroup_cols(b_in, P_TOTAL).reshape(1, P_TOTAL)
    w_uqT = w_uq.T.astype(BF16)
    w_iqT = w_iq.T.astype(BF16)
    w_ukT = w_uk.reshape(ATTN_WIDTH, KV_LORA_RANK).T.astype(BF16)
    w_uvT = w_uv.transpose(0, 2, 1).reshape(ATTN_WIDTH, KV_LORA_RANK).astype(BF16)

    mod = _ada(c.reshape(d, 1), w_ada, b_ada.reshape(1, -1))
    u = _modulate(x2, mod)
    proj = _proj_main(u, w_inT, b_cat[:, :P_PLAIN], 0)
    qk = _proj_main(u, w_inT, b_cat[:, P_PLAIN:P_MAIN], P_PLAIN, (conv_w, conv_b.reshape(1, -1)))
    proj_tail = _proj(u, w_tail, b_cat[:, P_MAIN:])

    qT, qiT = _qpath(proj, g_q.reshape(1, -1), w_uqT, w_iqT, tq)
    k, vT, kidx, widx = _kvpath(proj_tail, g_kv.reshape(1, -1), g_kidx.reshape(1, -1), b_kidx.reshape(1, -1), w_ukT, w_uvT, tq)
    wT = widx.T.reshape(IDX_HEADS, 1, s)
    keys, thr = _indexer(kidx, qiT, wT, tq, nsel)
    a_attn = _attention(qT, k, vT, keys, thr, proj, tq, tk)

    gates_t = proj_tail[:, SMALL_OFF - P_MAIN + SM_I:SMALL_OFF - P_MAIN + SM_F + MLSTM_HEADS].T
    a_mlstm = _mlstm(qk, proj, gates_t, g_mh.reshape(MLSTM_HEADS, 1, MLSTM_V_DIM), MLSTM_CHUNK)

    merged = _merge(a_attn, a_mlstm, w_attn_out.astype(BF16), w_mlstm_out.astype(BF16), proj)
    return _final(merged, w_out.astype(BF16), x2, mod, ln_g.reshape(1, -1), ln_b.reshape(1, -1))


def kernel(x, c, w_ada, b_ada, w_in, b_in, g_q, g_kv, w_uq, w_iq, w_uk, w_uv, g_kidx, b_kidx, conv_w, conv_b, g_mh,
           w_attn_out, w_mlstm_out, w_out, ln_g, ln_b):
    bsz, seq, d = x.shape
    assert bsz == 1 and w_ada.shape[0] == 1, "single batch, single layer"
    out = _layer(x.reshape(seq, d), c, w_ada[0], b_ada[0], w_in[0], b_in[0], g_q[0], g_kv[0], w_uq[0], w_iq[0],
                 w_uk[0], w_uv[0], g_kidx[0], b_kidx[0], conv_w[0], conv_b[0], g_mh[0], w_attn_out[0],
                 w_mlstm_out[0], w_out[0], ln_g[0], ln_b[0])
    return out.reshape(bsz, seq, d)
```

```python
import functools

import jax
import jax.numpy as jnp
from jax import lax
from jax.experimental import pallas as pl
from jax.experimental.pallas import tpu as pltpu

F32 = jnp.float32
BF16 = jnp.bfloat16
I32 = jnp.int32

D_MODEL = 4096
ATTN_HEADS = 32
ATTN_HEAD_DIM = 128
ATTN_WIDTH = ATTN_HEADS * ATTN_HEAD_DIM
Q_LORA_RANK = 1024
KV_LORA_RANK = 512
IDX_HEADS = 32
IDX_HEAD_DIM = 64
TOPK_MAX = 256
MLSTM_HEADS = 8
MLSTM_QK_DIM = (D_MODEL // 2) // MLSTM_HEADS
MLSTM_V_DIM = D_MODEL // MLSTM_HEADS
MLSTM_QK_WIDTH = MLSTM_HEADS * MLSTM_QK_DIM
MLSTM_WIDTH = MLSTM_HEADS * MLSTM_V_DIM
MLSTM_CHUNK = 256
MLSTM_GROUP = 2
CONV_WIDTH = 4
GATE_SOFTCAP = 15.0
DEEPNORM_ALPHA = 2.0 ** 0.25
NORM_EPS = 1e-6

IN_WIDTHS = (Q_LORA_RANK, KV_LORA_RANK, IDX_HEAD_DIM, IDX_HEADS, ATTN_WIDTH, 2 * MLSTM_QK_WIDTH, MLSTM_WIDTH,
             MLSTM_WIDTH, MLSTM_HEADS, MLSTM_HEADS, MLSTM_WIDTH, D_MODEL, D_MODEL)
IN_NAMES = ("q_lat", "kv_lat", "k_idx", "w_idx", "z_attn", "qk_m", "v_m", "o_m", "i_m", "f_m", "z_m", "g_attn", "g_mlstm")
IN_OFFSETS = {n: sum(IN_WIDTHS[:i]) for i, n in enumerate(IN_NAMES)}
IN_WIDTH_OF = dict(zip(IN_NAMES, IN_WIDTHS))

P_ORDER = ("z_attn", "v_m", "o_m", "z_m", "g_attn", "g_mlstm", "qk_m", "q_lat", "kv_lat", "k_idx", "w_idx", "i_m", "f_m")
P_OFFSETS = {}
_off = 0
for _n in P_ORDER:
    P_OFFSETS[_n] = _off
    _off += IN_WIDTH_OF[_n]
P_USED = _off
PROJ_TN = 1024
P_TOTAL = -(-P_USED // PROJ_TN) * PROJ_TN
SMALL_W = 128
SMALL_OFF = P_OFFSETS["k_idx"]
P_PLAIN = P_OFFSETS["qk_m"]
P_MAIN = P_OFFSETS["q_lat"]
P_TAIL = P_TOTAL - P_MAIN
assert P_PLAIN % PROJ_TN == 0 and P_MAIN % PROJ_TN == 0 and P_TAIL % PROJ_TN == 0
SM_WIDX = IDX_HEAD_DIM
SM_I = SM_WIDX + IDX_HEADS
SM_F = SM_I + MLSTM_HEADS

VMEM_CAP_BYTES = 60 * 1024 * 1024
VMEM_MB = dict(ada=32, modulate=40, proj=56, proj_main=56, qpath=48, kvpath=48, indexer=40, attn=56,
               mlstm=32, merge=48, final=56, regroup=40)

TQ = 256
TK = 512
ROW_TILE = 512
COL_TILE = 512
PROJ_TM = 1024
IDX_ROWS = 128

LOG2E = 1.4426950408889634
V_ONES = 16
V_ROWS = ATTN_HEAD_DIM + V_ONES
ATTN_ROWS = 256
HEAD_GROUP = 32
INT_MIN = -2 ** 31
KEY_NEG_INF = INT_MIN + 0x7FFFFF


def _cparams(sem, call):
    return pltpu.CompilerParams(dimension_semantics=sem, vmem_limit_bytes=min(VMEM_MB[call] * 1024 * 1024, VMEM_CAP_BYTES))


def _sigmoid(x):
    return jax.nn.sigmoid(x)


def _silu(x):
    return x * jax.nn.sigmoid(x)


def _const_spec(shape):
    nd = len(shape)
    return pl.BlockSpec(shape, lambda *_: (0,) * nd, pipeline_mode=pl.Buffered(1))


def _ada_kernel(c_ref, w_ref, b_ref, o_ref):
    c = c_ref[...]
    o_ref[...] = jnp.sum(w_ref[...] * _silu(c), axis=0, keepdims=True) + b_ref[...]


def _ada(c_col, w_ada, b_ada):
    d, n = w_ada.shape
    tn = COL_TILE
    return pl.pallas_call(
        _ada_kernel,
        out_shape=jax.ShapeDtypeStruct((1, n), F32),
        grid=(n // tn,),
        in_specs=[pl.BlockSpec((d, 1), lambda j: (0, 0)),
                  pl.BlockSpec((d, tn), lambda j: (0, j)),
                  pl.BlockSpec((1, tn), lambda j: (0, j))],
        out_specs=pl.BlockSpec((1, tn), lambda j: (0, j)),
        compiler_params=_cparams(("arbitrary",), "ada"),
        name="ada",
    )(c_col, w_ada, b_ada)


def _modulate_kernel(x_ref, shift_ref, scale_ref, u_ref):
    u_ref[...] = (x_ref[...] * (1.0 + scale_ref[...]) + shift_ref[...]).astype(BF16)


def _modulate(x2, mod):
    s, d = x2.shape
    tm = min(ROW_TILE, s)
    return pl.pallas_call(
        _modulate_kernel,
        out_shape=jax.ShapeDtypeStruct((s, d), BF16),
        grid=(s // tm,),
        in_specs=[pl.BlockSpec((tm, d), lambda i: (i, 0)),
                  pl.BlockSpec((1, d), lambda i: (0, 0)),
                  pl.BlockSpec((1, d), lambda i: (0, 1))],
        out_specs=pl.BlockSpec((tm, d), lambda i: (i, 0)),
        compiler_params=_cparams(("arbitrary",), "modulate"),
        name="modulate",
    )(x2, mod, mod)


def _proj_kernel(u_ref, w_ref, b_ref, o_ref):
    nt = (((1,), (1,)), ((), ()))
    o_ref[...] = lax.dot_general(u_ref[...], w_ref[...], nt, preferred_element_type=F32) + b_ref[...]


def _proj(u, w_catT, b_cat):
    s, d = u.shape
    n = w_catT.shape[0]
    tm = min(PROJ_TM, s)
    tn = PROJ_TN
    return pl.pallas_call(
        _proj_kernel,
        out_shape=jax.ShapeDtypeStruct((s, n), F32),
        grid=(n // tn, s // tm),
        in_specs=[pl.BlockSpec((tm, d), lambda j, i: (i, 0)),
                  pl.BlockSpec((tn, d), lambda j, i: (j, 0)),
                  pl.BlockSpec((1, tn), lambda j, i: (0, j))],
        out_specs=pl.BlockSpec((tm, tn), lambda j, i: (i, j)),
        compiler_params=_cparams(("arbitrary", "arbitrary"), "proj"),
        name="proj",
    )(u, w_catT, b_cat)


PM_CHUNK = 128


def _proj_main_kernel(starts_ref, u_ref, b_ref, *refs, n_m, conv):
    if conv:
        cw_ref, cb_ref, w_hbm, o_ref, wbf_ref, st_ref, sem, halo_ref = refs
    else:
        w_hbm, o_ref, wbf_ref, st_ref, sem = refs
    j = pl.program_id(0)
    i = pl.program_id(1)
    nj = pl.num_programs(0)
    step = j * n_m + i
    cpt = PROJ_TN // PM_CHUNK
    cps = cpt // n_m
    cpw = RG_TN // PM_CHUNK

    def chunk_copy(tile, c, slot):
        win = starts_ref[tile * (PROJ_TN // RG_TN) + c // cpw]
        row0 = pl.multiple_of(win * F32_SUBLANES + (c % cpw) * PM_CHUNK, F32_SUBLANES)
        return pltpu.make_async_copy(w_hbm.at[pl.ds(row0, PM_CHUNK), :], st_ref.at[slot], sem.at[slot])

    def cast_chunk(tile, c, slot):
        rows = pl.ds(pl.multiple_of(c * PM_CHUNK, PM_CHUNK), PM_CHUNK)
        wbf_ref[tile % 2, rows, :] = st_ref[slot].astype(BF16)

    def group(g):
        tile = g // n_m + 1
        return [(tile, (g % n_m) * cps + e, (g % 2) * cps + e) for e in range(cps)]

    @pl.when(step == 0)
    def _first_tile():
        for c in range(cpt):
            cp = chunk_copy(0, c, 0)
            cp.start()
            cp.wait()
            cast_chunk(0, c, 0)

        @pl.when(nj > 1)
        def _():
            for tile, c, slot in group(0):
                chunk_copy(tile, c, slot).start()

    @pl.when((step + 1) // n_m + 1 < nj)
    def _prefetch():
        for tile, c, slot in group(step + 1):
            chunk_copy(tile, c, slot).start()

    @pl.when(j + 1 < nj)
    def _stage_next_tile():
        for tile, c, slot in group(step):
            chunk_copy(tile, c, slot).wait()
            cast_chunk(tile, c, slot)

    nt = (((1,), (1,)), ((), ()))
    x = lax.dot_general(u_ref[...], wbf_ref[j % 2], nt, preferred_element_type=F32) + b_ref[...]
    if not conv:
        o_ref[...] = x.astype(o_ref.dtype)
        return
    prev = jnp.where(i > 0, halo_ref[...], 0.0)
    halo_ref[...] = x[-8:]
    head = jnp.concatenate([prev, x[:8]], axis=0)
    y = cb_ref[...]
    yh = cb_ref[...]
    for tap in range(CONV_WIDTH):
        dly = CONV_WIDTH - 1 - tap
        xs = x if dly == 0 else pltpu.roll(x, dly, 0)
        hs = head if dly == 0 else pltpu.roll(head, dly, 0)
        y = y + xs * cw_ref[tap:tap + 1, :]
        yh = yh + hs[8:] * cw_ref[tap:tap + 1, :]
    y = _silu(jnp.concatenate([yh, y[8:]], axis=0))
    kscale = jnp.where(j * PROJ_TN >= MLSTM_QK_WIDTH, MLSTM_QK_DIM ** -0.5, 1.0)
    o_ref[...] = (y * kscale).astype(BF16)


def _proj_main(u, w_inT, b_main, first_col, conv_wb=None):
    s, d = u.shape
    n = b_main.shape[1]
    conv = conv_wb is not None
    tm = min(PROJ_TM, s)
    tn = PROJ_TN
    n_m = s // tm
    cps = (tn // PM_CHUNK) // n_m
    assert cps * n_m * PM_CHUNK == tn, (s, tm)
    grid_spec = pltpu.PrefetchScalarGridSpec(
        num_scalar_prefetch=1,
        grid=(n // tn, n_m),
        in_specs=[pl.BlockSpec((tm, d), lambda j, i, t: (i, 0)),
                  pl.BlockSpec((1, tn), lambda j, i, t: (0, j))]
        + ([pl.BlockSpec((CONV_WIDTH, tn), lambda j, i, t: (0, j)), pl.BlockSpec((1, tn), lambda j, i, t: (0, j))] if conv else [])
        + [pl.BlockSpec(memory_space=pl.ANY)],
        out_specs=pl.BlockSpec((tm, tn), lambda j, i, t: (i, j)),
        scratch_shapes=[pltpu.VMEM((2, tn, d), BF16),
                        pltpu.VMEM((2 * cps, PM_CHUNK, d), F32),
                        pltpu.SemaphoreType.DMA((2 * cps,))]
        + ([pltpu.VMEM((8, tn), F32)] if conv else []),
    )
    return pl.pallas_call(
        functools.partial(_proj_main_kernel, n_m=n_m, conv=conv),
        out_shape=jax.ShapeDtypeStruct((s, n), BF16),
        grid_spec=grid_spec,
        compiler_params=_cparams(("arbitrary", "arbitrary"), "proj_main"),
        name="proj_qk" if conv else "proj_main",
    )(jnp.asarray(_window_starts(first_col, n), I32), u, b_main, *(conv_wb or ()), w_inT)


def _qpath_kernel(ql_ref, g_ref, wuq_ref, wiq_ref, qT_ref, qiT_ref, *, scale):
    x = ql_ref[...]
    cq = (x * lax.rsqrt(jnp.mean(x * x, axis=-1, keepdims=True) + NORM_EPS) * g_ref[...]).astype(BF16)
    nt = (((1,), (1,)), ((), ()))
    qT = lax.dot_general(wuq_ref[...], cq, nt, preferred_element_type=F32)
    qT_ref[...] = (qT * scale).reshape(qT_ref.shape).astype(BF16)
    qiT = lax.dot_general(wiq_ref[...], cq, nt, preferred_element_type=F32)
    qiT_ref[...] = qiT.reshape(qiT_ref.shape).astype(BF16)


def _qpath(proj_tail, g_q, w_uqT, w_iqT, tq):
    s = proj_tail.shape[0]
    r = Q_LORA_RANK
    return pl.pallas_call(
        functools.partial(_qpath_kernel, scale=ATTN_HEAD_DIM ** -0.5 * LOG2E),
        out_shape=(jax.ShapeDtypeStruct((ATTN_HEADS, ATTN_HEAD_DIM, s), BF16),
                   jax.ShapeDtypeStruct((IDX_HEADS, IDX_HEAD_DIM, s), BF16)),
        grid=(s // tq,),
        in_specs=[pl.BlockSpec((tq, r), lambda i: (i, (P_OFFSETS["q_lat"] - P_MAIN) // r)),
                  _const_spec((1, r)),
                  _const_spec(w_uqT.shape),
                  _const_spec(w_iqT.shape)],
        out_specs=(pl.BlockSpec((ATTN_HEADS, ATTN_HEAD_DIM, tq), lambda i: (0, 0, i)),
                   pl.BlockSpec((IDX_HEADS, IDX_HEAD_DIM, tq), lambda i: (0, 0, i))),
        compiler_params=_cparams(("arbitrary",), "qpath"),
        name="qpath",
    )(proj_tail, g_q, w_uqT, w_iqT)


def _kvpath_kernel(kvl_ref, sm_ref, gkv_ref, gk_ref, bk_ref, wuk_ref, wuv_ref, k_ref, vT_ref, kidx_ref, widx_ref, *, wscale):
    x = kvl_ref[...]
    ckv = (x * lax.rsqrt(jnp.mean(x * x, axis=-1, keepdims=True) + NORM_EPS) * gkv_ref[...]).astype(BF16)
    kfull = jnp.dot(ckv, wuk_ref[...], preferred_element_type=F32)
    for h in range(ATTN_HEADS):
        k_ref[h] = kfull[:, h * ATTN_HEAD_DIM:(h + 1) * ATTN_HEAD_DIM].astype(BF16)
    nt = (((1,), (1,)), ((), ()))
    vT = lax.dot_general(wuv_ref[...], ckv, nt, preferred_element_type=F32)
    vT_ref[:, :ATTN_HEAD_DIM, :] = vT.reshape(ATTN_HEADS, ATTN_HEAD_DIM, -1).astype(BF16)
    vT_ref[:, ATTN_HEAD_DIM:, :] = jnp.ones((ATTN_HEADS, V_ONES, vT_ref.shape[2]), BF16)
    sm = sm_ref[...]
    ki = sm[:, :IDX_HEAD_DIM]
    mu = jnp.mean(ki, axis=-1, keepdims=True)
    var = jnp.mean(jnp.square(ki - mu), axis=-1, keepdims=True)
    kidx_ref[...] = ((ki - mu) * lax.rsqrt(var + NORM_EPS) * gk_ref[...] + bk_ref[...]).astype(BF16)
    widx_ref[...] = sm[:, SM_WIDX:SM_WIDX + IDX_HEADS] * wscale


def _kvpath(proj_tail, g_kv, g_kidx, b_kidx, w_ukT, w_uvT, tm):
    s = proj_tail.shape[0]
    r = KV_LORA_RANK
    return pl.pallas_call(
        functools.partial(_kvpath_kernel, wscale=IDX_HEADS ** -0.5 * IDX_HEAD_DIM ** -0.5),
        out_shape=(jax.ShapeDtypeStruct((ATTN_HEADS, s, ATTN_HEAD_DIM), BF16),
                   jax.ShapeDtypeStruct((ATTN_HEADS, V_ROWS, s), BF16),
                   jax.ShapeDtypeStruct((s, IDX_HEAD_DIM), BF16),
                   jax.ShapeDtypeStruct((s, IDX_HEADS), F32)),
        grid=(s // tm,),
        in_specs=[pl.BlockSpec((tm, r), lambda i: (i, (P_OFFSETS["kv_lat"] - P_MAIN) // r)),
                  pl.BlockSpec((tm, SMALL_W), lambda i: (i, (SMALL_OFF - P_MAIN) // SMALL_W)),
                  _const_spec((1, r)),
                  _const_spec((1, IDX_HEAD_DIM)),
                  _const_spec((1, IDX_HEAD_DIM)),
                  _const_spec(w_ukT.shape),
                  _const_spec(w_uvT.shape)],
        out_specs=(pl.BlockSpec((ATTN_HEADS, tm, ATTN_HEAD_DIM), lambda i: (0, i, 0)),
                   pl.BlockSpec((ATTN_HEADS, V_ROWS, tm), lambda i: (0, 0, i)),
                   pl.BlockSpec((tm, IDX_HEAD_DIM), lambda i: (i, 0)),
                   pl.BlockSpec((tm, IDX_HEADS), lambda i: (i, 0))),
        compiler_params=_cparams(("arbitrary",), "kvpath"),
        name="kvpath",
    )(proj_tail, proj_tail, g_kv, g_kidx, b_kidx, w_ukT, w_uvT)


def _key_to_float(key):
    bits = jnp.where(key >= 0, key, key ^ 0x7FFFFFFF)
    return jnp.where(key < KEY_NEG_INF, -jnp.inf, pltpu.bitcast(bits, F32))


def _indexer_kernel(kidx_ref, qiT_ref, wT_ref, sc_ref, thr_ref, *, seq, tq, nsel):
    i = pl.program_id(0)
    ch = IDX_ROWS
    cb = tq
    n_score = (i + 1) * (tq // ch)
    n_count = i + 1
    tpos = i * tq + lax.broadcasted_iota(I32, (ch, tq), 1)

    def score_chunk(c, carry):
        r0 = pl.multiple_of(c * ch, ch)
        kc = kidx_ref[pl.ds(r0, ch), :]

        acc = jnp.zeros((ch, tq), F32)
        for h in range(IDX_HEADS):
            r = jnp.dot(kc, qiT_ref[h], preferred_element_type=F32)
            acc = acc + jnp.maximum(r, 0.0) * wT_ref[h]
        spos = r0 + lax.broadcasted_iota(I32, (ch, tq), 0)
        sc_ref[pl.ds(r0, ch), :] = jnp.where(spos <= tpos, acc, -jnp.inf)
        return carry
    lax.fori_loop(0, n_score, score_chunk, 0)

    def fill_chunk(c, carry):
        sc_ref[pl.ds(pl.multiple_of(c * cb, cb), cb), :] = jnp.full((cb, tq), -jnp.inf, F32)
        return carry
    lax.fori_loop(n_count, seq // cb, fill_chunk, 0)

    def count(pred):
        def body(c, part):
            r0 = pl.multiple_of(c * cb, cb)
            m = jnp.where(pred(sc_ref[pl.ds(r0, cb), :], r0), 1, 0)
            return part + jnp.sum(m.reshape(cb // 8, 8, tq), axis=0)
        part = lax.fori_loop(0, n_count, body, jnp.zeros((8, tq), I32))
        return jnp.sum(part, axis=0, keepdims=True)

    def count_ge(cand_key):
        cand = _key_to_float(cand_key)
        return count(lambda blk, r0: blk >= cand)

    t0 = jnp.where(count_ge(jnp.zeros((1, tq), I32)) >= nsel, 0, INT_MIN).astype(I32)

    def bit_step(b, t):
        cand = t + jnp.left_shift(jnp.int32(1), 30 - b)
        return jnp.where(count_ge(cand) >= nsel, cand, t)
    thr = _key_to_float(lax.fori_loop(0, 31, bit_step, t0))
    thr_ref[...] = thr

    tie = (count(lambda blk, r0: blk >= thr) > nsel) & (thr > -jnp.inf)

    @pl.when(jnp.max(tie.astype(I32)) > 0)
    def _break_ties():
        need = nsel - count(lambda blk, r0: blk > thr)

        def eq_below(j):
            return count(lambda blk, r0: (blk == thr) & (r0 + lax.broadcasted_iota(I32, (cb, tq), 0) < j))

        def jbit(b, j):
            test = j + jnp.left_shift(jnp.int32(1), (seq.bit_length() - 2) - b)
            return jnp.where(eq_below(test) < need, test, j)
        jlast = lax.fori_loop(0, seq.bit_length() - 1, jbit, jnp.zeros((1, tq), I32))

        def demote(c, carry):
            r0 = pl.multiple_of(c * cb, cb)
            blk = sc_ref[pl.ds(r0, cb), :]
            row = r0 + lax.broadcasted_iota(I32, (cb, tq), 0)
            sc_ref[pl.ds(r0, cb), :] = jnp.where(tie & (blk == thr) & (row > jlast), -jnp.inf, blk)
            return carry
        lax.fori_loop(0, n_count, demote, 0)


def _indexer(kidx, qiT, wT, tq, nsel):
    s = kidx.shape[0]
    return pl.pallas_call(
        functools.partial(_indexer_kernel, seq=s, tq=tq, nsel=nsel),
        out_shape=(jax.ShapeDtypeStruct((s, s), F32), jax.ShapeDtypeStruct((1, s), F32)),
        grid=(s // tq,),
        in_specs=[_const_spec((s, IDX_HEAD_DIM)),
                  pl.BlockSpec((IDX_HEADS, IDX_HEAD_DIM, tq), lambda i: (0, 0, i)),
                  pl.BlockSpec((IDX_HEADS, 1, tq), lambda i: (0, 0, i))],
        out_specs=(pl.BlockSpec((s, tq), lambda i: (0, i)),
                   pl.BlockSpec((1, tq), lambda i: (0, i))),
        compiler_params=_cparams(("arbitrary",), "indexer"),
        name="indexer",
    )(kidx, qiT, wT)


def _attn_kernel(tiles_ref, qT_ref, k_ref, vT_ref, keys_ref, thr_ref, z_ref, sl_ref, kf_ref, qf_ref, o_ref,
                 acc_ref, m_ref, mb_ref, lg_ref, p_ref, *, tq, tk):
    qi = tiles_ref[0, pl.program_id(0)]
    kj = tiles_ref[1, pl.program_id(0)]

    @pl.when(kj == 0)
    def _init():
        acc_ref[...] = jnp.zeros(acc_ref.shape, F32)
        m_ref[...] = jnp.full(m_ref.shape, -jnp.inf, F32)

    def _compute():
        spos = kj * tk + lax.broadcasted_iota(I32, (tk, tq), 0)
        tpos = qi * tq + lax.broadcasted_iota(I32, (tk, tq), 1)
        sel = (keys_ref[...] >= thr_ref[...]) & (spos <= tpos)
        mb_ref[...] = jnp.where(sel, 0.0, -jnp.inf)
        tile_off = (kj * tk - qi * tq).astype(F32)

        def group(g, carry):
            def logits(u):
                h = g * HEAD_GROUP + u
                qh = jnp.concatenate([qT_ref[h], qf_ref[h]], axis=0)
                part = jnp.full((8, tq), -jnp.inf, F32)
                for c in range(tk // ATTN_ROWS):
                    rows = pl.ds(c * ATTN_ROWS, ATTN_ROWS)
                    kh = jnp.concatenate([k_ref[h, rows, :], kf_ref[rows, :]], axis=1)
                    lg = jnp.dot(kh, qh, preferred_element_type=F32) + mb_ref[rows, :]
                    lg_ref[u % 2, rows, :] = lg
                    part = jnp.maximum(part, jnp.max(lg.reshape(ATTN_ROWS // 8, 8, tq), axis=0))
                shift = sl_ref[h] * tile_off
                m_old = m_ref[g, u]
                return m_old, jnp.maximum(m_old, jnp.max(part, axis=0, keepdims=True) + shift), shift

            def probs(u, m_old, m_new, shift):
                m_safe = jnp.where(m_new == -jnp.inf, 0.0, m_new)
                m_tile = m_safe - shift
                for c in range(tk // ATTN_ROWS):
                    rows = pl.ds(c * ATTN_ROWS, ATTN_ROWS)
                    p_ref[u % 2, rows, :] = jnp.exp2(lg_ref[u % 2, rows, :] - m_tile).astype(BF16)
                m_ref[g, u] = m_new
                return jnp.exp2(m_old - m_safe)

            def values(u, alpha):
                h = g * HEAD_GROUP + u
                acc_ref[g, u] = alpha * acc_ref[g, u] + jnp.dot(vT_ref[h], p_ref[u % 2], preferred_element_type=F32)

            stats = logits(0)
            alpha_prev = None
            for u in range(HEAD_GROUP):
                stats_next = logits(u + 1) if u + 1 < HEAD_GROUP else None
                alpha = probs(u, *stats)
                if u >= 1:
                    values(u - 1, alpha_prev)
                stats, alpha_prev = stats_next, alpha
            values(HEAD_GROUP - 1, alpha_prev)
            return carry
        lax.fori_loop(0, ATTN_HEADS // HEAD_GROUP, group, 0)
    _compute()

    @pl.when(kj == (qi * tq + tq - 1) // tk)
    def _finish():
        for h in range(ATTN_HEADS):
            g, u = divmod(h, HEAD_GROUP)
            cols = slice(h * ATTN_HEAD_DIM, (h + 1) * ATTN_HEAD_DIM)
            acc = acc_ref[g, u]
            o = (acc[:ATTN_HEAD_DIM] * (1.0 / acc[ATTN_HEAD_DIM:ATTN_HEAD_DIM + 1])).T
            o_ref[:, cols] = (o * _silu(z_ref[:, cols].astype(F32))).astype(BF16)


def _alibi_features(tq, tk):
    sigma = jnp.exp2(-8.0 * jnp.arange(1, ATTN_HEADS + 1, dtype=F32) / ATTN_HEADS) * LOG2E
    s1 = sigma.astype(BF16)
    s2 = (sigma - s1.astype(F32)).astype(BF16)
    s3 = (sigma - s1.astype(F32) - s2.astype(F32)).astype(BF16)
    pieces = jnp.stack([s1, s2, s3, s1, s2, s3], axis=1)
    qf = jnp.zeros((ATTN_HEADS, ATTN_HEAD_DIM, tq), BF16)
    qf = qf.at[:, :6, :].set(jnp.broadcast_to(pieces[:, :, None], (ATTN_HEADS, 6, tq)))
    r = jnp.arange(tk, dtype=I32)
    r_hi = ((r // 256) * 256).astype(BF16)
    r_lo = (r % 256).astype(BF16)
    kf = jnp.zeros((tk, ATTN_HEAD_DIM), BF16).at[:, :6].set(jnp.stack([r_hi, r_hi, r_hi, r_lo, r_lo, r_lo], axis=1))
    return jnp.broadcast_to(sigma[:, None, None], (ATTN_HEADS, 1, tq)), kf, qf


def _attention(qT, k, vT, keys, thr, proj, tq, tk):
    s = k.shape[1]
    ng = ATTN_HEADS // HEAD_GROUP
    tiles = [(qi, kj) for qi in range(s // tq) for kj in range((qi * tq + tq - 1) // tk + 1)]
    const3 = lambda shape: pl.BlockSpec(shape, lambda i, t: (0, 0, 0), pipeline_mode=pl.Buffered(1))
    grid_spec = pltpu.PrefetchScalarGridSpec(
        num_scalar_prefetch=1,
        grid=(len(tiles),),
        in_specs=[pl.BlockSpec((ATTN_HEADS, ATTN_HEAD_DIM, tq), lambda i, t: (0, 0, t[0, i])),
                  pl.BlockSpec((ATTN_HEADS, tk, ATTN_HEAD_DIM), lambda i, t: (0, t[1, i], 0)),
                  pl.BlockSpec((ATTN_HEADS, V_ROWS, tk), lambda i, t: (0, 0, t[1, i])),
                  pl.BlockSpec((tk, tq), lambda i, t: (t[1, i], t[0, i])),
                  pl.BlockSpec((1, tq), lambda i, t: (0, t[0, i])),
                  pl.BlockSpec((tq, ATTN_WIDTH), lambda i, t: (t[0, i], P_OFFSETS["z_attn"] // ATTN_WIDTH)),
                  const3((ATTN_HEADS, 1, tq)),
                  pl.BlockSpec((tk, ATTN_HEAD_DIM), lambda i, t: (0, 0), pipeline_mode=pl.Buffered(1)),
                  const3((ATTN_HEADS, ATTN_HEAD_DIM, tq))],
        out_specs=pl.BlockSpec((tq, ATTN_WIDTH), lambda i, t: (t[0, i], 0)),
        scratch_shapes=[pltpu.VMEM((ng, HEAD_GROUP, V_ROWS, tq), F32),
                        pltpu.VMEM((ng, HEAD_GROUP, 1, tq), F32),
                        pltpu.VMEM((tk, tq), F32),
                        pltpu.VMEM((2, tk, tq), F32),
                        pltpu.VMEM((2, tk, tq), BF16)],
    )
    return pl.pallas_call(
        functools.partial(_attn_kernel, tq=tq, tk=tk),
        out_shape=jax.ShapeDtypeStruct((s, ATTN_WIDTH), BF16),
        grid_spec=grid_spec,
        compiler_params=_cparams(("arbitrary",), "attn"),
        name="attn",
    )(jnp.asarray(tiles, I32).T, qT, k, vT, keys, thr, proj, *_alibi_features(tq, tk))


def _softcap(x):
    return GATE_SOFTCAP * jnp.tanh(x / GATE_SOFTCAP)


def _mlstm_kernel(q_ref, k_ref, v_ref, og_ref, z_ref, gt_ref, g_ref, out_ref, c_ref, n_ref, m_ref, *, chunk):
    ci = pl.program_id(1)
    L = chunk
    dk, dv = MLSTM_QK_DIM, MLSTM_V_DIM

    @pl.when(ci == 0)
    def _init():
        c_ref[...] = jnp.zeros(c_ref.shape, F32)
        n_ref[...] = jnp.zeros(n_ref.shape, F32)
        m_ref[...] = jnp.zeros(m_ref.shape, F32)

    gt = gt_ref[...]
    sub = lax.broadcasted_iota(I32, gt.shape, 0)
    r_i = lax.broadcasted_iota(I32, (L, L), 0)
    c_i = lax.broadcasted_iota(I32, (L, L), 1)
    eye = r_i == c_i
    tril = r_i >= c_i
    nt = (((1,), (1,)), ((), ()))
    tn = (((0,), (0,)), ((), ()))

    for j in range(MLSTM_GROUP):
        hd = pl.program_id(0) * MLSTM_GROUP + j
        ig_row = _softcap(jnp.sum(jnp.where(sub == hd, gt, 0.0), axis=0, keepdims=True))
        fg_row = _softcap(jnp.sum(jnp.where(sub == MLSTM_HEADS + hd, gt, 0.0), axis=0, keepdims=True))
        logf_row = jnp.minimum(fg_row, 0.0) - jnp.log1p(jnp.exp(-jnp.abs(fg_row)))
        ig_col = jnp.sum(jnp.where(eye, ig_row, 0.0), axis=1, keepdims=True)
        b_col = jnp.sum(jnp.where(tril, logf_row, 0.0), axis=1, keepdims=True)
        b_row = jnp.sum(jnp.where(eye, b_col, 0.0), axis=0, keepdims=True)
        dmat = jnp.where(tril, b_col - b_row + ig_row, -jnp.inf)
        m_prev = m_ref[j]
        m_inter = b_col + m_prev
        m_t = jnp.maximum(m_inter, jnp.max(dmat, axis=1, keepdims=True))

        qc = q_ref[:, j * dk:(j + 1) * dk]
        kc = k_ref[:, j * dk:(j + 1) * dk]
        vc = v_ref[:, j * dv:(j + 1) * dv]
        s = lax.dot_general(qc, kc, nt, preferred_element_type=F32) * jnp.exp(dmat - m_t)
        inter = jnp.exp(m_inter - m_t)
        num = (jnp.dot(s.astype(BF16), vc, preferred_element_type=F32)
               + inter * jnp.dot(qc, c_ref[j].astype(BF16), preferred_element_type=F32))
        qn = jnp.sum(qc.astype(F32) * n_ref[j], axis=1, keepdims=True)
        den = jnp.sum(s, axis=1, keepdims=True) + inter * qn
        hh = num / jnp.maximum(jnp.abs(den), jnp.exp(-m_t))

        g_last = b_col[L - 1:L, :]
        m_new = m_t[L - 1:L, :]
        wgt = jnp.exp(g_last - b_col + ig_col - m_new)
        decay = jnp.exp(g_last + m_prev - m_new)
        wk = wgt * kc.astype(F32)
        c_ref[j] = decay * c_ref[j] + lax.dot_general(wk.astype(BF16), vc, tn, preferred_element_type=F32)
        n_ref[j] = decay * n_ref[j] + jnp.sum(wk, axis=0, keepdims=True)
        m_ref[j] = m_new

        hn = hh * lax.rsqrt(jnp.mean(hh * hh, axis=-1, keepdims=True) + NORM_EPS) * g_ref[j]
        cols = slice(j * dv, (j + 1) * dv)
        out_ref[:, cols] = (hn * _sigmoid(og_ref[:, cols].astype(F32)) * _silu(z_ref[:, cols].astype(F32))).astype(BF16)


def _mlstm(qk, proj, gates_t, g_mh3, chunk):
    s = qk.shape[0]
    gdk, gdv = MLSTM_GROUP * MLSTM_QK_DIM, MLSTM_GROUP * MLSTM_V_DIM
    vb, ob, zb = (P_OFFSETS[n] // gdv for n in ("v_m", "o_m", "z_m"))
    return pl.pallas_call(
        functools.partial(_mlstm_kernel, chunk=chunk),
        out_shape=jax.ShapeDtypeStruct((s, MLSTM_WIDTH), BF16),
        grid=(MLSTM_HEADS // MLSTM_GROUP, s // chunk),
        in_specs=[pl.BlockSpec((chunk, gdk), lambda h, c: (c, h)),
                  pl.BlockSpec((chunk, gdk), lambda h, c: (c, MLSTM_QK_WIDTH // gdk + h)),
                  pl.BlockSpec((chunk, gdv), lambda h, c: (c, vb + h)),
                  pl.BlockSpec((chunk, gdv), lambda h, c: (c, ob + h)),
                  pl.BlockSpec((chunk, gdv), lambda h, c: (c, zb + h)),
                  pl.BlockSpec((2 * MLSTM_HEADS, chunk), lambda h, c: (0, c)),
                  pl.BlockSpec((MLSTM_GROUP, 1, MLSTM_V_DIM), lambda h, c: (h, 0, 0))],
        out_specs=pl.BlockSpec((chunk, gdv), lambda h, c: (c, h)),
        scratch_shapes=[pltpu.VMEM((MLSTM_GROUP, MLSTM_QK_DIM, MLSTM_V_DIM), F32),
                        pltpu.VMEM((MLSTM_GROUP, 1, MLSTM_QK_DIM), F32),
                        pltpu.VMEM((MLSTM_GROUP, 1, 1), F32)],
        compiler_params=_cparams(("arbitrary", "arbitrary"), "mlstm"),
        name="mlstm",
    )(qk, qk, proj, proj, proj, gates_t, g_mh3)


def _merge_kernel(a1_ref, a2_ref, w1_ref, w2_ref, ga_ref, gm_ref, o_ref):
    y1 = jnp.dot(a1_ref[...], w1_ref[...], preferred_element_type=F32)
    y2 = jnp.dot(a2_ref[...], w2_ref[...], preferred_element_type=F32)
    o_ref[...] = (_sigmoid(ga_ref[...].astype(F32)) * y1 + _sigmoid(gm_ref[...].astype(F32)) * y2).astype(BF16)


def _merge(a1, a2, w1, w2, proj):
    s, d = a1.shape
    tm = min(ROW_TILE, s)
    tn = COL_TILE
    gab, gmb = P_OFFSETS["g_attn"] // tn, P_OFFSETS["g_mlstm"] // tn
    return pl.pallas_call(
        _merge_kernel,
        out_shape=jax.ShapeDtypeStruct((s, D_MODEL), BF16),
        grid=(s // tm, D_MODEL // tn),
        in_specs=[pl.BlockSpec((tm, d), lambda i, j: (i, 0)),
                  pl.BlockSpec((tm, d), lambda i, j: (i, 0)),
                  pl.BlockSpec((d, tn), lambda i, j: (0, j)),
                  pl.BlockSpec((d, tn), lambda i, j: (0, j)),
                  pl.BlockSpec((tm, tn), lambda i, j: (i, gab + j)),
                  pl.BlockSpec((tm, tn), lambda i, j: (i, gmb + j))],
        out_specs=pl.BlockSpec((tm, tn), lambda i, j: (i, j)),
        compiler_params=_cparams(("arbitrary", "arbitrary"), "merge"),
        name="merge",
    )(a1, a2, w1, w2, proj, proj)


def _final_kernel(mg_ref, w_ref, x_ref, gate_ref, lg_ref, lb_ref, o_ref, *, tn, nn):
    j = pl.program_id(1)
    y = jnp.dot(mg_ref[...], w_ref[...], preferred_element_type=F32)
    for jj in range(nn):
        @pl.when(j == jj)
        def _store(jj=jj):
            o_ref[:, jj * tn:(jj + 1) * tn] = y

    @pl.when(j == nn - 1)
    def _norm():
        d = nn * tn
        ssum = 0.0
        for jj in range(nn):
            cols = slice(jj * tn, (jj + 1) * tn)
            r = DEEPNORM_ALPHA * x_ref[:, cols] + gate_ref[:, cols] * o_ref[:, cols]
            o_ref[:, cols] = r
            ssum = ssum + jnp.sum(r, axis=-1, keepdims=True)
        mu = ssum / d
        vsum = 0.0
        for jj in range(nn):
            cols = slice(jj * tn, (jj + 1) * tn)
            vsum = vsum + jnp.sum(jnp.square(o_ref[:, cols] - mu), axis=-1, keepdims=True)
        inv = lax.rsqrt(vsum / d + NORM_EPS)
        for jj in range(nn):
            cols = slice(jj * tn, (jj + 1) * tn)
            o_ref[:, cols] = (o_ref[:, cols] - mu) * inv * lg_ref[:, cols] + lb_ref[:, cols]


def _final(merged, w_out, x2, mod, ln_g, ln_b):
    s, d = x2.shape
    tm = min(ROW_TILE, s)
    tn = COL_TILE
    nn = d // tn
    return pl.pallas_call(
        functools.partial(_final_kernel, tn=tn, nn=nn),
        out_shape=jax.ShapeDtypeStruct((s, d), F32),
        grid=(s // tm, nn),
        in_specs=[pl.BlockSpec((tm, d), lambda i, j: (i, 0)),
                  pl.BlockSpec((d, tn), lambda i, j: (0, j)),
                  pl.BlockSpec((tm, d), lambda i, j: (i, 0)),
                  pl.BlockSpec((1, d), lambda i, j: (0, 2)),
                  pl.BlockSpec((1, d), lambda i, j: (0, 0)),
                  pl.BlockSpec((1, d), lambda i, j: (0, 0))],
        out_specs=pl.BlockSpec((tm, d), lambda i, j: (i, 0), pipeline_mode=pl.Buffered(1)),
        compiler_params=_cparams(("arbitrary", "arbitrary"), "final"),
        name="final",
    )(merged, w_out, x2, mod, ln_g, ln_b)


RG_TN = 512
F32_SUBLANES = 8
NARROW_A = ("k_idx", "w_idx")
NARROW_B = ("i_m", "f_m")


def _regroup_kernel(tbl_ref, main_ref, na_ref, nb_ref, o_ref, *, n_a, n_b):
    @pl.when(tbl_ref[pl.program_id(0)] >= 0)
    def _wide():
        o_ref[...] = main_ref[...].astype(BF16)

    @pl.when(tbl_ref[pl.program_id(0)] < 0)
    def _narrow():
        o_ref[:n_a, :] = na_ref[...].astype(BF16)
        o_ref[n_a:n_a + n_b, :] = nb_ref[...].astype(BF16)
        o_ref[n_a + n_b:, :] = jnp.zeros((o_ref.shape[0] - n_a - n_b, o_ref.shape[1]), BF16)


def _window_starts(first_col, n_cols):
    starts = []
    for oc in range(first_col, first_col + n_cols, RG_TN):
        if oc >= SMALL_OFF:
            starts.append(-1)
            continue
        seg = next(n for n in P_ORDER if P_OFFSETS[n] <= oc < P_OFFSETS[n] + IN_WIDTH_OF[n])
        start = IN_OFFSETS[seg] + oc - P_OFFSETS[seg]
        assert start % F32_SUBLANES == 0, (seg, start)
        starts.append(start // F32_SUBLANES)
    return starts


def _regroup_w(w_inT, first_col, n_cols):
    d = w_inT.shape[1]
    starts = _window_starts(first_col, n_cols)
    n_a = sum(IN_WIDTH_OF[n] for n in NARROW_A)
    n_b = sum(IN_WIDTH_OF[n] for n in NARROW_B)
    off_a, off_b = IN_OFFSETS[NARROW_A[0]], IN_OFFSETS[NARROW_B[0]]
    grid_spec = pltpu.PrefetchScalarGridSpec(
        num_scalar_prefetch=1,
        grid=(n_cols // RG_TN,),
        in_specs=[pl.BlockSpec((pl.Element(RG_TN), pl.Element(d)), lambda j, tbl: (jnp.maximum(tbl[j], 0) * F32_SUBLANES, 0)),
                  pl.BlockSpec((pl.Element(n_a), pl.Element(d)), lambda j, tbl: (off_a, 0)),
                  pl.BlockSpec((pl.Element(n_b), pl.Element(d)), lambda j, tbl: (off_b, 0))],
        out_specs=pl.BlockSpec((RG_TN, d), lambda j, tbl: (j, 0)),
    )
    return pl.pallas_call(
        functools.partial(_regroup_kernel, n_a=n_a, n_b=n_b),
        out_shape=jax.ShapeDtypeStruct((n_cols, d), BF16),
        grid_spec=grid_spec,
        compiler_params=_cparams(("arbitrary",), "regroup"),
        name="regroup",
    )(jnp.asarray(starts, I32), w_inT, w_inT, w_inT)


def _regroup_cols(a, pad_to):
    parts = [a[..., IN_OFFSETS[n]:IN_OFFSETS[n] + IN_WIDTH_OF[n]] for n in P_ORDER]
    parts.append(jnp.zeros(a.shape[:-1] + (pad_to - P_USED,), a.dtype))
    return jnp.concatenate(parts, axis=-1)


def _layer(x2, c, w_ada, b_ada, w_in, b_in, g_q, g_kv, w_uq, w_iq, w_uk, w_uv, g_kidx, b_kidx, conv_w, conv_b, g_mh,
           w_attn_out, w_mlstm_out, w_out, ln_g, ln_b):
    s, d = x2.shape
    assert d == D_MODEL and s % PROJ_TM == 0, (s, d)
    tq, tk = TQ, TK
    nsel = min(TOPK_MAX, s // 4)

    w_inT = w_in.T
    w_tail = _regroup_w(w_inT, P_MAIN, P_TAIL)
    b_cat = _regroup_cols(b_in, P_TOTAL).reshape(1, P_TOTAL)
    w_uqT = w_uq.T.astype(BF16)
    w_iqT = w_iq.T.astype(BF16)
    w_ukT = w_uk.reshape(ATTN_WIDTH, KV_LORA_RANK).T.astype(BF16)
    w_uvT = w_uv.transpose(0, 2, 1).reshape(ATTN_WIDTH, KV_LORA_RANK).astype(BF16)

    mod = _ada(c.reshape(d, 1), w_ada, b_ada.reshape(1, -1))
    u = _modulate(x2, mod)
    proj = _proj_main(u, w_inT, b_cat[:, :P_PLAIN], 0)
    qk = _proj_main(u, w_inT, b_cat[:, P_PLAIN:P_MAIN], P_PLAIN, (conv_w, conv_b.reshape(1, -1)))
    proj_tail = _proj(u, w_tail, b_cat[:, P_MAIN:])

    qT, qiT = _qpath(proj_tail, g_q.reshape(1, -1), w_uqT, w_iqT, tq)
    k, vT, kidx, widx = _kvpath(proj_tail, g_kv.reshape(1, -1), g_kidx.reshape(1, -1), b_kidx.reshape(1, -1), w_ukT, w_uvT, tq)
    wT = widx.T.reshape(IDX_HEADS, 1, s)
    keys, thr = _indexer(kidx, qiT, wT, tq, nsel)
    a_attn = _attention(qT, k, vT, keys, thr, proj, tq, tk)

    gates_t = proj_tail[:, SMALL_OFF - P_MAIN + SM_I:SMALL_OFF - P_MAIN + SM_F + MLSTM_HEADS].T
    a_mlstm = _mlstm(qk, proj, gates_t, g_mh.reshape(MLSTM_HEADS, 1, MLSTM_V_DIM), MLSTM_CHUNK)

    merged = _merge(a_attn, a_mlstm, w_attn_out.astype(BF16), w_mlstm_out.astype(BF16), proj)
    return _final(merged, w_out.astype(BF16), x2, mod, ln_g.reshape(1, -1), ln_b.reshape(1, -1))


def kernel(x, c, w_ada, b_ada, w_in, b_in, g_q, g_kv, w_uq, w_iq, w_uk, w_uv, g_kidx, b_kidx, conv_w, conv_b, g_mh,
           w_attn_out, w_mlstm_out, w_out, ln_g, ln_b):
    bsz, seq, d = x.shape
    assert bsz == 1 and w_ada.shape[0] == 1, "single batch, single layer"
    out = _layer(x.reshape(seq, d), c, w_ada[0], b_ada[0], w_in[0], b_in[0], g_q[0], g_kv[0], w_uq[0], w_iq[0],
                 w_uk[0], w_uv[0], g_kidx[0], b_kidx[0], conv_w[0], conv_b[0], g_mh[0], w_attn_out[0],
                 w_mlstm_out[0], w_out[0], ln_g[0], ln_b[0])
    return out.reshape(bsz, seq, d)
```

```python
import functools

import jax
import jax.numpy as jnp
from jax import lax
from jax.experimental import pallas as pl
from jax.experimental.pallas import tpu as pltpu

F32 = jnp.float32
BF16 = jnp.bfloat16
I32 = jnp.int32

D_MODEL = 4096
ATTN_HEADS = 32
ATTN_HEAD_DIM = 128
ATTN_WIDTH = ATTN_HEADS * ATTN_HEAD_DIM
Q_LORA_RANK = 1024
KV_LORA_RANK = 512
IDX_HEADS = 32
IDX_HEAD_DIM = 64
TOPK_MAX = 256
MLSTM_HEADS = 8
MLSTM_QK_DIM = (D_MODEL // 2) // MLSTM_HEADS
MLSTM_V_DIM = D_MODEL // MLSTM_HEADS
MLSTM_QK_WIDTH = MLSTM_HEADS * MLSTM_QK_DIM
MLSTM_WIDTH = MLSTM_HEADS * MLSTM_V_DIM
MLSTM_CHUNK = 256
MLSTM_GROUP = 2
CONV_WIDTH = 4
GATE_SOFTCAP = 15.0
DEEPNORM_ALPHA = 2.0 ** 0.25
NORM_EPS = 1e-6

IN_WIDTHS = (Q_LORA_RANK, KV_LORA_RANK, IDX_HEAD_DIM, IDX_HEADS, ATTN_WIDTH, 2 * MLSTM_QK_WIDTH, MLSTM_WIDTH,
             MLSTM_WIDTH, MLSTM_HEADS, MLSTM_HEADS, MLSTM_WIDTH, D_MODEL, D_MODEL)
IN_NAMES = ("q_lat", "kv_lat", "k_idx", "w_idx", "z_attn", "qk_m", "v_m", "o_m", "i_m", "f_m", "z_m", "g_attn", "g_mlstm")
IN_OFFSETS = {n: sum(IN_WIDTHS[:i]) for i, n in enumerate(IN_NAMES)}
IN_WIDTH_OF = dict(zip(IN_NAMES, IN_WIDTHS))

P_ORDER = ("z_attn", "v_m", "o_m", "z_m", "g_attn", "g_mlstm", "q_lat", "qk_m", "kv_lat", "k_idx", "w_idx", "i_m", "f_m")
P_OFFSETS = {}
_off = 0
for _n in P_ORDER:
    P_OFFSETS[_n] = _off
    _off += IN_WIDTH_OF[_n]
P_USED = _off
PROJ_TN = 1024
P_TOTAL = -(-P_USED // PROJ_TN) * PROJ_TN
SMALL_W = 128
SMALL_OFF = P_OFFSETS["k_idx"]
P_PLAIN = P_OFFSETS["qk_m"]
P_MAIN = P_OFFSETS["kv_lat"]
P_TAIL = P_TOTAL - P_MAIN
assert P_PLAIN % PROJ_TN == 0 and P_MAIN % PROJ_TN == 0 and P_TAIL % PROJ_TN == 0
SM_WIDX = IDX_HEAD_DIM
SM_I = SM_WIDX + IDX_HEADS
SM_F = SM_I + MLSTM_HEADS

VMEM_CAP_BYTES = 60 * 1024 * 1024
VMEM_MB = dict(ada=32, modulate=40, proj=56, proj_main=56, qpath=48, kvpath=48, indexer=40, attn=56,
               mlstm=32, merge=48, final=56, regroup=40)

TQ = 256
TK = 512
ROW_TILE = 512
COL_TILE = 512
PROJ_TM = 1024
IDX_ROWS = 128

LOG2E = 1.4426950408889634
V_ONES = 16
V_ROWS = ATTN_HEAD_DIM + V_ONES
ATTN_ROWS = 256
HEAD_GROUP = 32
INT_MIN = -2 ** 31
KEY_NEG_INF = INT_MIN + 0x7FFFFF
KEY16_NEG_INF = -2 ** 15 + 0x7F


def _cparams(sem, call):
    return pltpu.CompilerParams(dimension_semantics=sem, vmem_limit_bytes=min(VMEM_MB[call] * 1024 * 1024, VMEM_CAP_BYTES))


def _sigmoid(x):
    return jax.nn.sigmoid(x)


def _silu(x):
    return x * jax.nn.sigmoid(x)


def _const_spec(shape):
    nd = len(shape)
    return pl.BlockSpec(shape, lambda *_: (0,) * nd, pipeline_mode=pl.Buffered(1))


def _ada_kernel(c_ref, w_ref, b_ref, o_ref):
    c = c_ref[...]
    o_ref[...] = jnp.sum(w_ref[...] * _silu(c), axis=0, keepdims=True) + b_ref[...]


def _ada(c_col, w_ada, b_ada):
    d, n = w_ada.shape
    tn = COL_TILE
    return pl.pallas_call(
        _ada_kernel,
        out_shape=jax.ShapeDtypeStruct((1, n), F32),
        grid=(n // tn,),
        in_specs=[pl.BlockSpec((d, 1), lambda j: (0, 0)),
                  pl.BlockSpec((d, tn), lambda j: (0, j)),
                  pl.BlockSpec((1, tn), lambda j: (0, j))],
        out_specs=pl.BlockSpec((1, tn), lambda j: (0, j)),
        compiler_params=_cparams(("arbitrary",), "ada"),
        name="ada",
    )(c_col, w_ada, b_ada)


def _modulate_kernel(x_ref, shift_ref, scale_ref, u_ref):
    u_ref[...] = (x_ref[...] * (1.0 + scale_ref[...]) + shift_ref[...]).astype(BF16)


def _modulate(x2, mod):
    s, d = x2.shape
    tm = min(ROW_TILE, s)
    return pl.pallas_call(
        _modulate_kernel,
        out_shape=jax.ShapeDtypeStruct((s, d), BF16),
        grid=(s // tm,),
        in_specs=[pl.BlockSpec((tm, d), lambda i: (i, 0)),
                  pl.BlockSpec((1, d), lambda i: (0, 0)),
                  pl.BlockSpec((1, d), lambda i: (0, 1))],
        out_specs=pl.BlockSpec((tm, d), lambda i: (i, 0)),
        compiler_params=_cparams(("arbitrary",), "modulate"),
        name="modulate",
    )(x2, mod, mod)


def _proj_kernel(u_ref, w_ref, b_ref, o_ref):
    nt = (((1,), (1,)), ((), ()))
    o_ref[...] = lax.dot_general(u_ref[...], w_ref[...], nt, preferred_element_type=F32) + b_ref[...]


def _proj(u, w_catT, b_cat):
    s, d = u.shape
    n = w_catT.shape[0]
    tm = min(PROJ_TM, s)
    tn = PROJ_TN
    return pl.pallas_call(
        _proj_kernel,
        out_shape=jax.ShapeDtypeStruct((s, n), F32),
        grid=(n // tn, s // tm),
        in_specs=[pl.BlockSpec((tm, d), lambda j, i: (i, 0)),
                  pl.BlockSpec((tn, d), lambda j, i: (j, 0)),
                  pl.BlockSpec((1, tn), lambda j, i: (0, j))],
        out_specs=pl.BlockSpec((tm, tn), lambda j, i: (i, j)),
        compiler_params=_cparams(("arbitrary", "arbitrary"), "proj"),
        name="proj",
    )(u, w_catT, b_cat)


PM_CHUNK = 128


def _proj_main_kernel(starts_ref, u_ref, b_ref, *refs, n_m, conv):
    if conv:
        cw_ref, cb_ref, w_hbm, o_ref, wbf_ref, st_ref, sem, halo_ref = refs
    else:
        w_hbm, o_ref, wbf_ref, st_ref, sem = refs
    j = pl.program_id(0)
    i = pl.program_id(1)
    nj = pl.num_programs(0)
    step = j * n_m + i
    cpt = PROJ_TN // PM_CHUNK
    cps = cpt // n_m
    cpw = RG_TN // PM_CHUNK

    def chunk_copy(tile, c, slot):
        win = starts_ref[tile * (PROJ_TN // RG_TN) + c // cpw]
        row0 = pl.multiple_of(win * F32_SUBLANES + (c % cpw) * PM_CHUNK, F32_SUBLANES)
        return pltpu.make_async_copy(w_hbm.at[pl.ds(row0, PM_CHUNK), :], st_ref.at[slot], sem.at[slot])

    def cast_chunk(tile, c, slot):
        rows = pl.ds(pl.multiple_of(c * PM_CHUNK, PM_CHUNK), PM_CHUNK)
        wbf_ref[tile % 2, rows, :] = st_ref[slot].astype(BF16)

    def group(g):
        tile = g // n_m + 1
        return [(tile, (g % n_m) * cps + e, (g % 2) * cps + e) for e in range(cps)]

    @pl.when(step == 0)
    def _first_tile():
        for c in range(cpt):
            cp = chunk_copy(0, c, 0)
            cp.start()
            cp.wait()
            cast_chunk(0, c, 0)

        @pl.when(nj > 1)
        def _():
            for tile, c, slot in group(0):
                chunk_copy(tile, c, slot).start()

    @pl.when((step + 1) // n_m + 1 < nj)
    def _prefetch():
        for tile, c, slot in group(step + 1):
            chunk_copy(tile, c, slot).start()

    @pl.when(j + 1 < nj)
    def _stage_next_tile():
        for tile, c, slot in group(step):
            chunk_copy(tile, c, slot).wait()
            cast_chunk(tile, c, slot)

    nt = (((1,), (1,)), ((), ()))
    x = lax.dot_general(u_ref[...], wbf_ref[j % 2], nt, preferred_element_type=F32) + b_ref[...]
    if not conv:
        o_ref[...] = x
        return
    prev = jnp.where(i > 0, halo_ref[...], 0.0)
    halo_ref[...] = x[-8:]
    head = jnp.concatenate([prev, x[:8]], axis=0)
    y = cb_ref[...]
    yh = cb_ref[...]
    for tap in range(CONV_WIDTH):
        dly = CONV_WIDTH - 1 - tap
        xs = x if dly == 0 else pltpu.roll(x, dly, 0)
        hs = head if dly == 0 else pltpu.roll(head, dly, 0)
        y = y + xs * cw_ref[tap:tap + 1, :]
        yh = yh + hs[8:] * cw_ref[tap:tap + 1, :]
    y = _silu(jnp.concatenate([yh, y[8:]], axis=0))
    kscale = jnp.where(j * PROJ_TN >= MLSTM_QK_WIDTH, MLSTM_QK_DIM ** -0.5, 1.0)
    o_ref[...] = (y * kscale).astype(BF16)


def _proj_main(u, w_inT, b_main, first_col, conv_wb=None):
    s, d = u.shape
    n = b_main.shape[1]
    conv = conv_wb is not None
    tm = min(PROJ_TM, s)
    tn = PROJ_TN
    n_m = s // tm
    cps = (tn // PM_CHUNK) // n_m
    assert cps * n_m * PM_CHUNK == tn, (s, tm)
    grid_spec = pltpu.PrefetchScalarGridSpec(
        num_scalar_prefetch=1,
        grid=(n // tn, n_m),
        in_specs=[pl.BlockSpec((tm, d), lambda j, i, t: (i, 0)),
                  pl.BlockSpec((1, tn), lambda j, i, t: (0, j))]
        + ([pl.BlockSpec((CONV_WIDTH, tn), lambda j, i, t: (0, j)), pl.BlockSpec((1, tn), lambda j, i, t: (0, j))] if conv else [])
        + [pl.BlockSpec(memory_space=pl.ANY)],
        out_specs=pl.BlockSpec((tm, tn), lambda j, i, t: (i, j)),
        scratch_shapes=[pltpu.VMEM((2, tn, d), BF16),
                        pltpu.VMEM((2 * cps, PM_CHUNK, d), F32),
                        pltpu.SemaphoreType.DMA((2 * cps,))]
        + ([pltpu.VMEM((8, tn), F32)] if conv else []),
    )
    return pl.pallas_call(
        functools.partial(_proj_main_kernel, n_m=n_m, conv=conv),
        out_shape=jax.ShapeDtypeStruct((s, n), BF16 if conv else F32),
        grid_spec=grid_spec,
        compiler_params=_cparams(("arbitrary", "arbitrary"), "proj_main"),
        name="proj_qk" if conv else "proj_main",
    )(jnp.asarray(_window_starts(first_col, n), I32), u, b_main, *(conv_wb or ()), w_inT)


def _qpath_kernel(ql_ref, g_ref, wuq_ref, wiq_ref, qT_ref, qiT_ref, *, scale):
    x = ql_ref[...]
    cq = (x * lax.rsqrt(jnp.mean(x * x, axis=-1, keepdims=True) + NORM_EPS) * g_ref[...]).astype(BF16)
    nt = (((1,), (1,)), ((), ()))
    qT = lax.dot_general(wuq_ref[...], cq, nt, preferred_element_type=F32)
    qT_ref[...] = (qT * scale).reshape(qT_ref.shape).astype(BF16)
    qiT = lax.dot_general(wiq_ref[...], cq, nt, preferred_element_type=F32)
    qiT_ref[...] = qiT.reshape(qiT_ref.shape).astype(BF16)


def _qpath(proj, g_q, w_uqT, w_iqT, tq):
    s = proj.shape[0]
    r = Q_LORA_RANK
    return pl.pallas_call(
        functools.partial(_qpath_kernel, scale=ATTN_HEAD_DIM ** -0.5 * LOG2E),
        out_shape=(jax.ShapeDtypeStruct((ATTN_HEADS, ATTN_HEAD_DIM, s), BF16),
                   jax.ShapeDtypeStruct((IDX_HEADS, IDX_HEAD_DIM, s), BF16)),
        grid=(s // tq,),
        in_specs=[pl.BlockSpec((tq, r), lambda i: (i, P_OFFSETS["q_lat"] // r)),
                  _const_spec((1, r)),
                  _const_spec(w_uqT.shape),
                  _const_spec(w_iqT.shape)],
        out_specs=(pl.BlockSpec((ATTN_HEADS, ATTN_HEAD_DIM, tq), lambda i: (0, 0, i)),
                   pl.BlockSpec((IDX_HEADS, IDX_HEAD_DIM, tq), lambda i: (0, 0, i))),
        compiler_params=_cparams(("arbitrary",), "qpath"),
        name="qpath",
    )(proj, g_q, w_uqT, w_iqT)


def _kvpath_kernel(kvl_ref, sm_ref, gkv_ref, gk_ref, bk_ref, wuk_ref, wuv_ref, k_ref, vT_ref, kidx_ref, widx_ref, *, wscale):
    x = kvl_ref[...]
    ckv = (x * lax.rsqrt(jnp.mean(x * x, axis=-1, keepdims=True) + NORM_EPS) * gkv_ref[...]).astype(BF16)
    kfull = jnp.dot(ckv, wuk_ref[...], preferred_element_type=F32)
    for h in range(ATTN_HEADS):
        k_ref[h] = kfull[:, h * ATTN_HEAD_DIM:(h + 1) * ATTN_HEAD_DIM].astype(BF16)
    nt = (((1,), (1,)), ((), ()))
    vT = lax.dot_general(wuv_ref[...], ckv, nt, preferred_element_type=F32)
    vT_ref[:, :ATTN_HEAD_DIM, :] = vT.reshape(ATTN_HEADS, ATTN_HEAD_DIM, -1).astype(BF16)
    vT_ref[:, ATTN_HEAD_DIM:, :] = jnp.ones((ATTN_HEADS, V_ONES, vT_ref.shape[2]), BF16)
    sm = sm_ref[...]
    ki = sm[:, :IDX_HEAD_DIM]
    mu = jnp.mean(ki, axis=-1, keepdims=True)
    var = jnp.mean(jnp.square(ki - mu), axis=-1, keepdims=True)
    kidx_ref[...] = ((ki - mu) * lax.rsqrt(var + NORM_EPS) * gk_ref[...] + bk_ref[...]).astype(BF16)
    widx_ref[...] = sm[:, SM_WIDX:SM_WIDX + IDX_HEADS] * wscale


def _kvpath(proj_tail, g_kv, g_kidx, b_kidx, w_ukT, w_uvT, tm):
    s = proj_tail.shape[0]
    r = KV_LORA_RANK
    return pl.pallas_call(
        functools.partial(_kvpath_kernel, wscale=IDX_HEADS ** -0.5 * IDX_HEAD_DIM ** -0.5),
        out_shape=(jax.ShapeDtypeStruct((ATTN_HEADS, s, ATTN_HEAD_DIM), BF16),
                   jax.ShapeDtypeStruct((ATTN_HEADS, V_ROWS, s), BF16),
                   jax.ShapeDtypeStruct((s, IDX_HEAD_DIM), BF16),
                   jax.ShapeDtypeStruct((s, IDX_HEADS), F32)),
        grid=(s // tm,),
        in_specs=[pl.BlockSpec((tm, r), lambda i: (i, (P_OFFSETS["kv_lat"] - P_MAIN) // r)),
                  pl.BlockSpec((tm, SMALL_W), lambda i: (i, (SMALL_OFF - P_MAIN) // SMALL_W)),
                  _const_spec((1, r)),
                  _const_spec((1, IDX_HEAD_DIM)),
                  _const_spec((1, IDX_HEAD_DIM)),
                  _const_spec(w_ukT.shape),
                  _const_spec(w_uvT.shape)],
        out_specs=(pl.BlockSpec((ATTN_HEADS, tm, ATTN_HEAD_DIM), lambda i: (0, i, 0)),
                   pl.BlockSpec((ATTN_HEADS, V_ROWS, tm), lambda i: (0, 0, i)),
                   pl.BlockSpec((tm, IDX_HEAD_DIM), lambda i: (i, 0)),
                   pl.BlockSpec((tm, IDX_HEADS), lambda i: (i, 0))),
        compiler_params=_cparams(("arbitrary",), "kvpath"),
        name="kvpath",
    )(proj_tail, proj_tail, g_kv, g_kidx, b_kidx, w_ukT, w_uvT)


def _key_to_float(key):
    bits = jnp.where(key >= 0, key, key ^ 0x7FFFFFFF)
    return jnp.where(key < KEY_NEG_INF, -jnp.inf, pltpu.bitcast(bits, F32))


def _key16_to_float(key):
    bits = jnp.where(key >= 0, key, key ^ 0x7FFF)
    return jnp.where(key < KEY16_NEG_INF, -jnp.inf, pltpu.bitcast(jnp.left_shift(bits, 16), F32))


def _indexer_kernel(kidx_ref, qiT_ref, wT_ref, sc_ref, thr_ref, s16_ref, *, seq, tq, nsel):
    i = pl.program_id(0)
    ch = IDX_ROWS
    cb = tq
    n_score = (i + 1) * (tq // ch)
    n_count = i + 1
    tpos = i * tq + lax.broadcasted_iota(I32, (ch, tq), 1)

    def score_chunk(c, carry):
        r0 = pl.multiple_of(c * ch, ch)
        kc = kidx_ref[pl.ds(r0, ch), :]

        acc = jnp.zeros((ch, tq), F32)
        for h in range(IDX_HEADS):
            r = jnp.dot(kc, qiT_ref[h], preferred_element_type=F32)
            acc = acc + jnp.maximum(r, 0.0) * wT_ref[h]
        spos = r0 + lax.broadcasted_iota(I32, (ch, tq), 0)
        val = jnp.where(spos <= tpos, acc, -jnp.inf)
        sc_ref[pl.ds(r0, ch), :] = val
        s16_ref[pl.ds(r0, ch), :] = val.astype(BF16)
        return carry
    lax.fori_loop(0, n_score, score_chunk, 0)

    def fill_chunk(c, carry):
        sc_ref[pl.ds(pl.multiple_of(c * cb, cb), cb), :] = jnp.full((cb, tq), -jnp.inf, F32)
        return carry
    lax.fori_loop(n_count, seq // cb, fill_chunk, 0)

    def count(pred):
        def body(c, part):
            r0 = pl.multiple_of(c * cb, cb)
            m = jnp.where(pred(sc_ref[pl.ds(r0, cb), :], r0), 1, 0)
            return part + jnp.sum(m.reshape(cb // 8, 8, tq), axis=0)
        part = lax.fori_loop(0, n_count, body, jnp.zeros((8, tq), I32))
        return jnp.sum(part, axis=0, keepdims=True)

    def count_ge(cand_key):
        cand = _key_to_float(cand_key)
        return count(lambda blk, r0: blk >= cand)

    def count16_ge(cand_key16):
        cand = _key16_to_float(cand_key16).astype(BF16)

        def body(c, acc):
            blk = s16_ref[pl.ds(pl.multiple_of(c * cb, cb), cb), :]
            ones = jnp.where(blk >= cand, jnp.ones((), BF16), jnp.zeros((), BF16))
            part = ones[:16]
            for g in range(1, cb // 16):
                part = part + ones[g * 16:(g + 1) * 16]
            return acc + part.astype(F32)
        acc = lax.fori_loop(0, n_count, body, jnp.zeros((16, tq), F32))
        return jnp.sum(acc, axis=0, keepdims=True).astype(I32)

    r0_key = jnp.where(count16_ge(jnp.zeros((1, tq), I32)) >= nsel, 0, -2 ** 15).astype(I32)

    def bit16_step(b, t):
        cand = t + jnp.left_shift(jnp.int32(1), 14 - b)
        return jnp.where(count16_ge(cand) >= nsel, cand, t)
    rbits = pltpu.bitcast(_key16_to_float(lax.fori_loop(0, 15, bit16_step, r0_key)), I32)
    r_key = jnp.where(rbits >= 0, rbits, rbits ^ 0x7FFFFFFF)

    def bit_step(b, t):
        cand = t + jnp.left_shift(jnp.int32(1), 16 - b)
        return jnp.where(count_ge(cand) >= nsel, cand, t)
    thr = _key_to_float(lax.fori_loop(0, 17, bit_step, r_key - 2 ** 16))
    thr_ref[...] = thr

    tie = (count(lambda blk, r0: blk >= thr) > nsel) & (thr > -jnp.inf)

    @pl.when(jnp.max(tie.astype(I32)) > 0)
    def _break_ties():
        need = nsel - count(lambda blk, r0: blk > thr)

        def eq_below(j):
            return count(lambda blk, r0: (blk == thr) & (r0 + lax.broadcasted_iota(I32, (cb, tq), 0) < j))

        def jbit(b, j):
            test = j + jnp.left_shift(jnp.int32(1), (seq.bit_length() - 2) - b)
            return jnp.where(eq_below(test) < need, test, j)
        jlast = lax.fori_loop(0, seq.bit_length() - 1, jbit, jnp.zeros((1, tq), I32))

        def demote(c, carry):
            r0 = pl.multiple_of(c * cb, cb)
            blk = sc_ref[pl.ds(r0, cb), :]
            row = r0 + lax.broadcasted_iota(I32, (cb, tq), 0)
            sc_ref[pl.ds(r0, cb), :] = jnp.where(tie & (blk == thr) & (row > jlast), -jnp.inf, blk)
            return carry
        lax.fori_loop(0, n_count, demote, 0)


def _indexer(kidx, qiT, wT, tq, nsel):
    s = kidx.shape[0]
    return pl.pallas_call(
        functools.partial(_indexer_kernel, seq=s, tq=tq, nsel=nsel),
        out_shape=(jax.ShapeDtypeStruct((s, s), F32), jax.ShapeDtypeStruct((1, s), F32)),
        grid=(s // tq,),
        in_specs=[_const_spec((s, IDX_HEAD_DIM)),
                  pl.BlockSpec((IDX_HEADS, IDX_HEAD_DIM, tq), lambda i: (0, 0, i)),
                  pl.BlockSpec((IDX_HEADS, 1, tq), lambda i: (0, 0, i))],
        out_specs=(pl.BlockSpec((s, tq), lambda i: (0, i)),
                   pl.BlockSpec((1, tq), lambda i: (0, i))),
        scratch_shapes=[pltpu.VMEM((s, tq), BF16)],
        compiler_params=_cparams(("arbitrary",), "indexer"),
        name="indexer",
    )(kidx, qiT, wT)


def _attn_kernel(tiles_ref, qT_ref, k_ref, vT_ref, keys_ref, thr_ref, z_ref, sl_ref, kf_ref, qf_ref, o_ref,
                 acc_ref, m_ref, mb_ref, lg_ref, p_ref, *, tq, tk):
    qi = tiles_ref[0, pl.program_id(0)]
    kj = tiles_ref[1, pl.program_id(0)]

    @pl.when(kj == 0)
    def _init():
        acc_ref[...] = jnp.zeros(acc_ref.shape, F32)
        m_ref[...] = jnp.full(m_ref.shape, -jnp.inf, F32)

    def _compute():
        spos = kj * tk + lax.broadcasted_iota(I32, (tk, tq), 0)
        tpos = qi * tq + lax.broadcasted_iota(I32, (tk, tq), 1)
        sel = (keys_ref[...] >= thr_ref[...]) & (spos <= tpos)
        mb_ref[...] = jnp.where(sel, 0.0, -jnp.inf)
        tile_off = (kj * tk - qi * tq).astype(F32)

        def group(g, carry):
            def logits(u):
                h = g * HEAD_GROUP + u
                qh = jnp.concatenate([qT_ref[h], qf_ref[h]], axis=0)
                part = jnp.full((8, tq), -jnp.inf, F32)
                for c in range(tk // ATTN_ROWS):
                    rows = pl.ds(c * ATTN_ROWS, ATTN_ROWS)
                    kh = jnp.concatenate([k_ref[h, rows, :], kf_ref[rows, :]], axis=1)
                    lg = jnp.dot(kh, qh, preferred_element_type=F32) + mb_ref[rows, :]
                    lg_ref[u % 2, rows, :] = lg
                    part = jnp.maximum(part, jnp.max(lg.reshape(ATTN_ROWS // 8, 8, tq), axis=0))
                shift = sl_ref[h] * tile_off
                m_old = m_ref[g, u]
                return m_old, jnp.maximum(m_old, jnp.max(part, axis=0, keepdims=True) + shift), shift

            def probs(u, m_old, m_new, shift):
                m_safe = jnp.where(m_new == -jnp.inf, 0.0, m_new)
                m_tile = m_safe - shift
                for c in range(tk // ATTN_ROWS):
                    rows = pl.ds(c * ATTN_ROWS, ATTN_ROWS)
                    p_ref[u % 2, rows, :] = jnp.exp2(lg_ref[u % 2, rows, :] - m_tile).astype(BF16)
                m_ref[g, u] = m_new
                return jnp.exp2(m_old - m_safe)

            def values(u, alpha):
                h = g * HEAD_GROUP + u
                acc_ref[g, u] = alpha * acc_ref[g, u] + jnp.dot(vT_ref[h], p_ref[u % 2], preferred_element_type=F32)

            stats = logits(0)
            alpha_prev = None
            for u in range(HEAD_GROUP):
                stats_next = logits(u + 1) if u + 1 < HEAD_GROUP else None
                alpha = probs(u, *stats)
                if u >= 1:
                    values(u - 1, alpha_prev)
                stats, alpha_prev = stats_next, alpha
            values(HEAD_GROUP - 1, alpha_prev)
            return carry
        lax.fori_loop(0, ATTN_HEADS // HEAD_GROUP, group, 0)
    _compute()

    @pl.when(kj == (qi * tq + tq - 1) // tk)
    def _finish():
        for h in range(ATTN_HEADS):
            g, u = divmod(h, HEAD_GROUP)
            cols = slice(h * ATTN_HEAD_DIM, (h + 1) * ATTN_HEAD_DIM)
            acc = acc_ref[g, u]
            o = (acc[:ATTN_HEAD_DIM] * (1.0 / acc[ATTN_HEAD_DIM:ATTN_HEAD_DIM + 1])).T
            o_ref[:, cols] = (o * _silu(z_ref[:, cols])).astype(BF16)


def _alibi_features(tq, tk):
    sigma = jnp.exp2(-8.0 * jnp.arange(1, ATTN_HEADS + 1, dtype=F32) / ATTN_HEADS) * LOG2E
    s1 = sigma.astype(BF16)
    s2 = (sigma - s1.astype(F32)).astype(BF16)
    s3 = (sigma - s1.astype(F32) - s2.astype(F32)).astype(BF16)
    pieces = jnp.stack([s1, s2, s3, s1, s2, s3], axis=1)
    qf = jnp.zeros((ATTN_HEADS, ATTN_HEAD_DIM, tq), BF16)
    qf = qf.at[:, :6, :].set(jnp.broadcast_to(pieces[:, :, None], (ATTN_HEADS, 6, tq)))
    r = jnp.arange(tk, dtype=I32)
    r_hi = ((r // 256) * 256).astype(BF16)
    r_lo = (r % 256).astype(BF16)
    kf = jnp.zeros((tk, ATTN_HEAD_DIM), BF16).at[:, :6].set(jnp.stack([r_hi, r_hi, r_hi, r_lo, r_lo, r_lo], axis=1))
    return jnp.broadcast_to(sigma[:, None, None], (ATTN_HEADS, 1, tq)), kf, qf


def _attention(qT, k, vT, keys, thr, proj, tq, tk):
    s = k.shape[1]
    ng = ATTN_HEADS // HEAD_GROUP
    tiles = [(qi, kj) for qi in range(s // tq) for kj in range((qi * tq + tq - 1) // tk + 1)]
    const3 = lambda shape: pl.BlockSpec(shape, lambda i, t: (0, 0, 0), pipeline_mode=pl.Buffered(1))
    grid_spec = pltpu.PrefetchScalarGridSpec(
        num_scalar_prefetch=1,
        grid=(len(tiles),),
        in_specs=[pl.BlockSpec((ATTN_HEADS, ATTN_HEAD_DIM, tq), lambda i, t: (0, 0, t[0, i])),
                  pl.BlockSpec((ATTN_HEADS, tk, ATTN_HEAD_DIM), lambda i, t: (0, t[1, i], 0)),
                  pl.BlockSpec((ATTN_HEADS, V_ROWS, tk), lambda i, t: (0, 0, t[1, i])),
                  pl.BlockSpec((tk, tq), lambda i, t: (t[1, i], t[0, i])),
                  pl.BlockSpec((1, tq), lambda i, t: (0, t[0, i])),
                  pl.BlockSpec((tq, ATTN_WIDTH), lambda i, t: (t[0, i], P_OFFSETS["z_attn"] // ATTN_WIDTH)),
                  const3((ATTN_HEADS, 1, tq)),
                  pl.BlockSpec((tk, ATTN_HEAD_DIM), lambda i, t: (0, 0), pipeline_mode=pl.Buffered(1)),
                  const3((ATTN_HEADS, ATTN_HEAD_DIM, tq))],
        out_specs=pl.BlockSpec((tq, ATTN_WIDTH), lambda i, t: (t[0, i], 0)),
        scratch_shapes=[pltpu.VMEM((ng, HEAD_GROUP, V_ROWS, tq), F32),
                        pltpu.VMEM((ng, HEAD_GROUP, 1, tq), F32),
                        pltpu.VMEM((tk, tq), F32),
                        pltpu.VMEM((2, tk, tq), F32),
                        pltpu.VMEM((2, tk, tq), BF16)],
    )
    return pl.pallas_call(
        functools.partial(_attn_kernel, tq=tq, tk=tk),
        out_shape=jax.ShapeDtypeStruct((s, ATTN_WIDTH), BF16),
        grid_spec=grid_spec,
        compiler_params=_cparams(("arbitrary",), "attn"),
        name="attn",
    )(jnp.asarray(tiles, I32).T, qT, k, vT, keys, thr, proj, *_alibi_features(tq, tk))


def _softcap(x):
    return GATE_SOFTCAP * jnp.tanh(x / GATE_SOFTCAP)


def _mlstm_kernel(q_ref, k_ref, v_ref, og_ref, z_ref, gt_ref, g_ref, out_ref, c_ref, n_ref, m_ref, *, chunk):
    ci = pl.program_id(1)
    L = chunk
    dk, dv = MLSTM_QK_DIM, MLSTM_V_DIM

    @pl.when(ci == 0)
    def _init():
        c_ref[...] = jnp.zeros(c_ref.shape, F32)
        n_ref[...] = jnp.zeros(n_ref.shape, F32)
        m_ref[...] = jnp.zeros(m_ref.shape, F32)

    gt = gt_ref[...]
    sub = lax.broadcasted_iota(I32, gt.shape, 0)
    r_i = lax.broadcasted_iota(I32, (L, L), 0)
    c_i = lax.broadcasted_iota(I32, (L, L), 1)
    eye = r_i == c_i
    tril = r_i >= c_i
    nt = (((1,), (1,)), ((), ()))
    tn = (((0,), (0,)), ((), ()))

    for j in range(MLSTM_GROUP):
        hd = pl.program_id(0) * MLSTM_GROUP + j
        ig_row = _softcap(jnp.sum(jnp.where(sub == hd, gt, 0.0), axis=0, keepdims=True))
        fg_row = _softcap(jnp.sum(jnp.where(sub == MLSTM_HEADS + hd, gt, 0.0), axis=0, keepdims=True))
        logf_row = jnp.minimum(fg_row, 0.0) - jnp.log1p(jnp.exp(-jnp.abs(fg_row)))
        ig_col = jnp.sum(jnp.where(eye, ig_row, 0.0), axis=1, keepdims=True)
        b_col = jnp.sum(jnp.where(tril, logf_row, 0.0), axis=1, keepdims=True)
        b_row = jnp.sum(jnp.where(eye, b_col, 0.0), axis=0, keepdims=True)
        dmat = jnp.where(tril, b_col - b_row + ig_row, -jnp.inf)
        m_prev = m_ref[j]
        m_inter = b_col + m_prev
        m_t = jnp.maximum(m_inter, jnp.max(dmat, axis=1, keepdims=True))

        qc = q_ref[:, j * dk:(j + 1) * dk]
        kc = k_ref[:, j * dk:(j + 1) * dk]
        vc = v_ref[:, j * dv:(j + 1) * dv].astype(BF16)
        s = lax.dot_general(qc, kc, nt, preferred_element_type=F32) * jnp.exp(dmat - m_t)
        inter = jnp.exp(m_inter - m_t)
        num = (jnp.dot(s.astype(BF16), vc, preferred_element_type=F32)
               + inter * jnp.dot(qc, c_ref[j].astype(BF16), preferred_element_type=F32))
        qn = jnp.sum(qc.astype(F32) * n_ref[j], axis=1, keepdims=True)
        den = jnp.sum(s, axis=1, keepdims=True) + inter * qn
        hh = num / jnp.maximum(jnp.abs(den), jnp.exp(-m_t))

        g_last = b_col[L - 1:L, :]
        m_new = m_t[L - 1:L, :]
        wgt = jnp.exp(g_last - b_col + ig_col - m_new)
        decay = jnp.exp(g_last + m_prev - m_new)
        wk = wgt * kc.astype(F32)
        c_ref[j] = decay * c_ref[j] + lax.dot_general(wk.astype(BF16), vc, tn, preferred_element_type=F32)
        n_ref[j] = decay * n_ref[j] + jnp.sum(wk, axis=0, keepdims=True)
        m_ref[j] = m_new

        hn = hh * lax.rsqrt(jnp.mean(hh * hh, axis=-1, keepdims=True) + NORM_EPS) * g_ref[j]
        cols = slice(j * dv, (j + 1) * dv)
        out_ref[:, cols] = (hn * _sigmoid(og_ref[:, cols]) * _silu(z_ref[:, cols])).astype(BF16)


def _mlstm(qk, proj, gates_t, g_mh3, chunk):
    s = qk.shape[0]
    gdk, gdv = MLSTM_GROUP * MLSTM_QK_DIM, MLSTM_GROUP * MLSTM_V_DIM
    vb, ob, zb = (P_OFFSETS[n] // gdv for n in ("v_m", "o_m", "z_m"))
    return pl.pallas_call(
        functools.partial(_mlstm_kernel, chunk=chunk),
        out_shape=jax.ShapeDtypeStruct((s, MLSTM_WIDTH), BF16),
        grid=(MLSTM_HEADS // MLSTM_GROUP, s // chunk),
        in_specs=[pl.BlockSpec((chunk, gdk), lambda h, c: (c, h)),
                  pl.BlockSpec((chunk, gdk), lambda h, c: (c, MLSTM_QK_WIDTH // gdk + h)),
                  pl.BlockSpec((chunk, gdv), lambda h, c: (c, vb + h)),
                  pl.BlockSpec((chunk, gdv), lambda h, c: (c, ob + h)),
                  pl.BlockSpec((chunk, gdv), lambda h, c: (c, zb + h)),
                  pl.BlockSpec((2 * MLSTM_HEADS, chunk), lambda h, c: (0, c)),
                  pl.BlockSpec((MLSTM_GROUP, 1, MLSTM_V_DIM), lambda h, c: (h, 0, 0))],
        out_specs=pl.BlockSpec((chunk, gdv), lambda h, c: (c, h)),
        scratch_shapes=[pltpu.VMEM((MLSTM_GROUP, MLSTM_QK_DIM, MLSTM_V_DIM), F32),
                        pltpu.VMEM((MLSTM_GROUP, 1, MLSTM_QK_DIM), F32),
                        pltpu.VMEM((MLSTM_GROUP, 1, 1), F32)],
        compiler_params=_cparams(("arbitrary", "arbitrary"), "mlstm"),
        name="mlstm",
    )(qk, qk, proj, proj, proj, gates_t, g_mh3)


def _merge_kernel(a1_ref, a2_ref, w1_ref, w2_ref, ga_ref, gm_ref, o_ref):
    y1 = jnp.dot(a1_ref[...], w1_ref[...], preferred_element_type=F32)
    y2 = jnp.dot(a2_ref[...], w2_ref[...], preferred_element_type=F32)
    o_ref[...] = (_sigmoid(ga_ref[...]) * y1 + _sigmoid(gm_ref[...]) * y2).astype(BF16)


def _merge(a1, a2, w1, w2, proj):
    s, d = a1.shape
    tm = min(ROW_TILE, s)
    tn = COL_TILE
    gab, gmb = P_OFFSETS["g_attn"] // tn, P_OFFSETS["g_mlstm"] // tn
    return pl.pallas_call(
        _merge_kernel,
        out_shape=jax.ShapeDtypeStruct((s, D_MODEL), BF16),
        grid=(s // tm, D_MODEL // tn),
        in_specs=[pl.BlockSpec((tm, d), lambda i, j: (i, 0)),
                  pl.BlockSpec((tm, d), lambda i, j: (i, 0)),
                  pl.BlockSpec((d, tn), lambda i, j: (0, j)),
                  pl.BlockSpec((d, tn), lambda i, j: (0, j)),
                  pl.BlockSpec((tm, tn), lambda i, j: (i, gab + j)),
                  pl.BlockSpec((tm, tn), lambda i, j: (i, gmb + j))],
        out_specs=pl.BlockSpec((tm, tn), lambda i, j: (i, j)),
        compiler_params=_cparams(("arbitrary", "arbitrary"), "merge"),
        name="merge",
    )(a1, a2, w1, w2, proj, proj)


def _final_kernel(mg_ref, w_ref, x_ref, gate_ref, lg_ref, lb_ref, o_ref, *, tn, nn):
    j = pl.program_id(1)
    y = jnp.dot(mg_ref[...], w_ref[...], preferred_element_type=F32)
    for jj in range(nn):
        @pl.when(j == jj)
        def _store(jj=jj):
            o_ref[:, jj * tn:(jj + 1) * tn] = y

    @pl.when(j == nn - 1)
    def _norm():
        d = nn * tn
        ssum = 0.0
        for jj in range(nn):
            cols = slice(jj * tn, (jj + 1) * tn)
            r = DEEPNORM_ALPHA * x_ref[:, cols] + gate_ref[:, cols] * o_ref[:, cols]
            o_ref[:, cols] = r
            ssum = ssum + jnp.sum(r, axis=-1, keepdims=True)
        mu = ssum / d
        vsum = 0.0
        for jj in range(nn):
            cols = slice(jj * tn, (jj + 1) * tn)
            vsum = vsum + jnp.sum(jnp.square(o_ref[:, cols] - mu), axis=-1, keepdims=True)
        inv = lax.rsqrt(vsum / d + NORM_EPS)
        for jj in range(nn):
            cols = slice(jj * tn, (jj + 1) * tn)
            o_ref[:, cols] = (o_ref[:, cols] - mu) * inv * lg_ref[:, cols] + lb_ref[:, cols]


def _final(merged, w_out, x2, mod, ln_g, ln_b):
    s, d = x2.shape
    tm = min(ROW_TILE, s)
    tn = COL_TILE
    nn = d // tn
    return pl.pallas_call(
        functools.partial(_final_kernel, tn=tn, nn=nn),
        out_shape=jax.ShapeDtypeStruct((s, d), F32),
        grid=(s // tm, nn),
        in_specs=[pl.BlockSpec((tm, d), lambda i, j: (i, 0)),
                  pl.BlockSpec((d, tn), lambda i, j: (0, j)),
                  pl.BlockSpec((tm, d), lambda i, j: (i, 0)),
                  pl.BlockSpec((1, d), lambda i, j: (0, 2)),
                  pl.BlockSpec((1, d), lambda i, j: (0, 0)),
                  pl.BlockSpec((1, d), lambda i, j: (0, 0))],
        out_specs=pl.BlockSpec((tm, d), lambda i, j: (i, 0), pipeline_mode=pl.Buffered(1)),
        compiler_params=_cparams(("arbitrary", "arbitrary"), "final"),
        name="final",
    )(merged, w_out, x2, mod, ln_g, ln_b)


RG_TN = 512
F32_SUBLANES = 8
NARROW_A = ("k_idx", "w_idx")
NARROW_B = ("i_m", "f_m")


def _regroup_kernel(tbl_ref, main_ref, na_ref, nb_ref, o_ref, *, n_a, n_b):
    @pl.when(tbl_ref[pl.program_id(0)] >= 0)
    def _wide():
        o_ref[...] = main_ref[...].astype(BF16)

    @pl.when(tbl_ref[pl.program_id(0)] < 0)
    def _narrow():
        o_ref[:n_a, :] = na_ref[...].astype(BF16)
        o_ref[n_a:n_a + n_b, :] = nb_ref[...].astype(BF16)
        o_ref[n_a + n_b:, :] = jnp.zeros((o_ref.shape[0] - n_a - n_b, o_ref.shape[1]), BF16)


def _window_starts(first_col, n_cols):
    starts = []
    for oc in range(first_col, first_col + n_cols, RG_TN):
        if oc >= SMALL_OFF:
            starts.append(-1)
            continue
        seg = next(n for n in P_ORDER if P_OFFSETS[n] <= oc < P_OFFSETS[n] + IN_WIDTH_OF[n])
        start = IN_OFFSETS[seg] + oc - P_OFFSETS[seg]
        assert start % F32_SUBLANES == 0, (seg, start)
        starts.append(start // F32_SUBLANES)
    return starts


def _regroup_w(w_inT, first_col, n_cols):
    d = w_inT.shape[1]
    starts = _window_starts(first_col, n_cols)
    n_a = sum(IN_WIDTH_OF[n] for n in NARROW_A)
    n_b = sum(IN_WIDTH_OF[n] for n in NARROW_B)
    off_a, off_b = IN_OFFSETS[NARROW_A[0]], IN_OFFSETS[NARROW_B[0]]
    grid_spec = pltpu.PrefetchScalarGridSpec(
        num_scalar_prefetch=1,
        grid=(n_cols // RG_TN,),
        in_specs=[pl.BlockSpec((pl.Element(RG_TN), pl.Element(d)), lambda j, tbl: (jnp.maximum(tbl[j], 0) * F32_SUBLANES, 0)),
                  pl.BlockSpec((pl.Element(n_a), pl.Element(d)), lambda j, tbl: (off_a, 0)),
                  pl.BlockSpec((pl.Element(n_b), pl.Element(d)), lambda j, tbl: (off_b, 0))],
        out_specs=pl.BlockSpec((RG_TN, d), lambda j, tbl: (j, 0)),
    )
    return pl.pallas_call(
        functools.partial(_regroup_kernel, n_a=n_a, n_b=n_b),
        out_shape=jax.ShapeDtypeStruct((n_cols, d), BF16),
        grid_spec=grid_spec,
        compiler_params=_cparams(("arbitrary",), "regroup"),
        name="regroup",
    )(jnp.asarray(starts, I32), w_inT, w_inT, w_inT)


def _regroup_cols(a, pad_to):
    parts = [a[..., IN_OFFSETS[n]:IN_OFFSETS[n] + IN_WIDTH_OF[n]] for n in P_ORDER]
    parts.append(jnp.zeros(a.shape[:-1] + (pad_to - P_USED,), a.dtype))
    return jnp.concatenate(parts, axis=-1)


def _layer(x2, c, w_ada, b_ada, w_in, b_in, g_q, g_kv, w_uq, w_iq, w_uk, w_uv, g_kidx, b_kidx, conv_w, conv_b, g_mh,
           w_attn_out, w_mlstm_out, w_out, ln_g, ln_b):
    s, d = x2.shape
    assert d == D_MODEL and s % PROJ_TM == 0, (s, d)
    tq, tk = TQ, TK
    nsel = min(TOPK_MAX, s // 4)

    w_inT = w_in.T
    w_tail = _regroup_w(w_inT, P_MAIN, P_TAIL)
    b_cat = _regroup_cols(b_in, P_TOTAL).reshape(1, P_TOTAL)
    w_uqT = w_uq.T.astype(BF16)
    w_iqT = w_iq.T.astype(BF16)
    w_ukT = w_uk.reshape(ATTN_WIDTH, KV_LORA_RANK).T.astype(BF16)
    w_uvT = w_uv.transpose(0, 2, 1).reshape(ATTN_WIDTH, KV_LORA_RANK).astype(BF16)

    mod = _ada(c.reshape(d, 1), w_ada, b_ada.reshape(1, -1))
    u = _modulate(x2, mod)
    proj = _proj_main(u, w_inT, b_cat[:, :P_PLAIN], 0)
    qk = _proj_main(u, w_inT, b_cat[:, P_PLAIN:P_MAIN], P_PLAIN, (conv_w, conv_b.reshape(1, -1)))
    proj_tail = _proj(u, w_tail, b_cat[:, P_MAIN:])

    qT, qiT = _qpath(proj, g_q.reshape(1, -1), w_uqT, w_iqT, tq)
    k, vT, kidx, widx = _kvpath(proj_tail, g_kv.reshape(1, -1), g_kidx.reshape(1, -1), b_kidx.reshape(1, -1), w_ukT, w_uvT, tq)
    wT = widx.T.reshape(IDX_HEADS, 1, s)
    keys, thr = _indexer(kidx, qiT, wT, tq, nsel)
    a_attn = _attention(qT, k, vT, keys, thr, proj, tq, tk)

    gates_t = proj_tail[:, SMALL_OFF - P_MAIN + SM_I:SMALL_OFF - P_MAIN + SM_F + MLSTM_HEADS].T
    a_mlstm = _mlstm(qk, proj, gates_t, g_mh.reshape(MLSTM_HEADS, 1, MLSTM_V_DIM), MLSTM_CHUNK)

    merged = _merge(a_attn, a_mlstm, w_attn_out.astype(BF16), w_mlstm_out.astype(BF16), proj)
    return _final(merged, w_out.astype(BF16), x2, mod, ln_g.reshape(1, -1), ln_b.reshape(1, -1))


def kernel(x, c, w_ada, b_ada, w_in, b_in, g_q, g_kv, w_uq, w_iq, w_uk, w_uv, g_kidx, b_kidx, conv_w, conv_b, g_mh,
           w_attn_out, w_mlstm_out, w_out, ln_g, ln_b):
    bsz, seq, d = x.shape
    assert bsz == 1 and w_ada.shape[0] == 1, "single batch, single layer"
    out = _layer(x.reshape(seq, d), c, w_ada[0], b_ada[0], w_in[0], b_in[0], g_q[0], g_kv[0], w_uq[0], w_iq[0],
                 w_uk[0], w_uv[0], g_kidx[0], b_kidx[0], conv_w[0], conv_b[0], g_mh[0], w_attn_out[0],
                 w_mlstm_out[0], w_out[0], ln_g[0], ln_b[0])
    return out.reshape(bsz, seq, d)
```

```python
import functools

import jax
import jax.numpy as jnp
from jax import lax
from jax.experimental import pallas as pl
from jax.experimental.pallas import tpu as pltpu

F32 = jnp.float32
BF16 = jnp.bfloat16
I32 = jnp.int32

D_MODEL = 4096
ATTN_HEADS = 32
ATTN_HEAD_DIM = 128
ATTN_WIDTH = ATTN_HEADS * ATTN_HEAD_DIM
Q_LORA_RANK = 1024
KV_LORA_RANK = 512
IDX_HEADS = 32
IDX_HEAD_DIM = 64
TOPK_MAX = 256
MLSTM_HEADS = 8
MLSTM_QK_DIM = (D_MODEL // 2) // MLSTM_HEADS
MLSTM_V_DIM = D_MODEL // MLSTM_HEADS
MLSTM_QK_WIDTH = MLSTM_HEADS * MLSTM_QK_DIM
MLSTM_WIDTH = MLSTM_HEADS * MLSTM_V_DIM
MLSTM_CHUNK = 256
MLSTM_GROUP = 2
CONV_WIDTH = 4
GATE_SOFTCAP = 15.0
DEEPNORM_ALPHA = 2.0 ** 0.25
NORM_EPS = 1e-6

IN_WIDTHS = (Q_LORA_RANK, KV_LORA_RANK, IDX_HEAD_DIM, IDX_HEADS, ATTN_WIDTH, 2 * MLSTM_QK_WIDTH, MLSTM_WIDTH,
             MLSTM_WIDTH, MLSTM_HEADS, MLSTM_HEADS, MLSTM_WIDTH, D_MODEL, D_MODEL)
IN_NAMES = ("q_lat", "kv_lat", "k_idx", "w_idx", "z_attn", "qk_m", "v_m", "o_m", "i_m", "f_m", "z_m", "g_attn", "g_mlstm")
IN_OFFSETS = {n: sum(IN_WIDTHS[:i]) for i, n in enumerate(IN_NAMES)}
IN_WIDTH_OF = dict(zip(IN_NAMES, IN_WIDTHS))

P_ORDER = ("z_attn", "v_m", "o_m", "z_m", "g_attn", "g_mlstm", "q_lat", "qk_m", "kv_lat", "k_idx", "w_idx", "i_m", "f_m")
P_OFFSETS = {}
_off = 0
for _n in P_ORDER:
    P_OFFSETS[_n] = _off
    _off += IN_WIDTH_OF[_n]
P_USED = _off
PROJ_TN = 1024
P_TOTAL = -(-P_USED // PROJ_TN) * PROJ_TN
SMALL_W = 128
SMALL_OFF = P_OFFSETS["k_idx"]
P_PLAIN = P_OFFSETS["qk_m"]
P_MAIN = P_OFFSETS["kv_lat"]
P_TAIL = P_TOTAL - P_MAIN
assert P_PLAIN % PROJ_TN == 0 and P_MAIN % PROJ_TN == 0 and P_TAIL % PROJ_TN == 0
SM_WIDX = IDX_HEAD_DIM
SM_I = SM_WIDX + IDX_HEADS
SM_F = SM_I + MLSTM_HEADS

VMEM_CAP_BYTES = 60 * 1024 * 1024
VMEM_MB = dict(ada=32, modulate=40, proj=56, proj_main=56, qpath=48, kvpath=48, indexer=40, attn=56,
               mlstm=32, merge=48, final=56, regroup=40)

TQ = 256
TK = 512
ROW_TILE = 512
COL_TILE = 512
PROJ_TM = 1024
IDX_ROWS = 128

LOG2E = 1.4426950408889634
V_ONES = 16
V_ROWS = ATTN_HEAD_DIM + V_ONES
ATTN_ROWS = 256
HEAD_GROUP = 32
INT_MIN = -2 ** 31
KEY_NEG_INF = INT_MIN + 0x7FFFFF
KEY16_NEG_INF = -2 ** 15 + 0x7F


def _cparams(sem, call):
    return pltpu.CompilerParams(dimension_semantics=sem, vmem_limit_bytes=min(VMEM_MB[call] * 1024 * 1024, VMEM_CAP_BYTES))


def _sigmoid(x):
    return jax.nn.sigmoid(x)


def _silu(x):
    return x * jax.nn.sigmoid(x)


def _const_spec(shape):
    nd = len(shape)
    return pl.BlockSpec(shape, lambda *_: (0,) * nd, pipeline_mode=pl.Buffered(1))


def _ada_kernel(c_ref, w_ref, b_ref, o_ref):
    c = c_ref[...]
    o_ref[...] = jnp.sum(w_ref[...] * _silu(c), axis=0, keepdims=True) + b_ref[...]


def _ada(c_col, w_ada, b_ada):
    d, n = w_ada.shape
    tn = COL_TILE
    return pl.pallas_call(
        _ada_kernel,
        out_shape=jax.ShapeDtypeStruct((1, n), F32),
        grid=(n // tn,),
        in_specs=[pl.BlockSpec((d, 1), lambda j: (0, 0)),
                  pl.BlockSpec((d, tn), lambda j: (0, j)),
                  pl.BlockSpec((1, tn), lambda j: (0, j))],
        out_specs=pl.BlockSpec((1, tn), lambda j: (0, j)),
        compiler_params=_cparams(("arbitrary",), "ada"),
        name="ada",
    )(c_col, w_ada, b_ada)


def _modulate_kernel(x_ref, shift_ref, scale_ref, u_ref):
    u_ref[...] = (x_ref[...] * (1.0 + scale_ref[...]) + shift_ref[...]).astype(BF16)


def _modulate(x2, mod):
    s, d = x2.shape
    tm = min(ROW_TILE, s)
    return pl.pallas_call(
        _modulate_kernel,
        out_shape=jax.ShapeDtypeStruct((s, d), BF16),
        grid=(s // tm,),
        in_specs=[pl.BlockSpec((tm, d), lambda i: (i, 0)),
                  pl.BlockSpec((1, d), lambda i: (0, 0)),
                  pl.BlockSpec((1, d), lambda i: (0, 1))],
        out_specs=pl.BlockSpec((tm, d), lambda i: (i, 0)),
        compiler_params=_cparams(("arbitrary",), "modulate"),
        name="modulate",
    )(x2, mod, mod)


def _proj_kernel(u_ref, w_ref, b_ref, o_ref):
    nt = (((1,), (1,)), ((), ()))
    o_ref[...] = lax.dot_general(u_ref[...], w_ref[...], nt, preferred_element_type=F32) + b_ref[...]


def _proj(u, w_catT, b_cat):
    s, d = u.shape
    n = w_catT.shape[0]
    tm = min(PROJ_TM, s)
    tn = PROJ_TN
    return pl.pallas_call(
        _proj_kernel,
        out_shape=jax.ShapeDtypeStruct((s, n), F32),
        grid=(n // tn, s // tm),
        in_specs=[pl.BlockSpec((tm, d), lambda j, i: (i, 0)),
                  pl.BlockSpec((tn, d), lambda j, i: (j, 0)),
                  pl.BlockSpec((1, tn), lambda j, i: (0, j))],
        out_specs=pl.BlockSpec((tm, tn), lambda j, i: (i, j)),
        compiler_params=_cparams(("arbitrary", "arbitrary"), "proj"),
        name="proj",
    )(u, w_catT, b_cat)


PM_CHUNK = 128


def _proj_main_kernel(starts_ref, u_ref, b_ref, *refs, n_m, conv):
    if conv:
        cw_ref, cb_ref, w_hbm, o_ref, wbf_ref, st_ref, sem, halo_ref = refs
    else:
        w_hbm, o_ref, wbf_ref, st_ref, sem = refs
    j = pl.program_id(0)
    i = pl.program_id(1)
    nj = pl.num_programs(0)
    step = j * n_m + i
    cpt = PROJ_TN // PM_CHUNK
    cps = cpt // n_m
    cpw = RG_TN // PM_CHUNK

    def chunk_copy(tile, c, slot):
        win = starts_ref[tile * (PROJ_TN // RG_TN) + c // cpw]
        row0 = pl.multiple_of(win * F32_SUBLANES + (c % cpw) * PM_CHUNK, F32_SUBLANES)
        return pltpu.make_async_copy(w_hbm.at[pl.ds(row0, PM_CHUNK), :], st_ref.at[slot], sem.at[slot])

    def cast_chunk(tile, c, slot):
        rows = pl.ds(pl.multiple_of(c * PM_CHUNK, PM_CHUNK), PM_CHUNK)
        wbf_ref[tile % 2, rows, :] = st_ref[slot].astype(BF16)

    def group(g):
        tile = g // n_m + 1
        return [(tile, (g % n_m) * cps + e, (g % 2) * cps + e) for e in range(cps)]

    @pl.when(step == 0)
    def _first_tile():
        for c in range(cpt):
            cp = chunk_copy(0, c, 0)
            cp.start()
            cp.wait()
            cast_chunk(0, c, 0)

        @pl.when(nj > 1)
        def _():
            for tile, c, slot in group(0):
                chunk_copy(tile, c, slot).start()

    @pl.when((step + 1) // n_m + 1 < nj)
    def _prefetch():
        for tile, c, slot in group(step + 1):
            chunk_copy(tile, c, slot).start()

    @pl.when(j + 1 < nj)
    def _stage_next_tile():
        for tile, c, slot in group(step):
            chunk_copy(tile, c, slot).wait()
            cast_chunk(tile, c, slot)

    nt = (((1,), (1,)), ((), ()))
    x = lax.dot_general(u_ref[...], wbf_ref[j % 2], nt, preferred_element_type=F32) + b_ref[...]
    if not conv:
        o_ref[...] = x
        return
    prev = jnp.where(i > 0, halo_ref[...], 0.0)
    halo_ref[...] = x[-8:]
    head = jnp.concatenate([prev, x[:8]], axis=0)
    y = cb_ref[...]
    yh = cb_ref[...]
    for tap in range(CONV_WIDTH):
        dly = CONV_WIDTH - 1 - tap
        xs = x if dly == 0 else pltpu.roll(x, dly, 0)
        hs = head if dly == 0 else pltpu.roll(head, dly, 0)
        y = y + xs * cw_ref[tap:tap + 1, :]
        yh = yh + hs[8:] * cw_ref[tap:tap + 1, :]
    y = _silu(jnp.concatenate([yh, y[8:]], axis=0))
    kscale = jnp.where(j * PROJ_TN >= MLSTM_QK_WIDTH, MLSTM_QK_DIM ** -0.5, 1.0)
    o_ref[...] = (y * kscale).astype(BF16)


def _proj_main(u, w_inT, b_main, first_col, conv_wb=None):
    s, d = u.shape
    n = b_main.shape[1]
    conv = conv_wb is not None
    tm = min(PROJ_TM, s)
    tn = PROJ_TN
    n_m = s // tm
    cps = (tn // PM_CHUNK) // n_m
    assert cps * n_m * PM_CHUNK == tn, (s, tm)
    grid_spec = pltpu.PrefetchScalarGridSpec(
        num_scalar_prefetch=1,
        grid=(n // tn, n_m),
        in_specs=[pl.BlockSpec((tm, d), lambda j, i, t: (i, 0)),
                  pl.BlockSpec((1, tn), lambda j, i, t: (0, j))]
        + ([pl.BlockSpec((CONV_WIDTH, tn), lambda j, i, t: (0, j)), pl.BlockSpec((1, tn), lambda j, i, t: (0, j))] if conv else [])
        + [pl.BlockSpec(memory_space=pl.ANY)],
        out_specs=pl.BlockSpec((tm, tn), lambda j, i, t: (i, j)),
        scratch_shapes=[pltpu.VMEM((2, tn, d), BF16),
                        pltpu.VMEM((2 * cps, PM_CHUNK, d), F32),
                        pltpu.SemaphoreType.DMA((2 * cps,))]
        + ([pltpu.VMEM((8, tn), F32)] if conv else []),
    )
    return pl.pallas_call(
        functools.partial(_proj_main_kernel, n_m=n_m, conv=conv),
        out_shape=jax.ShapeDtypeStruct((s, n), BF16 if conv else F32),
        grid_spec=grid_spec,
        compiler_params=_cparams(("arbitrary", "arbitrary"), "proj_main"),
        name="proj_qk" if conv else "proj_main",
    )(jnp.asarray(_window_starts(first_col, n), I32), u, b_main, *(conv_wb or ()), w_inT)


def _qpath_kernel(ql_ref, g_ref, wuq_ref, wiq_ref, qT_ref, qiT_ref, *, scale):
    x = ql_ref[...]
    cq = (x * lax.rsqrt(jnp.mean(x * x, axis=-1, keepdims=True) + NORM_EPS) * g_ref[...]).astype(BF16)
    nt = (((1,), (1,)), ((), ()))
    qT = lax.dot_general(wuq_ref[...], cq, nt, preferred_element_type=F32)
    qT_ref[...] = (qT * scale).reshape(qT_ref.shape).astype(BF16)
    qiT = lax.dot_general(wiq_ref[...], cq, nt, preferred_element_type=F32)
    qiT_ref[...] = qiT.reshape(qiT_ref.shape).astype(BF16)


def _qpath(proj, g_q, w_uqT, w_iqT, tq):
    s = proj.shape[0]
    r = Q_LORA_RANK
    return pl.pallas_call(
        functools.partial(_qpath_kernel, scale=ATTN_HEAD_DIM ** -0.5 * LOG2E),
        out_shape=(jax.ShapeDtypeStruct((ATTN_HEADS, ATTN_HEAD_DIM, s), BF16),
                   jax.ShapeDtypeStruct((IDX_HEADS, IDX_HEAD_DIM, s), BF16)),
        grid=(s // tq,),
        in_specs=[pl.BlockSpec((tq, r), lambda i: (i, P_OFFSETS["q_lat"] // r)),
                  _const_spec((1, r)),
                  _const_spec(w_uqT.shape),
                  _const_spec(w_iqT.shape)],
        out_specs=(pl.BlockSpec((ATTN_HEADS, ATTN_HEAD_DIM, tq), lambda i: (0, 0, i)),
                   pl.BlockSpec((IDX_HEADS, IDX_HEAD_DIM, tq), lambda i: (0, 0, i))),
        compiler_params=_cparams(("arbitrary",), "qpath"),
        name="qpath",
    )(proj, g_q, w_uqT, w_iqT)


def _kvpath_kernel(kvl_ref, sm_ref, gkv_ref, gk_ref, bk_ref, wuk_ref, wuv_ref, k_ref, vT_ref, kidx_ref, widx_ref, *, wscale):
    x = kvl_ref[...]
    ckv = (x * lax.rsqrt(jnp.mean(x * x, axis=-1, keepdims=True) + NORM_EPS) * gkv_ref[...]).astype(BF16)
    kfull = jnp.dot(ckv, wuk_ref[...], preferred_element_type=F32)
    for h in range(ATTN_HEADS):
        k_ref[h] = kfull[:, h * ATTN_HEAD_DIM:(h + 1) * ATTN_HEAD_DIM].astype(BF16)
    nt = (((1,), (1,)), ((), ()))
    vT = lax.dot_general(wuv_ref[...], ckv, nt, preferred_element_type=F32)
    vT_ref[:, :ATTN_HEAD_DIM, :] = vT.reshape(ATTN_HEADS, ATTN_HEAD_DIM, -1).astype(BF16)
    vT_ref[:, ATTN_HEAD_DIM:, :] = jnp.ones((ATTN_HEADS, V_ONES, vT_ref.shape[2]), BF16)
    sm = sm_ref[...]
    ki = sm[:, :IDX_HEAD_DIM]
    mu = jnp.mean(ki, axis=-1, keepdims=True)
    var = jnp.mean(jnp.square(ki - mu), axis=-1, keepdims=True)
    kidx_ref[...] = ((ki - mu) * lax.rsqrt(var + NORM_EPS) * gk_ref[...] + bk_ref[...]).astype(BF16)
    widx_ref[...] = sm[:, SM_WIDX:SM_WIDX + IDX_HEADS] * wscale


def _kvpath(proj_tail, g_kv, g_kidx, b_kidx, w_ukT, w_uvT, tm):
    s = proj_tail.shape[0]
    r = KV_LORA_RANK
    return pl.pallas_call(
        functools.partial(_kvpath_kernel, wscale=IDX_HEADS ** -0.5 * IDX_HEAD_DIM ** -0.5),
        out_shape=(jax.ShapeDtypeStruct((ATTN_HEADS, s, ATTN_HEAD_DIM), BF16),
                   jax.ShapeDtypeStruct((ATTN_HEADS, V_ROWS, s), BF16),
                   jax.ShapeDtypeStruct((s, IDX_HEAD_DIM), BF16),
                   jax.ShapeDtypeStruct((s, IDX_HEADS), F32)),
        grid=(s // tm,),
        in_specs=[pl.BlockSpec((tm, r), lambda i: (i, (P_OFFSETS["kv_lat"] - P_MAIN) // r)),
                  pl.BlockSpec((tm, SMALL_W), lambda i: (i, (SMALL_OFF - P_MAIN) // SMALL_W)),
                  _const_spec((1, r)),
                  _const_spec((1, IDX_HEAD_DIM)),
                  _const_spec((1, IDX_HEAD_DIM)),
                  _const_spec(w_ukT.shape),
                  _const_spec(w_uvT.shape)],
        out_specs=(pl.BlockSpec((ATTN_HEADS, tm, ATTN_HEAD_DIM), lambda i: (0, i, 0)),
                   pl.BlockSpec((ATTN_HEADS, V_ROWS, tm), lambda i: (0, 0, i)),
                   pl.BlockSpec((tm, IDX_HEAD_DIM), lambda i: (i, 0)),
                   pl.BlockSpec((tm, IDX_HEADS), lambda i: (i, 0))),
        compiler_params=_cparams(("arbitrary",), "kvpath"),
        name="kvpath",
    )(proj_tail, proj_tail, g_kv, g_kidx, b_kidx, w_ukT, w_uvT)


def _key_to_float(key):
    bits = jnp.where(key >= 0, key, key ^ 0x7FFFFFFF)
    return jnp.where(key < KEY_NEG_INF, -jnp.inf, pltpu.bitcast(bits, F32))


def _key16_to_float(key):
    bits = jnp.where(key >= 0, key, key ^ 0x7FFF)
    return jnp.where(key < KEY16_NEG_INF, -jnp.inf, pltpu.bitcast(jnp.left_shift(bits, 16), F32))


def _indexer_kernel(kidx_ref, qiT_ref, wT_ref, sc_ref, thr_ref, s16_ref, *, seq, tq, nsel):
    i = pl.program_id(0)
    ch = IDX_ROWS
    cb = tq
    n_score = (i + 1) * (tq // ch)
    n_count = i + 1
    tpos = i * tq + lax.broadcasted_iota(I32, (ch, tq), 1)

    def score_chunk(c, carry):
        r0 = pl.multiple_of(c * ch, ch)
        kc = kidx_ref[pl.ds(r0, ch), :]

        acc = jnp.zeros((ch, tq), F32)
        for h in range(IDX_HEADS):
            r = jnp.dot(kc, qiT_ref[h], preferred_element_type=F32)
            acc = acc + jnp.maximum(r, 0.0) * wT_ref[h]
        spos = r0 + lax.broadcasted_iota(I32, (ch, tq), 0)
        val = jnp.where(spos <= tpos, acc, -jnp.inf)
        sc_ref[pl.ds(r0, ch), :] = val
        s16_ref[pl.ds(r0, ch), :] = val.astype(BF16)
        return carry
    lax.fori_loop(0, n_score, score_chunk, 0)

    def fill_chunk(c, carry):
        sc_ref[pl.ds(pl.multiple_of(c * cb, cb), cb), :] = jnp.full((cb, tq), -jnp.inf, F32)
        return carry
    lax.fori_loop(n_count, seq // cb, fill_chunk, 0)

    def count(pred):
        def body(c, part):
            r0 = pl.multiple_of(c * cb, cb)
            m = jnp.where(pred(sc_ref[pl.ds(r0, cb), :], r0), 1, 0)
            return part + jnp.sum(m.reshape(cb // 8, 8, tq), axis=0)
        part = lax.fori_loop(0, n_count, body, jnp.zeros((8, tq), I32))
        return jnp.sum(part, axis=0, keepdims=True)

    def count_ge(cand_key):
        cand = _key_to_float(cand_key)
        return count(lambda blk, r0: blk >= cand)

    def count16_ge(cand_key16):
        cand = _key16_to_float(cand_key16).astype(BF16)

        def body(c, acc):
            blk = s16_ref[pl.ds(pl.multiple_of(c * cb, cb), cb), :]
            ones = jnp.where(blk >= cand, jnp.ones((), BF16), jnp.zeros((), BF16))
            part = ones[:16]
            for g in range(1, cb // 16):
                part = part + ones[g * 16:(g + 1) * 16]
            return acc + part.astype(F32)
        acc = lax.fori_loop(0, n_count, body, jnp.zeros((16, tq), F32))
        return jnp.sum(acc, axis=0, keepdims=True).astype(I32)

    r0_key = jnp.where(count16_ge(jnp.zeros((1, tq), I32)) >= nsel, 0, -2 ** 15).astype(I32)

    def bit16_step(b, t):
        cand = t + jnp.left_shift(jnp.int32(1), 14 - b)
        return jnp.where(count16_ge(cand) >= nsel, cand, t)
    rbits = pltpu.bitcast(_key16_to_float(lax.fori_loop(0, 15, bit16_step, r0_key)), I32)
    r_key = jnp.where(rbits >= 0, rbits, rbits ^ 0x7FFFFFFF)

    def bit_cond(st):
        b, _, _, n_open = st
        return (b < 17) & (n_open > 0)

    def bit_step(st):
        b, t, settled, _ = st
        cand = t + jnp.left_shift(jnp.int32(1), 16 - b)
        cnt = count_ge(cand)
        t = jnp.where((cnt >= nsel) & (settled == 0), cand, t)
        settled = jnp.where(cnt == nsel, 1, settled)
        return b + 1, t, settled, jnp.sum(1 - settled)
    _, t_key, settled, n_open = lax.while_loop(
        bit_cond, bit_step, (jnp.int32(0), r_key - 2 ** 16, jnp.zeros((1, tq), I32), jnp.int32(tq)))
    thr = _key_to_float(t_key)
    thr_ref[...] = thr

    @pl.when(n_open > 0)
    def _unsettled():
        tie = (settled == 0) & (count(lambda blk, r0: blk >= thr) > nsel) & (thr > -jnp.inf)

        @pl.when(jnp.max(tie.astype(I32)) > 0)
        def _break_ties():
            need = nsel - count(lambda blk, r0: blk > thr)

            def eq_below(j):
                return count(lambda blk, r0: (blk == thr) & (r0 + lax.broadcasted_iota(I32, (cb, tq), 0) < j))

            def jbit(b, j):
                test = j + jnp.left_shift(jnp.int32(1), (seq.bit_length() - 2) - b)
                return jnp.where(eq_below(test) < need, test, j)
            jlast = lax.fori_loop(0, seq.bit_length() - 1, jbit, jnp.zeros((1, tq), I32))

            def demote(c, carry):
                r0 = pl.multiple_of(c * cb, cb)
                blk = sc_ref[pl.ds(r0, cb), :]
                row = r0 + lax.broadcasted_iota(I32, (cb, tq), 0)
                sc_ref[pl.ds(r0, cb), :] = jnp.where(tie & (blk == thr) & (row > jlast), -jnp.inf, blk)
                return carry
            lax.fori_loop(0, n_count, demote, 0)


def _indexer(kidx, qiT, wT, tq, nsel):
    s = kidx.shape[0]
    return pl.pallas_call(
        functools.partial(_indexer_kernel, seq=s, tq=tq, nsel=nsel),
        out_shape=(jax.ShapeDtypeStruct((s, s), F32), jax.ShapeDtypeStruct((1, s), F32)),
        grid=(s // tq,),
        in_specs=[_const_spec((s, IDX_HEAD_DIM)),
                  pl.BlockSpec((IDX_HEADS, IDX_HEAD_DIM, tq), lambda i: (0, 0, i)),
                  pl.BlockSpec((IDX_HEADS, 1, tq), lambda i: (0, 0, i))],
        out_specs=(pl.BlockSpec((s, tq), lambda i: (0, i)),
                   pl.BlockSpec((1, tq), lambda i: (0, i))),
        scratch_shapes=[pltpu.VMEM((s, tq), BF16)],
        compiler_params=_cparams(("arbitrary",), "indexer"),
        name="indexer",
    )(kidx, qiT, wT)


def _attn_kernel(tiles_ref, qT_ref, k_ref, vT_ref, keys_ref, thr_ref, z_ref, sl_ref, kf_ref, qf_ref, o_ref,
                 acc_ref, m_ref, mb_ref, lg_ref, p_ref, *, tq, tk):
    qi = tiles_ref[0, pl.program_id(0)]
    kj = tiles_ref[1, pl.program_id(0)]

    @pl.when(kj == 0)
    def _init():
        acc_ref[...] = jnp.zeros(acc_ref.shape, F32)
        m_ref[...] = jnp.full(m_ref.shape, -jnp.inf, F32)

    def _compute():
        spos = kj * tk + lax.broadcasted_iota(I32, (tk, tq), 0)
        tpos = qi * tq + lax.broadcasted_iota(I32, (tk, tq), 1)
        sel = (keys_ref[...] >= thr_ref[...]) & (spos <= tpos)
        mb_ref[...] = jnp.where(sel, 0.0, -jnp.inf)
        tile_off = (kj * tk - qi * tq).astype(F32)

        def group(g, carry):
            def logits(u):
                h = g * HEAD_GROUP + u
                qh = jnp.concatenate([qT_ref[h], qf_ref[h]], axis=0)
                part = jnp.full((8, tq), -jnp.inf, F32)
                for c in range(tk // ATTN_ROWS):
                    rows = pl.ds(c * ATTN_ROWS, ATTN_ROWS)
                    kh = jnp.concatenate([k_ref[h, rows, :], kf_ref[rows, :]], axis=1)
                    lg = jnp.dot(kh, qh, preferred_element_type=F32) + mb_ref[rows, :]
                    lg_ref[u % 2, rows, :] = lg
                    part = jnp.maximum(part, jnp.max(lg.reshape(ATTN_ROWS // 8, 8, tq), axis=0))
                shift = sl_ref[h] * tile_off
                m_old = m_ref[g, u]
                return m_old, jnp.maximum(m_old, jnp.max(part, axis=0, keepdims=True) + shift), shift

            def probs(u, m_old, m_new, shift):
                m_safe = jnp.where(m_new == -jnp.inf, 0.0, m_new)
                m_tile = m_safe - shift
                for c in range(tk // ATTN_ROWS):
                    rows = pl.ds(c * ATTN_ROWS, ATTN_ROWS)
                    p_ref[u % 2, rows, :] = jnp.exp2(lg_ref[u % 2, rows, :] - m_tile).astype(BF16)
                m_ref[g, u] = m_new
                return jnp.exp2(m_old - m_safe)

            def values(u, alpha):
                h = g * HEAD_GROUP + u
                acc_ref[g, u] = alpha * acc_ref[g, u] + jnp.dot(vT_ref[h], p_ref[u % 2], preferred_element_type=F32)

            stats = logits(0)
            alpha_prev = None
            for u in range(HEAD_GROUP):
                stats_next = logits(u + 1) if u + 1 < HEAD_GROUP else None
                alpha = probs(u, *stats)
                if u >= 1:
                    values(u - 1, alpha_prev)
                stats, alpha_prev = stats_next, alpha
            values(HEAD_GROUP - 1, alpha_prev)
            return carry
        lax.fori_loop(0, ATTN_HEADS // HEAD_GROUP, group, 0)
    _compute()

    @pl.when(kj == (qi * tq + tq - 1) // tk)
    def _finish():
        for h in range(ATTN_HEADS):
            g, u = divmod(h, HEAD_GROUP)
            cols = slice(h * ATTN_HEAD_DIM, (h + 1) * ATTN_HEAD_DIM)
            acc = acc_ref[g, u]
            o = (acc[:ATTN_HEAD_DIM] * (1.0 / acc[ATTN_HEAD_DIM:ATTN_HEAD_DIM + 1])).T
            o_ref[:, cols] = (o * _silu(z_ref[:, cols])).astype(BF16)


def _alibi_features(tq, tk):
    sigma = jnp.exp2(-8.0 * jnp.arange(1, ATTN_HEADS + 1, dtype=F32) / ATTN_HEADS) * LOG2E
    s1 = sigma.astype(BF16)
    s2 = (sigma - s1.astype(F32)).astype(BF16)
    s3 = (sigma - s1.astype(F32) - s2.astype(F32)).astype(BF16)
    pieces = jnp.stack([s1, s2, s3, s1, s2, s3], axis=1)
    qf = jnp.zeros((ATTN_HEADS, ATTN_HEAD_DIM, tq), BF16)
    qf = qf.at[:, :6, :].set(jnp.broadcast_to(pieces[:, :, None], (ATTN_HEADS, 6, tq)))
    r = jnp.arange(tk, dtype=I32)
    r_hi = ((r // 256) * 256).astype(BF16)
    r_lo = (r % 256).astype(BF16)
    kf = jnp.zeros((tk, ATTN_HEAD_DIM), BF16).at[:, :6].set(jnp.stack([r_hi, r_hi, r_hi, r_lo, r_lo, r_lo], axis=1))
    return jnp.broadcast_to(sigma[:, None, None], (ATTN_HEADS, 1, tq)), kf, qf


def _attention(qT, k, vT, keys, thr, proj, tq, tk):
    s = k.shape[1]
    ng = ATTN_HEADS // HEAD_GROUP
    tiles = [(qi, kj) for qi in range(s // tq) for kj in range((qi * tq + tq - 1) // tk + 1)]
    const3 = lambda shape: pl.BlockSpec(shape, lambda i, t: (0, 0, 0), pipeline_mode=pl.Buffered(1))
    grid_spec = pltpu.PrefetchScalarGridSpec(
        num_scalar_prefetch=1,
        grid=(len(tiles),),
        in_specs=[pl.BlockSpec((ATTN_HEADS, ATTN_HEAD_DIM, tq), lambda i, t: (0, 0, t[0, i])),
                  pl.BlockSpec((ATTN_HEADS, tk, ATTN_HEAD_DIM), lambda i, t: (0, t[1, i], 0)),
                  pl.BlockSpec((ATTN_HEADS, V_ROWS, tk), lambda i, t: (0, 0, t[1, i])),
                  pl.BlockSpec((tk, tq), lambda i, t: (t[1, i], t[0, i])),
                  pl.BlockSpec((1, tq), lambda i, t: (0, t[0, i])),
                  pl.BlockSpec((tq, ATTN_WIDTH), lambda i, t: (t[0, i], P_OFFSETS["z_attn"] // ATTN_WIDTH)),
                  const3((ATTN_HEADS, 1, tq)),
                  pl.BlockSpec((tk, ATTN_HEAD_DIM), lambda i, t: (0, 0), pipeline_mode=pl.Buffered(1)),
                  const3((ATTN_HEADS, ATTN_HEAD_DIM, tq))],
        out_specs=pl.BlockSpec((tq, ATTN_WIDTH), lambda i, t: (t[0, i], 0)),
        scratch_shapes=[pltpu.VMEM((ng, HEAD_GROUP, V_ROWS, tq), F32),
                        pltpu.VMEM((ng, HEAD_GROUP, 1, tq), F32),
                        pltpu.VMEM((tk, tq), F32),
                        pltpu.VMEM((2, tk, tq), F32),
                        pltpu.VMEM((2, tk, tq), BF16)],
    )
    return pl.pallas_call(
        functools.partial(_attn_kernel, tq=tq, tk=tk),
        out_shape=jax.ShapeDtypeStruct((s, ATTN_WIDTH), BF16),
        grid_spec=grid_spec,
        compiler_params=_cparams(("arbitrary",), "attn"),
        name="attn",
    )(jnp.asarray(tiles, I32).T, qT, k, vT, keys, thr, proj, *_alibi_features(tq, tk))


def _softcap(x):
    return GATE_SOFTCAP * jnp.tanh(x / GATE_SOFTCAP)


def _mlstm_kernel(q_ref, k_ref, v_ref, og_ref, z_ref, gt_ref, g_ref, out_ref, c_ref, n_ref, m_ref, *, chunk):
    ci = pl.program_id(1)
    L = chunk
    dk, dv = MLSTM_QK_DIM, MLSTM_V_DIM

    @pl.when(ci == 0)
    def _init():
        c_ref[...] = jnp.zeros(c_ref.shape, F32)
        n_ref[...] = jnp.zeros(n_ref.shape, F32)
        m_ref[...] = jnp.zeros(m_ref.shape, F32)

    gt = gt_ref[...]
    sub = lax.broadcasted_iota(I32, gt.shape, 0)
    r_i = lax.broadcasted_iota(I32, (L, L), 0)
    c_i = lax.broadcasted_iota(I32, (L, L), 1)
    eye = r_i == c_i
    tril = r_i >= c_i
    nt = (((1,), (1,)), ((), ()))
    tn = (((0,), (0,)), ((), ()))

    for j in range(MLSTM_GROUP):
        hd = pl.program_id(0) * MLSTM_GROUP + j
        ig_row = _softcap(jnp.sum(jnp.where(sub == hd, gt, 0.0), axis=0, keepdims=True))
        fg_row = _softcap(jnp.sum(jnp.where(sub == MLSTM_HEADS + hd, gt, 0.0), axis=0, keepdims=True))
        logf_row = jnp.minimum(fg_row, 0.0) - jnp.log1p(jnp.exp(-jnp.abs(fg_row)))
        ig_col = jnp.sum(jnp.where(eye, ig_row, 0.0), axis=1, keepdims=True)
        b_col = jnp.sum(jnp.where(tril, logf_row, 0.0), axis=1, keepdims=True)
        b_row = jnp.sum(jnp.where(eye, b_col, 0.0), axis=0, keepdims=True)
        dmat = jnp.where(tril, b_col - b_row + ig_row, -jnp.inf)
        m_prev = m_ref[j]
        m_inter = b_col + m_prev
        m_t = jnp.maximum(m_inter, jnp.max(dmat, axis=1, keepdims=True))

        qc = q_ref[:, j * dk:(j + 1) * dk]
        kc = k_ref[:, j * dk:(j + 1) * dk]
        vc = v_ref[:, j * dv:(j + 1) * dv].astype(BF16)
        s = lax.dot_general(qc, kc, nt, preferred_element_type=F32) * jnp.exp(dmat - m_t)
        inter = jnp.exp(m_inter - m_t)
        num = (jnp.dot(s.astype(BF16), vc, preferred_element_type=F32)
               + inter * jnp.dot(qc, c_ref[j].astype(BF16), preferred_element_type=F32))
        qn = jnp.sum(qc.astype(F32) * n_ref[j], axis=1, keepdims=True)
        den = jnp.sum(s, axis=1, keepdims=True) + inter * qn
        hh = num / jnp.maximum(jnp.abs(den), jnp.exp(-m_t))

        g_last = b_col[L - 1:L, :]
        m_new = m_t[L - 1:L, :]
        wgt = jnp.exp(g_last - b_col + ig_col - m_new)
        decay = jnp.exp(g_last + m_prev - m_new)
        wk = wgt * kc.astype(F32)
        c_ref[j] = decay * c_ref[j] + lax.dot_general(wk.astype(BF16), vc, tn, preferred_element_type=F32)
        n_ref[j] = decay * n_ref[j] + jnp.sum(wk, axis=0, keepdims=True)
        m_ref[j] = m_new

        hn = hh * lax.rsqrt(jnp.mean(hh * hh, axis=-1, keepdims=True) + NORM_EPS) * g_ref[j]
        cols = slice(j * dv, (j + 1) * dv)
        out_ref[:, cols] = (hn * _sigmoid(og_ref[:, cols]) * _silu(z_ref[:, cols])).astype(BF16)


def _mlstm(qk, proj, gates_t, g_mh3, chunk):
    s = qk.shape[0]
    gdk, gdv = MLSTM_GROUP * MLSTM_QK_DIM, MLSTM_GROUP * MLSTM_V_DIM
    vb, ob, zb = (P_OFFSETS[n] // gdv for n in ("v_m", "o_m", "z_m"))
    return pl.pallas_call(
        functools.partial(_mlstm_kernel, chunk=chunk),
        out_shape=jax.ShapeDtypeStruct((s, MLSTM_WIDTH), BF16),
        grid=(MLSTM_HEADS // MLSTM_GROUP, s // chunk),
        in_specs=[pl.BlockSpec((chunk, gdk), lambda h, c: (c, h)),
                  pl.BlockSpec((chunk, gdk), lambda h, c: (c, MLSTM_QK_WIDTH // gdk + h)),
                  pl.BlockSpec((chunk, gdv), lambda h, c: (c, vb + h)),
                  pl.BlockSpec((chunk, gdv), lambda h, c: (c, ob + h)),
                  pl.BlockSpec((chunk, gdv), lambda h, c: (c, zb + h)),
                  pl.BlockSpec((2 * MLSTM_HEADS, chunk), lambda h, c: (0, c)),
                  pl.BlockSpec((MLSTM_GROUP, 1, MLSTM_V_DIM), lambda h, c: (h, 0, 0))],
        out_specs=pl.BlockSpec((chunk, gdv), lambda h, c: (c, h)),
        scratch_shapes=[pltpu.VMEM((MLSTM_GROUP, MLSTM_QK_DIM, MLSTM_V_DIM), F32),
                        pltpu.VMEM((MLSTM_GROUP, 1, MLSTM_QK_DIM), F32),
                        pltpu.VMEM((MLSTM_GROUP, 1, 1), F32)],
        compiler_params=_cparams(("arbitrary", "arbitrary"), "mlstm"),
        name="mlstm",
    )(qk, qk, proj, proj, proj, gates_t, g_mh3)


def _merge_kernel(a1_ref, a2_ref, w1_ref, w2_ref, ga_ref, gm_ref, o_ref):
    y1 = jnp.dot(a1_ref[...], w1_ref[...], preferred_element_type=F32)
    y2 = jnp.dot(a2_ref[...], w2_ref[...], preferred_element_type=F32)
    o_ref[...] = (_sigmoid(ga_ref[...]) * y1 + _sigmoid(gm_ref[...]) * y2).astype(BF16)


def _merge(a1, a2, w1, w2, proj):
    s, d = a1.shape
    tm = min(ROW_TILE, s)
    tn = COL_TILE
    gab, gmb = P_OFFSETS["g_attn"] // tn, P_OFFSETS["g_mlstm"] // tn
    return pl.pallas_call(
        _merge_kernel,
        out_shape=jax.ShapeDtypeStruct((s, D_MODEL), BF16),
        grid=(s // tm, D_MODEL // tn),
        in_specs=[pl.BlockSpec((tm, d), lambda i, j: (i, 0)),
                  pl.BlockSpec((tm, d), lambda i, j: (i, 0)),
                  pl.BlockSpec((d, tn), lambda i, j: (0, j)),
                  pl.BlockSpec((d, tn), lambda i, j: (0, j)),
                  pl.BlockSpec((tm, tn), lambda i, j: (i, gab + j)),
                  pl.BlockSpec((tm, tn), lambda i, j: (i, gmb + j))],
        out_specs=pl.BlockSpec((tm, tn), lambda i, j: (i, j)),
        compiler_params=_cparams(("arbitrary", "arbitrary"), "merge"),
        name="merge",
    )(a1, a2, w1, w2, proj, proj)


def _final_kernel(mg_ref, w_ref, x_ref, gate_ref, lg_ref, lb_ref, o_ref, *, tn, nn):
    j = pl.program_id(1)
    y = jnp.dot(mg_ref[...], w_ref[...], preferred_element_type=F32)
    for jj in range(nn):
        @pl.when(j == jj)
        def _store(jj=jj):
            o_ref[:, jj * tn:(jj + 1) * tn] = y

    @pl.when(j == nn - 1)
    def _norm():
        d = nn * tn
        ssum = 0.0
        for jj in range(nn):
            cols = slice(jj * tn, (jj + 1) * tn)
            r = DEEPNORM_ALPHA * x_ref[:, cols] + gate_ref[:, cols] * o_ref[:, cols]
            o_ref[:, cols] = r
            ssum = ssum + jnp.sum(r, axis=-1, keepdims=True)
        mu = ssum / d
        vsum = 0.0
        for jj in range(nn):
            cols = slice(jj * tn, (jj + 1) * tn)
            vsum = vsum + jnp.sum(jnp.square(o_ref[:, cols] - mu), axis=-1, keepdims=True)
        inv = lax.rsqrt(vsum / d + NORM_EPS)
        for jj in range(nn):
            cols = slice(jj * tn, (jj + 1) * tn)
            o_ref[:, cols] = (o_ref[:, cols] - mu) * inv * lg_ref[:, cols] + lb_ref[:, cols]


def _final(merged, w_out, x2, mod, ln_g, ln_b):
    s, d = x2.shape
    tm = min(ROW_TILE, s)
    tn = COL_TILE
    nn = d // tn
    return pl.pallas_call(
        functools.partial(_final_kernel, tn=tn, nn=nn),
        out_shape=jax.ShapeDtypeStruct((s, d), F32),
        grid=(s // tm, nn),
        in_specs=[pl.BlockSpec((tm, d), lambda i, j: (i, 0)),
                  pl.BlockSpec((d, tn), lambda i, j: (0, j)),
                  pl.BlockSpec((tm, d), lambda i, j: (i, 0)),
                  pl.BlockSpec((1, d), lambda i, j: (0, 2)),
                  pl.BlockSpec((1, d), lambda i, j: (0, 0)),
                  pl.BlockSpec((1, d), lambda i, j: (0, 0))],
        out_specs=pl.BlockSpec((tm, d), lambda i, j: (i, 0), pipeline_mode=pl.Buffered(1)),
        compiler_params=_cparams(("arbitrary", "arbitrary"), "final"),
        name="final",
    )(merged, w_out, x2, mod, ln_g, ln_b)


RG_TN = 512
F32_SUBLANES = 8
NARROW_A = ("k_idx", "w_idx")
NARROW_B = ("i_m", "f_m")


def _regroup_kernel(tbl_ref, main_ref, na_ref, nb_ref, o_ref, *, n_a, n_b):
    @pl.when(tbl_ref[pl.program_id(0)] >= 0)
    def _wide():
        o_ref[...] = main_ref[...].astype(BF16)

    @pl.when(tbl_ref[pl.program_id(0)] < 0)
    def _narrow():
        o_ref[:n_a, :] = na_ref[...].astype(BF16)
        o_ref[n_a:n_a + n_b, :] = nb_ref[...].astype(BF16)
        o_ref[n_a + n_b:, :] = jnp.zeros((o_ref.shape[0] - n_a - n_b, o_ref.shape[1]), BF16)


def _window_starts(first_col, n_cols):
    starts = []
    for oc in range(first_col, first_col + n_cols, RG_TN):
        if oc >= SMALL_OFF:
            starts.append(-1)
            continue
        seg = next(n for n in P_ORDER if P_OFFSETS[n] <= oc < P_OFFSETS[n] + IN_WIDTH_OF[n])
        start = IN_OFFSETS[seg] + oc - P_OFFSETS[seg]
        assert start % F32_SUBLANES == 0, (seg, start)
        starts.append(start // F32_SUBLANES)
    return starts


def _regroup_w(w_inT, first_col, n_cols):
    d = w_inT.shape[1]
    starts = _window_starts(first_col, n_cols)
    n_a = sum(IN_WIDTH_OF[n] for n in NARROW_A)
    n_b = sum(IN_WIDTH_OF[n] for n in NARROW_B)
    off_a, off_b = IN_OFFSETS[NARROW_A[0]], IN_OFFSETS[NARROW_B[0]]
    grid_spec = pltpu.PrefetchScalarGridSpec(
        num_scalar_prefetch=1,
        grid=(n_cols // RG_TN,),
        in_specs=[pl.BlockSpec((pl.Element(RG_TN), pl.Element(d)), lambda j, tbl: (jnp.maximum(tbl[j], 0) * F32_SUBLANES, 0)),
                  pl.BlockSpec((pl.Element(n_a), pl.Element(d)), lambda j, tbl: (off_a, 0)),
                  pl.BlockSpec((pl.Element(n_b), pl.Element(d)), lambda j, tbl: (off_b, 0))],
        out_specs=pl.BlockSpec((RG_TN, d), lambda j, tbl: (j, 0)),
    )
    return pl.pallas_call(
        functools.partial(_regroup_kernel, n_a=n_a, n_b=n_b),
        out_shape=jax.ShapeDtypeStruct((n_cols, d), BF16),
        grid_spec=grid_spec,
        compiler_params=_cparams(("arbitrary",), "regroup"),
        name="regroup",
    )(jnp.asarray(starts, I32), w_inT, w_inT, w_inT)


def _regroup_cols(a, pad_to):
    parts = [a[..., IN_OFFSETS[n]:IN_OFFSETS[n] + IN_WIDTH_OF[n]] for n in P_ORDER]
    parts.append(jnp.zeros(a.shape[:-1] + (pad_to - P_USED,), a.dtype))
    return jnp.concatenate(parts, axis=-1)


def _layer(x2, c, w_ada, b_ada, w_in, b_in, g_q, g_kv, w_uq, w_iq, w_uk, w_uv, g_kidx, b_kidx, conv_w, conv_b, g_mh,
           w_attn_out, w_mlstm_out, w_out, ln_g, ln_b):
    s, d = x2.shape
    assert d == D_MODEL and s % PROJ_TM == 0, (s, d)
    tq, tk = TQ, TK
    nsel = min(TOPK_MAX, s // 4)

    w_inT = w_in.T
    w_tail = _regroup_w(w_inT, P_MAIN, P_TAIL)
    b_cat = _regroup_cols(b_in, P_TOTAL).reshape(1, P_TOTAL)
    w_uqT = w_uq.T.astype(BF16)
    w_iqT = w_iq.T.astype(BF16)
    w_ukT = w_uk.reshape(ATTN_WIDTH, KV_LORA_RANK).T.astype(BF16)
    w_uvT = w_uv.transpose(0, 2, 1).reshape(ATTN_WIDTH, KV_LORA_RANK).astype(BF16)

    mod = _ada(c.reshape(d, 1), w_ada, b_ada.reshape(1, -1))
    u = _modulate(x2, mod)
    proj = _proj_main(u, w_inT, b_cat[:, :P_PLAIN], 0)
    qk = _proj_main(u, w_inT, b_cat[:, P_PLAIN:P_MAIN], P_PLAIN, (conv_w, conv_b.reshape(1, -1)))
    proj_tail = _proj(u, w_tail, b_cat[:, P_MAIN:])

    qT, qiT = _qpath(proj, g_q.reshape(1, -1), w_uqT, w_iqT, tq)
    k, vT, kidx, widx = _kvpath(proj_tail, g_kv.reshape(1, -1), g_kidx.reshape(1, -1), b_kidx.reshape(1, -1), w_ukT, w_uvT, tq)
    wT = widx.T.reshape(IDX_HEADS, 1, s)
    keys, thr = _indexer(kidx, qiT, wT, tq, nsel)
    a_attn = _attention(qT, k, vT, keys, thr, proj, tq, tk)

    gates_t = proj_tail[:, SMALL_OFF - P_MAIN + SM_I:SMALL_OFF - P_MAIN + SM_F + MLSTM_HEADS].T
    a_mlstm = _mlstm(qk, proj, gates_t, g_mh.reshape(MLSTM_HEADS, 1, MLSTM_V_DIM), MLSTM_CHUNK)

    merged = _merge(a_attn, a_mlstm, w_attn_out.astype(BF16), w_mlstm_out.astype(BF16), proj)
    return _final(merged, w_out.astype(BF16), x2, mod, ln_g.reshape(1, -1), ln_b.reshape(1, -1))


def kernel(x, c, w_ada, b_ada, w_in, b_in, g_q, g_kv, w_uq, w_iq, w_uk, w_uv, g_kidx, b_kidx, conv_w, conv_b, g_mh,
           w_attn_out, w_mlstm_out, w_out, ln_g, ln_b):
    bsz, seq, d = x.shape
    assert bsz == 1 and w_ada.shape[0] == 1, "single batch, single layer"
    out = _layer(x.reshape(seq, d), c, w_ada[0], b_ada[0], w_in[0], b_in[0], g_q[0], g_kv[0], w_uq[0], w_iq[0],
                 w_uk[0], w_uv[0], g_kidx[0], b_kidx[0], conv_w[0], conv_b[0], g_mh[0], w_attn_out[0],
                 w_mlstm_out[0], w_out[0], ln_g[0], ln_b[0])
    return out.reshape(bsz, seq, d)
```

```python
import functools

import jax
import jax.numpy as jnp
from jax import lax
from jax.experimental import pallas as pl
from jax.experimental.pallas import tpu as pltpu

F32 = jnp.float32
BF16 = jnp.bfloat16
I32 = jnp.int32

D_MODEL = 4096
ATTN_HEADS = 32
ATTN_HEAD_DIM = 128
ATTN_WIDTH = ATTN_HEADS * ATTN_HEAD_DIM
Q_LORA_RANK = 1024
KV_LORA_RANK = 512
IDX_HEADS = 32
IDX_HEAD_DIM = 64
TOPK_MAX = 256
MLSTM_HEADS = 8
MLSTM_QK_DIM = (D_MODEL // 2) // MLSTM_HEADS
MLSTM_V_DIM = D_MODEL // MLSTM_HEADS
MLSTM_QK_WIDTH = MLSTM_HEADS * MLSTM_QK_DIM
MLSTM_WIDTH = MLSTM_HEADS * MLSTM_V_DIM
MLSTM_CHUNK = 256
MLSTM_GROUP = 2
CONV_WIDTH = 4
GATE_SOFTCAP = 15.0
DEEPNORM_ALPHA = 2.0 ** 0.25
NORM_EPS = 1e-6

IN_WIDTHS = (Q_LORA_RANK, KV_LORA_RANK, IDX_HEAD_DIM, IDX_HEADS, ATTN_WIDTH, 2 * MLSTM_QK_WIDTH, MLSTM_WIDTH,
             MLSTM_WIDTH, MLSTM_HEADS, MLSTM_HEADS, MLSTM_WIDTH, D_MODEL, D_MODEL)
IN_NAMES = ("q_lat", "kv_lat", "k_idx", "w_idx", "z_attn", "qk_m", "v_m", "o_m", "i_m", "f_m", "z_m", "g_attn", "g_mlstm")
IN_OFFSETS = {n: sum(IN_WIDTHS[:i]) for i, n in enumerate(IN_NAMES)}
IN_WIDTH_OF = dict(zip(IN_NAMES, IN_WIDTHS))

P_ORDER = ("z_attn", "v_m", "o_m", "z_m", "g_attn", "g_mlstm", "q_lat", "qk_m", "kv_lat", "k_idx", "w_idx", "i_m", "f_m")
P_OFFSETS = {}
_off = 0
for _n in P_ORDER:
    P_OFFSETS[_n] = _off
    _off += IN_WIDTH_OF[_n]
P_USED = _off
PROJ_TN = 1024
P_TOTAL = -(-P_USED // PROJ_TN) * PROJ_TN
SMALL_W = 128
SMALL_OFF = P_OFFSETS["k_idx"]
P_PLAIN = P_OFFSETS["qk_m"]
P_MAIN = P_OFFSETS["kv_lat"]
P_TAIL = P_TOTAL - P_MAIN
assert P_PLAIN % PROJ_TN == 0 and P_MAIN % PROJ_TN == 0 and P_TAIL % PROJ_TN == 0
SM_WIDX = IDX_HEAD_DIM
SM_I = SM_WIDX + IDX_HEADS
SM_F = SM_I + MLSTM_HEADS

VMEM_CAP_BYTES = 60 * 1024 * 1024
VMEM_MB = dict(ada=32, modulate=40, proj=56, proj_main=56, qpath=48, kvpath=48, indexer=56, attn=56,
               mlstm=32, merge=48, final=56, regroup=40)

TQ = 256
TK = 512
ROW_TILE = 512
COL_TILE = 512
PROJ_TM = 1024
IDX_ROWS = 128

LOG2E = 1.4426950408889634
V_ONES = 16
V_ROWS = ATTN_HEAD_DIM + V_ONES
ATTN_ROWS = 256
HEAD_GROUP = 32
INT_MIN = -2 ** 31
KEY_NEG_INF = INT_MIN + 0x7FFFFF
KEY16_NEG_INF = -2 ** 15 + 0x7F


def _cparams(sem, call):
    return pltpu.CompilerParams(dimension_semantics=sem, vmem_limit_bytes=min(VMEM_MB[call] * 1024 * 1024, VMEM_CAP_BYTES))


def _sigmoid(x):
    return jax.nn.sigmoid(x)


def _silu(x):
    return x * jax.nn.sigmoid(x)


def _const_spec(shape):
    nd = len(shape)
    return pl.BlockSpec(shape, lambda *_: (0,) * nd, pipeline_mode=pl.Buffered(1))


def _ada_kernel(c_ref, w_ref, b_ref, o_ref):
    c = c_ref[...]
    o_ref[...] = jnp.sum(w_ref[...] * _silu(c), axis=0, keepdims=True) + b_ref[...]


def _ada(c_col, w_ada, b_ada):
    d, n = w_ada.shape
    tn = COL_TILE
    return pl.pallas_call(
        _ada_kernel,
        out_shape=jax.ShapeDtypeStruct((1, n), F32),
        grid=(n // tn,),
        in_specs=[pl.BlockSpec((d, 1), lambda j: (0, 0)),
                  pl.BlockSpec((d, tn), lambda j: (0, j)),
                  pl.BlockSpec((1, tn), lambda j: (0, j))],
        out_specs=pl.BlockSpec((1, tn), lambda j: (0, j)),
        compiler_params=_cparams(("arbitrary",), "ada"),
        name="ada",
    )(c_col, w_ada, b_ada)


def _modulate_kernel(x_ref, shift_ref, scale_ref, u_ref):
    u_ref[...] = (x_ref[...] * (1.0 + scale_ref[...]) + shift_ref[...]).astype(BF16)


def _modulate(x2, mod):
    s, d = x2.shape
    tm = min(ROW_TILE, s)
    return pl.pallas_call(
        _modulate_kernel,
        out_shape=jax.ShapeDtypeStruct((s, d), BF16),
        grid=(s // tm,),
        in_specs=[pl.BlockSpec((tm, d), lambda i: (i, 0)),
                  pl.BlockSpec((1, d), lambda i: (0, 0)),
                  pl.BlockSpec((1, d), lambda i: (0, 1))],
        out_specs=pl.BlockSpec((tm, d), lambda i: (i, 0)),
        compiler_params=_cparams(("arbitrary",), "modulate"),
        name="modulate",
    )(x2, mod, mod)


def _proj_kernel(u_ref, w_ref, b_ref, o_ref):
    nt = (((1,), (1,)), ((), ()))
    o_ref[...] = lax.dot_general(u_ref[...], w_ref[...], nt, preferred_element_type=F32) + b_ref[...]


def _proj(u, w_catT, b_cat):
    s, d = u.shape
    n = w_catT.shape[0]
    tm = min(PROJ_TM, s)
    tn = PROJ_TN
    return pl.pallas_call(
        _proj_kernel,
        out_shape=jax.ShapeDtypeStruct((s, n), F32),
        grid=(n // tn, s // tm),
        in_specs=[pl.BlockSpec((tm, d), lambda j, i: (i, 0)),
                  pl.BlockSpec((tn, d), lambda j, i: (j, 0)),
                  pl.BlockSpec((1, tn), lambda j, i: (0, j))],
        out_specs=pl.BlockSpec((tm, tn), lambda j, i: (i, j)),
        compiler_params=_cparams(("arbitrary", "arbitrary"), "proj"),
        name="proj",
    )(u, w_catT, b_cat)


PM_CHUNK = 128


def _proj_main_kernel(starts_ref, u_ref, b_ref, *refs, n_m, conv):
    if conv:
        cw_ref, cb_ref, w_hbm, o_ref, wbf_ref, st_ref, sem, halo_ref = refs
    else:
        w_hbm, o_ref, wbf_ref, st_ref, sem = refs
    j = pl.program_id(0)
    i = pl.program_id(1)
    nj = pl.num_programs(0)
    step = j * n_m + i
    cpt = PROJ_TN // PM_CHUNK
    cps = cpt // n_m
    cpw = RG_TN // PM_CHUNK

    def chunk_copy(tile, c, slot):
        win = starts_ref[tile * (PROJ_TN // RG_TN) + c // cpw]
        row0 = pl.multiple_of(win * F32_SUBLANES + (c % cpw) * PM_CHUNK, F32_SUBLANES)
        return pltpu.make_async_copy(w_hbm.at[pl.ds(row0, PM_CHUNK), :], st_ref.at[slot], sem.at[slot])

    def cast_chunk(tile, c, slot):
        rows = pl.ds(pl.multiple_of(c * PM_CHUNK, PM_CHUNK), PM_CHUNK)
        wbf_ref[tile % 2, rows, :] = st_ref[slot].astype(BF16)

    def group(g):
        tile = g // n_m + 1
        return [(tile, (g % n_m) * cps + e, (g % 2) * cps + e) for e in range(cps)]

    @pl.when(step == 0)
    def _first_tile():
        for c in range(cpt):
            cp = chunk_copy(0, c, 0)
            cp.start()
            cp.wait()
            cast_chunk(0, c, 0)

        @pl.when(nj > 1)
        def _():
            for tile, c, slot in group(0):
                chunk_copy(tile, c, slot).start()

    @pl.when((step + 1) // n_m + 1 < nj)
    def _prefetch():
        for tile, c, slot in group(step + 1):
            chunk_copy(tile, c, slot).start()

    @pl.when(j + 1 < nj)
    def _stage_next_tile():
        for tile, c, slot in group(step):
            chunk_copy(tile, c, slot).wait()
            cast_chunk(tile, c, slot)

    nt = (((1,), (1,)), ((), ()))
    x = lax.dot_general(u_ref[...], wbf_ref[j % 2], nt, preferred_element_type=F32) + b_ref[...]
    if not conv:
        o_ref[...] = x
        return
    prev = jnp.where(i > 0, halo_ref[...], 0.0)
    halo_ref[...] = x[-8:]
    head = jnp.concatenate([prev, x[:8]], axis=0)
    y = cb_ref[...]
    yh = cb_ref[...]
    for tap in range(CONV_WIDTH):
        dly = CONV_WIDTH - 1 - tap
        xs = x if dly == 0 else pltpu.roll(x, dly, 0)
        hs = head if dly == 0 else pltpu.roll(head, dly, 0)
        y = y + xs * cw_ref[tap:tap + 1, :]
        yh = yh + hs[8:] * cw_ref[tap:tap + 1, :]
    y = _silu(jnp.concatenate([yh, y[8:]], axis=0))
    kscale = jnp.where(j * PROJ_TN >= MLSTM_QK_WIDTH, MLSTM_QK_DIM ** -0.5, 1.0)
    o_ref[...] = (y * kscale).astype(BF16)


def _proj_main(u, w_inT, b_main, first_col, conv_wb=None):
    s, d = u.shape
    n = b_main.shape[1]
    conv = conv_wb is not None
    tm = min(PROJ_TM, s)
    tn = PROJ_TN
    n_m = s // tm
    cps = (tn // PM_CHUNK) // n_m
    assert cps * n_m * PM_CHUNK == tn, (s, tm)
    grid_spec = pltpu.PrefetchScalarGridSpec(
        num_scalar_prefetch=1,
        grid=(n // tn, n_m),
        in_specs=[pl.BlockSpec((tm, d), lambda j, i, t: (i, 0)),
                  pl.BlockSpec((1, tn), lambda j, i, t: (0, j))]
        + ([pl.BlockSpec((CONV_WIDTH, tn), lambda j, i, t: (0, j)), pl.BlockSpec((1, tn), lambda j, i, t: (0, j))] if conv else [])
        + [pl.BlockSpec(memory_space=pl.ANY)],
        out_specs=pl.BlockSpec((tm, tn), lambda j, i, t: (i, j)),
        scratch_shapes=[pltpu.VMEM((2, tn, d), BF16),
                        pltpu.VMEM((2 * cps, PM_CHUNK, d), F32),
                        pltpu.SemaphoreType.DMA((2 * cps,))]
        + ([pltpu.VMEM((8, tn), F32)] if conv else []),
    )
    return pl.pallas_call(
        functools.partial(_proj_main_kernel, n_m=n_m, conv=conv),
        out_shape=jax.ShapeDtypeStruct((s, n), BF16 if conv else F32),
        grid_spec=grid_spec,
        compiler_params=_cparams(("arbitrary", "arbitrary"), "proj_main"),
        name="proj_qk" if conv else "proj_main",
    )(jnp.asarray(_window_starts(first_col, n), I32), u, b_main, *(conv_wb or ()), w_inT)


def _qpath_kernel(ql_ref, g_ref, wuq_ref, wiq_ref, qT_ref, qiT_ref, *, scale):
    x = ql_ref[...]
    cq = (x * lax.rsqrt(jnp.mean(x * x, axis=-1, keepdims=True) + NORM_EPS) * g_ref[...]).astype(BF16)
    nt = (((1,), (1,)), ((), ()))
    qT = lax.dot_general(wuq_ref[...], cq, nt, preferred_element_type=F32)
    qT_ref[...] = (qT * scale).reshape(qT_ref.shape).astype(BF16)
    qiT = lax.dot_general(wiq_ref[...], cq, nt, preferred_element_type=F32)
    qiT_ref[...] = qiT.reshape(qiT_ref.shape).astype(BF16)


def _qpath(proj, g_q, w_uqT, w_iqT, tq):
    s = proj.shape[0]
    r = Q_LORA_RANK
    return pl.pallas_call(
        functools.partial(_qpath_kernel, scale=ATTN_HEAD_DIM ** -0.5 * LOG2E),
        out_shape=(jax.ShapeDtypeStruct((ATTN_HEADS, ATTN_HEAD_DIM, s), BF16),
                   jax.ShapeDtypeStruct((IDX_HEADS, IDX_HEAD_DIM, s), BF16)),
        grid=(s // tq,),
        in_specs=[pl.BlockSpec((tq, r), lambda i: (i, P_OFFSETS["q_lat"] // r)),
                  _const_spec((1, r)),
                  _const_spec(w_uqT.shape),
                  _const_spec(w_iqT.shape)],
        out_specs=(pl.BlockSpec((ATTN_HEADS, ATTN_HEAD_DIM, tq), lambda i: (0, 0, i)),
                   pl.BlockSpec((IDX_HEADS, IDX_HEAD_DIM, tq), lambda i: (0, 0, i))),
        compiler_params=_cparams(("arbitrary",), "qpath"),
        name="qpath",
    )(proj, g_q, w_uqT, w_iqT)


def _kvpath_kernel(kvl_ref, sm_ref, gkv_ref, gk_ref, bk_ref, wuk_ref, wuv_ref, k_ref, vT_ref, kidx_ref, widx_ref, *, wscale):
    x = kvl_ref[...]
    ckv = (x * lax.rsqrt(jnp.mean(x * x, axis=-1, keepdims=True) + NORM_EPS) * gkv_ref[...]).astype(BF16)
    kfull = jnp.dot(ckv, wuk_ref[...], preferred_element_type=F32)
    for h in range(ATTN_HEADS):
        k_ref[h] = kfull[:, h * ATTN_HEAD_DIM:(h + 1) * ATTN_HEAD_DIM].astype(BF16)
    nt = (((1,), (1,)), ((), ()))
    vT = lax.dot_general(wuv_ref[...], ckv, nt, preferred_element_type=F32)
    vT_ref[:, :ATTN_HEAD_DIM, :] = vT.reshape(ATTN_HEADS, ATTN_HEAD_DIM, -1).astype(BF16)
    vT_ref[:, ATTN_HEAD_DIM:, :] = jnp.ones((ATTN_HEADS, V_ONES, vT_ref.shape[2]), BF16)
    sm = sm_ref[...]
    ki = sm[:, :IDX_HEAD_DIM]
    mu = jnp.mean(ki, axis=-1, keepdims=True)
    var = jnp.mean(jnp.square(ki - mu), axis=-1, keepdims=True)
    kidx_ref[...] = ((ki - mu) * lax.rsqrt(var + NORM_EPS) * gk_ref[...] + bk_ref[...]).astype(BF16)
    widx_ref[...] = sm[:, SM_WIDX:SM_WIDX + IDX_HEADS] * wscale


def _kvpath(proj_tail, g_kv, g_kidx, b_kidx, w_ukT, w_uvT, tm):
    s = proj_tail.shape[0]
    r = KV_LORA_RANK
    return pl.pallas_call(
        functools.partial(_kvpath_kernel, wscale=IDX_HEADS ** -0.5 * IDX_HEAD_DIM ** -0.5),
        out_shape=(jax.ShapeDtypeStruct((ATTN_HEADS, s, ATTN_HEAD_DIM), BF16),
                   jax.ShapeDtypeStruct((ATTN_HEADS, V_ROWS, s), BF16),
                   jax.ShapeDtypeStruct((s, IDX_HEAD_DIM), BF16),
                   jax.ShapeDtypeStruct((s, IDX_HEADS), F32)),
        grid=(s // tm,),
        in_specs=[pl.BlockSpec((tm, r), lambda i: (i, (P_OFFSETS["kv_lat"] - P_MAIN) // r)),
                  pl.BlockSpec((tm, SMALL_W), lambda i: (i, (SMALL_OFF - P_MAIN) // SMALL_W)),
                  _const_spec((1, r)),
                  _const_spec((1, IDX_HEAD_DIM)),
                  _const_spec((1, IDX_HEAD_DIM)),
                  _const_spec(w_ukT.shape),
                  _const_spec(w_uvT.shape)],
        out_specs=(pl.BlockSpec((ATTN_HEADS, tm, ATTN_HEAD_DIM), lambda i: (0, i, 0)),
                   pl.BlockSpec((ATTN_HEADS, V_ROWS, tm), lambda i: (0, 0, i)),
                   pl.BlockSpec((tm, IDX_HEAD_DIM), lambda i: (i, 0)),
                   pl.BlockSpec((tm, IDX_HEADS), lambda i: (i, 0))),
        compiler_params=_cparams(("arbitrary",), "kvpath"),
        name="kvpath",
    )(proj_tail, proj_tail, g_kv, g_kidx, b_kidx, w_ukT, w_uvT)


def _key_to_float(key):
    bits = jnp.where(key >= 0, key, key ^ 0x7FFFFFFF)
    return jnp.where(key < KEY_NEG_INF, -jnp.inf, pltpu.bitcast(bits, F32))


def _key16_to_float(key):
    bits = jnp.where(key >= 0, key, key ^ 0x7FFF)
    return jnp.where(key < KEY16_NEG_INF, -jnp.inf, pltpu.bitcast(jnp.left_shift(bits, 16), F32))


IDX_VISITS = 8
IDX_PASSES16 = 16


def _indexer_kernel(kidx_ref, qiT_ref, wT_ref, sc_ref, thr_ref, jl_ref, f0_ref, f1_ref, h0_ref, h1_ref, *, seq, tq, nsel):
    i = pl.program_id(0)
    n = seq // tq
    ch = IDX_ROWS
    cb = tq
    n_score = jnp.where(i < n, (i + 1) * (tq // ch), 0)
    n_prev = jnp.maximum(i, 1)
    searching = i >= 1

    @pl.when(i == 0)
    def _():
        h1_ref[:cb, :] = jnp.zeros((cb, tq), BF16)

    def run(cur_f, cur_h, prv_f, prv_h):
        tpos = i * tq + lax.broadcasted_iota(I32, (ch, tq), 1)

        def score_chunk(c):
            r0 = pl.multiple_of(c * ch, ch)
            kc = kidx_ref[pl.ds(r0, ch), :]
            acc = jnp.zeros((ch, tq), F32)
            for h in range(IDX_HEADS):
                r = jnp.dot(kc, qiT_ref[h], preferred_element_type=F32)
                acc = acc + jnp.maximum(r, 0.0) * wT_ref[h]
            spos = r0 + lax.broadcasted_iota(I32, (ch, tq), 0)
            val = jnp.where(spos <= tpos, acc, -jnp.inf)
            sc_ref[pl.ds(r0, ch), :] = val
            cur_f[pl.ds(r0, ch), :] = val
            cur_h[pl.ds(r0, ch), :] = val.astype(BF16)

        def visit(v, st):
            acc, t16 = st
            pb, cc = v // n_prev, v % n_prev
            cand_key = jnp.where(pb == 0, 0, t16 + jnp.left_shift(jnp.int32(1), jnp.maximum(15 - pb, 0)))
            cand = _key16_to_float(cand_key).astype(BF16)
            blk = prv_h[pl.ds(pl.multiple_of(cc * cb, cb), cb), :]
            ones = jnp.where(blk >= cand, jnp.ones((), BF16), jnp.zeros((), BF16))
            part = ones[:16]
            for g in range(1, cb // 16):
                part = part + ones[g * 16:(g + 1) * 16]
            acc = acc + part.astype(F32)
            last = cc == n_prev - 1
            take = jnp.sum(acc, axis=0, keepdims=True).astype(I32) >= nsel
            t_new = jnp.where(pb == 0, jnp.where(take, 0, -2 ** 15), jnp.where(take, cand_key, t16))
            t16 = jnp.where(last & (pb < IDX_PASSES16) & searching, t_new, t16)
            return jnp.where(last, 0.0, acc), t16

        def fused(c, st):
            score_chunk(c)
            for e in range(IDX_VISITS):
                st = visit(c * IDX_VISITS + e, st)
            return st
        st = (jnp.zeros((16, tq), F32), jnp.zeros((1, tq), I32))
        st = lax.fori_loop(0, n_score, fused, st)
        st = lax.fori_loop(0, jnp.where(i == n, IDX_PASSES16 * n_prev, 0), visit, st)
        _, t16 = st

        def fill_chunk(c, carry):
            sc_ref[pl.ds(pl.multiple_of(c * cb, cb), cb), :] = jnp.full((cb, tq), -jnp.inf, F32)
            return carry
        lax.fori_loop(jnp.where(i < n, i + 1, seq // cb), seq // cb, fill_chunk, 0)

        @pl.when(searching)
        def _finish_previous_tile():
            def count(pred):
                def body(c, part):
                    r0 = pl.multiple_of(c * cb, cb)
                    m = jnp.where(pred(prv_f[pl.ds(r0, cb), :], r0), 1, 0)
                    return part + jnp.sum(m.reshape(cb // 8, 8, tq), axis=0)
                part = lax.fori_loop(0, n_prev, body, jnp.zeros((8, tq), I32))
                return jnp.sum(part, axis=0, keepdims=True)

            rbits = pltpu.bitcast(_key16_to_float(t16), I32)
            r_key = jnp.where(rbits >= 0, rbits, rbits ^ 0x7FFFFFFF)

            def bit_cond(s_):
                b, _, _, n_open = s_
                return (b < 17) & (n_open > 0)

            def bit_step(s_):
                b, t, settled, _ = s_
                cand_key = t + jnp.left_shift(jnp.int32(1), 16 - b)
                cand = _key_to_float(cand_key)
                cnt = count(lambda blk, r0: blk >= cand)
                t = jnp.where((cnt >= nsel) & (settled == 0), cand_key, t)
                settled = jnp.where(cnt == nsel, 1, settled)
                return b + 1, t, settled, jnp.sum(1 - settled)
            _, t_key, settled, n_open = lax.while_loop(
                bit_cond, bit_step, (jnp.int32(0), r_key - 2 ** 16, jnp.zeros((1, tq), I32), jnp.int32(tq)))
            thr = _key_to_float(t_key)
            thr_ref[...] = thr
            jl_ref[...] = jnp.full((1, tq), seq, I32)

            @pl.when(n_open > 0)
            def _unsettled():
                tie = (settled == 0) & (count(lambda blk, r0: blk >= thr) > nsel) & (thr > -jnp.inf)

                @pl.when(jnp.max(tie.astype(I32)) > 0)
                def _break_ties():
                    need = nsel - count(lambda blk, r0: blk > thr)

                    def eq_below(j):
                        return count(lambda blk, r0: (blk == thr) & (r0 + lax.broadcasted_iota(I32, (cb, tq), 0) < j))

                    def jbit(b, j):
                        test = j + jnp.left_shift(jnp.int32(1), (seq.bit_length() - 2) - b)
                        return jnp.where(eq_below(test) < need, test, j)
                    jlast = lax.fori_loop(0, seq.bit_length() - 1, jbit, jnp.zeros((1, tq), I32))
                    jl_ref[...] = jnp.where(tie, jlast, seq)

    @pl.when(i % 2 == 0)
    def _even():
        run(f0_ref, h0_ref, f1_ref, h1_ref)

    @pl.when(i % 2 == 1)
    def _odd():
        run(f1_ref, h1_ref, f0_ref, h0_ref)


def _indexer(kidx, qiT, wT, tq, nsel):
    s = kidx.shape[0]
    n = s // tq
    cur = lambda i: jnp.minimum(i, n - 1)
    prv = lambda i: jnp.maximum(i - 1, 0)
    return pl.pallas_call(
        functools.partial(_indexer_kernel, seq=s, tq=tq, nsel=nsel),
        out_shape=(jax.ShapeDtypeStruct((s, s), F32), jax.ShapeDtypeStruct((1, s), F32),
                   jax.ShapeDtypeStruct((1, s), I32)),
        grid=(n + 1,),
        in_specs=[_const_spec((s, IDX_HEAD_DIM)),
                  pl.BlockSpec((IDX_HEADS, IDX_HEAD_DIM, tq), lambda i: (0, 0, cur(i))),
                  pl.BlockSpec((IDX_HEADS, 1, tq), lambda i: (0, 0, cur(i)))],
        out_specs=(pl.BlockSpec((s, tq), lambda i: (0, cur(i))),
                   pl.BlockSpec((1, tq), lambda i: (0, prv(i))),
                   pl.BlockSpec((1, tq), lambda i: (0, prv(i)))),
        scratch_shapes=[pltpu.VMEM((s, tq), F32), pltpu.VMEM((s, tq), F32),
                        pltpu.VMEM((s, tq), BF16), pltpu.VMEM((s, tq), BF16)],
        compiler_params=_cparams(("arbitrary",), "indexer"),
        name="indexer",
    )(kidx, qiT, wT)


def _attn_kernel(tiles_ref, qT_ref, k_ref, vT_ref, keys_ref, thr_ref, jl_ref, z_ref, sl_ref, kf_ref, qf_ref, o_ref,
                 acc_ref, m_ref, mb_ref, lg_ref, p_ref, *, tq, tk):
    qi = tiles_ref[0, pl.program_id(0)]
    kj = tiles_ref[1, pl.program_id(0)]

    @pl.when(kj == 0)
    def _init():
        acc_ref[...] = jnp.zeros(acc_ref.shape, F32)
        m_ref[...] = jnp.full(m_ref.shape, -jnp.inf, F32)

    def _compute():
        spos = kj * tk + lax.broadcasted_iota(I32, (tk, tq), 0)
        tpos = qi * tq + lax.broadcasted_iota(I32, (tk, tq), 1)
        sc = keys_ref[...]
        thr = thr_ref[...]
        sel = ((sc > thr) | ((sc == thr) & (spos <= jl_ref[...]))) & (spos <= tpos)
        mb_ref[...] = jnp.where(sel, 0.0, -jnp.inf)
        tile_off = (kj * tk - qi * tq).astype(F32)

        def group(g, carry):
            def logits(u):
                h = g * HEAD_GROUP + u
                qh = jnp.concatenate([qT_ref[h], qf_ref[h]], axis=0)
                part = jnp.full((8, tq), -jnp.inf, F32)
                for c in range(tk // ATTN_ROWS):
                    rows = pl.ds(c * ATTN_ROWS, ATTN_ROWS)
                    kh = jnp.concatenate([k_ref[h, rows, :], kf_ref[rows, :]], axis=1)
                    lg = jnp.dot(kh, qh, preferred_element_type=F32) + mb_ref[rows, :]
                    lg_ref[u % 2, rows, :] = lg
                    part = jnp.maximum(part, jnp.max(lg.reshape(ATTN_ROWS // 8, 8, tq), axis=0))
                shift = sl_ref[h] * tile_off
                m_old = m_ref[g, u]
                return m_old, jnp.maximum(m_old, jnp.max(part, axis=0, keepdims=True) + shift), shift

            def probs(u, m_old, m_new, shift):
                m_safe = jnp.where(m_new == -jnp.inf, 0.0, m_new)
                m_tile = m_safe - shift
                for c in range(tk // ATTN_ROWS):
                    rows = pl.ds(c * ATTN_ROWS, ATTN_ROWS)
                    p_ref[u % 2, rows, :] = jnp.exp2(lg_ref[u % 2, rows, :] - m_tile).astype(BF16)
                m_ref[g, u] = m_new
                return jnp.exp2(m_old - m_safe)

            def values(u, alpha):
                h = g * HEAD_GROUP + u
                acc_ref[g, u] = alpha * acc_ref[g, u] + jnp.dot(vT_ref[h], p_ref[u % 2], preferred_element_type=F32)

            stats = logits(0)
            alpha_prev = None
            for u in range(HEAD_GROUP):
                stats_next = logits(u + 1) if u + 1 < HEAD_GROUP else None
                alpha = probs(u, *stats)
                if u >= 1:
                    values(u - 1, alpha_prev)
                stats, alpha_prev = stats_next, alpha
            values(HEAD_GROUP - 1, alpha_prev)
            return carry
        lax.fori_loop(0, ATTN_HEADS // HEAD_GROUP, group, 0)
    _compute()

    @pl.when(kj == (qi * tq + tq - 1) // tk)
    def _finish():
        for h in range(ATTN_HEADS):
            g, u = divmod(h, HEAD_GROUP)
            cols = slice(h * ATTN_HEAD_DIM, (h + 1) * ATTN_HEAD_DIM)
            acc = acc_ref[g, u]
            o = (acc[:ATTN_HEAD_DIM] * (1.0 / acc[ATTN_HEAD_DIM:ATTN_HEAD_DIM + 1])).T
            o_ref[:, cols] = (o * _silu(z_ref[:, cols])).astype(BF16)


def _alibi_features(tq, tk):
    sigma = jnp.exp2(-8.0 * jnp.arange(1, ATTN_HEADS + 1, dtype=F32) / ATTN_HEADS) * LOG2E
    s1 = sigma.astype(BF16)
    s2 = (sigma - s1.astype(F32)).astype(BF16)
    s3 = (sigma - s1.astype(F32) - s2.astype(F32)).astype(BF16)
    pieces = jnp.stack([s1, s2, s3, s1, s2, s3], axis=1)
    qf = jnp.zeros((ATTN_HEADS, ATTN_HEAD_DIM, tq), BF16)
    qf = qf.at[:, :6, :].set(jnp.broadcast_to(pieces[:, :, None], (ATTN_HEADS, 6, tq)))
    r = jnp.arange(tk, dtype=I32)
    r_hi = ((r // 256) * 256).astype(BF16)
    r_lo = (r % 256).astype(BF16)
    kf = jnp.zeros((tk, ATTN_HEAD_DIM), BF16).at[:, :6].set(jnp.stack([r_hi, r_hi, r_hi, r_lo, r_lo, r_lo], axis=1))
    return jnp.broadcast_to(sigma[:, None, None], (ATTN_HEADS, 1, tq)), kf, qf


def _attention(qT, k, vT, keys, thr, jlast, proj, tq, tk):
    s = k.shape[1]
    ng = ATTN_HEADS // HEAD_GROUP
    tiles = [(qi, kj) for qi in range(s // tq) for kj in range((qi * tq + tq - 1) // tk + 1)]
    const3 = lambda shape: pl.BlockSpec(shape, lambda i, t: (0, 0, 0), pipeline_mode=pl.Buffered(1))
    grid_spec = pltpu.PrefetchScalarGridSpec(
        num_scalar_prefetch=1,
        grid=(len(tiles),),
        in_specs=[pl.BlockSpec((ATTN_HEADS, ATTN_HEAD_DIM, tq), lambda i, t: (0, 0, t[0, i])),
                  pl.BlockSpec((ATTN_HEADS, tk, ATTN_HEAD_DIM), lambda i, t: (0, t[1, i], 0)),
                  pl.BlockSpec((ATTN_HEADS, V_ROWS, tk), lambda i, t: (0, 0, t[1, i])),
                  pl.BlockSpec((tk, tq), lambda i, t: (t[1, i], t[0, i])),
                  pl.BlockSpec((1, tq), lambda i, t: (0, t[0, i])),
                  pl.BlockSpec((1, tq), lambda i, t: (0, t[0, i])),
                  pl.BlockSpec((tq, ATTN_WIDTH), lambda i, t: (t[0, i], P_OFFSETS["z_attn"] // ATTN_WIDTH)),
                  const3((ATTN_HEADS, 1, tq)),
                  pl.BlockSpec((tk, ATTN_HEAD_DIM), lambda i, t: (0, 0), pipeline_mode=pl.Buffered(1)),
                  const3((ATTN_HEADS, ATTN_HEAD_DIM, tq))],
        out_specs=pl.BlockSpec((tq, ATTN_WIDTH), lambda i, t: (t[0, i], 0)),
        scratch_shapes=[pltpu.VMEM((ng, HEAD_GROUP, V_ROWS, tq), F32),
                        pltpu.VMEM((ng, HEAD_GROUP, 1, tq), F32),
                        pltpu.VMEM((tk, tq), F32),
                        pltpu.VMEM((2, tk, tq), F32),
                        pltpu.VMEM((2, tk, tq), BF16)],
    )
    return pl.pallas_call(
        functools.partial(_attn_kernel, tq=tq, tk=tk),
        out_shape=jax.ShapeDtypeStruct((s, ATTN_WIDTH), BF16),
        grid_spec=grid_spec,
        compiler_params=_cparams(("arbitrary",), "attn"),
        name="attn",
    )(jnp.asarray(tiles, I32).T, qT, k, vT, keys, thr, jlast, proj, *_alibi_features(tq, tk))


def _softcap(x):
    return GATE_SOFTCAP * jnp.tanh(x / GATE_SOFTCAP)


def _mlstm_kernel(q_ref, k_ref, v_ref, og_ref, z_ref, gt_ref, g_ref, out_ref, c_ref, n_ref, m_ref, *, chunk):
    ci = pl.program_id(1)
    L = chunk
    dk, dv = MLSTM_QK_DIM, MLSTM_V_DIM

    @pl.when(ci == 0)
    def _init():
        c_ref[...] = jnp.zeros(c_ref.shape, F32)
        n_ref[...] = jnp.zeros(n_ref.shape, F32)
        m_ref[...] = jnp.zeros(m_ref.shape, F32)

    gt = gt_ref[...]
    sub = lax.broadcasted_iota(I32, gt.shape, 0)
    r_i = lax.broadcasted_iota(I32, (L, L), 0)
    c_i = lax.broadcasted_iota(I32, (L, L), 1)
    eye = r_i == c_i
    tril = r_i >= c_i
    nt = (((1,), (1,)), ((), ()))
    tn = (((0,), (0,)), ((), ()))

    for j in range(MLSTM_GROUP):
        hd = pl.program_id(0) * MLSTM_GROUP + j
        ig_row = _softcap(jnp.sum(jnp.where(sub == hd, gt, 0.0), axis=0, keepdims=True))
        fg_row = _softcap(jnp.sum(jnp.where(sub == MLSTM_HEADS + hd, gt, 0.0), axis=0, keepdims=True))
        logf_row = jnp.minimum(fg_row, 0.0) - jnp.log1p(jnp.exp(-jnp.abs(fg_row)))
        ig_col = jnp.sum(jnp.where(eye, ig_row, 0.0), axis=1, keepdims=True)
        b_col = jnp.sum(jnp.where(tril, logf_row, 0.0), axis=1, keepdims=True)
        b_row = jnp.sum(jnp.where(eye, b_col, 0.0), axis=0, keepdims=True)
        dmat = jnp.where(tril, b_col - b_row + ig_row, -jnp.inf)
        m_prev = m_ref[j]
        m_inter = b_col + m_prev
        m_t = jnp.maximum(m_inter, jnp.max(dmat, axis=1, keepdims=True))

        qc = q_ref[:, j * dk:(j + 1) * dk]
        kc = k_ref[:, j * dk:(j + 1) * dk]
        vc = v_ref[:, j * dv:(j + 1) * dv].astype(BF16)
        s = lax.dot_general(qc, kc, nt, preferred_element_type=F32) * jnp.exp(dmat - m_t)
        inter = jnp.exp(m_inter - m_t)
        num = (jnp.dot(s.astype(BF16), vc, preferred_element_type=F32)
               + inter * jnp.dot(qc, c_ref[j].astype(BF16), preferred_element_type=F32))
        qn = jnp.sum(qc.astype(F32) * n_ref[j], axis=1, keepdims=True)
        den = jnp.sum(s, axis=1, keepdims=True) + inter * qn
        hh = num / jnp.maximum(jnp.abs(den), jnp.exp(-m_t))

        g_last = b_col[L - 1:L, :]
        m_new = m_t[L - 1:L, :]
        wgt = jnp.exp(g_last - b_col + ig_col - m_new)
        decay = jnp.exp(g_last + m_prev - m_new)
        wk = wgt * kc.astype(F32)
        c_ref[j] = decay * c_ref[j] + lax.dot_general(wk.astype(BF16), vc, tn, preferred_element_type=F32)
        n_ref[j] = decay * n_ref[j] + jnp.sum(wk, axis=0, keepdims=True)
        m_ref[j] = m_new

        hn = hh * lax.rsqrt(jnp.mean(hh * hh, axis=-1, keepdims=True) + NORM_EPS) * g_ref[j]
        cols = slice(j * dv, (j + 1) * dv)
        out_ref[:, cols] = (hn * _sigmoid(og_ref[:, cols]) * _silu(z_ref[:, cols])).astype(BF16)


def _mlstm(qk, proj, gates_t, g_mh3, chunk):
    s = qk.shape[0]
    gdk, gdv = MLSTM_GROUP * MLSTM_QK_DIM, MLSTM_GROUP * MLSTM_V_DIM
    vb, ob, zb = (P_OFFSETS[n] // gdv for n in ("v_m", "o_m", "z_m"))
    return pl.pallas_call(
        functools.partial(_mlstm_kernel, chunk=chunk),
        out_shape=jax.ShapeDtypeStruct((s, MLSTM_WIDTH), BF16),
        grid=(MLSTM_HEADS // MLSTM_GROUP, s // chunk),
        in_specs=[pl.BlockSpec((chunk, gdk), lambda h, c: (c, h)),
                  pl.BlockSpec((chunk, gdk), lambda h, c: (c, MLSTM_QK_WIDTH // gdk + h)),
                  pl.BlockSpec((chunk, gdv), lambda h, c: (c, vb + h)),
                  pl.BlockSpec((chunk, gdv), lambda h, c: (c, ob + h)),
                  pl.BlockSpec((chunk, gdv), lambda h, c: (c, zb + h)),
                  pl.BlockSpec((2 * MLSTM_HEADS, chunk), lambda h, c: (0, c)),
                  pl.BlockSpec((MLSTM_GROUP, 1, MLSTM_V_DIM), lambda h, c: (h, 0, 0))],
        out_specs=pl.BlockSpec((chunk, gdv), lambda h, c: (c, h)),
        scratch_shapes=[pltpu.VMEM((MLSTM_GROUP, MLSTM_QK_DIM, MLSTM_V_DIM), F32),
                        pltpu.VMEM((MLSTM_GROUP, 1, MLSTM_QK_DIM), F32),
                        pltpu.VMEM((MLSTM_GROUP, 1, 1), F32)],
        compiler_params=_cparams(("arbitrary", "arbitrary"), "mlstm"),
        name="mlstm",
    )(qk, qk, proj, proj, proj, gates_t, g_mh3)


def _merge_kernel(a1_ref, a2_ref, w1_ref, w2_ref, ga_ref, gm_ref, o_ref):
    y1 = jnp.dot(a1_ref[...], w1_ref[...], preferred_element_type=F32)
    y2 = jnp.dot(a2_ref[...], w2_ref[...], preferred_element_type=F32)
    o_ref[...] = (_sigmoid(ga_ref[...]) * y1 + _sigmoid(gm_ref[...]) * y2).astype(BF16)


def _merge(a1, a2, w1, w2, proj):
    s, d = a1.shape
    tm = min(ROW_TILE, s)
    tn = COL_TILE
    gab, gmb = P_OFFSETS["g_attn"] // tn, P_OFFSETS["g_mlstm"] // tn
    return pl.pallas_call(
        _merge_kernel,
        out_shape=jax.ShapeDtypeStruct((s, D_MODEL), BF16),
        grid=(s // tm, D_MODEL // tn),
        in_specs=[pl.BlockSpec((tm, d), lambda i, j: (i, 0)),
                  pl.BlockSpec((tm, d), lambda i, j: (i, 0)),
                  pl.BlockSpec((d, tn), lambda i, j: (0, j)),
                  pl.BlockSpec((d, tn), lambda i, j: (0, j)),
                  pl.BlockSpec((tm, tn), lambda i, j: (i, gab + j)),
                  pl.BlockSpec((tm, tn), lambda i, j: (i, gmb + j))],
        out_specs=pl.BlockSpec((tm, tn), lambda i, j: (i, j)),
        compiler_params=_cparams(("arbitrary", "arbitrary"), "merge"),
        name="merge",
    )(a1, a2, w1, w2, proj, proj)


def _final_kernel(mg_ref, w_ref, x_ref, gate_ref, lg_ref, lb_ref, o_ref, *, tn, nn):
    j = pl.program_id(1)
    y = jnp.dot(mg_ref[...], w_ref[...], preferred_element_type=F32)
    for jj in range(nn):
        @pl.when(j == jj)
        def _store(jj=jj):
            o_ref[:, jj * tn:(jj + 1) * tn] = y

    @pl.when(j == nn - 1)
    def _norm():
        d = nn * tn
        ssum = 0.0
        for jj in range(nn):
            cols = slice(jj * tn, (jj + 1) * tn)
            r = DEEPNORM_ALPHA * x_ref[:, cols] + gate_ref[:, cols] * o_ref[:, cols]
            o_ref[:, cols] = r
            ssum = ssum + jnp.sum(r, axis=-1, keepdims=True)
        mu = ssum / d
        vsum = 0.0
        for jj in range(nn):
            cols = slice(jj * tn, (jj + 1) * tn)
            vsum = vsum + jnp.sum(jnp.square(o_ref[:, cols] - mu), axis=-1, keepdims=True)
        inv = lax.rsqrt(vsum / d + NORM_EPS)
        for jj in range(nn):
            cols = slice(jj * tn, (jj + 1) * tn)
            o_ref[:, cols] = (o_ref[:, cols] - mu) * inv * lg_ref[:, cols] + lb_ref[:, cols]


def _final(merged, w_out, x2, mod, ln_g, ln_b):
    s, d = x2.shape
    tm = min(ROW_TILE, s)
    tn = COL_TILE
    nn = d // tn
    return pl.pallas_call(
        functools.partial(_final_kernel, tn=tn, nn=nn),
        out_shape=jax.ShapeDtypeStruct((s, d), F32),
        grid=(s // tm, nn),
        in_specs=[pl.BlockSpec((tm, d), lambda i, j: (i, 0)),
                  pl.BlockSpec((d, tn), lambda i, j: (0, j)),
                  pl.BlockSpec((tm, d), lambda i, j: (i, 0)),
                  pl.BlockSpec((1, d), lambda i, j: (0, 2)),
                  pl.BlockSpec((1, d), lambda i, j: (0, 0)),
                  pl.BlockSpec((1, d), lambda i, j: (0, 0))],
        out_specs=pl.BlockSpec((tm, d), lambda i, j: (i, 0), pipeline_mode=pl.Buffered(1)),
        compiler_params=_cparams(("arbitrary", "arbitrary"), "final"),
        name="final",
    )(merged, w_out, x2, mod, ln_g, ln_b)


RG_TN = 512
F32_SUBLANES = 8
NARROW_A = ("k_idx", "w_idx")
NARROW_B = ("i_m", "f_m")


def _regroup_kernel(tbl_ref, main_ref, na_ref, nb_ref, o_ref, *, n_a, n_b):
    @pl.when(tbl_ref[pl.program_id(0)] >= 0)
    def _wide():
        o_ref[...] = main_ref[...].astype(BF16)

    @pl.when(tbl_ref[pl.program_id(0)] < 0)
    def _narrow():
        o_ref[:n_a, :] = na_ref[...].astype(BF16)
        o_ref[n_a:n_a + n_b, :] = nb_ref[...].astype(BF16)
        o_ref[n_a + n_b:, :] = jnp.zeros((o_ref.shape[0] - n_a - n_b, o_ref.shape[1]), BF16)


def _window_starts(first_col, n_cols):
    starts = []
    for oc in range(first_col, first_col + n_cols, RG_TN):
        if oc >= SMALL_OFF:
            starts.append(-1)
            continue
        seg = next(n for n in P_ORDER if P_OFFSETS[n] <= oc < P_OFFSETS[n] + IN_WIDTH_OF[n])
        start = IN_OFFSETS[seg] + oc - P_OFFSETS[seg]
        assert start % F32_SUBLANES == 0, (seg, start)
        starts.append(start // F32_SUBLANES)
    return starts


def _regroup_w(w_inT, first_col, n_cols):
    d = w_inT.shape[1]
    starts = _window_starts(first_col, n_cols)
    n_a = sum(IN_WIDTH_OF[n] for n in NARROW_A)
    n_b = sum(IN_WIDTH_OF[n] for n in NARROW_B)
    off_a, off_b = IN_OFFSETS[NARROW_A[0]], IN_OFFSETS[NARROW_B[0]]
    grid_spec = pltpu.PrefetchScalarGridSpec(
        num_scalar_prefetch=1,
        grid=(n_cols // RG_TN,),
        in_specs=[pl.BlockSpec((pl.Element(RG_TN), pl.Element(d)), lambda j, tbl: (jnp.maximum(tbl[j], 0) * F32_SUBLANES, 0)),
                  pl.BlockSpec((pl.Element(n_a), pl.Element(d)), lambda j, tbl: (off_a, 0)),
                  pl.BlockSpec((pl.Element(n_b), pl.Element(d)), lambda j, tbl: (off_b, 0))],
        out_specs=pl.BlockSpec((RG_TN, d), lambda j, tbl: (j, 0)),
    )
    return pl.pallas_call(
        functools.partial(_regroup_kernel, n_a=n_a, n_b=n_b),
        out_shape=jax.ShapeDtypeStruct((n_cols, d), BF16),
        grid_spec=grid_spec,
        compiler_params=_cparams(("arbitrary",), "regroup"),
        name="regroup",
    )(jnp.asarray(starts, I32), w_inT, w_inT, w_inT)


def _regroup_cols(a, pad_to):
    parts = [a[..., IN_OFFSETS[n]:IN_OFFSETS[n] + IN_WIDTH_OF[n]] for n in P_ORDER]
    parts.append(jnp.zeros(a.shape[:-1] + (pad_to - P_USED,), a.dtype))
    return jnp.concatenate(parts, axis=-1)


def _layer(x2, c, w_ada, b_ada, w_in, b_in, g_q, g_kv, w_uq, w_iq, w_uk, w_uv, g_kidx, b_kidx, conv_w, conv_b, g_mh,
           w_attn_out, w_mlstm_out, w_out, ln_g, ln_b):
    s, d = x2.shape
    assert d == D_MODEL and s % PROJ_TM == 0, (s, d)
    tq, tk = TQ, TK
    nsel = min(TOPK_MAX, s // 4)

    w_inT = w_in.T
    w_tail = _regroup_w(w_inT, P_MAIN, P_TAIL)
    b_cat = _regroup_cols(b_in, P_TOTAL).reshape(1, P_TOTAL)
    w_uqT = w_uq.T.astype(BF16)
    w_iqT = w_iq.T.astype(BF16)
    w_ukT = w_uk.reshape(ATTN_WIDTH, KV_LORA_RANK).T.astype(BF16)
    w_uvT = w_uv.transpose(0, 2, 1).reshape(ATTN_WIDTH, KV_LORA_RANK).astype(BF16)

    mod = _ada(c.reshape(d, 1), w_ada, b_ada.reshape(1, -1))
    u = _modulate(x2, mod)
    proj = _proj_main(u, w_inT, b_cat[:, :P_PLAIN], 0)
    qk = _proj_main(u, w_inT, b_cat[:, P_PLAIN:P_MAIN], P_PLAIN, (conv_w, conv_b.reshape(1, -1)))
    proj_tail = _proj(u, w_tail, b_cat[:, P_MAIN:])

    qT, qiT = _qpath(proj, g_q.reshape(1, -1), w_uqT, w_iqT, tq)
    k, vT, kidx, widx = _kvpath(proj_tail, g_kv.reshape(1, -1), g_kidx.reshape(1, -1), b_kidx.reshape(1, -1), w_ukT, w_uvT, tq)
    wT = widx.T.reshape(IDX_HEADS, 1, s)
    keys, thr, jlast = _indexer(kidx, qiT, wT, tq, nsel)
    a_attn = _attention(qT, k, vT, keys, thr, jlast, proj, tq, tk)

    gates_t = proj_tail[:, SMALL_OFF - P_MAIN + SM_I:SMALL_OFF - P_MAIN + SM_F + MLSTM_HEADS].T
    a_mlstm = _mlstm(qk, proj, gates_t, g_mh.reshape(MLSTM_HEADS, 1, MLSTM_V_DIM), MLSTM_CHUNK)

    merged = _merge(a_attn, a_mlstm, w_attn_out.astype(BF16), w_mlstm_out.astype(BF16), proj)
    return _final(merged, w_out.astype(BF16), x2, mod, ln_g.reshape(1, -1), ln_b.reshape(1, -1))


def kernel(x, c, w_ada, b_ada, w_in, b_in, g_q, g_kv, w_uq, w_iq, w_uk, w_uv, g_kidx, b_kidx, conv_w, conv_b, g_mh,
           w_attn_out, w_mlstm_out, w_out, ln_g, ln_b):
    bsz, seq, d = x.shape
    assert bsz == 1 and w_ada.shape[0] == 1, "single batch, single layer"
    out = _layer(x.reshape(seq, d), c, w_ada[0], b_ada[0], w_in[0], b_in[0], g_q[0], g_kv[0], w_uq[0], w_iq[0],
                 w_uk[0], w_uv[0], g_kidx[0], b_kidx[0], conv_w[0], conv_b[0], g_mh[0], w_attn_out[0],
                 w_mlstm_out[0], w_out[0], ln_g[0], ln_b[0])
    return out.reshape(bsz, seq, d)
```

```python
import functools

import jax
import jax.numpy as jnp
from jax import lax
from jax.experimental import pallas as pl
from jax.experimental.pallas import tpu as pltpu

F32 = jnp.float32
BF16 = jnp.bfloat16
I32 = jnp.int32

D_MODEL = 4096
ATTN_HEADS = 32
ATTN_HEAD_DIM = 128
ATTN_WIDTH = ATTN_HEADS * ATTN_HEAD_DIM
Q_LORA_RANK = 1024
KV_LORA_RANK = 512
IDX_HEADS = 32
IDX_HEAD_DIM = 64
TOPK_MAX = 256
MLSTM_HEADS = 8
MLSTM_QK_DIM = (D_MODEL // 2) // MLSTM_HEADS
MLSTM_V_DIM = D_MODEL // MLSTM_HEADS
MLSTM_QK_WIDTH = MLSTM_HEADS * MLSTM_QK_DIM
MLSTM_WIDTH = MLSTM_HEADS * MLSTM_V_DIM
MLSTM_CHUNK = 256
MLSTM_GROUP = 2
CONV_WIDTH = 4
GATE_SOFTCAP = 15.0
DEEPNORM_ALPHA = 2.0 ** 0.25
NORM_EPS = 1e-6

IN_WIDTHS = (Q_LORA_RANK, KV_LORA_RANK, IDX_HEAD_DIM, IDX_HEADS, ATTN_WIDTH, 2 * MLSTM_QK_WIDTH, MLSTM_WIDTH,
             MLSTM_WIDTH, MLSTM_HEADS, MLSTM_HEADS, MLSTM_WIDTH, D_MODEL, D_MODEL)
IN_NAMES = ("q_lat", "kv_lat", "k_idx", "w_idx", "z_attn", "qk_m", "v_m", "o_m", "i_m", "f_m", "z_m", "g_attn", "g_mlstm")
IN_OFFSETS = {n: sum(IN_WIDTHS[:i]) for i, n in enumerate(IN_NAMES)}
IN_WIDTH_OF = dict(zip(IN_NAMES, IN_WIDTHS))

P_ORDER = ("z_attn", "v_m", "o_m", "z_m", "g_attn", "g_mlstm", "q_lat", "qk_m", "kv_lat", "k_idx", "w_idx", "i_m", "f_m")
P_OFFSETS = {}
_off = 0
for _n in P_ORDER:
    P_OFFSETS[_n] = _off
    _off += IN_WIDTH_OF[_n]
P_USED = _off
PROJ_TN = 1024
P_TOTAL = -(-P_USED // PROJ_TN) * PROJ_TN
SMALL_W = 128
SMALL_OFF = P_OFFSETS["k_idx"]
P_PLAIN = P_OFFSETS["qk_m"]
P_MAIN = P_OFFSETS["kv_lat"]
P_TAIL = P_TOTAL - P_MAIN
assert P_PLAIN % PROJ_TN == 0 and P_MAIN % PROJ_TN == 0 and P_TAIL % PROJ_TN == 0
SM_WIDX = IDX_HEAD_DIM
SM_I = SM_WIDX + IDX_HEADS
SM_F = SM_I + MLSTM_HEADS

VMEM_CAP_BYTES = 60 * 1024 * 1024
VMEM_MB = dict(ada=32, modulate=40, proj=56, proj_main=56, qpath=48, kvpath=48, indexer=56, attn=56,
               mlstm=32, merge=48, final=56, regroup=40)

TQ = 256
TK = 512
ROW_TILE = 512
COL_TILE = 512
PROJ_TM = 1024
IDX_ROWS = 128

LOG2E = 1.4426950408889634
V_ONES = 16
V_ROWS = ATTN_HEAD_DIM + V_ONES
ATTN_ROWS = 256
HEAD_GROUP = 32
INT_MIN = -2 ** 31
KEY_NEG_INF = INT_MIN + 0x7FFFFF
KEY16_NEG_INF = -2 ** 15 + 0x7F


def _cparams(sem, call):
    return pltpu.CompilerParams(dimension_semantics=sem, vmem_limit_bytes=min(VMEM_MB[call] * 1024 * 1024, VMEM_CAP_BYTES))


def _sigmoid(x):
    return jax.nn.sigmoid(x)


def _silu(x):
    return x * jax.nn.sigmoid(x)


def _const_spec(shape):
    nd = len(shape)
    return pl.BlockSpec(shape, lambda *_: (0,) * nd, pipeline_mode=pl.Buffered(1))


def _ada_kernel(c_ref, w_ref, b_ref, o_ref):
    c = c_ref[...]
    o_ref[...] = jnp.sum(w_ref[...] * _silu(c), axis=0, keepdims=True) + b_ref[...]


def _ada(c_col, w_ada, b_ada):
    d, n = w_ada.shape
    tn = COL_TILE
    return pl.pallas_call(
        _ada_kernel,
        out_shape=jax.ShapeDtypeStruct((1, n), F32),
        grid=(n // tn,),
        in_specs=[pl.BlockSpec((d, 1), lambda j: (0, 0)),
                  pl.BlockSpec((d, tn), lambda j: (0, j)),
                  pl.BlockSpec((1, tn), lambda j: (0, j))],
        out_specs=pl.BlockSpec((1, tn), lambda j: (0, j)),
        compiler_params=_cparams(("arbitrary",), "ada"),
        name="ada",
    )(c_col, w_ada, b_ada)


def _modulate_kernel(x_ref, shift_ref, scale_ref, u_ref):
    u_ref[...] = (x_ref[...] * (1.0 + scale_ref[...]) + shift_ref[...]).astype(BF16)


def _modulate(x2, mod):
    s, d = x2.shape
    tm = min(ROW_TILE, s)
    return pl.pallas_call(
        _modulate_kernel,
        out_shape=jax.ShapeDtypeStruct((s, d), BF16),
        grid=(s // tm,),
        in_specs=[pl.BlockSpec((tm, d), lambda i: (i, 0)),
                  pl.BlockSpec((1, d), lambda i: (0, 0)),
                  pl.BlockSpec((1, d), lambda i: (0, 1))],
        out_specs=pl.BlockSpec((tm, d), lambda i: (i, 0)),
        compiler_params=_cparams(("arbitrary",), "modulate"),
        name="modulate",
    )(x2, mod, mod)


def _proj_kernel(u_ref, w_ref, b_ref, o_ref):
    nt = (((1,), (1,)), ((), ()))
    o_ref[...] = lax.dot_general(u_ref[...], w_ref[...], nt, preferred_element_type=F32) + b_ref[...]


def _proj(u, w_catT, b_cat):
    s, d = u.shape
    n = w_catT.shape[0]
    tm = min(PROJ_TM, s)
    tn = PROJ_TN
    return pl.pallas_call(
        _proj_kernel,
        out_shape=jax.ShapeDtypeStruct((s, n), F32),
        grid=(n // tn, s // tm),
        in_specs=[pl.BlockSpec((tm, d), lambda j, i: (i, 0)),
                  pl.BlockSpec((tn, d), lambda j, i: (j, 0)),
                  pl.BlockSpec((1, tn), lambda j, i: (0, j))],
        out_specs=pl.BlockSpec((tm, tn), lambda j, i: (i, j)),
        compiler_params=_cparams(("arbitrary", "arbitrary"), "proj"),
        name="proj",
    )(u, w_catT, b_cat)


PM_CHUNK = 128


def _proj_main_kernel(starts_ref, u_ref, b_ref, *refs, n_m, conv):
    if conv:
        cw_ref, cb_ref, w_hbm, o_ref, wbf_ref, st_ref, sem, halo_ref = refs
    else:
        w_hbm, o_ref, wbf_ref, st_ref, sem = refs
    j = pl.program_id(0)
    i = pl.program_id(1)
    nj = pl.num_programs(0)
    step = j * n_m + i
    cpt = PROJ_TN // PM_CHUNK
    cps = cpt // n_m
    cpw = RG_TN // PM_CHUNK

    def chunk_copy(tile, c, slot):
        win = starts_ref[tile * (PROJ_TN // RG_TN) + c // cpw]
        row0 = pl.multiple_of(win * F32_SUBLANES + (c % cpw) * PM_CHUNK, F32_SUBLANES)
        return pltpu.make_async_copy(w_hbm.at[pl.ds(row0, PM_CHUNK), :], st_ref.at[slot], sem.at[slot])

    def cast_chunk(tile, c, slot):
        rows = pl.ds(pl.multiple_of(c * PM_CHUNK, PM_CHUNK), PM_CHUNK)
        wbf_ref[tile % 2, rows, :] = st_ref[slot].astype(BF16)

    def group(g):
        tile = g // n_m + 1
        return [(tile, (g % n_m) * cps + e, (g % 2) * cps + e) for e in range(cps)]

    @pl.when(step == 0)
    def _first_tile():
        for c in range(cpt):
            cp = chunk_copy(0, c, 0)
            cp.start()
            cp.wait()
            cast_chunk(0, c, 0)

        @pl.when(nj > 1)
        def _():
            for tile, c, slot in group(0):
                chunk_copy(tile, c, slot).start()

    @pl.when((step + 1) // n_m + 1 < nj)
    def _prefetch():
        for tile, c, slot in group(step + 1):
            chunk_copy(tile, c, slot).start()

    @pl.when(j + 1 < nj)
    def _stage_next_tile():
        for tile, c, slot in group(step):
            chunk_copy(tile, c, slot).wait()
            cast_chunk(tile, c, slot)

    nt = (((1,), (1,)), ((), ()))
    x = lax.dot_general(u_ref[...], wbf_ref[j % 2], nt, preferred_element_type=F32) + b_ref[...]
    if not conv:
        o_ref[...] = x
        return
    prev = jnp.where(i > 0, halo_ref[...], 0.0)
    halo_ref[...] = x[-8:]
    head = jnp.concatenate([prev, x[:8]], axis=0)
    y = cb_ref[...]
    yh = cb_ref[...]
    for tap in range(CONV_WIDTH):
        dly = CONV_WIDTH - 1 - tap
        xs = x if dly == 0 else pltpu.roll(x, dly, 0)
        hs = head if dly == 0 else pltpu.roll(head, dly, 0)
        y = y + xs * cw_ref[tap:tap + 1, :]
        yh = yh + hs[8:] * cw_ref[tap:tap + 1, :]
    y = _silu(jnp.concatenate([yh, y[8:]], axis=0))
    kscale = jnp.where(j * PROJ_TN >= MLSTM_QK_WIDTH, MLSTM_QK_DIM ** -0.5, 1.0)
    o_ref[...] = (y * kscale).astype(BF16)


def _proj_main(u, w_inT, b_main, first_col, conv_wb=None):
    s, d = u.shape
    n = b_main.shape[1]
    conv = conv_wb is not None
    tm = min(PROJ_TM, s)
    tn = PROJ_TN
    n_m = s // tm
    cps = (tn // PM_CHUNK) // n_m
    assert cps * n_m * PM_CHUNK == tn, (s, tm)
    grid_spec = pltpu.PrefetchScalarGridSpec(
        num_scalar_prefetch=1,
        grid=(n // tn, n_m),
        in_specs=[pl.BlockSpec((tm, d), lambda j, i, t: (i, 0)),
                  pl.BlockSpec((1, tn), lambda j, i, t: (0, j))]
        + ([pl.BlockSpec((CONV_WIDTH, tn), lambda j, i, t: (0, j)), pl.BlockSpec((1, tn), lambda j, i, t: (0, j))] if conv else [])
        + [pl.BlockSpec(memory_space=pl.ANY)],
        out_specs=pl.BlockSpec((tm, tn), lambda j, i, t: (i, j)),
        scratch_shapes=[pltpu.VMEM((2, tn, d), BF16),
                        pltpu.VMEM((2 * cps, PM_CHUNK, d), F32),
                        pltpu.SemaphoreType.DMA((2 * cps,))]
        + ([pltpu.VMEM((8, tn), F32)] if conv else []),
    )
    return pl.pallas_call(
        functools.partial(_proj_main_kernel, n_m=n_m, conv=conv),
        out_shape=jax.ShapeDtypeStruct((s, n), BF16 if conv else F32),
        grid_spec=grid_spec,
        compiler_params=_cparams(("arbitrary", "arbitrary"), "proj_main"),
        name="proj_qk" if conv else "proj_main",
    )(jnp.asarray(_window_starts(first_col, n), I32), u, b_main, *(conv_wb or ()), w_inT)


def _qpath_kernel(ql_ref, g_ref, wuq_ref, wiq_ref, qT_ref, qiT_ref, *, scale):
    x = ql_ref[...]
    cq = (x * lax.rsqrt(jnp.mean(x * x, axis=-1, keepdims=True) + NORM_EPS) * g_ref[...]).astype(BF16)
    nt = (((1,), (1,)), ((), ()))
    qT = lax.dot_general(wuq_ref[...], cq, nt, preferred_element_type=F32)
    qT_ref[...] = (qT * scale).reshape(qT_ref.shape).astype(BF16)
    qiT = lax.dot_general(wiq_ref[...], cq, nt, preferred_element_type=F32)
    qiT_ref[...] = qiT.reshape(qiT_ref.shape).astype(BF16)


def _qpath(proj, g_q, w_uqT, w_iqT, tq):
    s = proj.shape[0]
    r = Q_LORA_RANK
    return pl.pallas_call(
        functools.partial(_qpath_kernel, scale=ATTN_HEAD_DIM ** -0.5 * LOG2E),
        out_shape=(jax.ShapeDtypeStruct((ATTN_HEADS, ATTN_HEAD_DIM, s), BF16),
                   jax.ShapeDtypeStruct((IDX_HEADS, IDX_HEAD_DIM, s), BF16)),
        grid=(s // tq,),
        in_specs=[pl.BlockSpec((tq, r), lambda i: (i, P_OFFSETS["q_lat"] // r)),
                  _const_spec((1, r)),
                  _const_spec(w_uqT.shape),
                  _const_spec(w_iqT.shape)],
        out_specs=(pl.BlockSpec((ATTN_HEADS, ATTN_HEAD_DIM, tq), lambda i: (0, 0, i)),
                   pl.BlockSpec((IDX_HEADS, IDX_HEAD_DIM, tq), lambda i: (0, 0, i))),
        compiler_params=_cparams(("arbitrary",), "qpath"),
        name="qpath",
    )(proj, g_q, w_uqT, w_iqT)


def _kvpath_kernel(kvl_ref, sm_ref, gkv_ref, gk_ref, bk_ref, wuk_ref, wuv_ref, k_ref, vT_ref, kidx_ref, widx_ref, *, wscale):
    x = kvl_ref[...]
    ckv = (x * lax.rsqrt(jnp.mean(x * x, axis=-1, keepdims=True) + NORM_EPS) * gkv_ref[...]).astype(BF16)
    kfull = jnp.dot(ckv, wuk_ref[...], preferred_element_type=F32)
    for h in range(ATTN_HEADS):
        k_ref[h] = kfull[:, h * ATTN_HEAD_DIM:(h + 1) * ATTN_HEAD_DIM].astype(BF16)
    nt = (((1,), (1,)), ((), ()))
    vT = lax.dot_general(wuv_ref[...], ckv, nt, preferred_element_type=F32)
    vT_ref[:, :ATTN_HEAD_DIM, :] = vT.reshape(ATTN_HEADS, ATTN_HEAD_DIM, -1).astype(BF16)
    vT_ref[:, ATTN_HEAD_DIM:, :] = jnp.ones((ATTN_HEADS, V_ONES, vT_ref.shape[2]), BF16)
    sm = sm_ref[...]
    ki = sm[:, :IDX_HEAD_DIM]
    mu = jnp.mean(ki, axis=-1, keepdims=True)
    var = jnp.mean(jnp.square(ki - mu), axis=-1, keepdims=True)
    kidx_ref[...] = ((ki - mu) * lax.rsqrt(var + NORM_EPS) * gk_ref[...] + bk_ref[...]).astype(BF16)
    widx_ref[...] = sm[:, SM_WIDX:SM_WIDX + IDX_HEADS] * wscale


def _kvpath(proj_tail, g_kv, g_kidx, b_kidx, w_ukT, w_uvT, tm):
    s = proj_tail.shape[0]
    r = KV_LORA_RANK
    return pl.pallas_call(
        functools.partial(_kvpath_kernel, wscale=IDX_HEADS ** -0.5 * IDX_HEAD_DIM ** -0.5),
        out_shape=(jax.ShapeDtypeStruct((ATTN_HEADS, s, ATTN_HEAD_DIM), BF16),
                   jax.ShapeDtypeStruct((ATTN_HEADS, V_ROWS, s), BF16),
                   jax.ShapeDtypeStruct((s, IDX_HEAD_DIM), BF16),
                   jax.ShapeDtypeStruct((s, IDX_HEADS), F32)),
        grid=(s // tm,),
        in_specs=[pl.BlockSpec((tm, r), lambda i: (i, (P_OFFSETS["kv_lat"] - P_MAIN) // r)),
                  pl.BlockSpec((tm, SMALL_W), lambda i: (i, (SMALL_OFF - P_MAIN) // SMALL_W)),
                  _const_spec((1, r)),
                  _const_spec((1, IDX_HEAD_DIM)),
                  _const_spec((1, IDX_HEAD_DIM)),
                  _const_spec(w_ukT.shape),
                  _const_spec(w_uvT.shape)],
        out_specs=(pl.BlockSpec((ATTN_HEADS, tm, ATTN_HEAD_DIM), lambda i: (0, i, 0)),
                   pl.BlockSpec((ATTN_HEADS, V_ROWS, tm), lambda i: (0, 0, i)),
                   pl.BlockSpec((tm, IDX_HEAD_DIM), lambda i: (i, 0)),
                   pl.BlockSpec((tm, IDX_HEADS), lambda i: (i, 0))),
        compiler_params=_cparams(("arbitrary",), "kvpath"),
        name="kvpath",
    )(proj_tail, proj_tail, g_kv, g_kidx, b_kidx, w_ukT, w_uvT)


def _key_to_float(key):
    bits = jnp.where(key >= 0, key, key ^ 0x7FFFFFFF)
    return jnp.where(key < KEY_NEG_INF, -jnp.inf, pltpu.bitcast(bits, F32))


def _key16_to_float(key):
    bits = jnp.where(key >= 0, key, key ^ 0x7FFF)
    return jnp.where(key < KEY16_NEG_INF, -jnp.inf, pltpu.bitcast(jnp.left_shift(bits, 16), F32))


IDX_VISITS = 8
IDX_PASSES16 = 16


def _indexer_kernel(kidx_ref, qiT_ref, wT_ref, sc_ref, thr_ref, jl_ref, f0_ref, f1_ref, h0_ref, h1_ref, *, seq, tq, nsel):
    i = pl.program_id(0)
    n = seq // tq
    ch = IDX_ROWS
    cb = tq
    n_score = jnp.where(i < n, (i + 1) * (tq // ch), 0)
    n_prev = jnp.maximum(i, 1)
    searching = i >= 1

    @pl.when(i == 0)
    def _():
        h1_ref[:cb, :] = jnp.zeros((cb, tq), BF16)

    def run(cur_f, cur_h, prv_f, prv_h):
        tpos = i * tq + lax.broadcasted_iota(I32, (ch, tq), 1)

        def score_chunk(c):
            r0 = pl.multiple_of(c * ch, ch)
            kc = kidx_ref[pl.ds(r0, ch), :]
            acc = jnp.zeros((ch, tq), F32)
            for h in range(IDX_HEADS):
                r = jnp.dot(kc, qiT_ref[h], preferred_element_type=F32)
                acc = acc + jnp.maximum(r, 0.0) * wT_ref[h]
            spos = r0 + lax.broadcasted_iota(I32, (ch, tq), 0)
            val = jnp.where(spos <= tpos, acc, -jnp.inf)
            sc_ref[pl.ds(r0, ch), :] = val
            cur_f[pl.ds(r0, ch), :] = val
            cur_h[pl.ds(r0, ch), :] = val.astype(BF16)

        def visit(v, st):
            acc, t16 = st
            pb, cc = v // n_prev, v % n_prev
            cand_key = jnp.where(pb == 0, 0, t16 + jnp.left_shift(jnp.int32(1), jnp.maximum(15 - pb, 0)))
            cand = _key16_to_float(cand_key).astype(BF16)
            blk = prv_h[pl.ds(pl.multiple_of(cc * cb, cb), cb), :]
            ones = jnp.where(blk >= cand, jnp.ones((), BF16), jnp.zeros((), BF16))
            part = ones[:16]
            for g in range(1, cb // 16):
                part = part + ones[g * 16:(g + 1) * 16]
            acc = acc + part.astype(F32)
            last = cc == n_prev - 1
            take = jnp.sum(acc, axis=0, keepdims=True).astype(I32) >= nsel
            t_new = jnp.where(pb == 0, jnp.where(take, 0, -2 ** 15), jnp.where(take, cand_key, t16))
            t16 = jnp.where(last & (pb < IDX_PASSES16) & searching, t_new, t16)
            return jnp.where(last, 0.0, acc), t16

        def fused(c, st):
            score_chunk(c)
            for e in range(IDX_VISITS):
                st = visit(c * IDX_VISITS + e, st)
            return st
        st = (jnp.zeros((16, tq), F32), jnp.zeros((1, tq), I32))
        st = lax.fori_loop(0, n_score, fused, st)
        st = lax.fori_loop(0, jnp.where(i == n, IDX_PASSES16 * n_prev, 0), visit, st)
        _, t16 = st

        def fill_chunk(c, carry):
            sc_ref[pl.ds(pl.multiple_of(c * cb, cb), cb), :] = jnp.full((cb, tq), -jnp.inf, F32)
            return carry
        lax.fori_loop(jnp.where(i < n, i + 1, seq // cb), seq // cb, fill_chunk, 0)

        @pl.when(searching)
        def _finish_previous_tile():
            def count(pred):
                def body(c, part):
                    r0 = pl.multiple_of(c * cb, cb)
                    m = jnp.where(pred(prv_f[pl.ds(r0, cb), :], r0), 1, 0)
                    return part + jnp.sum(m.reshape(cb // 8, 8, tq), axis=0)
                part = lax.fori_loop(0, n_prev, body, jnp.zeros((8, tq), I32))
                return jnp.sum(part, axis=0, keepdims=True)

            rbits = pltpu.bitcast(_key16_to_float(t16), I32)
            r_key = jnp.where(rbits >= 0, rbits, rbits ^ 0x7FFFFFFF)

            def bit_cond(s_):
                b, _, _, n_open = s_
                return (b < 17) & (n_open > 0)

            def bit_step(s_):
                b, t, settled, _ = s_
                cand_key = t + jnp.left_shift(jnp.int32(1), 16 - b)
                cand = _key_to_float(cand_key)
                cnt = count(lambda blk, r0: blk >= cand)
                t = jnp.where((cnt >= nsel) & (settled == 0), cand_key, t)
                settled = jnp.where(cnt == nsel, 1, settled)
                return b + 1, t, settled, jnp.sum(1 - settled)
            _, t_key, settled, n_open = lax.while_loop(
                bit_cond, bit_step, (jnp.int32(0), r_key - 2 ** 16, jnp.zeros((1, tq), I32), jnp.int32(tq)))
            thr = _key_to_float(t_key)
            thr_ref[...] = thr
            jl_ref[...] = jnp.full((1, tq), seq, I32)

            @pl.when(n_open > 0)
            def _unsettled():
                tie = (settled == 0) & (count(lambda blk, r0: blk >= thr) > nsel) & (thr > -jnp.inf)

                @pl.when(jnp.max(tie.astype(I32)) > 0)
                def _break_ties():
                    need = nsel - count(lambda blk, r0: blk > thr)

                    def eq_below(j):
                        return count(lambda blk, r0: (blk == thr) & (r0 + lax.broadcasted_iota(I32, (cb, tq), 0) < j))

                    def jbit(b, j):
                        test = j + jnp.left_shift(jnp.int32(1), (seq.bit_length() - 2) - b)
                        return jnp.where(eq_below(test) < need, test, j)
                    jlast = lax.fori_loop(0, seq.bit_length() - 1, jbit, jnp.zeros((1, tq), I32))
                    jl_ref[...] = jnp.where(tie, jlast, seq)

    @pl.when(i % 2 == 0)
    def _even():
        run(f0_ref, h0_ref, f1_ref, h1_ref)

    @pl.when(i % 2 == 1)
    def _odd():
        run(f1_ref, h1_ref, f0_ref, h0_ref)


def _indexer(kidx, qiT, wT, tq, nsel):
    s = kidx.shape[0]
    n = s // tq
    cur = lambda i: jnp.minimum(i, n - 1)
    prv = lambda i: jnp.maximum(i - 1, 0)
    return pl.pallas_call(
        functools.partial(_indexer_kernel, seq=s, tq=tq, nsel=nsel),
        out_shape=(jax.ShapeDtypeStruct((s, s), F32), jax.ShapeDtypeStruct((1, s), F32),
                   jax.ShapeDtypeStruct((1, s), I32)),
        grid=(n + 1,),
        in_specs=[_const_spec((s, IDX_HEAD_DIM)),
                  pl.BlockSpec((IDX_HEADS, IDX_HEAD_DIM, tq), lambda i: (0, 0, cur(i))),
                  pl.BlockSpec((IDX_HEADS, 1, tq), lambda i: (0, 0, cur(i)))],
        out_specs=(pl.BlockSpec((s, tq), lambda i: (0, cur(i))),
                   pl.BlockSpec((1, tq), lambda i: (0, prv(i))),
                   pl.BlockSpec((1, tq), lambda i: (0, prv(i)))),
        scratch_shapes=[pltpu.VMEM((s, tq), F32), pltpu.VMEM((s, tq), F32),
                        pltpu.VMEM((s, tq), BF16), pltpu.VMEM((s, tq), BF16)],
        compiler_params=_cparams(("arbitrary",), "indexer"),
        name="indexer",
    )(kidx, qiT, wT)


def _attn_kernel(tiles_ref, qT_ref, k_ref, vT_ref, keys_ref, thr_ref, jl_ref, z_ref, sl_ref, kf_ref, qf_ref, o_ref,
                 acc_ref, m_ref, mb_ref, lg_ref, p_ref, *, tq, tk):
    qi = tiles_ref[0, pl.program_id(0)]
    kj = tiles_ref[1, pl.program_id(0)]

    @pl.when(kj == 0)
    def _init():
        acc_ref[...] = jnp.zeros(acc_ref.shape, F32)
        m_ref[...] = jnp.full(m_ref.shape, -jnp.inf, F32)

    def _compute():
        spos = kj * tk + lax.broadcasted_iota(I32, (tk, tq), 0)
        tpos = qi * tq + lax.broadcasted_iota(I32, (tk, tq), 1)
        has_ties = tiles_ref[2, pl.program_id(0)] > 0

        @pl.when(has_ties)
        def _mask_with_tie_rows():
            sc = keys_ref[...]
            thr = thr_ref[...]
            sel = ((sc > thr) | ((sc == thr) & (spos <= jl_ref[...]))) & (spos <= tpos)
            mb_ref[...] = jnp.where(sel, 0.0, -jnp.inf)

        @pl.when(jnp.logical_not(has_ties))
        def _mask():
            mb_ref[...] = jnp.where((keys_ref[...] >= thr_ref[...]) & (spos <= tpos), 0.0, -jnp.inf)
        tile_off = (kj * tk - qi * tq).astype(F32)

        def group(g, carry):
            def logits(u):
                h = g * HEAD_GROUP + u
                qh = jnp.concatenate([qT_ref[h], qf_ref[h]], axis=0)
                part = jnp.full((8, tq), -jnp.inf, F32)
                for c in range(tk // ATTN_ROWS):
                    rows = pl.ds(c * ATTN_ROWS, ATTN_ROWS)
                    kh = jnp.concatenate([k_ref[h, rows, :], kf_ref[rows, :]], axis=1)
                    lg = jnp.dot(kh, qh, preferred_element_type=F32) + mb_ref[rows, :]
                    lg_ref[u % 2, rows, :] = lg
                    part = jnp.maximum(part, jnp.max(lg.reshape(ATTN_ROWS // 8, 8, tq), axis=0))
                shift = sl_ref[h] * tile_off
                m_old = m_ref[g, u]
                return m_old, jnp.maximum(m_old, jnp.max(part, axis=0, keepdims=True) + shift), shift

            def probs(u, m_old, m_new, shift):
                m_safe = jnp.where(m_new == -jnp.inf, 0.0, m_new)
                m_tile = m_safe - shift
                for c in range(tk // ATTN_ROWS):
                    rows = pl.ds(c * ATTN_ROWS, ATTN_ROWS)
                    p_ref[u % 2, rows, :] = jnp.exp2(lg_ref[u % 2, rows, :] - m_tile).astype(BF16)
                m_ref[g, u] = m_new
                return jnp.exp2(m_old - m_safe)

            def values(u, alpha):
                h = g * HEAD_GROUP + u
                acc_ref[g, u] = alpha * acc_ref[g, u] + jnp.dot(vT_ref[h], p_ref[u % 2], preferred_element_type=F32)

            stats = logits(0)
            alpha_prev = None
            for u in range(HEAD_GROUP):
                stats_next = logits(u + 1) if u + 1 < HEAD_GROUP else None
                alpha = probs(u, *stats)
                if u >= 1:
                    values(u - 1, alpha_prev)
                stats, alpha_prev = stats_next, alpha
            values(HEAD_GROUP - 1, alpha_prev)
            return carry
        lax.fori_loop(0, ATTN_HEADS // HEAD_GROUP, group, 0)
    _compute()

    @pl.when(kj == (qi * tq + tq - 1) // tk)
    def _finish():
        for h in range(ATTN_HEADS):
            g, u = divmod(h, HEAD_GROUP)
            cols = slice(h * ATTN_HEAD_DIM, (h + 1) * ATTN_HEAD_DIM)
            acc = acc_ref[g, u]
            o = (acc[:ATTN_HEAD_DIM] * (1.0 / acc[ATTN_HEAD_DIM:ATTN_HEAD_DIM + 1])).T
            o_ref[:, cols] = (o * _silu(z_ref[:, cols])).astype(BF16)


def _alibi_features(tq, tk):
    sigma = jnp.exp2(-8.0 * jnp.arange(1, ATTN_HEADS + 1, dtype=F32) / ATTN_HEADS) * LOG2E
    s1 = sigma.astype(BF16)
    s2 = (sigma - s1.astype(F32)).astype(BF16)
    s3 = (sigma - s1.astype(F32) - s2.astype(F32)).astype(BF16)
    pieces = jnp.stack([s1, s2, s3, s1, s2, s3], axis=1)
    qf = jnp.zeros((ATTN_HEADS, ATTN_HEAD_DIM, tq), BF16)
    qf = qf.at[:, :6, :].set(jnp.broadcast_to(pieces[:, :, None], (ATTN_HEADS, 6, tq)))
    r = jnp.arange(tk, dtype=I32)
    r_hi = ((r // 256) * 256).astype(BF16)
    r_lo = (r % 256).astype(BF16)
    kf = jnp.zeros((tk, ATTN_HEAD_DIM), BF16).at[:, :6].set(jnp.stack([r_hi, r_hi, r_hi, r_lo, r_lo, r_lo], axis=1))
    return jnp.broadcast_to(sigma[:, None, None], (ATTN_HEADS, 1, tq)), kf, qf


def _attention(qT, k, vT, keys, thr, jlast, proj, tq, tk):
    s = k.shape[1]
    ng = ATTN_HEADS // HEAD_GROUP
    tiles = [(qi, kj) for qi in range(s // tq) for kj in range((qi * tq + tq - 1) // tk + 1)]
    const3 = lambda shape: pl.BlockSpec(shape, lambda i, t: (0, 0, 0), pipeline_mode=pl.Buffered(1))
    grid_spec = pltpu.PrefetchScalarGridSpec(
        num_scalar_prefetch=1,
        grid=(len(tiles),),
        in_specs=[pl.BlockSpec((ATTN_HEADS, ATTN_HEAD_DIM, tq), lambda i, t: (0, 0, t[0, i])),
                  pl.BlockSpec((ATTN_HEADS, tk, ATTN_HEAD_DIM), lambda i, t: (0, t[1, i], 0)),
                  pl.BlockSpec((ATTN_HEADS, V_ROWS, tk), lambda i, t: (0, 0, t[1, i])),
                  pl.BlockSpec((tk, tq), lambda i, t: (t[1, i], t[0, i])),
                  pl.BlockSpec((1, tq), lambda i, t: (0, t[0, i])),
                  pl.BlockSpec((1, tq), lambda i, t: (0, t[0, i])),
                  pl.BlockSpec((tq, ATTN_WIDTH), lambda i, t: (t[0, i], P_OFFSETS["z_attn"] // ATTN_WIDTH)),
                  const3((ATTN_HEADS, 1, tq)),
                  pl.BlockSpec((tk, ATTN_HEAD_DIM), lambda i, t: (0, 0), pipeline_mode=pl.Buffered(1)),
                  const3((ATTN_HEADS, ATTN_HEAD_DIM, tq))],
        out_specs=pl.BlockSpec((tq, ATTN_WIDTH), lambda i, t: (t[0, i], 0)),
        scratch_shapes=[pltpu.VMEM((ng, HEAD_GROUP, V_ROWS, tq), F32),
                        pltpu.VMEM((ng, HEAD_GROUP, 1, tq), F32),
                        pltpu.VMEM((tk, tq), F32),
                        pltpu.VMEM((2, tk, tq), F32),
                        pltpu.VMEM((2, tk, tq), BF16)],
    )
    call = pl.pallas_call(
        functools.partial(_attn_kernel, tq=tq, tk=tk),
        out_shape=jax.ShapeDtypeStruct((s, ATTN_WIDTH), BF16),
        grid_spec=grid_spec,
        compiler_params=_cparams(("arbitrary",), "attn"),
        name="attn",
    )
    tile_tbl = jnp.asarray(tiles, I32).T
    tied = jnp.any(jlast.reshape(s // tq, tq) < s, axis=1).astype(I32)
    tile_tbl = jnp.concatenate([tile_tbl, tied[tile_tbl[0]][None]], axis=0)
    return call(tile_tbl, qT, k, vT, keys, thr, jlast, proj, *_alibi_features(tq, tk))


def _softcap(x):
    return GATE_SOFTCAP * jnp.tanh(x / GATE_SOFTCAP)


def _mlstm_kernel(q_ref, k_ref, v_ref, og_ref, z_ref, gt_ref, g_ref, out_ref, c_ref, n_ref, m_ref, *, chunk):
    ci = pl.program_id(1)
    L = chunk
    dk, dv = MLSTM_QK_DIM, MLSTM_V_DIM

    @pl.when(ci == 0)
    def _init():
        c_ref[...] = jnp.zeros(c_ref.shape, F32)
        n_ref[...] = jnp.zeros(n_ref.shape, F32)
        m_ref[...] = jnp.zeros(m_ref.shape, F32)

    gt = gt_ref[...]
    sub = lax.broadcasted_iota(I32, gt.shape, 0)
    r_i = lax.broadcasted_iota(I32, (L, L), 0)
    c_i = lax.broadcasted_iota(I32, (L, L), 1)
    eye = r_i == c_i
    tril = r_i >= c_i
    nt = (((1,), (1,)), ((), ()))
    tn = (((0,), (0,)), ((), ()))

    for j in range(MLSTM_GROUP):
        hd = pl.program_id(0) * MLSTM_GROUP + j
        ig_row = _softcap(jnp.sum(jnp.where(sub == hd, gt, 0.0), axis=0, keepdims=True))
        fg_row = _softcap(jnp.sum(jnp.where(sub == MLSTM_HEADS + hd, gt, 0.0), axis=0, keepdims=True))
        logf_row = jnp.minimum(fg_row, 0.0) - jnp.log1p(jnp.exp(-jnp.abs(fg_row)))
        ig_col = jnp.sum(jnp.where(eye, ig_row, 0.0), axis=1, keepdims=True)
        b_col = jnp.sum(jnp.where(tril, logf_row, 0.0), axis=1, keepdims=True)
        b_row = jnp.sum(jnp.where(eye, b_col, 0.0), axis=0, keepdims=True)
        dmat = jnp.where(tril, b_col - b_row + ig_row, -jnp.inf)
        m_prev = m_ref[j]
        m_inter = b_col + m_prev
        m_t = jnp.maximum(m_inter, jnp.max(dmat, axis=1, keepdims=True))

        qc = q_ref[:, j * dk:(j + 1) * dk]
        kc = k_ref[:, j * dk:(j + 1) * dk]
        vc = v_ref[:, j * dv:(j + 1) * dv].astype(BF16)
        s = lax.dot_general(qc, kc, nt, preferred_element_type=F32) * jnp.exp(dmat - m_t)
        inter = jnp.exp(m_inter - m_t)
        num = (jnp.dot(s.astype(BF16), vc, preferred_element_type=F32)
               + inter * jnp.dot(qc, c_ref[j].astype(BF16), preferred_element_type=F32))
        qn = jnp.sum(qc.astype(F32) * n_ref[j], axis=1, keepdims=True)
        den = jnp.sum(s, axis=1, keepdims=True) + inter * qn
        hh = num / jnp.maximum(jnp.abs(den), jnp.exp(-m_t))

        g_last = b_col[L - 1:L, :]
        m_new = m_t[L - 1:L, :]
        wgt = jnp.exp(g_last - b_col + ig_col - m_new)
        decay = jnp.exp(g_last + m_prev - m_new)
        wk = wgt * kc.astype(F32)
        c_ref[j] = decay * c_ref[j] + lax.dot_general(wk.astype(BF16), vc, tn, preferred_element_type=F32)
        n_ref[j] = decay * n_ref[j] + jnp.sum(wk, axis=0, keepdims=True)
        m_ref[j] = m_new

        hn = hh * lax.rsqrt(jnp.mean(hh * hh, axis=-1, keepdims=True) + NORM_EPS) * g_ref[j]
        cols = slice(j * dv, (j + 1) * dv)
        out_ref[:, cols] = (hn * _sigmoid(og_ref[:, cols]) * _silu(z_ref[:, cols])).astype(BF16)


def _mlstm(qk, proj, gates_t, g_mh3, chunk):
    s = qk.shape[0]
    gdk, gdv = MLSTM_GROUP * MLSTM_QK_DIM, MLSTM_GROUP * MLSTM_V_DIM
    vb, ob, zb = (P_OFFSETS[n] // gdv for n in ("v_m", "o_m", "z_m"))
    return pl.pallas_call(
        functools.partial(_mlstm_kernel, chunk=chunk),
        out_shape=jax.ShapeDtypeStruct((s, MLSTM_WIDTH), BF16),
        grid=(MLSTM_HEADS // MLSTM_GROUP, s // chunk),
        in_specs=[pl.BlockSpec((chunk, gdk), lambda h, c: (c, h)),
                  pl.BlockSpec((chunk, gdk), lambda h, c: (c, MLSTM_QK_WIDTH // gdk + h)),
                  pl.BlockSpec((chunk, gdv), lambda h, c: (c, vb + h)),
                  pl.BlockSpec((chunk, gdv), lambda h, c: (c, ob + h)),
                  pl.BlockSpec((chunk, gdv), lambda h, c: (c, zb + h)),
                  pl.BlockSpec((2 * MLSTM_HEADS, chunk), lambda h, c: (0, c)),
                  pl.BlockSpec((MLSTM_GROUP, 1, MLSTM_V_DIM), lambda h, c: (h, 0, 0))],
        out_specs=pl.BlockSpec((chunk, gdv), lambda h, c: (c, h)),
        scratch_shapes=[pltpu.VMEM((MLSTM_GROUP, MLSTM_QK_DIM, MLSTM_V_DIM), F32),
                        pltpu.VMEM((MLSTM_GROUP, 1, MLSTM_QK_DIM), F32),
                        pltpu.VMEM((MLSTM_GROUP, 1, 1), F32)],
        compiler_params=_cparams(("arbitrary", "arbitrary"), "mlstm"),
        name="mlstm",
    )(qk, qk, proj, proj, proj, gates_t, g_mh3)


def _merge_kernel(a1_ref, a2_ref, w1_ref, w2_ref, ga_ref, gm_ref, o_ref):
    y1 = jnp.dot(a1_ref[...], w1_ref[...], preferred_element_type=F32)
    y2 = jnp.dot(a2_ref[...], w2_ref[...], preferred_element_type=F32)
    o_ref[...] = (_sigmoid(ga_ref[...]) * y1 + _sigmoid(gm_ref[...]) * y2).astype(BF16)


def _merge(a1, a2, w1, w2, proj):
    s, d = a1.shape
    tm = min(ROW_TILE, s)
    tn = COL_TILE
    gab, gmb = P_OFFSETS["g_attn"] // tn, P_OFFSETS["g_mlstm"] // tn
    return pl.pallas_call(
        _merge_kernel,
        out_shape=jax.ShapeDtypeStruct((s, D_MODEL), BF16),
        grid=(s // tm, D_MODEL // tn),
        in_specs=[pl.BlockSpec((tm, d), lambda i, j: (i, 0)),
                  pl.BlockSpec((tm, d), lambda i, j: (i, 0)),
                  pl.BlockSpec((d, tn), lambda i, j: (0, j)),
                  pl.BlockSpec((d, tn), lambda i, j: (0, j)),
                  pl.BlockSpec((tm, tn), lambda i, j: (i, gab + j)),
                  pl.BlockSpec((tm, tn), lambda i, j: (i, gmb + j))],
        out_specs=pl.BlockSpec((tm, tn), lambda i, j: (i, j)),
        compiler_params=_cparams(("arbitrary", "arbitrary"), "merge"),
        name="merge",
    )(a1, a2, w1, w2, proj, proj)


def _final_kernel(mg_ref, w_ref, x_ref, gate_ref, lg_ref, lb_ref, o_ref, *, tn, nn):
    j = pl.program_id(1)
    y = jnp.dot(mg_ref[...], w_ref[...], preferred_element_type=F32)
    for jj in range(nn):
        @pl.when(j == jj)
        def _store(jj=jj):
            o_ref[:, jj * tn:(jj + 1) * tn] = y

    @pl.when(j == nn - 1)
    def _norm():
        d = nn * tn
        ssum = 0.0
        for jj in range(nn):
            cols = slice(jj * tn, (jj + 1) * tn)
            r = DEEPNORM_ALPHA * x_ref[:, cols] + gate_ref[:, cols] * o_ref[:, cols]
            o_ref[:, cols] = r
            ssum = ssum + jnp.sum(r, axis=-1, keepdims=True)
        mu = ssum / d
        vsum = 0.0
        for jj in range(nn):
            cols = slice(jj * tn, (jj + 1) * tn)
            vsum = vsum + jnp.sum(jnp.square(o_ref[:, cols] - mu), axis=-1, keepdims=True)
        inv = lax.rsqrt(vsum / d + NORM_EPS)
        for jj in range(nn):
            cols = slice(jj * tn, (jj + 1) * tn)
            o_ref[:, cols] = (o_ref[:, cols] - mu) * inv * lg_ref[:, cols] + lb_ref[:, cols]


def _final(merged, w_out, x2, mod, ln_g, ln_b):
    s, d = x2.shape
    tm = min(ROW_TILE, s)
    tn = COL_TILE
    nn = d // tn
    return pl.pallas_call(
        functools.partial(_final_kernel, tn=tn, nn=nn),
        out_shape=jax.ShapeDtypeStruct((s, d), F32),
        grid=(s // tm, nn),
        in_specs=[pl.BlockSpec((tm, d), lambda i, j: (i, 0)),
                  pl.BlockSpec((d, tn), lambda i, j: (0, j)),
                  pl.BlockSpec((tm, d), lambda i, j: (i, 0)),
                  pl.BlockSpec((1, d), lambda i, j: (0, 2)),
                  pl.BlockSpec((1, d), lambda i, j: (0, 0)),
                  pl.BlockSpec((1, d), lambda i, j: (0, 0))],
        out_specs=pl.BlockSpec((tm, d), lambda i, j: (i, 0), pipeline_mode=pl.Buffered(1)),
        compiler_params=_cparams(("arbitrary", "arbitrary"), "final"),
        name="final",
    )(merged, w_out, x2, mod, ln_g, ln_b)


RG_TN = 512
F32_SUBLANES = 8
NARROW_A = ("k_idx", "w_idx")
NARROW_B = ("i_m", "f_m")


def _regroup_kernel(tbl_ref, main_ref, na_ref, nb_ref, o_ref, *, n_a, n_b):
    @pl.when(tbl_ref[pl.program_id(0)] >= 0)
    def _wide():
        o_ref[...] = main_ref[...].astype(BF16)

    @pl.when(tbl_ref[pl.program_id(0)] < 0)
    def _narrow():
        o_ref[:n_a, :] = na_ref[...].astype(BF16)
        o_ref[n_a:n_a + n_b, :] = nb_ref[...].astype(BF16)
        o_ref[n_a + n_b:, :] = jnp.zeros((o_ref.shape[0] - n_a - n_b, o_ref.shape[1]), BF16)


def _window_starts(first_col, n_cols):
    starts = []
    for oc in range(first_col, first_col + n_cols, RG_TN):
        if oc >= SMALL_OFF:
            starts.append(-1)
            continue
        seg = next(n for n in P_ORDER if P_OFFSETS[n] <= oc < P_OFFSETS[n] + IN_WIDTH_OF[n])
        start = IN_OFFSETS[seg] + oc - P_OFFSETS[seg]
        assert start % F32_SUBLANES == 0, (seg, start)
        starts.append(start // F32_SUBLANES)
    return starts


def _regroup_w(w_inT, first_col, n_cols):
    d = w_inT.shape[1]
    starts = _window_starts(first_col, n_cols)
    n_a = sum(IN_WIDTH_OF[n] for n in NARROW_A)
    n_b = sum(IN_WIDTH_OF[n] for n in NARROW_B)
    off_a, off_b = IN_OFFSETS[NARROW_A[0]], IN_OFFSETS[NARROW_B[0]]
    grid_spec = pltpu.PrefetchScalarGridSpec(
        num_scalar_prefetch=1,
        grid=(n_cols // RG_TN,),
        in_specs=[pl.BlockSpec((pl.Element(RG_TN), pl.Element(d)), lambda j, tbl: (jnp.maximum(tbl[j], 0) * F32_SUBLANES, 0)),
                  pl.BlockSpec((pl.Element(n_a), pl.Element(d)), lambda j, tbl: (off_a, 0)),
                  pl.BlockSpec((pl.Element(n_b), pl.Element(d)), lambda j, tbl: (off_b, 0))],
        out_specs=pl.BlockSpec((RG_TN, d), lambda j, tbl: (j, 0)),
    )
    return pl.pallas_call(
        functools.partial(_regroup_kernel, n_a=n_a, n_b=n_b),
        out_shape=jax.ShapeDtypeStruct((n_cols, d), BF16),
        grid_spec=grid_spec,
        compiler_params=_cparams(("arbitrary",), "regroup"),
        name="regroup",
    )(jnp.asarray(starts, I32), w_inT, w_inT, w_inT)


def _regroup_cols(a, pad_to):
    parts = [a[..., IN_OFFSETS[n]:IN_OFFSETS[n] + IN_WIDTH_OF[n]] for n in P_ORDER]
    parts.append(jnp.zeros(a.shape[:-1] + (pad_to - P_USED,), a.dtype))
    return jnp.concatenate(parts, axis=-1)


def _layer(x2, c, w_ada, b_ada, w_in, b_in, g_q, g_kv, w_uq, w_iq, w_uk, w_uv, g_kidx, b_kidx, conv_w, conv_b, g_mh,
           w_attn_out, w_mlstm_out, w_out, ln_g, ln_b):
    s, d = x2.shape
    assert d == D_MODEL and s % PROJ_TM == 0, (s, d)
    tq, tk = TQ, TK
    nsel = min(TOPK_MAX, s // 4)

    w_inT = w_in.T
    w_tail = _regroup_w(w_inT, P_MAIN, P_TAIL)
    b_cat = _regroup_cols(b_in, P_TOTAL).reshape(1, P_TOTAL)
    w_uqT = w_uq.T.astype(BF16)
    w_iqT = w_iq.T.astype(BF16)
    w_ukT = w_uk.reshape(ATTN_WIDTH, KV_LORA_RANK).T.astype(BF16)
    w_uvT = w_uv.transpose(0, 2, 1).reshape(ATTN_WIDTH, KV_LORA_RANK).astype(BF16)

    mod = _ada(c.reshape(d, 1), w_ada, b_ada.reshape(1, -1))
    u = _modulate(x2, mod)
    proj = _proj_main(u, w_inT, b_cat[:, :P_PLAIN], 0)
    qk = _proj_main(u, w_inT, b_cat[:, P_PLAIN:P_MAIN], P_PLAIN, (conv_w, conv_b.reshape(1, -1)))
    proj_tail = _proj(u, w_tail, b_cat[:, P_MAIN:])

    qT, qiT = _qpath(proj, g_q.reshape(1, -1), w_uqT, w_iqT, tq)
    k, vT, kidx, widx = _kvpath(proj_tail, g_kv.reshape(1, -1), g_kidx.reshape(1, -1), b_kidx.reshape(1, -1), w_ukT, w_uvT, tq)
    wT = widx.T.reshape(IDX_HEADS, 1, s)
    keys, thr, jlast = _indexer(kidx, qiT, wT, tq, nsel)
    a_attn = _attention(qT, k, vT, keys, thr, jlast, proj, tq, tk)

    gates_t = proj_tail[:, SMALL_OFF - P_MAIN + SM_I:SMALL_OFF - P_MAIN + SM_F + MLSTM_HEADS].T
    a_mlstm = _mlstm(qk, proj, gates_t, g_mh.reshape(MLSTM_HEADS, 1, MLSTM_V_DIM), MLSTM_CHUNK)

    merged = _merge(a_attn, a_mlstm, w_attn_out.astype(BF16), w_mlstm_out.astype(BF16), proj)
    return _final(merged, w_out.astype(BF16), x2, mod, ln_g.reshape(1, -1), ln_b.reshape(1, -1))


def kernel(x, c, w_ada, b_ada, w_in, b_in, g_q, g_kv, w_uq, w_iq, w_uk, w_uv, g_kidx, b_kidx, conv_w, conv_b, g_mh,
           w_attn_out, w_mlstm_out, w_out, ln_g, ln_b):
    bsz, seq, d = x.shape
    assert bsz == 1 and w_ada.shape[0] == 1, "single batch, single layer"
    out = _layer(x.reshape(seq, d), c, w_ada[0], b_ada[0], w_in[0], b_in[0], g_q[0], g_kv[0], w_uq[0], w_iq[0],
                 w_uk[0], w_uv[0], g_kidx[0], b_kidx[0], conv_w[0], conv_b[0], g_mh[0], w_attn_out[0],
                 w_mlstm_out[0], w_out[0], ln_g[0], ln_b[0])
    return out.reshape(bsz, seq, d)
```

```python
import functools

import jax
import jax.numpy as jnp
from jax import lax
from jax.experimental import pallas as pl
from jax.experimental.pallas import tpu as pltpu

F32 = jnp.float32
BF16 = jnp.bfloat16
I32 = jnp.int32

D_MODEL = 4096
ATTN_HEADS = 32
ATTN_HEAD_DIM = 128
ATTN_WIDTH = ATTN_HEADS * ATTN_HEAD_DIM
Q_LORA_RANK = 1024
KV_LORA_RANK = 512
IDX_HEADS = 32
IDX_HEAD_DIM = 64
TOPK_MAX = 256
MLSTM_HEADS = 8
MLSTM_QK_DIM = (D_MODEL // 2) // MLSTM_HEADS
MLSTM_V_DIM = D_MODEL // MLSTM_HEADS
MLSTM_QK_WIDTH = MLSTM_HEADS * MLSTM_QK_DIM
MLSTM_WIDTH = MLSTM_HEADS * MLSTM_V_DIM
MLSTM_CHUNK = 256
MLSTM_GROUP = 8
CONV_WIDTH = 4
GATE_SOFTCAP = 15.0
DEEPNORM_ALPHA = 2.0 ** 0.25
NORM_EPS = 1e-6

IN_WIDTHS = (Q_LORA_RANK, KV_LORA_RANK, IDX_HEAD_DIM, IDX_HEADS, ATTN_WIDTH, 2 * MLSTM_QK_WIDTH, MLSTM_WIDTH,
             MLSTM_WIDTH, MLSTM_HEADS, MLSTM_HEADS, MLSTM_WIDTH, D_MODEL, D_MODEL)
IN_NAMES = ("q_lat", "kv_lat", "k_idx", "w_idx", "z_attn", "qk_m", "v_m", "o_m", "i_m", "f_m", "z_m", "g_attn", "g_mlstm")
IN_OFFSETS = {n: sum(IN_WIDTHS[:i]) for i, n in enumerate(IN_NAMES)}
IN_WIDTH_OF = dict(zip(IN_NAMES, IN_WIDTHS))

P_ORDER = ("z_attn", "v_m", "o_m", "z_m", "g_attn", "g_mlstm", "q_lat", "qk_m", "kv_lat", "k_idx", "w_idx", "i_m", "f_m")
P_OFFSETS = {}
_off = 0
for _n in P_ORDER:
    P_OFFSETS[_n] = _off
    _off += IN_WIDTH_OF[_n]
P_USED = _off
PROJ_TN = 1024
P_TOTAL = -(-P_USED // PROJ_TN) * PROJ_TN
SMALL_W = 128
SMALL_OFF = P_OFFSETS["k_idx"]
P_PLAIN = P_OFFSETS["qk_m"]
P_MAIN = P_OFFSETS["kv_lat"]
P_TAIL = P_TOTAL - P_MAIN
assert P_PLAIN % PROJ_TN == 0 and P_MAIN % PROJ_TN == 0 and P_TAIL % PROJ_TN == 0
SM_WIDX = IDX_HEAD_DIM
SM_I = SM_WIDX + IDX_HEADS
SM_F = SM_I + MLSTM_HEADS

VMEM_CAP_BYTES = 60 * 1024 * 1024
VMEM_MB = dict(ada=32, modulate=40, proj=56, proj_main=56, qpath=48, kvpath=48, indexer=56, attn=56,
               mlstm=48, merge=48, final=56, regroup=40)

TQ = 256
TK = 512
ROW_TILE = 512
COL_TILE = 512
PROJ_TM = 1024
IDX_ROWS = 128

LOG2E = 1.4426950408889634
V_ONES = 16
V_ROWS = ATTN_HEAD_DIM + V_ONES
ATTN_ROWS = 256
HEAD_GROUP = 32
INT_MIN = -2 ** 31
KEY_NEG_INF = INT_MIN + 0x7FFFFF
KEY16_NEG_INF = -2 ** 15 + 0x7F


def _cparams(sem, call):
    return pltpu.CompilerParams(dimension_semantics=sem, vmem_limit_bytes=min(VMEM_MB[call] * 1024 * 1024, VMEM_CAP_BYTES))


def _sigmoid(x):
    return jax.nn.sigmoid(x)


def _silu(x):
    return x * jax.nn.sigmoid(x)


def _const_spec(shape):
    nd = len(shape)
    return pl.BlockSpec(shape, lambda *_: (0,) * nd, pipeline_mode=pl.Buffered(1))


def _ada_kernel(c_ref, w_ref, b_ref, o_ref):
    c = c_ref[...]
    o_ref[...] = jnp.sum(w_ref[...] * _silu(c), axis=0, keepdims=True) + b_ref[...]


def _ada(c_col, w_ada, b_ada):
    d, n = w_ada.shape
    tn = COL_TILE
    return pl.pallas_call(
        _ada_kernel,
        out_shape=jax.ShapeDtypeStruct((1, n), F32),
        grid=(n // tn,),
        in_specs=[pl.BlockSpec((d, 1), lambda j: (0, 0)),
                  pl.BlockSpec((d, tn), lambda j: (0, j)),
                  pl.BlockSpec((1, tn), lambda j: (0, j))],
        out_specs=pl.BlockSpec((1, tn), lambda j: (0, j)),
        compiler_params=_cparams(("arbitrary",), "ada"),
        name="ada",
    )(c_col, w_ada, b_ada)


def _modulate_kernel(x_ref, shift_ref, scale_ref, u_ref):
    u_ref[...] = (x_ref[...] * (1.0 + scale_ref[...]) + shift_ref[...]).astype(BF16)


def _modulate(x2, mod):
    s, d = x2.shape
    tm = min(ROW_TILE, s)
    return pl.pallas_call(
        _modulate_kernel,
        out_shape=jax.ShapeDtypeStruct((s, d), BF16),
        grid=(s // tm,),
        in_specs=[pl.BlockSpec((tm, d), lambda i: (i, 0)),
                  pl.BlockSpec((1, d), lambda i: (0, 0)),
                  pl.BlockSpec((1, d), lambda i: (0, 1))],
        out_specs=pl.BlockSpec((tm, d), lambda i: (i, 0)),
        compiler_params=_cparams(("arbitrary",), "modulate"),
        name="modulate",
    )(x2, mod, mod)


def _proj_kernel(u_ref, w_ref, b_ref, o_ref):
    nt = (((1,), (1,)), ((), ()))
    o_ref[...] = lax.dot_general(u_ref[...], w_ref[...], nt, preferred_element_type=F32) + b_ref[...]


def _proj(u, w_catT, b_cat):
    s, d = u.shape
    n = w_catT.shape[0]
    tm = min(PROJ_TM, s)
    tn = PROJ_TN
    return pl.pallas_call(
        _proj_kernel,
        out_shape=jax.ShapeDtypeStruct((s, n), F32),
        grid=(n // tn, s // tm),
        in_specs=[pl.BlockSpec((tm, d), lambda j, i: (i, 0)),
                  pl.BlockSpec((tn, d), lambda j, i: (j, 0)),
                  pl.BlockSpec((1, tn), lambda j, i: (0, j))],
        out_specs=pl.BlockSpec((tm, tn), lambda j, i: (i, j)),
        compiler_params=_cparams(("arbitrary", "arbitrary"), "proj"),
        name="proj",
    )(u, w_catT, b_cat)


PM_CHUNK = 128


def _proj_main_kernel(starts_ref, u_ref, b_ref, *refs, n_m, conv):
    if conv:
        cw_ref, cb_ref, w_hbm, o_ref, wbf_ref, st_ref, sem, halo_ref = refs
    else:
        w_hbm, o_ref, wbf_ref, st_ref, sem = refs
    j = pl.program_id(0)
    i = pl.program_id(1)
    nj = pl.num_programs(0)
    step = j * n_m + i
    cpt = PROJ_TN // PM_CHUNK
    cps = cpt // n_m
    cpw = RG_TN // PM_CHUNK

    def chunk_copy(tile, c, slot):
        win = starts_ref[tile * (PROJ_TN // RG_TN) + c // cpw]
        row0 = pl.multiple_of(win * F32_SUBLANES + (c % cpw) * PM_CHUNK, F32_SUBLANES)
        return pltpu.make_async_copy(w_hbm.at[pl.ds(row0, PM_CHUNK), :], st_ref.at[slot], sem.at[slot])

    def cast_chunk(tile, c, slot):
        rows = pl.ds(pl.multiple_of(c * PM_CHUNK, PM_CHUNK), PM_CHUNK)
        wbf_ref[tile % 2, rows, :] = st_ref[slot].astype(BF16)

    def group(g):
        tile = g // n_m + 1
        return [(tile, (g % n_m) * cps + e, (g % 2) * cps + e) for e in range(cps)]

    @pl.when(step == 0)
    def _first_tile():
        for c in range(cpt):
            cp = chunk_copy(0, c, 0)
            cp.start()
            cp.wait()
            cast_chunk(0, c, 0)

        @pl.when(nj > 1)
        def _():
            for tile, c, slot in group(0):
                chunk_copy(tile, c, slot).start()

    @pl.when((step + 1) // n_m + 1 < nj)
    def _prefetch():
        for tile, c, slot in group(step + 1):
            chunk_copy(tile, c, slot).start()

    @pl.when(j + 1 < nj)
    def _stage_next_tile():
        for tile, c, slot in group(step):
            chunk_copy(tile, c, slot).wait()
            cast_chunk(tile, c, slot)

    nt = (((1,), (1,)), ((), ()))
    x = lax.dot_general(u_ref[...], wbf_ref[j % 2], nt, preferred_element_type=F32) + b_ref[...]
    if not conv:
        o_ref[...] = x
        return
    prev = jnp.where(i > 0, halo_ref[...], 0.0)
    halo_ref[...] = x[-8:]
    head = jnp.concatenate([prev, x[:8]], axis=0)
    y = cb_ref[...]
    yh = cb_ref[...]
    for tap in range(CONV_WIDTH):
        dly = CONV_WIDTH - 1 - tap
        xs = x if dly == 0 else pltpu.roll(x, dly, 0)
        hs = head if dly == 0 else pltpu.roll(head, dly, 0)
        y = y + xs * cw_ref[tap:tap + 1, :]
        yh = yh + hs[8:] * cw_ref[tap:tap + 1, :]
    y = _silu(jnp.concatenate([yh, y[8:]], axis=0))
    kscale = jnp.where(j * PROJ_TN >= MLSTM_QK_WIDTH, MLSTM_QK_DIM ** -0.5, 1.0)
    o_ref[...] = (y * kscale).astype(BF16)


def _proj_main(u, w_inT, b_main, first_col, conv_wb=None):
    s, d = u.shape
    n = b_main.shape[1]
    conv = conv_wb is not None
    tm = min(PROJ_TM, s)
    tn = PROJ_TN
    n_m = s // tm
    cps = (tn // PM_CHUNK) // n_m
    assert cps * n_m * PM_CHUNK == tn, (s, tm)
    grid_spec = pltpu.PrefetchScalarGridSpec(
        num_scalar_prefetch=1,
        grid=(n // tn, n_m),
        in_specs=[pl.BlockSpec((tm, d), lambda j, i, t: (i, 0)),
                  pl.BlockSpec((1, tn), lambda j, i, t: (0, j))]
        + ([pl.BlockSpec((CONV_WIDTH, tn), lambda j, i, t: (0, j)), pl.BlockSpec((1, tn), lambda j, i, t: (0, j))] if conv else [])
        + [pl.BlockSpec(memory_space=pl.ANY)],
        out_specs=pl.BlockSpec((tm, tn), lambda j, i, t: (i, j)),
        scratch_shapes=[pltpu.VMEM((2, tn, d), BF16),
                        pltpu.VMEM((2 * cps, PM_CHUNK, d), F32),
                        pltpu.SemaphoreType.DMA((2 * cps,))]
        + ([pltpu.VMEM((8, tn), F32)] if conv else []),
    )
    return pl.pallas_call(
        functools.partial(_proj_main_kernel, n_m=n_m, conv=conv),
        out_shape=jax.ShapeDtypeStruct((s, n), BF16 if conv else F32),
        grid_spec=grid_spec,
        compiler_params=_cparams(("arbitrary", "arbitrary"), "proj_main"),
        name="proj_qk" if conv else "proj_main",
    )(jnp.asarray(_window_starts(first_col, n), I32), u, b_main, *(conv_wb or ()), w_inT)


def _qpath_kernel(ql_ref, g_ref, wuq_ref, wiq_ref, qT_ref, qiT_ref, *, scale):
    x = ql_ref[...]
    cq = (x * lax.rsqrt(jnp.mean(x * x, axis=-1, keepdims=True) + NORM_EPS) * g_ref[...]).astype(BF16)
    nt = (((1,), (1,)), ((), ()))
    qT = lax.dot_general(wuq_ref[...], cq, nt, preferred_element_type=F32)
    qT_ref[...] = (qT * scale).reshape(qT_ref.shape).astype(BF16)
    qiT = lax.dot_general(wiq_ref[...], cq, nt, preferred_element_type=F32)
    qiT_ref[...] = qiT.reshape(qiT_ref.shape).astype(BF16)


def _qpath(proj, g_q, w_uqT, w_iqT, tq):
    s = proj.shape[0]
    r = Q_LORA_RANK
    return pl.pallas_call(
        functools.partial(_qpath_kernel, scale=ATTN_HEAD_DIM ** -0.5 * LOG2E),
        out_shape=(jax.ShapeDtypeStruct((ATTN_HEADS, ATTN_HEAD_DIM, s), BF16),
                   jax.ShapeDtypeStruct((IDX_HEADS, IDX_HEAD_DIM, s), BF16)),
        grid=(s // tq,),
        in_specs=[pl.BlockSpec((tq, r), lambda i: (i, P_OFFSETS["q_lat"] // r)),
                  _const_spec((1, r)),
                  _const_spec(w_uqT.shape),
                  _const_spec(w_iqT.shape)],
        out_specs=(pl.BlockSpec((ATTN_HEADS, ATTN_HEAD_DIM, tq), lambda i: (0, 0, i)),
                   pl.BlockSpec((IDX_HEADS, IDX_HEAD_DIM, tq), lambda i: (0, 0, i))),
        compiler_params=_cparams(("arbitrary",), "qpath"),
        name="qpath",
    )(proj, g_q, w_uqT, w_iqT)


def _kvpath_kernel(kvl_ref, sm_ref, gkv_ref, gk_ref, bk_ref, wuk_ref, wuv_ref, k_ref, vT_ref, kidx_ref, widx_ref, *, wscale):
    x = kvl_ref[...]
    ckv = (x * lax.rsqrt(jnp.mean(x * x, axis=-1, keepdims=True) + NORM_EPS) * gkv_ref[...]).astype(BF16)
    kfull = jnp.dot(ckv, wuk_ref[...], preferred_element_type=F32)
    for h in range(ATTN_HEADS):
        k_ref[h] = kfull[:, h * ATTN_HEAD_DIM:(h + 1) * ATTN_HEAD_DIM].astype(BF16)
    nt = (((1,), (1,)), ((), ()))
    vT = lax.dot_general(wuv_ref[...], ckv, nt, preferred_element_type=F32)
    vT_ref[:, :ATTN_HEAD_DIM, :] = vT.reshape(ATTN_HEADS, ATTN_HEAD_DIM, -1).astype(BF16)
    vT_ref[:, ATTN_HEAD_DIM:, :] = jnp.ones((ATTN_HEADS, V_ONES, vT_ref.shape[2]), BF16)
    sm = sm_ref[...]
    ki = sm[:, :IDX_HEAD_DIM]
    mu = jnp.mean(ki, axis=-1, keepdims=True)
    var = jnp.mean(jnp.square(ki - mu), axis=-1, keepdims=True)
    kidx_ref[...] = ((ki - mu) * lax.rsqrt(var + NORM_EPS) * gk_ref[...] + bk_ref[...]).astype(BF16)
    widx_ref[...] = sm[:, SM_WIDX:SM_WIDX + IDX_HEADS] * wscale


def _kvpath(proj_tail, g_kv, g_kidx, b_kidx, w_ukT, w_uvT, tm):
    s = proj_tail.shape[0]
    r = KV_LORA_RANK
    return pl.pallas_call(
        functools.partial(_kvpath_kernel, wscale=IDX_HEADS ** -0.5 * IDX_HEAD_DIM ** -0.5),
        out_shape=(jax.ShapeDtypeStruct((ATTN_HEADS, s, ATTN_HEAD_DIM), BF16),
                   jax.ShapeDtypeStruct((ATTN_HEADS, V_ROWS, s), BF16),
                   jax.ShapeDtypeStruct((s, IDX_HEAD_DIM), BF16),
                   jax.ShapeDtypeStruct((s, IDX_HEADS), F32)),
        grid=(s // tm,),
        in_specs=[pl.BlockSpec((tm, r), lambda i: (i, (P_OFFSETS["kv_lat"] - P_MAIN) // r)),
                  pl.BlockSpec((tm, SMALL_W), lambda i: (i, (SMALL_OFF - P_MAIN) // SMALL_W)),
                  _const_spec((1, r)),
                  _const_spec((1, IDX_HEAD_DIM)),
                  _const_spec((1, IDX_HEAD_DIM)),
                  _const_spec(w_ukT.shape),
                  _const_spec(w_uvT.shape)],
        out_specs=(pl.BlockSpec((ATTN_HEADS, tm, ATTN_HEAD_DIM), lambda i: (0, i, 0)),
                   pl.BlockSpec((ATTN_HEADS, V_ROWS, tm), lambda i: (0, 0, i)),
                   pl.BlockSpec((tm, IDX_HEAD_DIM), lambda i: (i, 0)),
                   pl.BlockSpec((tm, IDX_HEADS), lambda i: (i, 0))),
        compiler_params=_cparams(("arbitrary",), "kvpath"),
        name="kvpath",
    )(proj_tail, proj_tail, g_kv, g_kidx, b_kidx, w_ukT, w_uvT)


def _key_to_float(key):
    bits = jnp.where(key >= 0, key, key ^ 0x7FFFFFFF)
    return jnp.where(key < KEY_NEG_INF, -jnp.inf, pltpu.bitcast(bits, F32))


def _key16_to_float(key):
    bits = jnp.where(key >= 0, key, key ^ 0x7FFF)
    return jnp.where(key < KEY16_NEG_INF, -jnp.inf, pltpu.bitcast(jnp.left_shift(bits, 16), F32))


IDX_VISITS = 8
IDX_PASSES16 = 16


def _indexer_kernel(kidx_ref, qiT_ref, wT_ref, sc_ref, thr_ref, jl_ref, f0_ref, f1_ref, h0_ref, h1_ref, *, seq, tq, nsel):
    i = pl.program_id(0)
    n = seq // tq
    ch = IDX_ROWS
    cb = tq
    n_score = jnp.where(i < n, (i + 1) * (tq // ch), 0)
    n_prev = jnp.maximum(i, 1)
    searching = i >= 1

    @pl.when(i == 0)
    def _():
        h1_ref[:cb, :] = jnp.zeros((cb, tq), BF16)

    def run(cur_f, cur_h, prv_f, prv_h):
        tpos = i * tq + lax.broadcasted_iota(I32, (ch, tq), 1)

        def score_chunk(c):
            r0 = pl.multiple_of(c * ch, ch)
            kc = kidx_ref[pl.ds(r0, ch), :]
            acc = jnp.zeros((ch, tq), F32)
            for h in range(IDX_HEADS):
                r = jnp.dot(kc, qiT_ref[h], preferred_element_type=F32)
                acc = acc + jnp.maximum(r, 0.0) * wT_ref[h]
            spos = r0 + lax.broadcasted_iota(I32, (ch, tq), 0)
            val = jnp.where(spos <= tpos, acc, -jnp.inf)
            sc_ref[pl.ds(r0, ch), :] = val
            cur_f[pl.ds(r0, ch), :] = val
            cur_h[pl.ds(r0, ch), :] = val.astype(BF16)

        def visit(v, st):
            acc, t16 = st
            pb, cc = v // n_prev, v % n_prev
            cand_key = jnp.where(pb == 0, 0, t16 + jnp.left_shift(jnp.int32(1), jnp.maximum(15 - pb, 0)))
            cand = _key16_to_float(cand_key).astype(BF16)
            blk = prv_h[pl.ds(pl.multiple_of(cc * cb, cb), cb), :]
            ones = jnp.where(blk >= cand, jnp.ones((), BF16), jnp.zeros((), BF16))
            part = ones[:16]
            for g in range(1, cb // 16):
                part = part + ones[g * 16:(g + 1) * 16]
            acc = acc + part.astype(F32)
            last = cc == n_prev - 1
            take = jnp.sum(acc, axis=0, keepdims=True).astype(I32) >= nsel
            t_new = jnp.where(pb == 0, jnp.where(take, 0, -2 ** 15), jnp.where(take, cand_key, t16))
            t16 = jnp.where(last & (pb < IDX_PASSES16) & searching, t_new, t16)
            return jnp.where(last, 0.0, acc), t16

        def fused(c, st):
            score_chunk(c)
            for e in range(IDX_VISITS):
                st = visit(c * IDX_VISITS + e, st)
            return st
        st = (jnp.zeros((16, tq), F32), jnp.zeros((1, tq), I32))
        st = lax.fori_loop(0, n_score, fused, st)
        st = lax.fori_loop(0, jnp.where(i == n, IDX_PASSES16 * n_prev, 0), visit, st)
        _, t16 = st

        def fill_chunk(c, carry):
            sc_ref[pl.ds(pl.multiple_of(c * cb, cb), cb), :] = jnp.full((cb, tq), -jnp.inf, F32)
            return carry
        lax.fori_loop(jnp.where(i < n, i + 1, seq // cb), seq // cb, fill_chunk, 0)

        @pl.when(searching)
        def _finish_previous_tile():
            def count(pred):
                def body(c, part):
                    r0 = pl.multiple_of(c * cb, cb)
                    m = jnp.where(pred(prv_f[pl.ds(r0, cb), :], r0), 1, 0)
                    return part + jnp.sum(m.reshape(cb // 8, 8, tq), axis=0)
                part = lax.fori_loop(0, n_prev, body, jnp.zeros((8, tq), I32))
                return jnp.sum(part, axis=0, keepdims=True)

            rbits = pltpu.bitcast(_key16_to_float(t16), I32)
            r_key = jnp.where(rbits >= 0, rbits, rbits ^ 0x7FFFFFFF)

            def bit_cond(s_):
                b, _, _, n_open = s_
                return (b < 17) & (n_open > 0)

            def bit_step(s_):
                b, t, settled, _ = s_
                cand_key = t + jnp.left_shift(jnp.int32(1), 16 - b)
                cand = _key_to_float(cand_key)
                cnt = count(lambda blk, r0: blk >= cand)
                t = jnp.where((cnt >= nsel) & (settled == 0), cand_key, t)
                settled = jnp.where(cnt == nsel, 1, settled)
                return b + 1, t, settled, jnp.sum(1 - settled)
            _, t_key, settled, n_open = lax.while_loop(
                bit_cond, bit_step, (jnp.int32(0), r_key - 2 ** 16, jnp.zeros((1, tq), I32), jnp.int32(tq)))
            thr = _key_to_float(t_key)
            thr_ref[...] = thr
            jl_ref[...] = jnp.full((1, tq), seq, I32)

            @pl.when(n_open > 0)
            def _unsettled():
                tie = (settled == 0) & (count(lambda blk, r0: blk >= thr) > nsel) & (thr > -jnp.inf)

                @pl.when(jnp.max(tie.astype(I32)) > 0)
                def _break_ties():
                    need = nsel - count(lambda blk, r0: blk > thr)

                    def eq_below(j):
                        return count(lambda blk, r0: (blk == thr) & (r0 + lax.broadcasted_iota(I32, (cb, tq), 0) < j))

                    def jbit(b, j):
                        test = j + jnp.left_shift(jnp.int32(1), (seq.bit_length() - 2) - b)
                        return jnp.where(eq_below(test) < need, test, j)
                    jlast = lax.fori_loop(0, seq.bit_length() - 1, jbit, jnp.zeros((1, tq), I32))
                    jl_ref[...] = jnp.where(tie, jlast, seq)

    @pl.when(i % 2 == 0)
    def _even():
        run(f0_ref, h0_ref, f1_ref, h1_ref)

    @pl.when(i % 2 == 1)
    def _odd():
        run(f1_ref, h1_ref, f0_ref, h0_ref)


def _indexer(kidx, qiT, wT, tq, nsel):
    s = kidx.shape[0]
    n = s // tq
    cur = lambda i: jnp.minimum(i, n - 1)
    prv = lambda i: jnp.maximum(i - 1, 0)
    return pl.pallas_call(
        functools.partial(_indexer_kernel, seq=s, tq=tq, nsel=nsel),
        out_shape=(jax.ShapeDtypeStruct((s, s), F32), jax.ShapeDtypeStruct((1, s), F32),
                   jax.ShapeDtypeStruct((1, s), I32)),
        grid=(n + 1,),
        in_specs=[_const_spec((s, IDX_HEAD_DIM)),
                  pl.BlockSpec((IDX_HEADS, IDX_HEAD_DIM, tq), lambda i: (0, 0, cur(i))),
                  pl.BlockSpec((IDX_HEADS, 1, tq), lambda i: (0, 0, cur(i)))],
        out_specs=(pl.BlockSpec((s, tq), lambda i: (0, cur(i))),
                   pl.BlockSpec((1, tq), lambda i: (0, prv(i))),
                   pl.BlockSpec((1, tq), lambda i: (0, prv(i)))),
        scratch_shapes=[pltpu.VMEM((s, tq), F32), pltpu.VMEM((s, tq), F32),
                        pltpu.VMEM((s, tq), BF16), pltpu.VMEM((s, tq), BF16)],
        compiler_params=_cparams(("arbitrary",), "indexer"),
        name="indexer",
    )(kidx, qiT, wT)


def _attn_kernel(tiles_ref, qT_ref, k_ref, vT_ref, keys_ref, thr_ref, jl_ref, z_ref, sl_ref, kf_ref, qf_ref, o_ref,
                 acc_ref, m_ref, mb_ref, lg_ref, p_ref, *, tq, tk):
    qi = tiles_ref[0, pl.program_id(0)]
    kj = tiles_ref[1, pl.program_id(0)]

    @pl.when(kj == 0)
    def _init():
        acc_ref[...] = jnp.zeros(acc_ref.shape, F32)
        m_ref[...] = jnp.full(m_ref.shape, -jnp.inf, F32)

    def _compute():
        spos = kj * tk + lax.broadcasted_iota(I32, (tk, tq), 0)
        tpos = qi * tq + lax.broadcasted_iota(I32, (tk, tq), 1)
        has_ties = tiles_ref[2, pl.program_id(0)] > 0

        @pl.when(has_ties)
        def _mask_with_tie_rows():
            sc = keys_ref[...]
            thr = thr_ref[...]
            sel = ((sc > thr) | ((sc == thr) & (spos <= jl_ref[...]))) & (spos <= tpos)
            mb_ref[...] = jnp.where(sel, 0.0, -jnp.inf)

        @pl.when(jnp.logical_not(has_ties))
        def _mask():
            mb_ref[...] = jnp.where((keys_ref[...] >= thr_ref[...]) & (spos <= tpos), 0.0, -jnp.inf)
        tile_off = (kj * tk - qi * tq).astype(F32)

        def group(g, carry):
            def logits(u):
                h = g * HEAD_GROUP + u
                qh = jnp.concatenate([qT_ref[h], qf_ref[h]], axis=0)
                part = jnp.full((8, tq), -jnp.inf, F32)
                for c in range(tk // ATTN_ROWS):
                    rows = pl.ds(c * ATTN_ROWS, ATTN_ROWS)
                    kh = jnp.concatenate([k_ref[h, rows, :], kf_ref[rows, :]], axis=1)
                    lg = jnp.dot(kh, qh, preferred_element_type=F32) + mb_ref[rows, :]
                    lg_ref[u % 2, rows, :] = lg
                    part = jnp.maximum(part, jnp.max(lg.reshape(ATTN_ROWS // 8, 8, tq), axis=0))
                shift = sl_ref[h] * tile_off
                m_old = m_ref[g, u]
                return m_old, jnp.maximum(m_old, jnp.max(part, axis=0, keepdims=True) + shift), shift

            def probs(u, m_old, m_new, shift):
                m_safe = jnp.where(m_new == -jnp.inf, 0.0, m_new)
                m_tile = m_safe - shift
                for c in range(tk // ATTN_ROWS):
                    rows = pl.ds(c * ATTN_ROWS, ATTN_ROWS)
                    p_ref[u % 2, rows, :] = jnp.exp2(lg_ref[u % 2, rows, :] - m_tile).astype(BF16)
                m_ref[g, u] = m_new
                return jnp.exp2(m_old - m_safe)

            def values(u, alpha):
                h = g * HEAD_GROUP + u
                acc_ref[g, u] = alpha * acc_ref[g, u] + jnp.dot(vT_ref[h], p_ref[u % 2], preferred_element_type=F32)

            stats = logits(0)
            alpha_prev = None
            for u in range(HEAD_GROUP):
                stats_next = logits(u + 1) if u + 1 < HEAD_GROUP else None
                alpha = probs(u, *stats)
                if u >= 1:
                    values(u - 1, alpha_prev)
                stats, alpha_prev = stats_next, alpha
            values(HEAD_GROUP - 1, alpha_prev)
            return carry
        lax.fori_loop(0, ATTN_HEADS // HEAD_GROUP, group, 0)
    _compute()

    @pl.when(kj == (qi * tq + tq - 1) // tk)
    def _finish():
        for h in range(ATTN_HEADS):
            g, u = divmod(h, HEAD_GROUP)
            cols = slice(h * ATTN_HEAD_DIM, (h + 1) * ATTN_HEAD_DIM)
            acc = acc_ref[g, u]
            o = (acc[:ATTN_HEAD_DIM] * (1.0 / acc[ATTN_HEAD_DIM:ATTN_HEAD_DIM + 1])).T
            o_ref[:, cols] = (o * _silu(z_ref[:, cols])).astype(BF16)


def _alibi_features(tq, tk):
    sigma = jnp.exp2(-8.0 * jnp.arange(1, ATTN_HEADS + 1, dtype=F32) / ATTN_HEADS) * LOG2E
    s1 = sigma.astype(BF16)
    s2 = (sigma - s1.astype(F32)).astype(BF16)
    s3 = (sigma - s1.astype(F32) - s2.astype(F32)).astype(BF16)
    pieces = jnp.stack([s1, s2, s3, s1, s2, s3], axis=1)
    qf = jnp.zeros((ATTN_HEADS, ATTN_HEAD_DIM, tq), BF16)
    qf = qf.at[:, :6, :].set(jnp.broadcast_to(pieces[:, :, None], (ATTN_HEADS, 6, tq)))
    r = jnp.arange(tk, dtype=I32)
    r_hi = ((r // 256) * 256).astype(BF16)
    r_lo = (r % 256).astype(BF16)
    kf = jnp.zeros((tk, ATTN_HEAD_DIM), BF16).at[:, :6].set(jnp.stack([r_hi, r_hi, r_hi, r_lo, r_lo, r_lo], axis=1))
    return jnp.broadcast_to(sigma[:, None, None], (ATTN_HEADS, 1, tq)), kf, qf


def _attention(qT, k, vT, keys, thr, jlast, proj, tq, tk):
    s = k.shape[1]
    ng = ATTN_HEADS // HEAD_GROUP
    tiles = [(qi, kj) for qi in range(s // tq) for kj in range((qi * tq + tq - 1) // tk + 1)]
    const3 = lambda shape: pl.BlockSpec(shape, lambda i, t: (0, 0, 0), pipeline_mode=pl.Buffered(1))
    grid_spec = pltpu.PrefetchScalarGridSpec(
        num_scalar_prefetch=1,
        grid=(len(tiles),),
        in_specs=[pl.BlockSpec((ATTN_HEADS, ATTN_HEAD_DIM, tq), lambda i, t: (0, 0, t[0, i])),
                  pl.BlockSpec((ATTN_HEADS, tk, ATTN_HEAD_DIM), lambda i, t: (0, t[1, i], 0)),
                  pl.BlockSpec((ATTN_HEADS, V_ROWS, tk), lambda i, t: (0, 0, t[1, i])),
                  pl.BlockSpec((tk, tq), lambda i, t: (t[1, i], t[0, i])),
                  pl.BlockSpec((1, tq), lambda i, t: (0, t[0, i])),
                  pl.BlockSpec((1, tq), lambda i, t: (0, t[0, i])),
                  pl.BlockSpec((tq, ATTN_WIDTH), lambda i, t: (t[0, i], P_OFFSETS["z_attn"] // ATTN_WIDTH)),
                  const3((ATTN_HEADS, 1, tq)),
                  pl.BlockSpec((tk, ATTN_HEAD_DIM), lambda i, t: (0, 0), pipeline_mode=pl.Buffered(1)),
                  const3((ATTN_HEADS, ATTN_HEAD_DIM, tq))],
        out_specs=pl.BlockSpec((tq, ATTN_WIDTH), lambda i, t: (t[0, i], 0)),
        scratch_shapes=[pltpu.VMEM((ng, HEAD_GROUP, V_ROWS, tq), F32),
                        pltpu.VMEM((ng, HEAD_GROUP, 1, tq), F32),
                        pltpu.VMEM((tk, tq), F32),
                        pltpu.VMEM((2, tk, tq), F32),
                        pltpu.VMEM((2, tk, tq), BF16)],
    )
    call = pl.pallas_call(
        functools.partial(_attn_kernel, tq=tq, tk=tk),
        out_shape=jax.ShapeDtypeStruct((s, ATTN_WIDTH), BF16),
        grid_spec=grid_spec,
        compiler_params=_cparams(("arbitrary",), "attn"),
        name="attn",
    )
    tile_tbl = jnp.asarray(tiles, I32).T
    tied = jnp.any(jlast.reshape(s // tq, tq) < s, axis=1).astype(I32)
    tile_tbl = jnp.concatenate([tile_tbl, tied[tile_tbl[0]][None]], axis=0)
    return call(tile_tbl, qT, k, vT, keys, thr, jlast, proj, *_alibi_features(tq, tk))


def _softcap(x):
    return GATE_SOFTCAP * jnp.tanh(x / GATE_SOFTCAP)


def _mlstm_kernel(q_ref, k_ref, v_ref, og_ref, z_ref, gt_ref, g_ref, out_ref, c_ref, n_ref, m_ref, *, chunk):
    ci = pl.program_id(1)
    L = chunk
    dk, dv = MLSTM_QK_DIM, MLSTM_V_DIM

    @pl.when(ci == 0)
    def _init():
        c_ref[...] = jnp.zeros(c_ref.shape, F32)
        n_ref[...] = jnp.zeros(n_ref.shape, F32)
        m_ref[...] = jnp.zeros(m_ref.shape, F32)

    gt = gt_ref[...]
    sub = lax.broadcasted_iota(I32, gt.shape, 0)
    r_i = lax.broadcasted_iota(I32, (L, L), 0)
    c_i = lax.broadcasted_iota(I32, (L, L), 1)
    eye = r_i == c_i
    tril = r_i >= c_i
    nt = (((1,), (1,)), ((), ()))
    tn = (((0,), (0,)), ((), ()))

    for j in range(MLSTM_GROUP):
        hd = pl.program_id(0) * MLSTM_GROUP + j
        ig_row = _softcap(jnp.sum(jnp.where(sub == hd, gt, 0.0), axis=0, keepdims=True))
        fg_row = _softcap(jnp.sum(jnp.where(sub == MLSTM_HEADS + hd, gt, 0.0), axis=0, keepdims=True))
        logf_row = jnp.minimum(fg_row, 0.0) - jnp.log1p(jnp.exp(-jnp.abs(fg_row)))
        ig_col = jnp.sum(jnp.where(eye, ig_row, 0.0), axis=1, keepdims=True)
        b_col = jnp.sum(jnp.where(tril, logf_row, 0.0), axis=1, keepdims=True)
        b_row = jnp.sum(jnp.where(eye, b_col, 0.0), axis=0, keepdims=True)
        dmat = jnp.where(tril, b_col - b_row + ig_row, -jnp.inf)
        m_prev = m_ref[j]
        m_inter = b_col + m_prev
        m_t = jnp.maximum(m_inter, jnp.max(dmat, axis=1, keepdims=True))

        qc = q_ref[:, j * dk:(j + 1) * dk]
        kc = k_ref[:, j * dk:(j + 1) * dk]
        vc = v_ref[:, j * dv:(j + 1) * dv].astype(BF16)
        s = lax.dot_general(qc, kc, nt, preferred_element_type=F32) * jnp.exp(dmat - m_t)
        inter = jnp.exp(m_inter - m_t)
        num = (jnp.dot(s.astype(BF16), vc, preferred_element_type=F32)
               + inter * jnp.dot(qc, c_ref[j].astype(BF16), preferred_element_type=F32))
        qn = jnp.sum(qc.astype(F32) * n_ref[j], axis=1, keepdims=True)
        den = jnp.sum(s, axis=1, keepdims=True) + inter * qn
        hh = num / jnp.maximum(jnp.abs(den), jnp.exp(-m_t))

        g_last = b_col[L - 1:L, :]
        m_new = m_t[L - 1:L, :]
        wgt = jnp.exp(g_last - b_col + ig_col - m_new)
        decay = jnp.exp(g_last + m_prev - m_new)
        wk = wgt * kc.astype(F32)
        c_ref[j] = decay * c_ref[j] + lax.dot_general(wk.astype(BF16), vc, tn, preferred_element_type=F32)
        n_ref[j] = decay * n_ref[j] + jnp.sum(wk, axis=0, keepdims=True)
        m_ref[j] = m_new

        hn = hh * lax.rsqrt(jnp.mean(hh * hh, axis=-1, keepdims=True) + NORM_EPS) * g_ref[j]
        cols = slice(j * dv, (j + 1) * dv)
        out_ref[:, cols] = (hn * _sigmoid(og_ref[:, cols]) * _silu(z_ref[:, cols])).astype(BF16)


def _mlstm(qk, proj, gates_t, g_mh3, chunk):
    s = qk.shape[0]
    gdk, gdv = MLSTM_GROUP * MLSTM_QK_DIM, MLSTM_GROUP * MLSTM_V_DIM
    vb, ob, zb = (P_OFFSETS[n] // gdv for n in ("v_m", "o_m", "z_m"))
    return pl.pallas_call(
        functools.partial(_mlstm_kernel, chunk=chunk),
        out_shape=jax.ShapeDtypeStruct((s, MLSTM_WIDTH), BF16),
        grid=(MLSTM_HEADS // MLSTM_GROUP, s // chunk),
        in_specs=[pl.BlockSpec((chunk, gdk), lambda h, c: (c, h)),
                  pl.BlockSpec((chunk, gdk), lambda h, c: (c, MLSTM_QK_WIDTH // gdk + h)),
                  pl.BlockSpec((chunk, gdv), lambda h, c: (c, vb + h)),
                  pl.BlockSpec((chunk, gdv), lambda h, c: (c, ob + h)),
                  pl.BlockSpec((chunk, gdv), lambda h, c: (c, zb + h)),
                  pl.BlockSpec((2 * MLSTM_HEADS, chunk), lambda h, c: (0, c)),
                  pl.BlockSpec((MLSTM_GROUP, 1, MLSTM_V_DIM), lambda h, c: (h, 0, 0))],
        out_specs=pl.BlockSpec((chunk, gdv), lambda h, c: (c, h)),
        scratch_shapes=[pltpu.VMEM((MLSTM_GROUP, MLSTM_QK_DIM, MLSTM_V_DIM), F32),
                        pltpu.VMEM((MLSTM_GROUP, 1, MLSTM_QK_DIM), F32),
                        pltpu.VMEM((MLSTM_GROUP, 1, 1), F32)],
        compiler_params=_cparams(("arbitrary", "arbitrary"), "mlstm"),
        name="mlstm",
    )(qk, qk, proj, proj, proj, gates_t, g_mh3)


def _merge_kernel(a1_ref, a2_ref, w1_ref, w2_ref, ga_ref, gm_ref, o_ref):
    y1 = jnp.dot(a1_ref[...], w1_ref[...], preferred_element_type=F32)
    y2 = jnp.dot(a2_ref[...], w2_ref[...], preferred_element_type=F32)
    o_ref[...] = (_sigmoid(ga_ref[...]) * y1 + _sigmoid(gm_ref[...]) * y2).astype(BF16)


def _merge(a1, a2, w1, w2, proj):
    s, d = a1.shape
    tm = min(ROW_TILE, s)
    tn = COL_TILE
    gab, gmb = P_OFFSETS["g_attn"] // tn, P_OFFSETS["g_mlstm"] // tn
    return pl.pallas_call(
        _merge_kernel,
        out_shape=jax.ShapeDtypeStruct((s, D_MODEL), BF16),
        grid=(s // tm, D_MODEL // tn),
        in_specs=[pl.BlockSpec((tm, d), lambda i, j: (i, 0)),
                  pl.BlockSpec((tm, d), lambda i, j: (i, 0)),
                  pl.BlockSpec((d, tn), lambda i, j: (0, j)),
                  pl.BlockSpec((d, tn), lambda i, j: (0, j)),
                  pl.BlockSpec((tm, tn), lambda i, j: (i, gab + j)),
                  pl.BlockSpec((tm, tn), lambda i, j: (i, gmb + j))],
        out_specs=pl.BlockSpec((tm, tn), lambda i, j: (i, j)),
        compiler_params=_cparams(("arbitrary", "arbitrary"), "merge"),
        name="merge",
    )(a1, a2, w1, w2, proj, proj)


def _final_kernel(mg_ref, w_ref, x_ref, gate_ref, lg_ref, lb_ref, o_ref, *, tn, nn):
    j = pl.program_id(1)
    y = jnp.dot(mg_ref[...], w_ref[...], preferred_element_type=F32)
    for jj in range(nn):
        @pl.when(j == jj)
        def _store(jj=jj):
            o_ref[:, jj * tn:(jj + 1) * tn] = y

    @pl.when(j == nn - 1)
    def _norm():
        d = nn * tn
        ssum = 0.0
        for jj in range(nn):
            cols = slice(jj * tn, (jj + 1) * tn)
            r = DEEPNORM_ALPHA * x_ref[:, cols] + gate_ref[:, cols] * o_ref[:, cols]
            o_ref[:, cols] = r
            ssum = ssum + jnp.sum(r, axis=-1, keepdims=True)
        mu = ssum / d
        vsum = 0.0
        for jj in range(nn):
            cols = slice(jj * tn, (jj + 1) * tn)
            vsum = vsum + jnp.sum(jnp.square(o_ref[:, cols] - mu), axis=-1, keepdims=True)
        inv = lax.rsqrt(vsum / d + NORM_EPS)
        for jj in range(nn):
            cols = slice(jj * tn, (jj + 1) * tn)
            o_ref[:, cols] = (o_ref[:, cols] - mu) * inv * lg_ref[:, cols] + lb_ref[:, cols]


def _final(merged, w_out, x2, mod, ln_g, ln_b):
    s, d = x2.shape
    tm = min(ROW_TILE, s)
    tn = COL_TILE
    nn = d // tn
    return pl.pallas_call(
        functools.partial(_final_kernel, tn=tn, nn=nn),
        out_shape=jax.ShapeDtypeStruct((s, d), F32),
        grid=(s // tm, nn),
        in_specs=[pl.BlockSpec((tm, d), lambda i, j: (i, 0)),
                  pl.BlockSpec((d, tn), lambda i, j: (0, j)),
                  pl.BlockSpec((tm, d), lambda i, j: (i, 0)),
                  pl.BlockSpec((1, d), lambda i, j: (0, 2)),
                  pl.BlockSpec((1, d), lambda i, j: (0, 0)),
                  pl.BlockSpec((1, d), lambda i, j: (0, 0))],
        out_specs=pl.BlockSpec((tm, d), lambda i, j: (i, 0), pipeline_mode=pl.Buffered(1)),
        compiler_params=_cparams(("arbitrary", "arbitrary"), "final"),
        name="final",
    )(merged, w_out, x2, mod, ln_g, ln_b)


RG_TN = 512
F32_SUBLANES = 8
NARROW_A = ("k_idx", "w_idx")
NARROW_B = ("i_m", "f_m")


def _regroup_kernel(tbl_ref, main_ref, na_ref, nb_ref, o_ref, *, n_a, n_b):
    @pl.when(tbl_ref[pl.program_id(0)] >= 0)
    def _wide():
        o_ref[...] = main_ref[...].astype(BF16)

    @pl.when(tbl_ref[pl.program_id(0)] < 0)
    def _narrow():
        o_ref[:n_a, :] = na_ref[...].astype(BF16)
        o_ref[n_a:n_a + n_b, :] = nb_ref[...].astype(BF16)
        o_ref[n_a + n_b:, :] = jnp.zeros((o_ref.shape[0] - n_a - n_b, o_ref.shape[1]), BF16)


def _window_starts(first_col, n_cols):
    starts = []
    for oc in range(first_col, first_col + n_cols, RG_TN):
        if oc >= SMALL_OFF:
            starts.append(-1)
            continue
        seg = next(n for n in P_ORDER if P_OFFSETS[n] <= oc < P_OFFSETS[n] + IN_WIDTH_OF[n])
        start = IN_OFFSETS[seg] + oc - P_OFFSETS[seg]
        assert start % F32_SUBLANES == 0, (seg, start)
        starts.append(start // F32_SUBLANES)
    return starts


def _regroup_w(w_inT, first_col, n_cols):
    d = w_inT.shape[1]
    starts = _window_starts(first_col, n_cols)
    n_a = sum(IN_WIDTH_OF[n] for n in NARROW_A)
    n_b = sum(IN_WIDTH_OF[n] for n in NARROW_B)
    off_a, off_b = IN_OFFSETS[NARROW_A[0]], IN_OFFSETS[NARROW_B[0]]
    grid_spec = pltpu.PrefetchScalarGridSpec(
        num_scalar_prefetch=1,
        grid=(n_cols // RG_TN,),
        in_specs=[pl.BlockSpec((pl.Element(RG_TN), pl.Element(d)), lambda j, tbl: (jnp.maximum(tbl[j], 0) * F32_SUBLANES, 0)),
                  pl.BlockSpec((pl.Element(n_a), pl.Element(d)), lambda j, tbl: (off_a, 0)),
                  pl.BlockSpec((pl.Element(n_b), pl.Element(d)), lambda j, tbl: (off_b, 0))],
        out_specs=pl.BlockSpec((RG_TN, d), lambda j, tbl: (j, 0)),
    )
    return pl.pallas_call(
        functools.partial(_regroup_kernel, n_a=n_a, n_b=n_b),
        out_shape=jax.ShapeDtypeStruct((n_cols, d), BF16),
        grid_spec=grid_spec,
        compiler_params=_cparams(("arbitrary",), "regroup"),
        name="regroup",
    )(jnp.asarray(starts, I32), w_inT, w_inT, w_inT)


def _regroup_cols(a, pad_to):
    parts = [a[..., IN_OFFSETS[n]:IN_OFFSETS[n] + IN_WIDTH_OF[n]] for n in P_ORDER]
    parts.append(jnp.zeros(a.shape[:-1] + (pad_to - P_USED,), a.dtype))
    return jnp.concatenate(parts, axis=-1)


def _layer(x2, c, w_ada, b_ada, w_in, b_in, g_q, g_kv, w_uq, w_iq, w_uk, w_uv, g_kidx, b_kidx, conv_w, conv_b, g_mh,
           w_attn_out, w_mlstm_out, w_out, ln_g, ln_b):
    s, d = x2.shape
    assert d == D_MODEL and s % PROJ_TM == 0, (s, d)
    tq, tk = TQ, TK
    nsel = min(TOPK_MAX, s // 4)

    w_inT = w_in.T
    w_tail = _regroup_w(w_inT, P_MAIN, P_TAIL)
    b_cat = _regroup_cols(b_in, P_TOTAL).reshape(1, P_TOTAL)
    w_uqT = w_uq.T.astype(BF16)
    w_iqT = w_iq.T.astype(BF16)
    w_ukT = w_uk.reshape(ATTN_WIDTH, KV_LORA_RANK).T.astype(BF16)
    w_uvT = w_uv.transpose(0, 2, 1).reshape(ATTN_WIDTH, KV_LORA_RANK).astype(BF16)

    mod = _ada(c.reshape(d, 1), w_ada, b_ada.reshape(1, -1))
    u = _modulate(x2, mod)
    proj = _proj_main(u, w_inT, b_cat[:, :P_PLAIN], 0)
    qk = _proj_main(u, w_inT, b_cat[:, P_PLAIN:P_MAIN], P_PLAIN, (conv_w, conv_b.reshape(1, -1)))
    proj_tail = _proj(u, w_tail, b_cat[:, P_MAIN:])

    qT, qiT = _qpath(proj, g_q.reshape(1, -1), w_uqT, w_iqT, tq)
    k, vT, kidx, widx = _kvpath(proj_tail, g_kv.reshape(1, -1), g_kidx.reshape(1, -1), b_kidx.reshape(1, -1), w_ukT, w_uvT, tq)
    wT = widx.T.reshape(IDX_HEADS, 1, s)
    keys, thr, jlast = _indexer(kidx, qiT, wT, tq, nsel)
    a_attn = _attention(qT, k, vT, keys, thr, jlast, proj, tq, tk)

    gates_t = proj_tail[:, SMALL_OFF - P_MAIN + SM_I:SMALL_OFF - P_MAIN + SM_F + MLSTM_HEADS].T
    a_mlstm = _mlstm(qk, proj, gates_t, g_mh.reshape(MLSTM_HEADS, 1, MLSTM_V_DIM), MLSTM_CHUNK)

    merged = _merge(a_attn, a_mlstm, w_attn_out.astype(BF16), w_mlstm_out.astype(BF16), proj)
    return _final(merged, w_out.astype(BF16), x2, mod, ln_g.reshape(1, -1), ln_b.reshape(1, -1))


def kernel(x, c, w_ada, b_ada, w_in, b_in, g_q, g_kv, w_uq, w_iq, w_uk, w_uv, g_kidx, b_kidx, conv_w, conv_b, g_mh,
           w_attn_out, w_mlstm_out, w_out, ln_g, ln_b):
    bsz, seq, d = x.shape
    assert bsz == 1 and w_ada.shape[0] == 1, "single batch, single layer"
    out = _layer(x.reshape(seq, d), c, w_ada[0], b_ada[0], w_in[0], b_in[0], g_q[0], g_kv[0], w_uq[0], w_iq[0],
                 w_uk[0], w_uv[0], g_kidx[0], b_kidx[0], conv_w[0], conv_b[0], g_mh[0], w_attn_out[0],
                 w_mlstm_out[0], w_out[0], ln_g[0], ln_b[0])
    return out.reshape(bsz, seq, d)
```

```python
import functools

import jax
import jax.numpy as jnp
from jax import lax
from jax.experimental import pallas as pl
from jax.experimental.pallas import tpu as pltpu

F32 = jnp.float32
BF16 = jnp.bfloat16
I32 = jnp.int32

D_MODEL = 4096
ATTN_HEADS = 32
ATTN_HEAD_DIM = 128
ATTN_WIDTH = ATTN_HEADS * ATTN_HEAD_DIM
Q_LORA_RANK = 1024
KV_LORA_RANK = 512
IDX_HEADS = 32
IDX_HEAD_DIM = 64
TOPK_MAX = 256
MLSTM_HEADS = 8
MLSTM_QK_DIM = (D_MODEL // 2) // MLSTM_HEADS
MLSTM_V_DIM = D_MODEL // MLSTM_HEADS
MLSTM_QK_WIDTH = MLSTM_HEADS * MLSTM_QK_DIM
MLSTM_WIDTH = MLSTM_HEADS * MLSTM_V_DIM
MLSTM_CHUNK = 256
MLSTM_GROUP = 8
CONV_WIDTH = 4
GATE_SOFTCAP = 15.0
DEEPNORM_ALPHA = 2.0 ** 0.25
NORM_EPS = 1e-6

IN_WIDTHS = (Q_LORA_RANK, KV_LORA_RANK, IDX_HEAD_DIM, IDX_HEADS, ATTN_WIDTH, 2 * MLSTM_QK_WIDTH, MLSTM_WIDTH,
             MLSTM_WIDTH, MLSTM_HEADS, MLSTM_HEADS, MLSTM_WIDTH, D_MODEL, D_MODEL)
IN_NAMES = ("q_lat", "kv_lat", "k_idx", "w_idx", "z_attn", "qk_m", "v_m", "o_m", "i_m", "f_m", "z_m", "g_attn", "g_mlstm")
IN_OFFSETS = {n: sum(IN_WIDTHS[:i]) for i, n in enumerate(IN_NAMES)}
IN_WIDTH_OF = dict(zip(IN_NAMES, IN_WIDTHS))

P_ORDER = ("z_attn", "v_m", "o_m", "z_m", "g_attn", "g_mlstm", "q_lat", "qk_m", "kv_lat", "k_idx", "w_idx", "i_m", "f_m")
P_OFFSETS = {}
_off = 0
for _n in P_ORDER:
    P_OFFSETS[_n] = _off
    _off += IN_WIDTH_OF[_n]
P_USED = _off
PROJ_TN = 1024
P_TOTAL = -(-P_USED // PROJ_TN) * PROJ_TN
SMALL_W = 128
SMALL_OFF = P_OFFSETS["k_idx"]
P_PLAIN = P_OFFSETS["qk_m"]
P_MAIN = P_OFFSETS["kv_lat"]
P_TAIL = P_TOTAL - P_MAIN
assert P_PLAIN % PROJ_TN == 0 and P_MAIN % PROJ_TN == 0 and P_TAIL % PROJ_TN == 0
SM_WIDX = IDX_HEAD_DIM
SM_I = SM_WIDX + IDX_HEADS
SM_F = SM_I + MLSTM_HEADS

VMEM_CAP_BYTES = 60 * 1024 * 1024
VMEM_MB = dict(ada=32, modulate=40, proj=56, proj_main=56, qpath=48, kvpath=48, indexer=56, attn=56,
               mlstm=48, merge=48, final=56, regroup=40)

TQ = 256
LATENT_TM = 512
TK = 512
ROW_TILE = 512
COL_TILE = 512
PROJ_TM = 1024
IDX_ROWS = 128

LOG2E = 1.4426950408889634
V_ONES = 16
V_ROWS = ATTN_HEAD_DIM + V_ONES
ATTN_ROWS = 256
HEAD_GROUP = 32
INT_MIN = -2 ** 31
KEY_NEG_INF = INT_MIN + 0x7FFFFF
KEY16_NEG_INF = -2 ** 15 + 0x7F


def _cparams(sem, call):
    return pltpu.CompilerParams(dimension_semantics=sem, vmem_limit_bytes=min(VMEM_MB[call] * 1024 * 1024, VMEM_CAP_BYTES))


def _sigmoid(x):
    return jax.nn.sigmoid(x)


def _silu(x):
    return x * jax.nn.sigmoid(x)


def _const_spec(shape):
    nd = len(shape)
    return pl.BlockSpec(shape, lambda *_: (0,) * nd, pipeline_mode=pl.Buffered(1))


def _ada_kernel(c_ref, w_ref, b_ref, o_ref):
    c = c_ref[...]
    o_ref[...] = jnp.sum(w_ref[...] * _silu(c), axis=0, keepdims=True) + b_ref[...]


def _ada(c_col, w_ada, b_ada):
    d, n = w_ada.shape
    tn = COL_TILE
    return pl.pallas_call(
        _ada_kernel,
        out_shape=jax.ShapeDtypeStruct((1, n), F32),
        grid=(n // tn,),
        in_specs=[pl.BlockSpec((d, 1), lambda j: (0, 0)),
                  pl.BlockSpec((d, tn), lambda j: (0, j)),
                  pl.BlockSpec((1, tn), lambda j: (0, j))],
        out_specs=pl.BlockSpec((1, tn), lambda j: (0, j)),
        compiler_params=_cparams(("arbitrary",), "ada"),
        name="ada",
    )(c_col, w_ada, b_ada)


def _modulate_kernel(x_ref, shift_ref, scale_ref, u_ref):
    u_ref[...] = (x_ref[...] * (1.0 + scale_ref[...]) + shift_ref[...]).astype(BF16)


def _modulate(x2, mod):
    s, d = x2.shape
    tm = min(ROW_TILE, s)
    return pl.pallas_call(
        _modulate_kernel,
        out_shape=jax.ShapeDtypeStruct((s, d), BF16),
        grid=(s // tm,),
        in_specs=[pl.BlockSpec((tm, d), lambda i: (i, 0)),
                  pl.BlockSpec((1, d), lambda i: (0, 0)),
                  pl.BlockSpec((1, d), lambda i: (0, 1))],
        out_specs=pl.BlockSpec((tm, d), lambda i: (i, 0)),
        compiler_params=_cparams(("arbitrary",), "modulate"),
        name="modulate",
    )(x2, mod, mod)


def _proj_kernel(u_ref, w_ref, b_ref, o_ref):
    nt = (((1,), (1,)), ((), ()))
    o_ref[...] = lax.dot_general(u_ref[...], w_ref[...], nt, preferred_element_type=F32) + b_ref[...]


def _proj(u, w_catT, b_cat):
    s, d = u.shape
    n = w_catT.shape[0]
    tm = min(PROJ_TM, s)
    tn = PROJ_TN
    return pl.pallas_call(
        _proj_kernel,
        out_shape=jax.ShapeDtypeStruct((s, n), F32),
        grid=(n // tn, s // tm),
        in_specs=[pl.BlockSpec((tm, d), lambda j, i: (i, 0)),
                  pl.BlockSpec((tn, d), lambda j, i: (j, 0)),
                  pl.BlockSpec((1, tn), lambda j, i: (0, j))],
        out_specs=pl.BlockSpec((tm, tn), lambda j, i: (i, j)),
        compiler_params=_cparams(("arbitrary", "arbitrary"), "proj"),
        name="proj",
    )(u, w_catT, b_cat)


PM_CHUNK = 128


def _proj_main_kernel(starts_ref, u_ref, b_ref, *refs, n_m, conv):
    if conv:
        cw_ref, cb_ref, w_hbm, o_ref, wbf_ref, st_ref, sem, halo_ref = refs
    else:
        w_hbm, o_ref, wbf_ref, st_ref, sem = refs
    j = pl.program_id(0)
    i = pl.program_id(1)
    nj = pl.num_programs(0)
    step = j * n_m + i
    cpt = PROJ_TN // PM_CHUNK
    cps = cpt // n_m
    cpw = RG_TN // PM_CHUNK

    def chunk_copy(tile, c, slot):
        win = starts_ref[tile * (PROJ_TN // RG_TN) + c // cpw]
        row0 = pl.multiple_of(win * F32_SUBLANES + (c % cpw) * PM_CHUNK, F32_SUBLANES)
        return pltpu.make_async_copy(w_hbm.at[pl.ds(row0, PM_CHUNK), :], st_ref.at[slot], sem.at[slot])

    def cast_chunk(tile, c, slot):
        rows = pl.ds(pl.multiple_of(c * PM_CHUNK, PM_CHUNK), PM_CHUNK)
        wbf_ref[tile % 2, rows, :] = st_ref[slot].astype(BF16)

    def group(g):
        tile = g // n_m + 1
        return [(tile, (g % n_m) * cps + e, (g % 2) * cps + e) for e in range(cps)]

    @pl.when(step == 0)
    def _first_tile():
        for c in range(cpt):
            cp = chunk_copy(0, c, 0)
            cp.start()
            cp.wait()
            cast_chunk(0, c, 0)

        @pl.when(nj > 1)
        def _():
            for tile, c, slot in group(0):
                chunk_copy(tile, c, slot).start()

    @pl.when((step + 1) // n_m + 1 < nj)
    def _prefetch():
        for tile, c, slot in group(step + 1):
            chunk_copy(tile, c, slot).start()

    @pl.when(j + 1 < nj)
    def _stage_next_tile():
        for tile, c, slot in group(step):
            chunk_copy(tile, c, slot).wait()
            cast_chunk(tile, c, slot)

    nt = (((1,), (1,)), ((), ()))
    x = lax.dot_general(u_ref[...], wbf_ref[j % 2], nt, preferred_element_type=F32) + b_ref[...]
    if not conv:
        o_ref[...] = x
        return
    prev = jnp.where(i > 0, halo_ref[...], 0.0)
    halo_ref[...] = x[-8:]
    head = jnp.concatenate([prev, x[:8]], axis=0)
    y = cb_ref[...]
    yh = cb_ref[...]
    for tap in range(CONV_WIDTH):
        dly = CONV_WIDTH - 1 - tap
        xs = x if dly == 0 else pltpu.roll(x, dly, 0)
        hs = head if dly == 0 else pltpu.roll(head, dly, 0)
        y = y + xs * cw_ref[tap:tap + 1, :]
        yh = yh + hs[8:] * cw_ref[tap:tap + 1, :]
    y = _silu(jnp.concatenate([yh, y[8:]], axis=0))
    kscale = jnp.where(j * PROJ_TN >= MLSTM_QK_WIDTH, MLSTM_QK_DIM ** -0.5, 1.0)
    o_ref[...] = (y * kscale).astype(BF16)


def _proj_main(u, w_inT, b_main, first_col, conv_wb=None):
    s, d = u.shape
    n = b_main.shape[1]
    conv = conv_wb is not None
    tm = min(PROJ_TM, s)
    tn = PROJ_TN
    n_m = s // tm
    cps = (tn // PM_CHUNK) // n_m
    assert cps * n_m * PM_CHUNK == tn, (s, tm)
    grid_spec = pltpu.PrefetchScalarGridSpec(
        num_scalar_prefetch=1,
        grid=(n // tn, n_m),
        in_specs=[pl.BlockSpec((tm, d), lambda j, i, t: (i, 0)),
                  pl.BlockSpec((1, tn), lambda j, i, t: (0, j))]
        + ([pl.BlockSpec((CONV_WIDTH, tn), lambda j, i, t: (0, j)), pl.BlockSpec((1, tn), lambda j, i, t: (0, j))] if conv else [])
        + [pl.BlockSpec(memory_space=pl.ANY)],
        out_specs=pl.BlockSpec((tm, tn), lambda j, i, t: (i, j)),
        scratch_shapes=[pltpu.VMEM((2, tn, d), BF16),
                        pltpu.VMEM((2 * cps, PM_CHUNK, d), F32),
                        pltpu.SemaphoreType.DMA((2 * cps,))]
        + ([pltpu.VMEM((8, tn), F32)] if conv else []),
    )
    return pl.pallas_call(
        functools.partial(_proj_main_kernel, n_m=n_m, conv=conv),
        out_shape=jax.ShapeDtypeStruct((s, n), BF16 if conv else F32),
        grid_spec=grid_spec,
        compiler_params=_cparams(("arbitrary", "arbitrary"), "proj_main"),
        name="proj_qk" if conv else "proj_main",
    )(jnp.asarray(_window_starts(first_col, n), I32), u, b_main, *(conv_wb or ()), w_inT)


def _qpath_kernel(ql_ref, g_ref, wuq_ref, wiq_ref, qT_ref, qiT_ref, *, scale):
    x = ql_ref[...]
    cq = (x * lax.rsqrt(jnp.mean(x * x, axis=-1, keepdims=True) + NORM_EPS) * g_ref[...]).astype(BF16)
    nt = (((1,), (1,)), ((), ()))
    qT = lax.dot_general(wuq_ref[...], cq, nt, preferred_element_type=F32)
    qT_ref[...] = (qT * scale).reshape(qT_ref.shape).astype(BF16)
    qiT = lax.dot_general(wiq_ref[...], cq, nt, preferred_element_type=F32)
    qiT_ref[...] = qiT.reshape(qiT_ref.shape).astype(BF16)


def _qpath(proj, g_q, w_uqT, w_iqT, tq):
    s = proj.shape[0]
    r = Q_LORA_RANK
    return pl.pallas_call(
        functools.partial(_qpath_kernel, scale=ATTN_HEAD_DIM ** -0.5 * LOG2E),
        out_shape=(jax.ShapeDtypeStruct((ATTN_HEADS, ATTN_HEAD_DIM, s), BF16),
                   jax.ShapeDtypeStruct((IDX_HEADS, IDX_HEAD_DIM, s), BF16)),
        grid=(s // tq,),
        in_specs=[pl.BlockSpec((tq, r), lambda i: (i, P_OFFSETS["q_lat"] // r)),
                  _const_spec((1, r)),
                  _const_spec(w_uqT.shape),
                  _const_spec(w_iqT.shape)],
        out_specs=(pl.BlockSpec((ATTN_HEADS, ATTN_HEAD_DIM, tq), lambda i: (0, 0, i)),
                   pl.BlockSpec((IDX_HEADS, IDX_HEAD_DIM, tq), lambda i: (0, 0, i))),
        compiler_params=_cparams(("arbitrary",), "qpath"),
        name="qpath",
    )(proj, g_q, w_uqT, w_iqT)


def _kvpath_kernel(kvl_ref, sm_ref, gkv_ref, gk_ref, bk_ref, wuk_ref, wuv_ref, k_ref, vT_ref, kidx_ref, widx_ref, *, wscale):
    x = kvl_ref[...]
    ckv = (x * lax.rsqrt(jnp.mean(x * x, axis=-1, keepdims=True) + NORM_EPS) * gkv_ref[...]).astype(BF16)
    kfull = jnp.dot(ckv, wuk_ref[...], preferred_element_type=F32)
    for h in range(ATTN_HEADS):
        k_ref[h] = kfull[:, h * ATTN_HEAD_DIM:(h + 1) * ATTN_HEAD_DIM].astype(BF16)
    nt = (((1,), (1,)), ((), ()))
    vT = lax.dot_general(wuv_ref[...], ckv, nt, preferred_element_type=F32)
    vT_ref[:, :ATTN_HEAD_DIM, :] = vT.reshape(ATTN_HEADS, ATTN_HEAD_DIM, -1).astype(BF16)
    vT_ref[:, ATTN_HEAD_DIM:, :] = jnp.ones((ATTN_HEADS, V_ONES, vT_ref.shape[2]), BF16)
    sm = sm_ref[...]
    ki = sm[:, :IDX_HEAD_DIM]
    mu = jnp.mean(ki, axis=-1, keepdims=True)
    var = jnp.mean(jnp.square(ki - mu), axis=-1, keepdims=True)
    kidx_ref[...] = ((ki - mu) * lax.rsqrt(var + NORM_EPS) * gk_ref[...] + bk_ref[...]).astype(BF16)
    widx_ref[...] = sm[:, SM_WIDX:SM_WIDX + IDX_HEADS] * wscale


def _kvpath(proj_tail, g_kv, g_kidx, b_kidx, w_ukT, w_uvT, tm):
    s = proj_tail.shape[0]
    r = KV_LORA_RANK
    return pl.pallas_call(
        functools.partial(_kvpath_kernel, wscale=IDX_HEADS ** -0.5 * IDX_HEAD_DIM ** -0.5),
        out_shape=(jax.ShapeDtypeStruct((ATTN_HEADS, s, ATTN_HEAD_DIM), BF16),
                   jax.ShapeDtypeStruct((ATTN_HEADS, V_ROWS, s), BF16),
                   jax.ShapeDtypeStruct((s, IDX_HEAD_DIM), BF16),
                   jax.ShapeDtypeStruct((s, IDX_HEADS), F32)),
        grid=(s // tm,),
        in_specs=[pl.BlockSpec((tm, r), lambda i: (i, (P_OFFSETS["kv_lat"] - P_MAIN) // r)),
                  pl.BlockSpec((tm, SMALL_W), lambda i: (i, (SMALL_OFF - P_MAIN) // SMALL_W)),
                  _const_spec((1, r)),
                  _const_spec((1, IDX_HEAD_DIM)),
                  _const_spec((1, IDX_HEAD_DIM)),
                  _const_spec(w_ukT.shape),
                  _const_spec(w_uvT.shape)],
        out_specs=(pl.BlockSpec((ATTN_HEADS, tm, ATTN_HEAD_DIM), lambda i: (0, i, 0)),
                   pl.BlockSpec((ATTN_HEADS, V_ROWS, tm), lambda i: (0, 0, i)),
                   pl.BlockSpec((tm, IDX_HEAD_DIM), lambda i: (i, 0)),
                   pl.BlockSpec((tm, IDX_HEADS), lambda i: (i, 0))),
        compiler_params=_cparams(("arbitrary",), "kvpath"),
        name="kvpath",
    )(proj_tail, proj_tail, g_kv, g_kidx, b_kidx, w_ukT, w_uvT)


def _key_to_float(key):
    bits = jnp.where(key >= 0, key, key ^ 0x7FFFFFFF)
    return jnp.where(key < KEY_NEG_INF, -jnp.inf, pltpu.bitcast(bits, F32))


def _key16_to_float(key):
    bits = jnp.where(key >= 0, key, key ^ 0x7FFF)
    return jnp.where(key < KEY16_NEG_INF, -jnp.inf, pltpu.bitcast(jnp.left_shift(bits, 16), F32))


IDX_VISITS = 8
IDX_PASSES16 = 16


def _indexer_kernel(kidx_ref, qiT_ref, wT_ref, sc_ref, thr_ref, jl_ref, f0_ref, f1_ref, h0_ref, h1_ref, *, seq, tq, nsel):
    i = pl.program_id(0)
    n = seq // tq
    ch = IDX_ROWS
    cb = tq
    n_score = jnp.where(i < n, (i + 1) * (tq // ch), 0)
    n_prev = jnp.maximum(i, 1)
    searching = i >= 1

    @pl.when(i == 0)
    def _():
        h1_ref[:cb, :] = jnp.zeros((cb, tq), BF16)

    def run(cur_f, cur_h, prv_f, prv_h):
        tpos = i * tq + lax.broadcasted_iota(I32, (ch, tq), 1)

        def score_chunk(c):
            r0 = pl.multiple_of(c * ch, ch)
            kc = kidx_ref[pl.ds(r0, ch), :]
            acc = jnp.zeros((ch, tq), F32)
            for h in range(IDX_HEADS):
                r = jnp.dot(kc, qiT_ref[h], preferred_element_type=F32)
                acc = acc + jnp.maximum(r, 0.0) * wT_ref[h]
            spos = r0 + lax.broadcasted_iota(I32, (ch, tq), 0)
            val = jnp.where(spos <= tpos, acc, -jnp.inf)
            sc_ref[pl.ds(r0, ch), :] = val
            cur_f[pl.ds(r0, ch), :] = val
            cur_h[pl.ds(r0, ch), :] = val.astype(BF16)

        def visit(v, st):
            acc, t16 = st
            pb, cc = v // n_prev, v % n_prev
            cand_key = jnp.where(pb == 0, 0, t16 + jnp.left_shift(jnp.int32(1), jnp.maximum(15 - pb, 0)))
            cand = _key16_to_float(cand_key).astype(BF16)
            blk = prv_h[pl.ds(pl.multiple_of(cc * cb, cb), cb), :]
            ones = jnp.where(blk >= cand, jnp.ones((), BF16), jnp.zeros((), BF16))
            part = ones[:16]
            for g in range(1, cb // 16):
                part = part + ones[g * 16:(g + 1) * 16]
            acc = acc + part.astype(F32)
            last = cc == n_prev - 1
            take = jnp.sum(acc, axis=0, keepdims=True).astype(I32) >= nsel
            t_new = jnp.where(pb == 0, jnp.where(take, 0, -2 ** 15), jnp.where(take, cand_key, t16))
            t16 = jnp.where(last & (pb < IDX_PASSES16) & searching, t_new, t16)
            return jnp.where(last, 0.0, acc), t16

        def fused(c, st):
            score_chunk(c)
            for e in range(IDX_VISITS):
                st = visit(c * IDX_VISITS + e, st)
            return st
        st = (jnp.zeros((16, tq), F32), jnp.zeros((1, tq), I32))
        st = lax.fori_loop(0, n_score, fused, st)
        st = lax.fori_loop(0, jnp.where(i == n, IDX_PASSES16 * n_prev, 0), visit, st)
        _, t16 = st

        def fill_chunk(c, carry):
            sc_ref[pl.ds(pl.multiple_of(c * cb, cb), cb), :] = jnp.full((cb, tq), -jnp.inf, F32)
            return carry
        lax.fori_loop(jnp.where(i < n, i + 1, seq // cb), seq // cb, fill_chunk, 0)

        @pl.when(searching)
        def _finish_previous_tile():
            def count(pred):
                def body(c, part):
                    r0 = pl.multiple_of(c * cb, cb)
                    m = jnp.where(pred(prv_f[pl.ds(r0, cb), :], r0), 1, 0)
                    return part + jnp.sum(m.reshape(cb // 8, 8, tq), axis=0)
                part = lax.fori_loop(0, n_prev, body, jnp.zeros((8, tq), I32))
                return jnp.sum(part, axis=0, keepdims=True)

            rbits = pltpu.bitcast(_key16_to_float(t16), I32)
            r_key = jnp.where(rbits >= 0, rbits, rbits ^ 0x7FFFFFFF)

            def bit_cond(s_):
                b, _, _, n_open = s_
                return (b < 17) & (n_open > 0)

            def bit_step(s_):
                b, t, settled, _ = s_
                cand_key = t + jnp.left_shift(jnp.int32(1), 16 - b)
                cand = _key_to_float(cand_key)
                cnt = count(lambda blk, r0: blk >= cand)
                t = jnp.where((cnt >= nsel) & (settled == 0), cand_key, t)
                settled = jnp.where(cnt == nsel, 1, settled)
                return b + 1, t, settled, jnp.sum(1 - settled)
            _, t_key, settled, n_open = lax.while_loop(
                bit_cond, bit_step, (jnp.int32(0), r_key - 2 ** 16, jnp.zeros((1, tq), I32), jnp.int32(tq)))
            thr = _key_to_float(t_key)
            thr_ref[...] = thr
            jl_ref[...] = jnp.full((1, tq), seq, I32)

            @pl.when(n_open > 0)
            def _unsettled():
                tie = (settled == 0) & (count(lambda blk, r0: blk >= thr) > nsel) & (thr > -jnp.inf)

                @pl.when(jnp.max(tie.astype(I32)) > 0)
                def _break_ties():
                    need = nsel - count(lambda blk, r0: blk > thr)

                    def eq_below(j):
                        return count(lambda blk, r0: (blk == thr) & (r0 + lax.broadcasted_iota(I32, (cb, tq), 0) < j))

                    def jbit(b, j):
                        test = j + jnp.left_shift(jnp.int32(1), (seq.bit_length() - 2) - b)
                        return jnp.where(eq_below(test) < need, test, j)
                    jlast = lax.fori_loop(0, seq.bit_length() - 1, jbit, jnp.zeros((1, tq), I32))
                    jl_ref[...] = jnp.where(tie, jlast, seq)

    @pl.when(i % 2 == 0)
    def _even():
        run(f0_ref, h0_ref, f1_ref, h1_ref)

    @pl.when(i % 2 == 1)
    def _odd():
        run(f1_ref, h1_ref, f0_ref, h0_ref)


def _indexer(kidx, qiT, wT, tq, nsel):
    s = kidx.shape[0]
    n = s // tq
    cur = lambda i: jnp.minimum(i, n - 1)
    prv = lambda i: jnp.maximum(i - 1, 0)
    return pl.pallas_call(
        functools.partial(_indexer_kernel, seq=s, tq=tq, nsel=nsel),
        out_shape=(jax.ShapeDtypeStruct((s, s), F32), jax.ShapeDtypeStruct((1, s), F32),
                   jax.ShapeDtypeStruct((1, s), I32)),
        grid=(n + 1,),
        in_specs=[_const_spec((s, IDX_HEAD_DIM)),
                  pl.BlockSpec((IDX_HEADS, IDX_HEAD_DIM, tq), lambda i: (0, 0, cur(i))),
                  pl.BlockSpec((IDX_HEADS, 1, tq), lambda i: (0, 0, cur(i)))],
        out_specs=(pl.BlockSpec((s, tq), lambda i: (0, cur(i))),
                   pl.BlockSpec((1, tq), lambda i: (0, prv(i))),
                   pl.BlockSpec((1, tq), lambda i: (0, prv(i)))),
        scratch_shapes=[pltpu.VMEM((s, tq), F32), pltpu.VMEM((s, tq), F32),
                        pltpu.VMEM((s, tq), BF16), pltpu.VMEM((s, tq), BF16)],
        compiler_params=_cparams(("arbitrary",), "indexer"),
        name="indexer",
    )(kidx, qiT, wT)


def _attn_kernel(tiles_ref, qT_ref, k_ref, vT_ref, keys_ref, thr_ref, jl_ref, z_ref, sl_ref, kf_ref, qf_ref, o_ref,
                 acc_ref, m_ref, mb_ref, lg_ref, p_ref, *, tq, tk):
    qi = tiles_ref[0, pl.program_id(0)]
    kj = tiles_ref[1, pl.program_id(0)]

    @pl.when(kj == 0)
    def _init():
        acc_ref[...] = jnp.zeros(acc_ref.shape, F32)
        m_ref[...] = jnp.full(m_ref.shape, -jnp.inf, F32)

    def _compute():
        spos = kj * tk + lax.broadcasted_iota(I32, (tk, tq), 0)
        tpos = qi * tq + lax.broadcasted_iota(I32, (tk, tq), 1)
        has_ties = tiles_ref[2, pl.program_id(0)] > 0

        @pl.when(has_ties)
        def _mask_with_tie_rows():
            sc = keys_ref[...]
            thr = thr_ref[...]
            sel = ((sc > thr) | ((sc == thr) & (spos <= jl_ref[...]))) & (spos <= tpos)
            mb_ref[...] = jnp.where(sel, 0.0, -jnp.inf)

        @pl.when(jnp.logical_not(has_ties))
        def _mask():
            mb_ref[...] = jnp.where((keys_ref[...] >= thr_ref[...]) & (spos <= tpos), 0.0, -jnp.inf)
        tile_off = (kj * tk - qi * tq).astype(F32)

        def group(g, carry):
            def logits(u):
                h = g * HEAD_GROUP + u
                qh = jnp.concatenate([qT_ref[h], qf_ref[h]], axis=0)
                part = jnp.full((8, tq), -jnp.inf, F32)
                for c in range(tk // ATTN_ROWS):
                    rows = pl.ds(c * ATTN_ROWS, ATTN_ROWS)
                    kh = jnp.concatenate([k_ref[h, rows, :], kf_ref[rows, :]], axis=1)
                    lg = jnp.dot(kh, qh, preferred_element_type=F32) + mb_ref[rows, :]
                    lg_ref[u % 2, rows, :] = lg
                    part = jnp.maximum(part, jnp.max(lg.reshape(ATTN_ROWS // 8, 8, tq), axis=0))
                shift = sl_ref[h] * tile_off
                m_old = m_ref[g, u]
                return m_old, jnp.maximum(m_old, jnp.max(part, axis=0, keepdims=True) + shift), shift

            def probs(u, m_old, m_new, shift):
                m_safe = jnp.where(m_new == -jnp.inf, 0.0, m_new)
                m_tile = m_safe - shift
                for c in range(tk // ATTN_ROWS):
                    rows = pl.ds(c * ATTN_ROWS, ATTN_ROWS)
                    p_ref[u % 2, rows, :] = jnp.exp2(lg_ref[u % 2, rows, :] - m_tile).astype(BF16)
                m_ref[g, u] = m_new
                return jnp.exp2(m_old - m_safe)

            def values(u, alpha):
                h = g * HEAD_GROUP + u
                acc_ref[g, u] = alpha * acc_ref[g, u] + jnp.dot(vT_ref[h], p_ref[u % 2], preferred_element_type=F32)

            stats = logits(0)
            alpha_prev = None
            for u in range(HEAD_GROUP):
                stats_next = logits(u + 1) if u + 1 < HEAD_GROUP else None
                alpha = probs(u, *stats)
                if u >= 1:
                    values(u - 1, alpha_prev)
                stats, alpha_prev = stats_next, alpha
            values(HEAD_GROUP - 1, alpha_prev)
            return carry
        lax.fori_loop(0, ATTN_HEADS // HEAD_GROUP, group, 0)
    _compute()

    @pl.when(kj == (qi * tq + tq - 1) // tk)
    def _finish():
        for h in range(ATTN_HEADS):
            g, u = divmod(h, HEAD_GROUP)
            cols = slice(h * ATTN_HEAD_DIM, (h + 1) * ATTN_HEAD_DIM)
            acc = acc_ref[g, u]
            o = (acc[:ATTN_HEAD_DIM] * (1.0 / acc[ATTN_HEAD_DIM:ATTN_HEAD_DIM + 1])).T
            o_ref[:, cols] = (o * _silu(z_ref[:, cols])).astype(BF16)


def _alibi_features(tq, tk):
    sigma = jnp.exp2(-8.0 * jnp.arange(1, ATTN_HEADS + 1, dtype=F32) / ATTN_HEADS) * LOG2E
    s1 = sigma.astype(BF16)
    s2 = (sigma - s1.astype(F32)).astype(BF16)
    s3 = (sigma - s1.astype(F32) - s2.astype(F32)).astype(BF16)
    pieces = jnp.stack([s1, s2, s3, s1, s2, s3], axis=1)
    qf = jnp.zeros((ATTN_HEADS, ATTN_HEAD_DIM, tq), BF16)
    qf = qf.at[:, :6, :].set(jnp.broadcast_to(pieces[:, :, None], (ATTN_HEADS, 6, tq)))
    r = jnp.arange(tk, dtype=I32)
    r_hi = ((r // 256) * 256).astype(BF16)
    r_lo = (r % 256).astype(BF16)
    kf = jnp.zeros((tk, ATTN_HEAD_DIM), BF16).at[:, :6].set(jnp.stack([r_hi, r_hi, r_hi, r_lo, r_lo, r_lo], axis=1))
    return jnp.broadcast_to(sigma[:, None, None], (ATTN_HEADS, 1, tq)), kf, qf


def _attention(qT, k, vT, keys, thr, jlast, proj, tq, tk):
    s = k.shape[1]
    ng = ATTN_HEADS // HEAD_GROUP
    tiles = [(qi, kj) for qi in range(s // tq) for kj in range((qi * tq + tq - 1) // tk + 1)]
    const3 = lambda shape: pl.BlockSpec(shape, lambda i, t: (0, 0, 0), pipeline_mode=pl.Buffered(1))
    grid_spec = pltpu.PrefetchScalarGridSpec(
        num_scalar_prefetch=1,
        grid=(len(tiles),),
        in_specs=[pl.BlockSpec((ATTN_HEADS, ATTN_HEAD_DIM, tq), lambda i, t: (0, 0, t[0, i])),
                  pl.BlockSpec((ATTN_HEADS, tk, ATTN_HEAD_DIM), lambda i, t: (0, t[1, i], 0)),
                  pl.BlockSpec((ATTN_HEADS, V_ROWS, tk), lambda i, t: (0, 0, t[1, i])),
                  pl.BlockSpec((tk, tq), lambda i, t: (t[1, i], t[0, i])),
                  pl.BlockSpec((1, tq), lambda i, t: (0, t[0, i])),
                  pl.BlockSpec((1, tq), lambda i, t: (0, t[0, i])),
                  pl.BlockSpec((tq, ATTN_WIDTH), lambda i, t: (t[0, i], P_OFFSETS["z_attn"] // ATTN_WIDTH)),
                  const3((ATTN_HEADS, 1, tq)),
                  pl.BlockSpec((tk, ATTN_HEAD_DIM), lambda i, t: (0, 0), pipeline_mode=pl.Buffered(1)),
                  const3((ATTN_HEADS, ATTN_HEAD_DIM, tq))],
        out_specs=pl.BlockSpec((tq, ATTN_WIDTH), lambda i, t: (t[0, i], 0)),
        scratch_shapes=[pltpu.VMEM((ng, HEAD_GROUP, V_ROWS, tq), F32),
                        pltpu.VMEM((ng, HEAD_GROUP, 1, tq), F32),
                        pltpu.VMEM((tk, tq), F32),
                        pltpu.VMEM((2, tk, tq), F32),
                        pltpu.VMEM((2, tk, tq), BF16)],
    )
    call = pl.pallas_call(
        functools.partial(_attn_kernel, tq=tq, tk=tk),
        out_shape=jax.ShapeDtypeStruct((s, ATTN_WIDTH), BF16),
        grid_spec=grid_spec,
        compiler_params=_cparams(("arbitrary",), "attn"),
        name="attn",
    )
    tile_tbl = jnp.asarray(tiles, I32).T
    tied = jnp.any(jlast.reshape(s // tq, tq) < s, axis=1).astype(I32)
    tile_tbl = jnp.concatenate([tile_tbl, tied[tile_tbl[0]][None]], axis=0)
    return call(tile_tbl, qT, k, vT, keys, thr, jlast, proj, *_alibi_features(tq, tk))


def _softcap(x):
    return GATE_SOFTCAP * jnp.tanh(x / GATE_SOFTCAP)


def _mlstm_kernel(q_ref, k_ref, v_ref, og_ref, z_ref, gt_ref, g_ref, out_ref, c_ref, n_ref, m_ref, *, chunk):
    ci = pl.program_id(1)
    L = chunk
    dk, dv = MLSTM_QK_DIM, MLSTM_V_DIM

    @pl.when(ci == 0)
    def _init():
        c_ref[...] = jnp.zeros(c_ref.shape, F32)
        n_ref[...] = jnp.zeros(n_ref.shape, F32)
        m_ref[...] = jnp.zeros(m_ref.shape, F32)

    gt = gt_ref[...]
    sub = lax.broadcasted_iota(I32, gt.shape, 0)
    r_i = lax.broadcasted_iota(I32, (L, L), 0)
    c_i = lax.broadcasted_iota(I32, (L, L), 1)
    eye = r_i == c_i
    tril = r_i >= c_i
    nt = (((1,), (1,)), ((), ()))
    tn = (((0,), (0,)), ((), ()))

    for j in range(MLSTM_GROUP):
        hd = pl.program_id(0) * MLSTM_GROUP + j
        ig_row = _softcap(jnp.sum(jnp.where(sub == hd, gt, 0.0), axis=0, keepdims=True))
        fg_row = _softcap(jnp.sum(jnp.where(sub == MLSTM_HEADS + hd, gt, 0.0), axis=0, keepdims=True))
        logf_row = jnp.minimum(fg_row, 0.0) - jnp.log1p(jnp.exp(-jnp.abs(fg_row)))
        ig_col = jnp.sum(jnp.where(eye, ig_row, 0.0), axis=1, keepdims=True)
        b_col = jnp.sum(jnp.where(tril, logf_row, 0.0), axis=1, keepdims=True)
        b_row = jnp.sum(jnp.where(eye, b_col, 0.0), axis=0, keepdims=True)
        dmat = jnp.where(tril, b_col - b_row + ig_row, -jnp.inf)
        m_prev = m_ref[j]
        m_inter = b_col + m_prev
        m_t = jnp.maximum(m_inter, jnp.max(dmat, axis=1, keepdims=True))

        qc = q_ref[:, j * dk:(j + 1) * dk]
        kc = k_ref[:, j * dk:(j + 1) * dk]
        vc = v_ref[:, j * dv:(j + 1) * dv].astype(BF16)
        s = lax.dot_general(qc, kc, nt, preferred_element_type=F32) * jnp.exp(dmat - m_t)
        inter = jnp.exp(m_inter - m_t)
        num = (jnp.dot(s.astype(BF16), vc, preferred_element_type=F32)
               + inter * jnp.dot(qc, c_ref[j].astype(BF16), preferred_element_type=F32))
        qn = jnp.sum(qc.astype(F32) * n_ref[j], axis=1, keepdims=True)
        den = jnp.sum(s, axis=1, keepdims=True) + inter * qn
        hh = num / jnp.maximum(jnp.abs(den), jnp.exp(-m_t))

        g_last = b_col[L - 1:L, :]
        m_new = m_t[L - 1:L, :]
        wgt = jnp.exp(g_last - b_col + ig_col - m_new)
        decay = jnp.exp(g_last + m_prev - m_new)
        wk = wgt * kc.astype(F32)
        c_ref[j] = decay * c_ref[j] + lax.dot_general(wk.astype(BF16), vc, tn, preferred_element_type=F32)
        n_ref[j] = decay * n_ref[j] + jnp.sum(wk, axis=0, keepdims=True)
        m_ref[j] = m_new

        hn = hh * lax.rsqrt(jnp.mean(hh * hh, axis=-1, keepdims=True) + NORM_EPS) * g_ref[j]
        cols = slice(j * dv, (j + 1) * dv)
        out_ref[:, cols] = (hn * _sigmoid(og_ref[:, cols]) * _silu(z_ref[:, cols])).astype(BF16)


def _mlstm(qk, proj, gates_t, g_mh3, chunk):
    s = qk.shape[0]
    gdk, gdv = MLSTM_GROUP * MLSTM_QK_DIM, MLSTM_GROUP * MLSTM_V_DIM
    vb, ob, zb = (P_OFFSETS[n] // gdv for n in ("v_m", "o_m", "z_m"))
    return pl.pallas_call(
        functools.partial(_mlstm_kernel, chunk=chunk),
        out_shape=jax.ShapeDtypeStruct((s, MLSTM_WIDTH), BF16),
        grid=(MLSTM_HEADS // MLSTM_GROUP, s // chunk),
        in_specs=[pl.BlockSpec((chunk, gdk), lambda h, c: (c, h)),
                  pl.BlockSpec((chunk, gdk), lambda h, c: (c, MLSTM_QK_WIDTH // gdk + h)),
                  pl.BlockSpec((chunk, gdv), lambda h, c: (c, vb + h)),
                  pl.BlockSpec((chunk, gdv), lambda h, c: (c, ob + h)),
                  pl.BlockSpec((chunk, gdv), lambda h, c: (c, zb + h)),
                  pl.BlockSpec((2 * MLSTM_HEADS, chunk), lambda h, c: (0, c)),
                  pl.BlockSpec((MLSTM_GROUP, 1, MLSTM_V_DIM), lambda h, c: (h, 0, 0))],
        out_specs=pl.BlockSpec((chunk, gdv), lambda h, c: (c, h)),
        scratch_shapes=[pltpu.VMEM((MLSTM_GROUP, MLSTM_QK_DIM, MLSTM_V_DIM), F32),
                        pltpu.VMEM((MLSTM_GROUP, 1, MLSTM_QK_DIM), F32),
                        pltpu.VMEM((MLSTM_GROUP, 1, 1), F32)],
        compiler_params=_cparams(("arbitrary", "arbitrary"), "mlstm"),
        name="mlstm",
    )(qk, qk, proj, proj, proj, gates_t, g_mh3)


def _merge_kernel(a1_ref, a2_ref, w1_ref, w2_ref, ga_ref, gm_ref, o_ref):
    y1 = jnp.dot(a1_ref[...], w1_ref[...], preferred_element_type=F32)
    y2 = jnp.dot(a2_ref[...], w2_ref[...], preferred_element_type=F32)
    o_ref[...] = (_sigmoid(ga_ref[...]) * y1 + _sigmoid(gm_ref[...]) * y2).astype(BF16)


def _merge(a1, a2, w1, w2, proj):
    s, d = a1.shape
    tm = min(ROW_TILE, s)
    tn = COL_TILE
    gab, gmb = P_OFFSETS["g_attn"] // tn, P_OFFSETS["g_mlstm"] // tn
    return pl.pallas_call(
        _merge_kernel,
        out_shape=jax.ShapeDtypeStruct((s, D_MODEL), BF16),
        grid=(s // tm, D_MODEL // tn),
        in_specs=[pl.BlockSpec((tm, d), lambda i, j: (i, 0)),
                  pl.BlockSpec((tm, d), lambda i, j: (i, 0)),
                  pl.BlockSpec((d, tn), lambda i, j: (0, j)),
                  pl.BlockSpec((d, tn), lambda i, j: (0, j)),
                  pl.BlockSpec((tm, tn), lambda i, j: (i, gab + j)),
                  pl.BlockSpec((tm, tn), lambda i, j: (i, gmb + j))],
        out_specs=pl.BlockSpec((tm, tn), lambda i, j: (i, j)),
        compiler_params=_cparams(("arbitrary", "arbitrary"), "merge"),
        name="merge",
    )(a1, a2, w1, w2, proj, proj)


def _final_kernel(mg_ref, w_ref, x_ref, gate_ref, lg_ref, lb_ref, o_ref, *, tn, nn):
    j = pl.program_id(1)
    for jj in range(nn):
        @pl.when(j == jj)
        def _project(jj=jj):
            o_ref[:, jj * tn:(jj + 1) * tn] = jnp.dot(mg_ref[...], w_ref[...], preferred_element_type=F32)

    @pl.when(j == nn - 1)
    def _norm():
        d = nn * tn
        ssum = 0.0
        for jj in range(nn):
            cols = slice(jj * tn, (jj + 1) * tn)
            r = DEEPNORM_ALPHA * x_ref[:, cols] + gate_ref[:, cols] * o_ref[:, cols]
            o_ref[:, cols] = r
            ssum = ssum + jnp.sum(r, axis=-1, keepdims=True)
        mu = ssum / d
        vsum = 0.0
        for jj in range(nn):
            cols = slice(jj * tn, (jj + 1) * tn)
            vsum = vsum + jnp.sum(jnp.square(o_ref[:, cols] - mu), axis=-1, keepdims=True)
        inv = lax.rsqrt(vsum / d + NORM_EPS)
        for jj in range(nn):
            cols = slice(jj * tn, (jj + 1) * tn)
            o_ref[:, cols] = (o_ref[:, cols] - mu) * inv * lg_ref[:, cols] + lb_ref[:, cols]


def _final(merged, w_out, x2, mod, ln_g, ln_b):
    s, d = x2.shape
    tm = min(ROW_TILE, s)
    tn = COL_TILE
    nn = d // tn
    return pl.pallas_call(
        functools.partial(_final_kernel, tn=tn, nn=nn),
        out_shape=jax.ShapeDtypeStruct((s, d), F32),
        grid=(s // tm, nn),
        in_specs=[pl.BlockSpec((tm, d), lambda i, j: (i, 0)),
                  pl.BlockSpec((d, tn), lambda i, j: (0, j)),
                  pl.BlockSpec((tm, d), lambda i, j: (i, 0)),
                  pl.BlockSpec((1, d), lambda i, j: (0, 2)),
                  pl.BlockSpec((1, d), lambda i, j: (0, 0)),
                  pl.BlockSpec((1, d), lambda i, j: (0, 0))],
        out_specs=pl.BlockSpec((tm, d), lambda i, j: (i, 0), pipeline_mode=pl.Buffered(1)),
        compiler_params=_cparams(("arbitrary", "arbitrary"), "final"),
        name="final",
    )(merged, w_out, x2, mod, ln_g, ln_b)


RG_TN = 512
F32_SUBLANES = 8
NARROW_A = ("k_idx", "w_idx")
NARROW_B = ("i_m", "f_m")


def _regroup_kernel(tbl_ref, main_ref, na_ref, nb_ref, o_ref, *, n_a, n_b):
    @pl.when(tbl_ref[pl.program_id(0)] >= 0)
    def _wide():
        o_ref[...] = main_ref[...].astype(BF16)

    @pl.when(tbl_ref[pl.program_id(0)] < 0)
    def _narrow():
        o_ref[:n_a, :] = na_ref[...].astype(BF16)
        o_ref[n_a:n_a + n_b, :] = nb_ref[...].astype(BF16)
        o_ref[n_a + n_b:, :] = jnp.zeros((o_ref.shape[0] - n_a - n_b, o_ref.shape[1]), BF16)


def _window_starts(first_col, n_cols):
    starts = []
    for oc in range(first_col, first_col + n_cols, RG_TN):
        if oc >= SMALL_OFF:
            starts.append(-1)
            continue
        seg = next(n for n in P_ORDER if P_OFFSETS[n] <= oc < P_OFFSETS[n] + IN_WIDTH_OF[n])
        start = IN_OFFSETS[seg] + oc - P_OFFSETS[seg]
        assert start % F32_SUBLANES == 0, (seg, start)
        starts.append(start // F32_SUBLANES)
    return starts


def _regroup_w(w_inT, first_col, n_cols):
    d = w_inT.shape[1]
    starts = _window_starts(first_col, n_cols)
    n_a = sum(IN_WIDTH_OF[n] for n in NARROW_A)
    n_b = sum(IN_WIDTH_OF[n] for n in NARROW_B)
    off_a, off_b = IN_OFFSETS[NARROW_A[0]], IN_OFFSETS[NARROW_B[0]]
    grid_spec = pltpu.PrefetchScalarGridSpec(
        num_scalar_prefetch=1,
        grid=(n_cols // RG_TN,),
        in_specs=[pl.BlockSpec((pl.Element(RG_TN), pl.Element(d)), lambda j, tbl: (jnp.maximum(tbl[j], 0) * F32_SUBLANES, 0)),
                  pl.BlockSpec((pl.Element(n_a), pl.Element(d)), lambda j, tbl: (off_a, 0)),
                  pl.BlockSpec((pl.Element(n_b), pl.Element(d)), lambda j, tbl: (off_b, 0))],
        out_specs=pl.BlockSpec((RG_TN, d), lambda j, tbl: (j, 0)),
    )
    return pl.pallas_call(
        functools.partial(_regroup_kernel, n_a=n_a, n_b=n_b),
        out_shape=jax.ShapeDtypeStruct((n_cols, d), BF16),
        grid_spec=grid_spec,
        compiler_params=_cparams(("arbitrary",), "regroup"),
        name="regroup",
    )(jnp.asarray(starts, I32), w_inT, w_inT, w_inT)


def _regroup_cols(a, pad_to):
    parts = [a[..., IN_OFFSETS[n]:IN_OFFSETS[n] + IN_WIDTH_OF[n]] for n in P_ORDER]
    parts.append(jnp.zeros(a.shape[:-1] + (pad_to - P_USED,), a.dtype))
    return jnp.concatenate(parts, axis=-1)


def _layer(x2, c, w_ada, b_ada, w_in, b_in, g_q, g_kv, w_uq, w_iq, w_uk, w_uv, g_kidx, b_kidx, conv_w, conv_b, g_mh,
           w_attn_out, w_mlstm_out, w_out, ln_g, ln_b):
    s, d = x2.shape
    assert d == D_MODEL and s % PROJ_TM == 0, (s, d)
    tq, tk = TQ, TK
    nsel = min(TOPK_MAX, s // 4)

    w_inT = w_in.T
    w_tail = _regroup_w(w_inT, P_MAIN, P_TAIL)
    b_cat = _regroup_cols(b_in, P_TOTAL).reshape(1, P_TOTAL)
    w_uqT = w_uq.T.astype(BF16)
    w_iqT = w_iq.T.astype(BF16)
    w_ukT = w_uk.reshape(ATTN_WIDTH, KV_LORA_RANK).T.astype(BF16)
    w_uvT = w_uv.transpose(0, 2, 1).reshape(ATTN_WIDTH, KV_LORA_RANK).astype(BF16)

    mod = _ada(c.reshape(d, 1), w_ada, b_ada.reshape(1, -1))
    u = _modulate(x2, mod)
    proj = _proj_main(u, w_inT, b_cat[:, :P_PLAIN], 0)
    qk = _proj_main(u, w_inT, b_cat[:, P_PLAIN:P_MAIN], P_PLAIN, (conv_w, conv_b.reshape(1, -1)))
    proj_tail = _proj(u, w_tail, b_cat[:, P_MAIN:])

    qT, qiT = _qpath(proj, g_q.reshape(1, -1), w_uqT, w_iqT, LATENT_TM)
    k, vT, kidx, widx = _kvpath(proj_tail, g_kv.reshape(1, -1), g_kidx.reshape(1, -1), b_kidx.reshape(1, -1), w_ukT, w_uvT,
                                LATENT_TM)
    wT = widx.T.reshape(IDX_HEADS, 1, s)
    keys, thr, jlast = _indexer(kidx, qiT, wT, tq, nsel)
    a_attn = _attention(qT, k, vT, keys, thr, jlast, proj, tq, tk)

    gates_t = proj_tail[:, SMALL_OFF - P_MAIN + SM_I:SMALL_OFF - P_MAIN + SM_F + MLSTM_HEADS].T
    a_mlstm = _mlstm(qk, proj, gates_t, g_mh.reshape(MLSTM_HEADS, 1, MLSTM_V_DIM), MLSTM_CHUNK)

    merged = _merge(a_attn, a_mlstm, w_attn_out.astype(BF16), w_mlstm_out.astype(BF16), proj)
    return _final(merged, w_out.astype(BF16), x2, mod, ln_g.reshape(1, -1), ln_b.reshape(1, -1))


def kernel(x, c, w_ada, b_ada, w_in, b_in, g_q, g_kv, w_uq, w_iq, w_uk, w_uv, g_kidx, b_kidx, conv_w, conv_b, g_mh,
           w_attn_out, w_mlstm_out, w_out, ln_g, ln_b):
    bsz, seq, d = x.shape
    assert bsz == 1 and w_ada.shape[0] == 1, "single batch, single layer"
    out = _layer(x.reshape(seq, d), c, w_ada[0], b_ada[0], w_in[0], b_in[0], g_q[0], g_kv[0], w_uq[0], w_iq[0],
                 w_uk[0], w_uv[0], g_kidx[0], b_kidx[0], conv_w[0], conv_b[0], g_mh[0], w_attn_out[0],
                 w_mlstm_out[0], w_out[0], ln_g[0], ln_b[0])
    return out.reshape(bsz, seq, d)
```
